```python
import jax, jax.numpy as jnp
from jax import lax
import numpy as np

D_MODEL = 1024
BATCH = 8
SEQ = 8192
DEPTH = 2

GRID_W = 64
CTX_LEN = 256
CONV_WIDTH = D_MODEL
CONV_KERNEL = 31
N_HEADS = 16
N_KV_HEADS = 4
HEAD_DIM = 64
GROUP = N_HEADS // N_KV_HEADS
ATTN_WIDTH = N_HEADS * HEAD_DIM
KV_WIDTH = N_KV_HEADS * HEAD_DIM
Q_BLOCK = 128
ROPE_THETA = 10000.0
ROPE_AXIS_DIM = HEAD_DIM // 2
EPS = 1e-6
ATTN_SCALE = HEAD_DIM ** -0.5
IN_SPLITS = (2 * CONV_WIDTH, CONV_WIDTH, ATTN_WIDTH, KV_WIDTH, KV_WIDTH, ATTN_WIDTH, 2 * D_MODEL)
IN_WIDTH = sum(IN_SPLITS)
IN_OFFSETS = tuple(int(o) for o in np.cumsum(IN_SPLITS)[:-1])

kernel_name = "hybrid_conformer_gqa_prefix_dit_block"


def rms_norm(x, g):
    xf = x.astype(jnp.float32)
    y = xf * lax.rsqrt(jnp.mean(xf * xf, axis=-1, keepdims=True) + EPS)
    return (y * g.astype(jnp.float32)).astype(x.dtype)


def layer_norm(x, g, b):
    xf = x.astype(jnp.float32)
    mu = jnp.mean(xf, axis=-1, keepdims=True)
    var = jnp.mean(jnp.square(xf - mu), axis=-1, keepdims=True)
    y = (xf - mu) * lax.rsqrt(var + EPS)
    return (y * g.astype(jnp.float32) + b.astype(jnp.float32)).astype(x.dtype)


def axial_rope_tables(n_tokens):
    rows = n_tokens // GRID_W
    row = jnp.repeat(jnp.arange(rows, dtype=jnp.float32), GRID_W)
    col = jnp.tile(jnp.arange(GRID_W, dtype=jnp.float32), rows)
    inv_freq = ROPE_THETA ** (-jnp.arange(0, ROPE_AXIS_DIM, 2, dtype=jnp.float32) / ROPE_AXIS_DIM)
    ang = jnp.concatenate([row[:, None] * inv_freq, col[:, None] * inv_freq], axis=-1)
    return jnp.cos(ang), jnp.sin(ang)


def apply_rope(x, cos, sin):
    cos = cos.astype(x.dtype)[None, :, None, :]
    sin = sin.astype(x.dtype)[None, :, None, :]
    x1, x2 = jnp.split(x, 2, axis=-1)
    return jnp.concatenate([x1 * cos - x2 * sin, x2 * cos + x1 * sin], axis=-1)


def gqa_attend(qblk, k_all, v_all):
    s = jnp.einsum('bqkgd,bskd->bkgqs', qblk, k_all).astype(jnp.float32) * ATTN_SCALE
    p = jax.nn.softmax(s, axis=-1).astype(v_all.dtype)
    return jnp.einsum('bkgqs,bskd->bqkgd', p, v_all)


def latent_attention(q, k_all, v_all):
    b, n = q.shape[0], q.shape[1]
    nblk = n // Q_BLOCK
    qb = q.reshape(b, nblk, Q_BLOCK, N_KV_HEADS, GROUP, HEAD_DIM).transpose(1, 0, 2, 3, 4, 5)
    o = lax.map(lambda qblk: gqa_attend(qblk, k_all, v_all), qb)
    return o.transpose(1, 0, 2, 3, 4, 5).reshape(b, n, ATTN_WIDTH)


def conv_module(u, gate, conv_w, conv_b, ln_g, ln_b, w_conv_out):
    a, g = jnp.split(u, 2, axis=-1)
    y = a * jax.nn.sigmoid(g)
    pad = CONV_KERNEL // 2
    y = lax.conv_general_dilated(
        y, conv_w[:, None, :], window_strides=(1,), padding=[(pad, pad)],
        dimension_numbers=('NWC', 'WIO', 'NWC'), feature_group_count=CONV_WIDTH) + conv_b
    y = jax.nn.silu(layer_norm(y, ln_g, ln_b))
    y = y * jax.nn.silu(gate)
    return y @ w_conv_out


def split_proj(p):
    return jnp.split(p, IN_OFFSETS, axis=-1)


def merge_branches(y_conv, y_attn, gm, w_out):
    ga, gb = jnp.split(gm, 2, axis=-1)
    return (jax.nn.sigmoid(ga) * y_conv + jax.nn.sigmoid(gb) * y_attn) @ w_out


def _fwd_setup_inputs(seed: int = 0) -> dict:
    key = jax.random.key(seed)
    ks = jax.random.split(key, 20)
    f32 = jnp.float32

    def nrm(k, shape, scale):
        return jax.random.normal(k, shape, f32) * scale

    L, D = DEPTH, D_MODEL
    return {
        "x": nrm(ks[0], (BATCH, SEQ, D), 1.0),
        "c": nrm(ks[1], (BATCH, D), 1.0),
        "ctx": nrm(ks[2], (BATCH, CTX_LEN, D), 1.0),
        "c_ctx": nrm(ks[3], (D,), 1.0),
        "w_mod": nrm(ks[4], (L, D, 3 * D), 0.5 * D ** -0.5),
        "b_mod": nrm(ks[5], (L, 3 * D), 0.02),
        "g_pre": 1.0 + nrm(ks[6], (L, D), 0.02),
        "g_post": 1.0 + nrm(ks[7], (L, D), 0.02),
        "w_in": nrm(ks[8], (L, D, IN_WIDTH), D ** -0.5),
        "conv_w": nrm(ks[9], (L, CONV_KERNEL, CONV_WIDTH), CONV_KERNEL ** -0.5),
        "conv_b": nrm(ks[10], (L, CONV_WIDTH), 0.02),
        "ln_g": 1.0 + nrm(ks[11], (L, CONV_WIDTH), 0.02),
        "ln_b": nrm(ks[12], (L, CONV_WIDTH), 0.02),
        "w_conv_out": nrm(ks[13], (L, CONV_WIDTH, D), CONV_WIDTH ** -0.5),
        "q_norm_g": 1.0 + nrm(ks[14], (L, HEAD_DIM), 0.02),
        "k_norm_g": 1.0 + nrm(ks[15], (L, HEAD_DIM), 0.02),
        "w_attn_out": nrm(ks[16], (L, ATTN_WIDTH, D), ATTN_WIDTH ** -0.5),
        "w_out": nrm(ks[17], (L, D, D), D ** -0.5),
    }


def _fwd_reference(x, c, ctx, c_ctx, w_mod, b_mod, g_pre, g_post, w_in, conv_w, conv_b,
              ln_g, ln_b, w_conv_out, q_norm_g, k_norm_g, w_attn_out, w_out):
    b, n, _ = x.shape
    cos, sin = axial_rope_tables(n)

    for l in range(DEPTH):
        last = l == DEPTH - 1
        sh, sc, gt = jnp.split(jax.nn.silu(c) @ w_mod[l] + b_mod[l], 3, axis=-1)
        shc, scc, gtc = jnp.split(jax.nn.silu(c_ctx) @ w_mod[l] + b_mod[l], 3, axis=-1)

        h = rms_norm(x, g_pre[l]) * (1.0 + sc[:, None, :]) + sh[:, None, :]
        hc = rms_norm(ctx, g_pre[l]) * (1.0 + scc) + shc

        ua, gate_a, q, k, v, gate_b, gm = split_proj(h @ w_in[l])
        ua_c, gate_a_c, q_c, k_c, v_c, gate_b_c, gm_c = split_proj(hc @ w_in[l])

        q = apply_rope(rms_norm(q.reshape(b, n, N_HEADS, HEAD_DIM), q_norm_g[l]), cos, sin)
        k = apply_rope(rms_norm(k.reshape(b, n, N_KV_HEADS, HEAD_DIM), k_norm_g[l]), cos, sin)
        v = v.reshape(b, n, N_KV_HEADS, HEAD_DIM)
        k_c = rms_norm(k_c.reshape(b, CTX_LEN, N_KV_HEADS, HEAD_DIM), k_norm_g[l])
        v_c = v_c.reshape(b, CTX_LEN, N_KV_HEADS, HEAD_DIM)
        k_all = jnp.concatenate([k, k_c], axis=1)
        v_all = jnp.concatenate([v, v_c], axis=1)
        o = latent_attention(q, k_all, v_all)
        y_attn = (o * jax.nn.silu(gate_b)) @ w_attn_out[l]

        y_conv = conv_module(ua, gate_a, conv_w[l], conv_b[l], ln_g[l], ln_b[l], w_conv_out[l])

        out = merge_branches(y_conv, y_attn, gm, w_out[l])
        x_new = x + gt[:, None, :] * rms_norm(out, g_post[l])

        if not last:
            q_c = rms_norm(q_c.reshape(b, CTX_LEN, N_KV_HEADS, GROUP, HEAD_DIM), q_norm_g[l])
            o_c = gqa_attend(q_c, k_c, v_c).reshape(b, CTX_LEN, ATTN_WIDTH)
            y_attn_c = (o_c * jax.nn.silu(gate_b_c)) @ w_attn_out[l]
            y_conv_c = conv_module(ua_c, gate_a_c, conv_w[l], conv_b[l], ln_g[l], ln_b[l], w_conv_out[l])
            out_c = merge_branches(y_conv_c, y_attn_c, gm_c, w_out[l])
            ctx = ctx + gtc * rms_norm(out_c, g_post[l])
        x = x_new

    return x


import jax as _jax
import jax.numpy as _jnp

TWIN_FORMAT = 'train_step'
FWD_PARAMS = ['x', 'c', 'ctx', 'c_ctx', 'w_mod', 'b_mod', 'g_pre', 'g_post', 'w_in', 'conv_w', 'conv_b', 'ln_g', 'ln_b', 'w_conv_out', 'q_norm_g', 'k_norm_g', 'w_attn_out', 'w_out']
TWIN_WEIGHTS = ['c_ctx', 'w_mod', 'b_mod', 'g_pre', 'g_post', 'w_in', 'conv_w', 'conv_b', 'ln_g', 'ln_b', 'w_conv_out', 'q_norm_g', 'k_norm_g', 'w_attn_out', 'w_out']
TWIN_DIFF_INPUT = 'x'
TWIN_INPUTS = ['x', 'c', 'ctx', 'c_ctx', 'w_mod', 'b_mod', 'g_pre', 'g_post', 'w_in', 'conv_w', 'conv_b', 'ln_g', 'ln_b', 'w_conv_out', 'q_norm_g', 'k_norm_g', 'w_attn_out', 'w_out', 'loss_target', 'm_c_ctx', 'm_w_mod', 'm_b_mod', 'm_g_pre', 'm_g_post', 'm_w_in', 'm_conv_w', 'm_conv_b', 'm_ln_g', 'm_ln_b', 'm_w_conv_out', 'm_q_norm_g', 'm_k_norm_g', 'm_w_attn_out', 'm_w_out', 'v_c_ctx', 'v_w_mod', 'v_b_mod', 'v_g_pre', 'v_g_post', 'v_w_in', 'v_conv_w', 'v_conv_b', 'v_ln_g', 'v_ln_b', 'v_w_conv_out', 'v_q_norm_g', 'v_k_norm_g', 'v_w_attn_out', 'v_w_out']
TWIN_OUTPUTS = ['loss', 'grad_x', 'grad_c_ctx', 'grad_w_mod', 'grad_b_mod', 'grad_g_pre', 'grad_g_post', 'grad_w_in', 'grad_conv_w', 'grad_conv_b', 'grad_ln_g', 'grad_ln_b', 'grad_w_conv_out', 'grad_q_norm_g', 'grad_k_norm_g', 'grad_w_attn_out', 'grad_w_out', 'delta_c_ctx', 'delta_w_mod', 'delta_b_mod', 'delta_g_pre', 'delta_g_post', 'delta_w_in', 'delta_conv_w', 'delta_conv_b', 'delta_ln_g', 'delta_ln_b', 'delta_w_conv_out', 'delta_q_norm_g', 'delta_k_norm_g', 'delta_w_attn_out', 'delta_w_out', 'new_m_c_ctx', 'new_m_w_mod', 'new_m_b_mod', 'new_m_g_pre', 'new_m_g_post', 'new_m_w_in', 'new_m_conv_w', 'new_m_conv_b', 'new_m_ln_g', 'new_m_ln_b', 'new_m_w_conv_out', 'new_m_q_norm_g', 'new_m_k_norm_g', 'new_m_w_attn_out', 'new_m_w_out', 'new_v_c_ctx', 'new_v_w_mod', 'new_v_b_mod', 'new_v_g_pre', 'new_v_g_post', 'new_v_w_in', 'new_v_conv_w', 'new_v_conv_b', 'new_v_ln_g', 'new_v_ln_b', 'new_v_w_conv_out', 'new_v_q_norm_g', 'new_v_k_norm_g', 'new_v_w_attn_out', 'new_v_w_out']
TWIN_LEAF_KINDS = {'loss': 'loss', 'grad_x': 'grad_x', 'grad_c_ctx': 'grad_w', 'grad_w_mod': 'grad_w', 'grad_b_mod': 'grad_w', 'grad_g_pre': 'grad_w', 'grad_g_post': 'grad_w', 'grad_w_in': 'grad_w', 'grad_conv_w': 'grad_w', 'grad_conv_b': 'grad_w', 'grad_ln_g': 'grad_w', 'grad_ln_b': 'grad_w', 'grad_w_conv_out': 'grad_w', 'grad_q_norm_g': 'grad_w', 'grad_k_norm_g': 'grad_w', 'grad_w_attn_out': 'grad_w', 'grad_w_out': 'grad_w', 'delta_c_ctx': 'delta_w', 'delta_w_mod': 'delta_w', 'delta_b_mod': 'delta_w', 'delta_g_pre': 'delta_w', 'delta_g_post': 'delta_w', 'delta_w_in': 'delta_w', 'delta_conv_w': 'delta_w', 'delta_conv_b': 'delta_w', 'delta_ln_g': 'delta_w', 'delta_ln_b': 'delta_w', 'delta_w_conv_out': 'delta_w', 'delta_q_norm_g': 'delta_w', 'delta_k_norm_g': 'delta_w', 'delta_w_attn_out': 'delta_w', 'delta_w_out': 'delta_w', 'new_m_c_ctx': 'new_m', 'new_m_w_mod': 'new_m', 'new_m_b_mod': 'new_m', 'new_m_g_pre': 'new_m', 'new_m_g_post': 'new_m', 'new_m_w_in': 'new_m', 'new_m_conv_w': 'new_m', 'new_m_conv_b': 'new_m', 'new_m_ln_g': 'new_m', 'new_m_ln_b': 'new_m', 'new_m_w_conv_out': 'new_m', 'new_m_q_norm_g': 'new_m', 'new_m_k_norm_g': 'new_m', 'new_m_w_attn_out': 'new_m', 'new_m_w_out': 'new_m', 'new_v_c_ctx': 'new_v', 'new_v_w_mod': 'new_v', 'new_v_b_mod': 'new_v', 'new_v_g_pre': 'new_v', 'new_v_g_post': 'new_v', 'new_v_w_in': 'new_v', 'new_v_conv_w': 'new_v', 'new_v_conv_b': 'new_v', 'new_v_ln_g': 'new_v', 'new_v_ln_b': 'new_v', 'new_v_w_conv_out': 'new_v', 'new_v_q_norm_g': 'new_v', 'new_v_k_norm_g': 'new_v', 'new_v_w_attn_out': 'new_v', 'new_v_w_out': 'new_v'}


def _forward(args):
    return _fwd_reference(*[args[k] for k in FWD_PARAMS])


def _output_shape():
    def fwd():
        inp = _fwd_setup_inputs(0)
        return _fwd_reference(*[inp[k] for k in FWD_PARAMS])
    out = _jax.eval_shape(fwd)
    return out.shape, out.dtype

N_MICROBATCH = 1
ADAM_LR = 0.001
ADAM_B1 = 0.9
ADAM_B2 = 0.999
ADAM_EPS = 1e-08
ADAM_WD = 0.01
ADAM_STEP = 10
PER_EXAMPLE_BATCH_AXIS = {'x': 0, 'c': 0, 'ctx': 0, 'loss_target': 0}
SHARED_INPUTS = []
_WEIGHT_DTYPES = {'c_ctx': _jnp.float32, 'w_mod': _jnp.float32, 'b_mod': _jnp.float32, 'g_pre': _jnp.float32, 'g_post': _jnp.float32, 'w_in': _jnp.float32, 'conv_w': _jnp.float32, 'conv_b': _jnp.float32, 'ln_g': _jnp.float32, 'ln_b': _jnp.float32, 'w_conv_out': _jnp.float32, 'q_norm_g': _jnp.float32, 'k_norm_g': _jnp.float32, 'w_attn_out': _jnp.float32, 'w_out': _jnp.float32}
MOMENT_SCALE = {'c_ctx': 5.077065e-02, 'w_mod': 2.438904e+00, 'b_mod': 5.270998e+00, 'g_pre': 1.939815e-01, 'g_post': 6.658040e+00, 'w_in': 8.295197e-02, 'conv_w': 1.198664e-01, 'conv_b': 3.540777e-01, 'ln_g': 1.860229e-01, 'ln_b': 2.140793e-01, 'w_conv_out': 1.406626e-01, 'q_norm_g': 5.921712e-02, 'k_norm_g': 5.784179e-02, 'w_attn_out': 1.259002e-01, 'w_out': 1.909654e-01}


def _to_microbatches(a, axis):
    t = _jnp.moveaxis(a, axis, 0)
    t = t.reshape((N_MICROBATCH, t.shape[0] // N_MICROBATCH) + t.shape[1:])
    return _jnp.moveaxis(t, 1, axis + 1)


def setup_inputs(seed: int = 0) -> dict:
    inp = _fwd_setup_inputs(seed)
    key = _jax.random.fold_in(_jax.random.key(seed), 7919)
    shape, _ = _output_shape()
    out = dict(inp)
    out["loss_target"] = _jax.random.normal(_jax.random.fold_in(key, 0), shape, _jnp.float32)
    for i, name in enumerate(TWIN_WEIGHTS):
        w = inp[name].astype(_jnp.float32)
        if MOMENT_SCALE is None:
            s = _jnp.sqrt(_jnp.mean(_jnp.square(w)) + 1e-30)
        else:
            s = MOMENT_SCALE[name]
        km, kv = _jax.random.split(_jax.random.fold_in(key, i + 1))
        out[name] = w
        out["m_" + name] = s * _jax.random.normal(km, w.shape, _jnp.float32)
        out["v_" + name] = (s * s) * _jax.random.uniform(kv, w.shape, _jnp.float32, 0.5, 1.5)
    if N_MICROBATCH > 1:
        for name, axis in PER_EXAMPLE_BATCH_AXIS.items():
            out[name] = _to_microbatches(out[name], axis)
    return {'x': out['x'], 'c': out['c'], 'ctx': out['ctx'], 'c_ctx': out['c_ctx'], 'w_mod': out['w_mod'], 'b_mod': out['b_mod'], 'g_pre': out['g_pre'], 'g_post': out['g_post'], 'w_in': out['w_in'], 'conv_w': out['conv_w'], 'conv_b': out['conv_b'], 'ln_g': out['ln_g'], 'ln_b': out['ln_b'], 'w_conv_out': out['w_conv_out'], 'q_norm_g': out['q_norm_g'], 'k_norm_g': out['k_norm_g'], 'w_attn_out': out['w_attn_out'], 'w_out': out['w_out'], 'loss_target': out['loss_target'], 'm_c_ctx': out['m_c_ctx'], 'm_w_mod': out['m_w_mod'], 'm_b_mod': out['m_b_mod'], 'm_g_pre': out['m_g_pre'], 'm_g_post': out['m_g_post'], 'm_w_in': out['m_w_in'], 'm_conv_w': out['m_conv_w'], 'm_conv_b': out['m_conv_b'], 'm_ln_g': out['m_ln_g'], 'm_ln_b': out['m_ln_b'], 'm_w_conv_out': out['m_w_conv_out'], 'm_q_norm_g': out['m_q_norm_g'], 'm_k_norm_g': out['m_k_norm_g'], 'm_w_attn_out': out['m_w_attn_out'], 'm_w_out': out['m_w_out'], 'v_c_ctx': out['v_c_ctx'], 'v_w_mod': out['v_w_mod'], 'v_b_mod': out['v_b_mod'], 'v_g_pre': out['v_g_pre'], 'v_g_post': out['v_g_post'], 'v_w_in': out['v_w_in'], 'v_conv_w': out['v_conv_w'], 'v_conv_b': out['v_conv_b'], 'v_ln_g': out['v_ln_g'], 'v_ln_b': out['v_ln_b'], 'v_w_conv_out': out['v_w_conv_out'], 'v_q_norm_g': out['v_q_norm_g'], 'v_k_norm_g': out['v_k_norm_g'], 'v_w_attn_out': out['v_w_attn_out'], 'v_w_out': out['v_w_out']}


def _loss(weights, diff, rest, loss_target):
    with _jax.named_scope("forward"):
        args = {**rest, TWIN_DIFF_INPUT: diff, **{k: w.astype(_WEIGHT_DTYPES[k]) for k, w in weights.items()}}
        y = _forward(args)
    with _jax.named_scope("loss_head"):
        err = _jnp.square(y.astype(_jnp.float32) - loss_target)
        return 0.5 * _jnp.sum(_jnp.mean(err, axis=-1)) if err.ndim else 0.5 * err


def _adamw(w, g, m, v):
    m = ADAM_B1 * m + (1.0 - ADAM_B1) * g
    v = ADAM_B2 * v + (1.0 - ADAM_B2) * _jnp.square(g)
    m_hat = m / (1.0 - ADAM_B1 ** ADAM_STEP)
    v_hat = v / (1.0 - ADAM_B2 ** ADAM_STEP)
    delta = -ADAM_LR * (m_hat / (_jnp.sqrt(v_hat) + ADAM_EPS) + ADAM_WD * w)
    return delta, m, v


def reference(x, c, ctx, c_ctx, w_mod, b_mod, g_pre, g_post, w_in, conv_w, conv_b, ln_g, ln_b, w_conv_out, q_norm_g, k_norm_g, w_attn_out, w_out, loss_target, m_c_ctx, m_w_mod, m_b_mod, m_g_pre, m_g_post, m_w_in, m_conv_w, m_conv_b, m_ln_g, m_ln_b, m_w_conv_out, m_q_norm_g, m_k_norm_g, m_w_attn_out, m_w_out, v_c_ctx, v_w_mod, v_b_mod, v_g_pre, v_g_post, v_w_in, v_conv_w, v_conv_b, v_ln_g, v_ln_b, v_w_conv_out, v_q_norm_g, v_k_norm_g, v_w_attn_out, v_w_out):
    given = dict(x=x, c=c, ctx=ctx, c_ctx=c_ctx, w_mod=w_mod, b_mod=b_mod, g_pre=g_pre, g_post=g_post, w_in=w_in, conv_w=conv_w, conv_b=conv_b, ln_g=ln_g, ln_b=ln_b, w_conv_out=w_conv_out, q_norm_g=q_norm_g, k_norm_g=k_norm_g, w_attn_out=w_attn_out, w_out=w_out, loss_target=loss_target, m_c_ctx=m_c_ctx, m_w_mod=m_w_mod, m_b_mod=m_b_mod, m_g_pre=m_g_pre, m_g_post=m_g_post, m_w_in=m_w_in, m_conv_w=m_conv_w, m_conv_b=m_conv_b, m_ln_g=m_ln_g, m_ln_b=m_ln_b, m_w_conv_out=m_w_conv_out, m_q_norm_g=m_q_norm_g, m_k_norm_g=m_k_norm_g, m_w_attn_out=m_w_attn_out, m_w_out=m_w_out, v_c_ctx=v_c_ctx, v_w_mod=v_w_mod, v_b_mod=v_b_mod, v_g_pre=v_g_pre, v_g_post=v_g_post, v_w_in=v_w_in, v_conv_w=v_conv_w, v_conv_b=v_conv_b, v_ln_g=v_ln_g, v_ln_b=v_ln_b, v_w_conv_out=v_w_conv_out, v_q_norm_g=v_q_norm_g, v_k_norm_g=v_k_norm_g, v_w_attn_out=v_w_attn_out, v_w_out=v_w_out)
    weights = {n: given[n] for n in TWIN_WEIGHTS}
    shared = {n: given[n] for n in SHARED_INPUTS}
    per_example = {n: given[n] for n in ['x', 'c', 'ctx']}
    grad_fn = _jax.value_and_grad(_loss, argnums=(0, 1))

    def one_microbatch(ex, loss_target):
        ex = dict(ex)
        diff = ex.pop(TWIN_DIFF_INPUT)
        return grad_fn(weights, diff, {**shared, **ex}, loss_target)

    if N_MICROBATCH == 1:
        loss, (grad_w, grad_x) = one_microbatch(per_example, given["loss_target"])
    else:
        def body(carry, xs):
            loss_sum, grad_sum = carry
            l_k, (gw_k, gx_k) = one_microbatch(xs[0], xs[1])
            with _jax.named_scope("update"):
                return (loss_sum + l_k, _jax.tree.map(_jnp.add, grad_sum, gw_k)), gx_k

        init = (_jnp.zeros((), _jnp.float32), _jax.tree.map(_jnp.zeros_like, weights))
        (loss, grad_w), grad_x = _jax.lax.scan(body, init, (per_example, given["loss_target"]))
    with _jax.named_scope("update"):
        delta_w, new_m, new_v = {}, {}, {}
        for n in TWIN_WEIGHTS:
            delta_w[n], new_m[n], new_v[n] = _adamw(weights[n], grad_w[n], given["m_" + n], given["v_" + n])
    return (loss, grad_x, *[grad_w[n] for n in TWIN_WEIGHTS], *[delta_w[n] for n in TWIN_WEIGHTS],
            *[new_m[n] for n in TWIN_WEIGHTS], *[new_v[n] for n in TWIN_WEIGHTS])
```

```python
import math

import jax
import jax.numpy as jnp
from jax import lax
from jax.experimental import pallas as pl
from jax.experimental.pallas import tpu as pltpu

F32 = jnp.float32
BF16 = jnp.bfloat16

HEAD_DIM = 64
GROUP = 4
GRID_W = 64
ROPE_THETA = 10000.0
EPS = 1e-6
ATTN_SCALE = HEAD_DIM ** -0.5
HALO = 16

ADAM_LR = 0.001
ADAM_B1 = 0.9
ADAM_B2 = 0.999
ADAM_EPS = 1e-08
ADAM_WD = 0.01
ADAM_STEP = 10

N_DEV = 8
MESH_AXES = ("x", "y", "c")
V7X_VMEM_LIMIT = 56 * 1024 * 1024
NEG_BIG = -1e30

MESH = pl.DeviceIdType.MESH
ANY = pl.BlockSpec(memory_space=pl.ANY)


def _pcall(body, **kw):
    return pl.pallas_call(body, **kw)


def _cp(*sem):
    return pltpu.CompilerParams(dimension_semantics=sem, vmem_limit_bytes=V7X_VMEM_LIMIT)


def _sig(x):
    return 1.0 / (1.0 + jnp.exp(-x))


def _mean(x):
    return jnp.mean(x, axis=-1, keepdims=True)


def _colsum(x):
    return jnp.sum(x, axis=0, keepdims=True)


def _bf_round(x):
    return x.astype(BF16).astype(F32)


def _dot(a, b):
    return jnp.dot(a, b, preferred_element_type=F32)


def _dot_nt(a, b):
    return lax.dot_general(a, b, (((1,), (1,)), ((), ())), preferred_element_type=F32)


def _dot_tn(a, b):
    return lax.dot_general(a, b, (((0,), (0,)), ((), ())), preferred_element_type=F32)


def _split_dot(x, m):
    hi = x.astype(BF16)
    lo = (x - hi.astype(F32)).astype(BF16)
    return _dot(hi, m) + _dot(lo, m)


def _full(shape):
    nd = len(shape)
    return pl.BlockSpec(shape, lambda *_: (0,) * nd)


def _rows(tm, width, colblk=0):
    return pl.BlockSpec((tm, width), lambda i: (i, colblk))


def _my_place():
    return lax.axis_index("x"), lax.axis_index("y"), lax.axis_index("c")


def _all_gather(shard, name):
    rows, cols = shard.shape

    def body(x_ref, out_ref, send_sems, recv_sems, local_sem):
        x, y, c = _my_place()
        me, sibling = (x, y, c), (x, y, 1 - c)
        chips = [(1 - x, y), (x, 1 - y), (1 - x, 1 - y)]

        def slab(px, py, pc):
            return out_ref.at[4 * px + 2 * py + pc]

        def copy(k, block, to, src=None):
            return pltpu.make_async_remote_copy(
                src_ref=slab(*block) if src is None else src, dst_ref=slab(*block),
                send_sem=send_sems.at[k], recv_sem=recv_sems.at[k],
                device_id=to, device_id_type=MESH)

        mine = pltpu.make_async_copy(x_ref, slab(*me), local_sem)
        mine.start()
        first = [copy(0, me, sibling, src=x_ref)]
        first += [copy(1 + j, me, (*chip, c), src=x_ref) for j, chip in enumerate(chips)]
        for cp in first:
            cp.start()
        passed = [copy(4 + j, (*chip, c), sibling) for j, chip in enumerate(chips)]
        for j, chip in enumerate(chips):
            copy(1 + j, (*chip, c), me).wait_recv()
            passed[j].start()
        copy(0, sibling, me).wait_recv()
        for j, chip in enumerate(chips):
            copy(4 + j, (*chip, 1 - c), me).wait_recv()
        for cp in first + passed:
            cp.wait_send()
        mine.wait()

    return _pcall(
        body, name=name,
        out_shape=jax.ShapeDtypeStruct((N_DEV, rows, cols), shard.dtype),
        in_specs=[ANY], out_specs=ANY,
        scratch_shapes=[pltpu.SemaphoreType.DMA((7,)), pltpu.SemaphoreType.DMA((7,)),
                        pltpu.SemaphoreType.DMA],
    )(shard)


def _swap_with_sibling(buf, name):
    _, _, rows, cols = buf.shape

    def body(buf_ref, recv_ref, send_sems, recv_sems):
        x, y, c = _my_place()
        copies = [
            pltpu.make_async_remote_copy(
                src_ref=buf_ref.at[k, 1 - c], dst_ref=recv_ref.at[k],
                send_sem=send_sems.at[k], recv_sem=recv_sems.at[k],
                device_id=(x, y, 1 - c), device_id_type=MESH)
            for k in range(4)]
        for cp in copies:
            cp.start()
        for cp in copies:
            cp.wait()

    return _pcall(
        body, name=name,
        out_shape=jax.ShapeDtypeStruct((4, rows, cols), buf.dtype),
        in_specs=[ANY], out_specs=ANY,
        scratch_shapes=[pltpu.SemaphoreType.DMA((4,)), pltpu.SemaphoreType.DMA((4,))],
    )(buf)


def _exchange_chips(part, name):
    _, rows, cols = part.shape

    def body(s_ref, recv_ref, send_sems, recv_sems, local_sem):
        x, y, c = _my_place()
        mychip = 2 * x + y
        chips = [(1 - x, y), (x, 1 - y), (1 - x, 1 - y)]
        mine = pltpu.make_async_copy(s_ref.at[mychip], recv_ref.at[mychip], local_sem)
        mine.start()
        copies = [
            pltpu.make_async_remote_copy(
                src_ref=s_ref.at[2 * px + py], dst_ref=recv_ref.at[mychip],
                send_sem=send_sems.at[j], recv_sem=recv_sems.at[j],
                device_id=(px, py, c), device_id_type=MESH)
            for j, (px, py) in enumerate(chips)]
        for cp in copies:
            cp.start()
        for cp in copies:
            cp.wait()
        mine.wait()

    return _pcall(
        body, name=name,
        out_shape=jax.ShapeDtypeStruct((4, rows, cols), part.dtype),
        in_specs=[ANY], out_specs=ANY,
        scratch_shapes=[pltpu.SemaphoreType.DMA((3,)), pltpu.SemaphoreType.DMA((3,)),
                        pltpu.SemaphoreType.DMA],
    )(part)


def _add_sibling_part(buf, recv, name):
    _, _, rows, cols = buf.shape
    tr = 512 if rows % 512 == 0 else rows
    core = lax.axis_index("c").astype(jnp.int32).reshape(1)

    def body(core_ref, a_ref, b_ref, o_ref):
        o_ref[...] = a_ref[...] + b_ref[...]

    grid_spec = pltpu.PrefetchScalarGridSpec(
        num_scalar_prefetch=1, grid=(4, rows // tr),
        in_specs=[pl.BlockSpec((None, None, tr, cols), lambda k, r, cr: (k, cr[0], r, 0)),
                  pl.BlockSpec((None, tr, cols), lambda k, r, cr: (k, r, 0))],
        out_specs=pl.BlockSpec((None, tr, cols), lambda k, r, cr: (k, r, 0)))
    return _pcall(body, name=name, grid_spec=grid_spec,
                  out_shape=jax.ShapeDtypeStruct((4, rows, cols), F32),
                  compiler_params=_cp("parallel", "parallel"))(core, buf, recv)


def _sum_adamw(stack, w, m, v, name):
    ns, rows, cols = stack.shape
    tr = 256 if rows % 256 == 0 else rows
    c1 = 1.0 - ADAM_B1 ** ADAM_STEP
    c2 = 1.0 - ADAM_B2 ** ADAM_STEP

    def body(s_ref, w_ref, m_ref, v_ref, g_out, d_out, m_out, v_out):
        g = s_ref[0]
        for k in range(1, ns):
            g = g + s_ref[k]
        m_new = ADAM_B1 * m_ref[...] + (1.0 - ADAM_B1) * g
        v_new = ADAM_B2 * v_ref[...] + (1.0 - ADAM_B2) * (g * g)
        m_hat = m_new / c1
        v_hat = v_new / c2
        g_out[...] = g
        d_out[...] = -ADAM_LR * (m_hat / (jnp.sqrt(v_hat) + ADAM_EPS) + ADAM_WD * w_ref[...])
        m_out[...] = m_new
        v_out[...] = v_new

    blk = pl.BlockSpec((tr, cols), lambda i: (i, 0))
    return _pcall(
        body, name=name, grid=(rows // tr,),
        in_specs=[pl.BlockSpec((ns, tr, cols), lambda i: (0, i, 0)), blk, blk, blk],
        out_specs=[blk] * 4,
        out_shape=[jax.ShapeDtypeStruct((rows, cols), F32)] * 4,
        compiler_params=_cp("parallel"))(stack, w, m, v)


def _mod_fwd(cvec, wmod, bmod, name):
    _, d = cvec.shape

    def body(c_ref, w_ref, b_ref, o_ref):
        cv = c_ref[...]
        cs = cv * _sig(cv)
        o_ref[...] = _dot(cs.astype(BF16), w_ref[...]) + b_ref[...]

    return _pcall(
        body, name=name, grid=(3,),
        in_specs=[_full((8, d)), pl.BlockSpec((d, d), lambda n: (0, n)), pl.BlockSpec((1, d), lambda n: (0, n))],
        out_specs=pl.BlockSpec((8, d), lambda n: (0, n)),
        out_shape=jax.ShapeDtypeStruct((8, 3 * d), F32),
        compiler_params=_cp("parallel"))(cvec, wmod, bmod)


def _mod_bwd(dmod, cvec, cvec_t, wmod, name):
    _, d = cvec.shape

    def body(dm_ref, c_ref, ct_ref, w_ref, dw_ref, db_ref, dc_ref):
        n = pl.program_id(0)
        dm = dm_ref[...]
        ct = ct_ref[...]
        cs_t = _bf_round(ct * _sig(ct))
        d0 = _bf_round(dm[0:1, :])
        d1 = _bf_round(dm[1:2, :])
        dw_ref[...] = cs_t[:, 0:1] * d0 + cs_t[:, 1:2] * d1
        db_ref[...] = dm[0:1, :] + dm[1:2, :]

        @pl.when(n == 0)
        def _():
            dc_ref[...] = jnp.zeros_like(dc_ref)

        dc_ref[...] += _dot_nt(dm.astype(BF16), w_ref[...])

        @pl.when(n == 2)
        def _():
            cv = c_ref[...]
            s = _sig(cv)
            dc_ref[...] = dc_ref[...] * (s * (1.0 + cv * (1.0 - s)))

    return _pcall(
        body, name=name, grid=(3,),
        in_specs=[pl.BlockSpec((8, d), lambda n: (0, n)), _full((8, d)), _full((d, 128)),
                  pl.BlockSpec((d, d), lambda n: (0, n))],
        out_specs=[pl.BlockSpec((d, d), lambda n: (0, n)), pl.BlockSpec((1, d), lambda n: (0, n)),
                   _full((8, d))],
        out_shape=[jax.ShapeDtypeStruct((d, 3 * d), F32), jax.ShapeDtypeStruct((1, 3 * d), F32),
                   jax.ShapeDtypeStruct((8, d), F32)],
        compiler_params=_cp("arbitrary"))(dmod, cvec, cvec_t, wmod)


def _seg_rows(mod_ref, lo, hi, is_ctx):
    return jnp.where(is_ctx, mod_ref[1:2, lo:hi], mod_ref[0:1, lo:hi])


def _inproj(xa, modv, gpre, wp, n_lat, tm, name):
    na, d = xa.shape
    wcols = wp.shape[1]
    tn = d // 2

    def body(x_ref, mod_ref, g_ref, w_ref, p_ref, h_ref, h_s):
        i = pl.program_id(0)

        @pl.when(pl.program_id(1) == 0)
        def _():
            x = x_ref[...]
            r = lax.rsqrt(_mean(x * x) + EPS)
            row = i * tm + lax.broadcasted_iota(jnp.int32, (tm, 1), 0)
            is_ctx = row >= n_lat
            sh = _seg_rows(mod_ref, 0, d, is_ctx)
            sc = _seg_rows(mod_ref, d, 2 * d, is_ctx)
            hb = ((x * r * g_ref[...]) * (1.0 + sc) + sh).astype(BF16)
            h_s[...] = hb
            h_ref[...] = hb

        p_ref[...] = _dot(h_s[...], w_ref[...])

    return _pcall(
        body, name=name, grid=(na // tm, wcols // tn),
        in_specs=[pl.BlockSpec((tm, d), lambda i, j: (i, 0)), _full((8, 3 * d)), _full((1, d)),
                  pl.BlockSpec((d, tn), lambda i, j: (0, j))],
        out_specs=[pl.BlockSpec((tm, tn), lambda i, j: (i, j)), pl.BlockSpec((tm, d), lambda i, j: (i, 0))],
        out_shape=[jax.ShapeDtypeStruct((na, wcols), F32), jax.ShapeDtypeStruct((na, d), BF16)],
        scratch_shapes=[pltpu.VMEM((tm, d), BF16)],
        compiler_params=_cp("parallel", "arbitrary"))(xa, modv, gpre, wp)


def _lane_tile(t, width):
    if width >= 128:
        return jnp.tile(t, (1, width // 128))
    return t[:, :width]


def _partner(x):
    w = x.shape[-1]
    lane = lax.broadcasted_iota(jnp.int32, x.shape, 1)
    low = (lane % HEAD_DIM) < (HEAD_DIM // 2)
    return jnp.where(low, pltpu.roll(x, w - HEAD_DIM // 2, 1), pltpu.roll(x, HEAD_DIM // 2, 1))


def _qknorm_fwd(p, cos_t, sin_t, gq, gk, bd, tm, name):
    na = p.shape[0]
    d = gq.shape[1]
    kw = d // GROUP

    def body(q_ref, kv_ref, cos_ref, sin_ref, gq_ref, gk_ref, bd_ref, qo_ref, ko_ref, vo_ref):
        cos = cos_ref[...]
        sin = sin_ref[...]

        def norm_rope(xh, g, w):
            ms = _split_dot(xh * xh, bd_ref[0:w, 0:w]) * (1.0 / HEAD_DIM)
            xn = xh * lax.rsqrt(ms + EPS) * g
            return xn * _lane_tile(cos, w) + _partner(xn) * _lane_tile(sin, w)

        qo_ref[...] = (norm_rope(q_ref[...], gq_ref[...], d) * ATTN_SCALE).astype(BF16)
        kv = kv_ref[...]
        ko_ref[...] = norm_rope(kv[:, 0:kw], gk_ref[...], kw).astype(BF16)
        vo_ref[...] = kv[:, kw:2 * kw].astype(BF16)

    return _pcall(
        body, name=name, grid=(na // tm,),
        in_specs=[_rows(tm, d, 3), _rows(tm, d // 2, 14), _rows(tm, 128), _rows(tm, 128),
                  _full((1, d)), _full((1, kw)), _full((d, d))],
        out_specs=[_rows(tm, d), _rows(tm, kw), _rows(tm, kw)],
        out_shape=[jax.ShapeDtypeStruct((na, d), BF16), jax.ShapeDtypeStruct((na, kw), BF16),
                   jax.ShapeDtypeStruct((na, kw), BF16)],
        compiler_params=_cp("parallel"))(p, p, cos_t, sin_t, gq, gk, bd)


def _attn_fwd(q_hm, k_hm, v_hm, n_lat, tq, tk, name):
    h, na, hd = q_hm.shape
    kv = k_hm.shape[0]
    nq, nk = na // tq, na // tk
    nql, nkl = n_lat // tq, n_lat // tk
    rows = GROUP * tq

    def body(q_ref, k_ref, v_ref, o_ref, lse_ref, m_s, l_s, acc_s):
        i = pl.program_id(1)
        j = pl.program_id(2)

        @pl.when(j == 0)
        def _():
            m_s[...] = jnp.full_like(m_s, NEG_BIG)
            l_s[...] = jnp.zeros_like(l_s)
            acc_s[...] = jnp.zeros_like(acc_s)

        @pl.when(jnp.logical_or(i < nql, j >= nkl))
        def _():
            q = q_ref[...].reshape(rows, hd)
            s = _dot_nt(q, k_ref[...])
            m_prev = m_s[...]
            m_new = jnp.maximum(m_prev, jnp.max(s, axis=-1, keepdims=True))
            alpha = jnp.exp(m_prev - m_new)
            pr = jnp.exp(s - m_new)
            l_s[...] = alpha * l_s[...] + jnp.sum(pr, axis=-1, keepdims=True)
            acc_s[...] = alpha * acc_s[...] + _dot(pr.astype(BF16), v_ref[...])
            m_s[...] = m_new

        @pl.when(j == nk - 1)
        def _():
            l = l_s[...]
            o_ref[...] = (acc_s[...] / l).reshape(GROUP, tq, hd)
            lse_ref[...] = (m_s[...] + jnp.log(l)).reshape(GROUP, tq, 1)

    return _pcall(
        body, name=name, grid=(kv, nq, nk),
        in_specs=[pl.BlockSpec((GROUP, tq, hd), lambda g, i, j: (g, i, 0)),
                  pl.BlockSpec((None, tk, hd), lambda g, i, j: (g, j, 0)),
                  pl.BlockSpec((None, tk, hd), lambda g, i, j: (g, j, 0))],
        out_specs=[pl.BlockSpec((GROUP, tq, hd), lambda g, i, j: (g, i, 0)),
                   pl.BlockSpec((GROUP, tq, 1), lambda g, i, j: (g, i, 0))],
        out_shape=[jax.ShapeDtypeStruct((h, na, hd), F32), jax.ShapeDtypeStruct((h, na, 1), F32)],
        scratch_shapes=[pltpu.VMEM((rows, 1), F32), pltpu.VMEM((rows, 1), F32), pltpu.VMEM((rows, hd), F32)],
        compiler_params=_cp("parallel", "parallel", "arbitrary"))(q_hm, k_hm, v_hm)


def _window(win_ref, prev, cur, nxt, first, last, tm):
    win_ref[0:HALO, :] = jnp.where(first, 0.0, prev)
    win_ref[HALO:HALO + tm, :] = cur
    win_ref[HALO + tm:HALO + tm + HALO, :] = jnp.where(last, 0.0, nxt)


def _halo_specs(tm, d, na, colblk):
    per = tm // HALO
    last_blk = na // HALO - 1
    prev = pl.BlockSpec((HALO, d), lambda i: (jnp.maximum(i * per - 1, 0), colblk))
    nxt = pl.BlockSpec((HALO, d), lambda i: (jnp.minimum((i + 1) * per, last_blk), colblk))
    return prev, nxt


def _seq_ends(i, n_lat, na, tm):
    first = jnp.logical_or(i == 0, i == n_lat // tm)
    last = jnp.logical_or(i == n_lat // tm - 1, i == na // tm - 1)
    return first, last


def _conv_fwd(p, conv_w, conv_b, ln_g, ln_b, n_lat, tm, name):
    na = p.shape[0]
    ktaps, d = conv_w.shape
    pad = ktaps // 2

    def body(a_ref, ap_ref, an_ref, g_ref, gp_ref, gn_ref, ga_ref, w_ref, cb_ref, lg_ref, lb_ref,
             y5_ref, y2_ref, win):
        i = pl.program_id(0)
        first, last = _seq_ends(i, n_lat, na, tm)
        _window(win, ap_ref[...] * _sig(gp_ref[...]), a_ref[...] * _sig(g_ref[...]),
                an_ref[...] * _sig(gn_ref[...]), first, last, tm)
        acc = jnp.zeros((tm, d), F32)
        for k in range(ktaps):
            acc = acc + w_ref[k:k + 1, :] * win[pl.ds(HALO - pad + k, tm), :]
        y2 = acc + cb_ref[...]
        y2_ref[...] = y2
        xc = y2 - _mean(y2)
        y3 = xc * lax.rsqrt(_mean(xc * xc) + EPS) * lg_ref[...] + lb_ref[...]
        gate = ga_ref[...]
        y5_ref[...] = ((y3 * _sig(y3)) * (gate * _sig(gate))).astype(BF16)

    ap, an = _halo_specs(tm, d, na, 0)
    gp, gn = _halo_specs(tm, d, na, 1)
    return _pcall(
        body, name=name, grid=(na // tm,),
        in_specs=[_rows(tm, d, 0), ap, an, _rows(tm, d, 1), gp, gn, _rows(tm, d, 2),
                  _full((ktaps, d)), _full((1, d)), _full((1, d)), _full((1, d))],
        out_specs=[_rows(tm, d), _rows(tm, d)],
        out_shape=[jax.ShapeDtypeStruct((na, d), BF16), jax.ShapeDtypeStruct((na, d), F32)],
        scratch_shapes=[pltpu.VMEM((tm + 2 * HALO, d), F32)],
        compiler_params=_cp("parallel"))(p, p, p, p, p, p, p, conv_w, conv_b, ln_g, ln_b)


def _merge_fwd(y5, o_tm, p, xa, modv, gpost, wc, wa, wo, n_lat, tm, name):
    na, d = xa.shape

    def body(y5_ref, o_ref, gb_ref, ma_ref, mb_ref, x_ref, mod_ref, gp_ref, wc_ref, wa_ref, wo_ref,
             xn_ref, yc_ref, ya_ref, out_ref, z_ref, og_ref):
        is_ctx = pl.program_id(0) >= n_lat // tm
        gate_b = gb_ref[...]
        og = (o_ref[...] * (gate_b * _sig(gate_b))).astype(BF16)
        og_ref[...] = og
        yc = _dot(y5_ref[...], wc_ref[...])
        ya = _dot(og, wa_ref[...])
        yc_ref[...] = yc
        ya_ref[...] = ya
        z = (_sig(ma_ref[...]) * yc + _sig(mb_ref[...]) * ya).astype(BF16)
        z_ref[...] = z
        out = _dot(z, wo_ref[...])
        out_ref[...] = out
        gt = _seg_rows(mod_ref, 2 * d, 3 * d, is_ctx)
        xn_ref[...] = x_ref[...] + gt * (out * lax.rsqrt(_mean(out * out) + EPS) * gp_ref[...])

    f32o = jax.ShapeDtypeStruct((na, d), F32)
    bfo = jax.ShapeDtypeStruct((na, d), BF16)
    return _pcall(
        body, name=name, grid=(na // tm,),
        in_specs=[_rows(tm, d), _rows(tm, d), _rows(tm, d, 4), _rows(tm, d, 5), _rows(tm, d, 6), _rows(tm, d),
                  _full((8, 3 * d)), _full((1, d)), _full((d, d)), _full((d, d)), _full((d, d))],
        out_specs=[_rows(tm, d)] * 6,
        out_shape=[f32o, f32o, f32o, f32o, bfo, bfo],
        compiler_params=_cp("parallel"))(y5, o_tm, p, p, p, xa, modv, gpost, wc, wa, wo)


def _loss_grad(xa, target, n_lat, tm, name):
    na, d = xa.shape
    nlt = n_lat // tm

    def body(x_ref, t_ref, dx_ref, loss_ref):
        i = pl.program_id(0)

        @pl.when(i == 0)
        def _():
            loss_ref[...] = jnp.zeros_like(loss_ref)

        @pl.when(i < nlt)
        def _():
            err = x_ref[...] - t_ref[...]
            dx_ref[...] = err * (1.0 / d)
            loss_ref[...] += 0.5 * jnp.sum(_mean(err * err))

        @pl.when(i >= nlt)
        def _():
            dx_ref[...] = jnp.zeros_like(dx_ref)

    return _pcall(
        body, name=name, grid=(na // tm,),
        in_specs=[_rows(tm, d), pl.BlockSpec((tm, d), lambda i: (jnp.minimum(i, nlt - 1), 0))],
        out_specs=[_rows(tm, d), _full((8, 128))],
        out_shape=[jax.ShapeDtypeStruct((na, d), F32), jax.ShapeDtypeStruct((8, 128), F32)],
        compiler_params=_cp("arbitrary"))(xa, target)


def _merge_bwd(dxn, out, yc, ya, o_tm, p, y2, modv, gpost, ln_g, ln_b, wo_t, wc_t, wa_t, esel, n_lat, tm, name):
    na, d = dxn.shape

    def body(dx_ref, out_ref, yc_ref, ya_ref, o_ref, gb_ref, ma_ref, mb_ref, gta_ref, y2_ref,
             mod_ref, gp_ref, lg_ref, lb_ref, wot_ref, wct_ref, wat_ref, es_ref,
             dgb_ref, dma_ref, dmb_ref, dgta_ref, do_ref, dout_ref, dyc_ref, dya_ref, dy2_ref, dl_ref, acc_ref):
        i = pl.program_id(0)
        ctx_tile = i >= n_lat // tm

        @pl.when(i == 0)
        def _():
            acc_ref[...] = jnp.zeros_like(acc_ref)

        gt = _seg_rows(mod_ref, 2 * d, 3 * d, ctx_tile)
        gp = gp_ref[...]
        dx = dx_ref[...]
        out = out_ref[...]
        r2 = lax.rsqrt(_mean(out * out) + EPS)
        n2 = out * r2
        dgt = _colsum(dx * (n2 * gp))

        @pl.when(ctx_tile)
        def _():
            acc_ref[1:2, :] += dgt

        @pl.when(jnp.logical_not(ctx_tile))
        def _():
            acc_ref[0:1, :] += dgt

        acc_ref[2:3, :] += _colsum(dx * gt * n2)
        dn2 = dx * gt * gp
        dout = (r2 * (dn2 - n2 * _mean(dn2 * n2))).astype(BF16)
        dout_ref[...] = dout
        dz = _dot(dout, wot_ref[...])
        sa = _sig(ma_ref[...])
        sb = _sig(mb_ref[...])
        dyc = (dz * sa).astype(BF16)
        dya = (dz * sb).astype(BF16)
        dyc_ref[...] = dyc
        dya_ref[...] = dya
        dma_ref[...] = (dz * yc_ref[...] * sa * (1.0 - sa)).astype(BF16)
        dmb_ref[...] = (dz * ya_ref[...] * sb * (1.0 - sb)).astype(BF16)
        dy5 = _dot(dyc, wct_ref[...])
        dog = _dot(dya, wat_ref[...])

        gate_b = gb_ref[...]
        sgb = _sig(gate_b)
        o = o_ref[...]
        do = dog * (gate_b * sgb)
        do_ref[...] = do.astype(BF16)
        dgb_ref[...] = (dog * o * (sgb * (1.0 + gate_b * (1.0 - sgb)))).astype(BF16)
        dl_ref[...] = _split_dot(do * o, es_ref[...])

        y2 = y2_ref[...]
        xc = y2 - _mean(y2)
        rstd = lax.rsqrt(_mean(xc * xc) + EPS)
        xhat = xc * rstd
        lg = lg_ref[...]
        y3 = xhat * lg + lb_ref[...]
        s3 = _sig(y3)
        gate_a = gta_ref[...]
        sga = _sig(gate_a)
        dgta_ref[...] = (dy5 * (y3 * s3) * (sga * (1.0 + gate_a * (1.0 - sga)))).astype(BF16)
        dy3 = dy5 * (gate_a * sga) * (s3 * (1.0 + y3 * (1.0 - s3)))
        acc_ref[3:4, :] += _colsum(dy3 * xhat)
        acc_ref[4:5, :] += _colsum(dy3)
        dxh = dy3 * lg
        dy2 = rstd * (dxh - _mean(dxh) - xhat * _mean(dxh * xhat))
        dy2_ref[...] = dy2
        acc_ref[5:6, :] += _colsum(dy2)

    f32o = jax.ShapeDtypeStruct((na, d), F32)
    bfo = jax.ShapeDtypeStruct((na, d), BF16)
    r = _rows(tm, d)
    return _pcall(
        body, name=name, grid=(na // tm,),
        in_specs=[r, r, r, r, r, _rows(tm, d, 4), _rows(tm, d, 5), _rows(tm, d, 6), _rows(tm, d, 2), r,
                  _full((8, 3 * d)), _full((1, d)), _full((1, d)), _full((1, d)),
                  _full((d, d)), _full((d, d)), _full((d, d)), _full((d, 128))],
        out_specs=[r] * 9 + [_rows(tm, 128), _full((8, d))],
        out_shape=[bfo] * 8 + [f32o, jax.ShapeDtypeStruct((na, 128), F32), jax.ShapeDtypeStruct((8, d), F32)],
        compiler_params=_cp("arbitrary"),
    )(dxn, out, yc, ya, o_tm, p, p, p, p, y2, modv, gpost, ln_g, ln_b, wo_t, wc_t, wa_t, esel)


def _attn_bwd(q_hm, k_hm, kt_hm, v_hm, do_hm, lse_r, dl_r, n_lat, tq, tk, name):
    h, na, hd = q_hm.shape
    kv = k_hm.shape[0]
    nq, nk = na // tq, na // tk
    nql, nkl = n_lat // tq, n_lat // tk
    cols = GROUP * tq

    def body(q_ref, k_ref, kt_ref, v_ref, do_ref, lse_ref, dl_ref, dq_hbm, dk_ref, dv_ref,
             dq_acc, dk_acc, dv_acc, sem):
        g = pl.program_id(0)
        j = pl.program_id(1)
        i = pl.program_id(2)

        @pl.when(jnp.logical_and(j == 0, i == 0))
        def _():
            dq_acc[...] = jnp.zeros_like(dq_acc)

        @pl.when(i == 0)
        def _():
            dk_acc[...] = jnp.zeros_like(dk_acc)
            dv_acc[...] = jnp.zeros_like(dv_acc)

        @pl.when(jnp.logical_or(i < nql, j >= nkl))
        def _():
            q = q_ref[...].reshape(cols, hd)
            do = do_ref[...].reshape(cols, hd)
            lse = jnp.concatenate([lse_ref[a] for a in range(GROUP)], axis=1)
            dl = jnp.concatenate([dl_ref[a] for a in range(GROUP)], axis=1)
            p_t = jnp.exp(_dot_nt(k_ref[...], q) - lse)
            ds_t = (p_t * (_dot_nt(v_ref[...], do) - dl)).astype(BF16)
            dv_acc[...] += _dot(p_t.astype(BF16), do)
            dk_acc[...] += _dot(ds_t, q)
            dq_t = _dot(kt_ref[...], ds_t)
            at = pl.multiple_of(i * tq, tq)
            for a in range(GROUP):
                dq_acc[a, :, pl.ds(at, tq)] += dq_t[:, a * tq:(a + 1) * tq]

        @pl.when(i == nq - 1)
        def _():
            dk_ref[...] = dk_acc[...]
            dv_ref[...] = dv_acc[...]

        @pl.when(jnp.logical_and(j == nk - 1, i == nq - 1))
        def _():
            cp = pltpu.make_async_copy(dq_acc, dq_hbm.at[pl.ds(g * GROUP, GROUP)], sem)
            cp.start()
            cp.wait()

    qspec = pl.BlockSpec((GROUP, tq, hd), lambda g, j, i: (g, i, 0))
    kspec = pl.BlockSpec((None, tk, hd), lambda g, j, i: (g, j, 0))
    rspec = pl.BlockSpec((GROUP, 1, tq), lambda g, j, i: (g, 0, i))
    return _pcall(
        body, name=name, grid=(kv, nk, nq),
        in_specs=[qspec, kspec, pl.BlockSpec((None, hd, tk), lambda g, j, i: (g, 0, j)), kspec, qspec, rspec, rspec],
        out_specs=[ANY, kspec, kspec],
        out_shape=[jax.ShapeDtypeStruct((h, hd, na), F32), jax.ShapeDtypeStruct((kv, na, hd), F32),
                   jax.ShapeDtypeStruct((kv, na, hd), F32)],
        scratch_shapes=[pltpu.VMEM((GROUP, hd, na), F32), pltpu.VMEM((tk, hd), F32), pltpu.VMEM((tk, hd), F32),
                        pltpu.SemaphoreType.DMA],
        compiler_params=_cp("arbitrary", "arbitrary", "arbitrary"),
    )(q_hm, k_hm, kt_hm, v_hm, do_hm, lse_r, dl_r)


def _qknorm_bwd(dq_tm, dk_tm, dv_tm, p, cos_t, sin_t, gq, gk, bd, tm, name):
    na = p.shape[0]
    d = gq.shape[1]
    kw = d // GROUP

    def body(dq_ref, dk_ref, dv_ref, q_ref, kv_ref, cos_ref, sin_ref, gq_ref, gk_ref, bd_ref,
             dqo_ref, dkvo_ref, acc_ref):
        @pl.when(pl.program_id(0) == 0)
        def _():
            acc_ref[...] = jnp.zeros_like(acc_ref)

        cos = cos_ref[...]
        sin = sin_ref[...]

        def back(dy, xh, g, w):
            bdw = bd_ref[0:w, 0:w]
            dn = dy * _lane_tile(cos, w) - _partner(dy) * _lane_tile(sin, w)
            rs = lax.rsqrt(_split_dot(xh * xh, bdw) * (1.0 / HEAD_DIM) + EPS)
            y = xh * rs
            dg = _colsum(dn * y)
            dyn = dn * g
            dx = rs * (dyn - y * (_split_dot(dyn * y, bdw) * (1.0 / HEAD_DIM)))
            return dx, dg

        dq, dgq = back(dq_ref[...] * ATTN_SCALE, q_ref[...], gq_ref[...], d)
        dqo_ref[...] = dq.astype(BF16)
        acc_ref[0:1, :] += dgq
        kv = kv_ref[...]
        dk, dgk = back(dk_ref[...], kv[:, 0:kw], gk_ref[...], kw)
        acc_ref[1:2, 0:kw] += dgk
        dkvo_ref[:, 0:kw] = dk.astype(BF16)
        dkvo_ref[:, kw:2 * kw] = dv_ref[...].astype(BF16)

    return _pcall(
        body, name=name, grid=(na // tm,),
        in_specs=[_rows(tm, d), _rows(tm, kw), _rows(tm, kw), _rows(tm, d, 3), _rows(tm, d // 2, 14),
                  _rows(tm, 128), _rows(tm, 128), _full((1, d)), _full((1, kw)), _full((d, d))],
        out_specs=[_rows(tm, d), _rows(tm, d // 2), _full((8, d))],
        out_shape=[jax.ShapeDtypeStruct((na, d), BF16), jax.ShapeDtypeStruct((na, d // 2), BF16),
                   jax.ShapeDtypeStruct((8, d), F32)],
        compiler_params=_cp("arbitrary"))(dq_tm, dk_tm, dv_tm, p, p, cos_t, sin_t, gq, gk, bd)


def _conv_bwd(dy2, p, conv_w, n_lat, tm, name):
    na = p.shape[0]
    ktaps, d = conv_w.shape
    pad = ktaps // 2

    def body(dy_ref, dyp_ref, dyn_ref, a_ref, ap_ref, an_ref, g_ref, gp_ref, gn_ref, w_ref,
             da_ref, dg_ref, dw_ref, dwin, ywin):
        i = pl.program_id(0)

        @pl.when(i == 0)
        def _():
            dw_ref[...] = jnp.zeros_like(dw_ref)

        first, last = _seq_ends(i, n_lat, na, tm)
        dy = dy_ref[...]
        a = a_ref[...]
        sg = _sig(g_ref[...])
        _window(dwin, dyp_ref[...], dy, dyn_ref[...], first, last, tm)
        _window(ywin, ap_ref[...] * _sig(gp_ref[...]), a * sg, an_ref[...] * _sig(gn_ref[...]), first, last, tm)
        dy1 = jnp.zeros((tm, d), F32)
        for k in range(ktaps):
            dy1 = dy1 + w_ref[k:k + 1, :] * dwin[pl.ds(HALO + pad - k, tm), :]
            dw_ref[k:k + 1, :] += _colsum(dy * ywin[pl.ds(HALO - pad + k, tm), :])
        da_ref[...] = (dy1 * sg).astype(BF16)
        dg_ref[...] = (dy1 * a * sg * (1.0 - sg)).astype(BF16)

    dyp, dyn = _halo_specs(tm, d, na, 0)
    ap, an = _halo_specs(tm, d, na, 0)
    gp, gn = _halo_specs(tm, d, na, 1)
    bfo = jax.ShapeDtypeStruct((na, d), BF16)
    return _pcall(
        body, name=name, grid=(na // tm,),
        in_specs=[_rows(tm, d), dyp, dyn, _rows(tm, d, 0), ap, an, _rows(tm, d, 1), gp, gn, _full((ktaps, d))],
        out_specs=[_rows(tm, d), _rows(tm, d), _full((ktaps, d))],
        out_shape=[bfo, bfo, jax.ShapeDtypeStruct((ktaps, d), F32)],
        scratch_shapes=[pltpu.VMEM((tm + 2 * HALO, d), F32), pltpu.VMEM((tm + 2 * HALO, d), F32)],
        compiler_params=_cp("arbitrary"))(dy2, dy2, dy2, p, p, p, p, p, p, conv_w)


def _inproj_bwd(segs, dkv, xa, dxn, modv, gpre, wp_t, n_lat, tm, name):
    na, d = xa.shape
    nseg = len(segs)
    wrows = wp_t.shape[0]

    def body(*refs):
        seg_refs = refs[:nseg]
        dkv_ref, x_ref, dxn_ref, mod_ref, g_ref, wt_hbm, dx_ref, acc_ref, wt, sem = refs[nseg:]
        i = pl.program_id(0)
        ctx_tile = i >= n_lat // tm

        @pl.when(i == 0)
        def _():
            cp = pltpu.make_async_copy(wt_hbm, wt, sem)
            cp.start()
            cp.wait()
            acc_ref[...] = jnp.zeros_like(acc_ref)

        dh = _dot(dkv_ref[...], wt[nseg * d:wrows, :])
        for s in range(nseg):
            dh = dh + _dot(seg_refs[s][...], wt[s * d:(s + 1) * d, :])
        x = x_ref[...]
        r = lax.rsqrt(_mean(x * x) + EPS)
        xn = x * r
        g = g_ref[...]
        sc1 = 1.0 + _seg_rows(mod_ref, d, 2 * d, ctx_tile)
        dsh = _colsum(dh)
        dsc = _colsum(dh * (xn * g))

        @pl.when(ctx_tile)
        def _():
            acc_ref[1:2, :] += dsh
            acc_ref[3:4, :] += dsc

        @pl.when(jnp.logical_not(ctx_tile))
        def _():
            acc_ref[0:1, :] += dsh
            acc_ref[2:3, :] += dsc

        acc_ref[4:5, :] += _colsum(dh * xn * sc1)
        dxh = dh * g * sc1
        dx_ref[...] = dxn_ref[...] + r * (dxh - xn * _mean(dxh * xn))

    r_ = _rows(tm, d)
    return _pcall(
        body, name=name, grid=(na // tm,),
        in_specs=[r_] * nseg + [_rows(tm, d // 2), r_, r_, _full((8, 3 * d)), _full((1, d)), ANY],
        out_specs=[r_, _full((8, d))],
        out_shape=[jax.ShapeDtypeStruct((na, d), F32), jax.ShapeDtypeStruct((8, d), F32)],
        scratch_shapes=[pltpu.VMEM(wp_t.shape, BF16), pltpu.SemaphoreType.DMA],
        compiler_params=_cp("arbitrary"))(*segs, dkv, xa, dxn, modv, gpre, wp_t)


def _grad_matmul(a, b, tk, name):
    na, ka = a.shape
    nb = b.shape[1]
    tn = min(nb, 512)

    def body(a_ref, b_ref, o_ref):
        @pl.when(pl.program_id(1) == 0)
        def _():
            o_ref[...] = jnp.zeros_like(o_ref)

        o_ref[...] += _dot_tn(a_ref[...], b_ref[...])

    return _pcall(
        body, name=name, grid=(nb // tn, na // tk),
        in_specs=[pl.BlockSpec((tk, ka), lambda n, k: (k, 0)), pl.BlockSpec((tk, tn), lambda n, k: (k, n))],
        out_specs=pl.BlockSpec((ka, tn), lambda n, k: (0, n)),
        out_shape=jax.ShapeDtypeStruct((ka, nb), F32),
        compiler_params=_cp("parallel", "arbitrary"))(a, b)


def _pack(parts, cols, row_mult, dtype):
    flat = jnp.concatenate([q.astype(dtype).reshape(-1) for q in parts])
    rows = -(-flat.shape[0] // (cols * row_mult)) * row_mult
    return jnp.pad(flat, (0, rows * cols - flat.shape[0])).reshape(rows, cols)


def _pack_by_dest(parts, cols, row_mult):
    flat = jnp.concatenate([q.reshape(N_DEV, -1) for q in parts], axis=1)
    rows = -(-flat.shape[1] // (cols * row_mult)) * row_mult
    return jnp.pad(flat, ((0, 0), (0, rows * cols - flat.shape[1]))).reshape(N_DEV, rows, cols)


def _unpack(flat, shapes):
    out, off = [], 0
    for s in shapes:
        n = math.prod(s)
        out.append(flat[..., off:off + n].reshape(flat.shape[:-1] + tuple(s)))
        off += n
    return out


def _cols_by_dest(g):
    l, a, w8 = g.shape
    return g.reshape(l, a, N_DEV, w8 // N_DEV).transpose(2, 0, 1, 3)


def _rows_by_dest(g):
    l, r8, b = g.shape
    return g.reshape(l, N_DEV, r8 // N_DEV, b).transpose(1, 0, 2, 3)


def _cols_from_src(s):
    n, l, a, w = s.shape
    return s.transpose(1, 2, 0, 3).reshape(l, a, n * w)


def _rows_from_src(s):
    n, l, r, b = s.shape
    return s.transpose(1, 0, 2, 3).reshape(l, n * r, b)


def _rope_tables(n_lat, n_ctx):
    half = HEAD_DIM // 2
    rows = n_lat // GRID_W
    row = jnp.repeat(jnp.arange(rows, dtype=F32), GRID_W)
    col = jnp.tile(jnp.arange(GRID_W, dtype=F32), rows)
    inv_freq = ROPE_THETA ** (-jnp.arange(0, half, 2, dtype=F32) / half)
    ang = jnp.concatenate([row[:, None] * inv_freq, col[:, None] * inv_freq], axis=-1)
    cos, sin = jnp.cos(ang), jnp.sin(ang)
    cos_t = jnp.concatenate([jnp.tile(cos, (1, 4)), jnp.ones((n_ctx, 128), F32)], axis=0)
    sin_t = jnp.concatenate([jnp.tile(jnp.concatenate([-sin, sin], axis=-1), (1, 2)),
                             jnp.zeros((n_ctx, 128), F32)], axis=0)
    return cos_t, sin_t


def _to_heads(t):
    na, w = t.shape
    return t.reshape(na, w // HEAD_DIM, HEAD_DIM).transpose(1, 0, 2)


def _from_heads(t):
    nh, na, hd = t.shape
    return t.transpose(1, 0, 2).reshape(na, nh * hd)


def kernel(x, c, ctx, c_ctx, w_mod, b_mod, g_pre, g_post, w_in, conv_w, conv_b, ln_g, ln_b, w_conv_out, q_norm_g, k_norm_g, w_attn_out, w_out, loss_target, m_c_ctx, m_w_mod, m_b_mod, m_g_pre, m_g_post, m_w_in, m_conv_w, m_conv_b, m_ln_g, m_ln_b, m_w_conv_out, m_q_norm_g, m_k_norm_g, m_w_attn_out, m_w_out, v_c_ctx, v_w_mod, v_b_mod, v_g_pre, v_g_post, v_w_in, v_conv_w, v_conv_b, v_ln_g, v_ln_b, v_w_conv_out, v_q_norm_g, v_k_norm_g, v_w_attn_out, v_w_out):
    depth, d, _ = w_mod.shape
    n_lat, n_ctx = x.shape[1], ctx.shape[1]
    na = n_lat + n_ctx
    heads = d // HEAD_DIM
    kw = d // GROUP
    ktaps = conv_w.shape[1]
    tm = n_ctx
    tm_half = tm // 2
    tbig = 3 * tm if na % (3 * tm) == 0 else tm

    big_bf = [w_mod, w_in, w_conv_out, w_attn_out, w_out]
    gathered = _all_gather(_pack(big_bf, d, 16, BF16), "gather_weights")
    s_mod, s_in, s_co, s_ao, s_oo = _unpack(gathered.reshape(N_DEV, -1), [q.shape for q in big_bf])
    wmod_f = _cols_from_src(s_mod)
    win_f = _cols_from_src(s_in)
    wc_f, wa_f, wo_f = _rows_from_src(s_co), _rows_from_src(s_ao), _rows_from_src(s_oo)
    wp_f = jnp.concatenate([win_f[:, :, :4 * d], win_f[:, :, 4 * d + 2 * kw:], win_f[:, :, 4 * d:4 * d + 2 * kw]], axis=2)
    convw_g = _all_gather(_pack([conv_w], 128, 8, F32), "gather_conv_w")
    convw_f = _cols_from_src(_unpack(convw_g.reshape(N_DEV, -1), [conv_w.shape])[0])

    cos_t, sin_t = _rope_tables(n_lat, n_ctx)
    lane = jnp.arange(d)
    bd = (lane[:, None] // HEAD_DIM == lane[None, :] // HEAD_DIM).astype(BF16)
    esel = (lane[:, None] // HEAD_DIM == jnp.arange(128)[None, :]).astype(BF16)
    cvec = jnp.zeros((8, d), F32).at[0].set(c[0]).at[1].set(c_ctx)
    cvec_t = jnp.zeros((d, 128), F32).at[:, 0].set(c[0]).at[:, 1].set(c_ctx)

    xa = jnp.concatenate([x[0], ctx[0]], axis=0)

    saved = []
    for l in range(depth):
        tag = f"_l{l}"
        gq = jnp.tile(q_norm_g[l], heads)[None, :]
        gk = jnp.tile(k_norm_g[l], heads // GROUP)[None, :]
        modv = _mod_fwd(cvec, wmod_f[l], b_mod[l][None, :], "mod_fwd" + tag)
        p, hb = _inproj(xa, modv, g_pre[l][None, :], wp_f[l], n_lat, tbig, "inproj" + tag)
        qr, kr, vb = _qknorm_fwd(p, cos_t, sin_t, gq, gk, bd, tm, "qknorm_fwd" + tag)
        q_hm, k_hm, v_hm = _to_heads(qr), _to_heads(kr), _to_heads(vb)
        o_hm, lse = _attn_fwd(q_hm, k_hm, v_hm, n_lat, tm, tm, "attn_fwd" + tag)
        o_tm = _from_heads(o_hm)
        y5, y2 = _conv_fwd(p, convw_f[l], conv_b[l][None, :], ln_g[l][None, :], ln_b[l][None, :], n_lat, tm,
                           "conv_fwd" + tag)
        xa_new, yc, ya, out, z, og = _merge_fwd(y5, o_tm, p, xa, modv, g_post[l][None, :], wc_f[l], wa_f[l], wo_f[l],
                                                n_lat, tm, "merge_fwd" + tag)
        saved.append(dict(xa=xa, modv=modv, p=p, hb=hb, q_hm=q_hm, k_hm=k_hm, v_hm=v_hm, o_tm=o_tm, lse=lse,
                          y5=y5, y2=y2, yc=yc, ya=ya, out=out, z=z, og=og, gq=gq, gk=gk))
        xa = xa_new

    dxa, loss_blk = _loss_grad(xa, loss_target[0], n_lat, tm, "loss_grad")
    loss = lax.psum(loss_blk[0, 0], MESH_AXES)

    g_wmod, g_win, g_convw, g_wc, g_wa, g_wo = [], [], [], [], [], []
    g_bmod, g_gpre, g_gpost, g_convb, g_lng, g_lnb, g_qg, g_kg = [], [], [], [], [], [], [], []
    g_cctx = jnp.zeros((d,), F32)
    for l in reversed(range(depth)):
        tag = f"_l{l}"
        s = saved[l]
        p = s["p"]
        (dgb, dma, dmb, dgta, do_tm, dout, dyc, dya, dy2, dl128, acc_m) = _merge_bwd(
            dxa, s["out"], s["yc"], s["ya"], s["o_tm"], p, s["y2"], s["modv"], g_post[l][None, :],
            ln_g[l][None, :], ln_b[l][None, :], wo_f[l].T, wc_f[l].T, wa_f[l].T, esel, n_lat, tm_half,
            "merge_bwd" + tag)
        lse_r = s["lse"].reshape(heads, 1, na)
        dl_r = dl128[:, :heads].T.reshape(heads, 1, na)
        kt_hm = s["k_hm"].transpose(0, 2, 1)
        dqt_hm, dk_hm, dv_hm = _attn_bwd(s["q_hm"], s["k_hm"], kt_hm, s["v_hm"], _to_heads(do_tm), lse_r, dl_r,
                                         n_lat, tm, tm, "attn_bwd" + tag)
        dq_tm = dqt_hm.transpose(2, 0, 1).reshape(na, d)
        dq, dkv, acc_q = _qknorm_bwd(dq_tm, _from_heads(dk_hm), _from_heads(dv_hm), p, cos_t, sin_t,
                                     s["gq"], s["gk"], bd, tm, "qknorm_bwd" + tag)
        da, dg, dconvw = _conv_bwd(dy2, p, convw_f[l], n_lat, tm, "conv_bwd" + tag)
        segs = [da, dg, dgta, dq, dgb, dma, dmb]
        dxa, acc_h = _inproj_bwd(segs, dkv, s["xa"], dxa, s["modv"], g_pre[l][None, :], wp_f[l].T, n_lat, tm,
                                 "inproj_bwd" + tag)

        dwp = [_grad_matmul(s["hb"], sg, tbig, f"grad_w_in{k}" + tag) for k, sg in enumerate(segs + [dkv])]
        g_win.append(jnp.concatenate(dwp[:4] + [dwp[7]] + dwp[4:7], axis=1))
        g_wc.append(_grad_matmul(s["y5"], dyc, tbig, "grad_w_conv_out" + tag))
        g_wa.append(_grad_matmul(s["og"], dya, tbig, "grad_w_attn_out" + tag))
        g_wo.append(_grad_matmul(s["z"], dout, tbig, "grad_w_out" + tag))
        g_convw.append(dconvw)

        dmod = jnp.zeros((8, 3 * d), F32)
        dmod = dmod.at[0].set(jnp.concatenate([acc_h[0], acc_h[2], acc_m[0]]))
        dmod = dmod.at[1].set(jnp.concatenate([acc_h[1], acc_h[3], acc_m[1]]))
        dwm, dbm, dcv = _mod_bwd(dmod, cvec, cvec_t, wmod_f[l], "mod_bwd" + tag)
        g_wmod.append(dwm)
        g_bmod.append(dbm[0])
        g_cctx = g_cctx + dcv[1]
        g_gpre.append(acc_h[4])
        g_gpost.append(acc_m[2])
        g_lng.append(acc_m[3])
        g_lnb.append(acc_m[4])
        g_convb.append(acc_m[5])
        g_qg.append(acc_q[0].reshape(heads, HEAD_DIM).sum(0))
        g_kg.append(acc_q[1, :kw].reshape(heads // GROUP, HEAD_DIM).sum(0))

    grad_x = dxa[:n_lat][None]

    def stack(lst):
        return jnp.stack(lst[::-1])

    big_names = ["w_mod", "w_in", "w_conv_out", "w_attn_out", "w_out", "conv_w"]
    big_w = dict(w_mod=w_mod, w_in=w_in, w_conv_out=w_conv_out, w_attn_out=w_attn_out, w_out=w_out, conv_w=conv_w)
    big_m = dict(w_mod=m_w_mod, w_in=m_w_in, w_conv_out=m_w_conv_out, w_attn_out=m_w_attn_out, w_out=m_w_out,
                 conv_w=m_conv_w)
    big_v = dict(w_mod=v_w_mod, w_in=v_w_in, w_conv_out=v_w_conv_out, w_attn_out=v_w_attn_out, w_out=v_w_out,
                 conv_w=v_conv_w)
    by_dest = [_cols_by_dest(stack(g_wmod)), _cols_by_dest(stack(g_win)), _rows_by_dest(stack(g_wc)),
               _rows_by_dest(stack(g_wa)), _rows_by_dest(stack(g_wo)), _cols_by_dest(stack(g_convw))]
    send = _pack_by_dest(by_dest, d, 512)
    rows = send.shape[1]
    send = send.reshape(4, 2, rows, d)
    from_sibling = _swap_with_sibling(send, "reduce_sibling")
    chip_part = _add_sibling_part(send, from_sibling, "reduce_sibling_add")
    from_chips = _exchange_chips(chip_part, "reduce_chips")
    big_shapes = [big_w[n].shape for n in big_names]

    def pack_mine(tree):
        return _pack([tree[n] for n in big_names], d, 512, F32)

    big_out = _sum_adamw(from_chips, pack_mine(big_w), pack_mine(big_m), pack_mine(big_v), "adamw_sharded")
    big_g, big_d, big_nm, big_nv = [dict(zip(big_names, _unpack(o.reshape(-1), big_shapes))) for o in big_out]

    small_names = ["c_ctx", "b_mod", "g_pre", "g_post", "conv_b", "ln_g", "ln_b", "q_norm_g", "k_norm_g"]
    small_w = dict(c_ctx=c_ctx, b_mod=b_mod, g_pre=g_pre, g_post=g_post, conv_b=conv_b, ln_g=ln_g, ln_b=ln_b,
                   q_norm_g=q_norm_g, k_norm_g=k_norm_g)
    small_m = dict(c_ctx=m_c_ctx, b_mod=m_b_mod, g_pre=m_g_pre, g_post=m_g_post, conv_b=m_conv_b, ln_g=m_ln_g,
                   ln_b=m_ln_b, q_norm_g=m_q_norm_g, k_norm_g=m_k_norm_g)
    small_v = dict(c_ctx=v_c_ctx, b_mod=v_b_mod, g_pre=v_g_pre, g_post=v_g_post, conv_b=v_conv_b, ln_g=v_ln_g,
                   ln_b=v_ln_b, q_norm_g=v_q_norm_g, k_norm_g=v_k_norm_g)
    small_g = dict(c_ctx=g_cctx, b_mod=stack(g_bmod), g_pre=stack(g_gpre), g_post=stack(g_gpost),
                   conv_b=stack(g_convb), ln_g=stack(g_lng), ln_b=stack(g_lnb), q_norm_g=stack(g_qg),
                   k_norm_g=stack(g_kg))
    small_shapes = [small_w[n].shape for n in small_names]

    def pack_small(tree):
        return _pack([tree[n] for n in small_names], d, 8, F32)

    small_parts = _all_gather(pack_small(small_g), "gather_small_grads")
    small_out = _sum_adamw(small_parts, pack_small(small_w), pack_small(small_m), pack_small(small_v),
                           "adamw_replicated")
    sm_g, sm_d, sm_nm, sm_nv = [dict(zip(small_names, _unpack(o.reshape(-1), small_shapes))) for o in small_out]

    order = ["c_ctx", "w_mod", "b_mod", "g_pre", "g_post", "w_in", "conv_w", "conv_b", "ln_g", "ln_b",
             "w_conv_out", "q_norm_g", "k_norm_g", "w_attn_out", "w_out"]

    def pick(big, small):
        return [big[n] if n in big else small[n] for n in order]

    return (loss, grad_x, *pick(big_g, sm_g), *pick(big_d, sm_d), *pick(big_nm, sm_nm), *pick(big_nv, sm_nv))
```

```python
import math

import jax
import jax.numpy as jnp
from jax import lax
from jax.experimental import pallas as pl
from jax.experimental.pallas import tpu as pltpu

F32 = jnp.float32
BF16 = jnp.bfloat16

HEAD_DIM = 64
GROUP = 4
GRID_W = 64
ROPE_THETA = 10000.0
EPS = 1e-6
ATTN_SCALE = HEAD_DIM ** -0.5
HALO = 16

ADAM_LR = 0.001
ADAM_B1 = 0.9
ADAM_B2 = 0.999
ADAM_EPS = 1e-08
ADAM_WD = 0.01
ADAM_STEP = 10

N_DEV = 8
MESH_AXES = ("x", "y", "c")
V7X_VMEM_LIMIT = 56 * 1024 * 1024
NEG_BIG = -1e30

MESH = pl.DeviceIdType.MESH
ANY = pl.BlockSpec(memory_space=pl.ANY)


def _pcall(body, **kw):
    return pl.pallas_call(body, **kw)


def _cp(*sem):
    return pltpu.CompilerParams(dimension_semantics=sem, vmem_limit_bytes=V7X_VMEM_LIMIT)


def _sig(x):
    return 1.0 / (1.0 + jnp.exp(-x))


def _mean(x):
    return jnp.mean(x, axis=-1, keepdims=True)


def _colsum(x):
    return jnp.sum(x, axis=0, keepdims=True)


def _bf_round(x):
    return x.astype(BF16).astype(F32)


def _dot(a, b):
    return jnp.dot(a, b, preferred_element_type=F32)


def _dot_nt(a, b):
    return lax.dot_general(a, b, (((1,), (1,)), ((), ())), preferred_element_type=F32)


def _dot_tn(a, b):
    return lax.dot_general(a, b, (((0,), (0,)), ((), ())), preferred_element_type=F32)


def _split_dot(x, m):
    hi = x.astype(BF16)
    lo = (x - hi.astype(F32)).astype(BF16)
    return _dot(hi, m) + _dot(lo, m)


def _full(shape):
    nd = len(shape)
    return pl.BlockSpec(shape, lambda *_: (0,) * nd)


def _rows(tm, width, colblk=0):
    return pl.BlockSpec((tm, width), lambda i: (i, colblk))


def _my_place():
    return lax.axis_index("x"), lax.axis_index("y"), lax.axis_index("c")


def _sem_arrays(n):
    return [pltpu.SemaphoreType.DMA((n,)), pltpu.SemaphoreType.DMA((n,))]


def _all_gather(shards, name):
    n = len(shards)

    def body(*refs):
        x_refs, out_refs = refs[:n], refs[n:2 * n]
        send_sems, recv_sems, local_sems = refs[2 * n:]
        x, y, c = _my_place()
        me, sibling = (x, y, c), (x, y, 1 - c)
        chips = [(1 - x, y), (x, 1 - y), (1 - x, 1 - y)]

        def slab(a, px, py, pc):
            return out_refs[a].at[4 * px + 2 * py + pc]

        def copies(k, block, to, from_input=False):
            return [pltpu.make_async_remote_copy(
                src_ref=x_refs[a] if from_input else slab(a, *block), dst_ref=slab(a, *block),
                send_sem=send_sems.at[k * n + a], recv_sem=recv_sems.at[k * n + a],
                device_id=to, device_id_type=MESH) for a in range(n)]

        mine = [pltpu.make_async_copy(x_refs[a], slab(a, *me), local_sems.at[a]) for a in range(n)]
        for cp in mine:
            cp.start()
        first = copies(0, me, sibling, True)
        for j, chip in enumerate(chips):
            first += copies(1 + j, me, (*chip, c), True)
        for cp in first:
            cp.start()
        passed = []
        for j, chip in enumerate(chips):
            for cp in copies(1 + j, (*chip, c), me):
                cp.wait_recv()
            onward = copies(4 + j, (*chip, c), sibling)
            for cp in onward:
                cp.start()
            passed += onward
        for cp in copies(0, sibling, me):
            cp.wait_recv()
        for j, chip in enumerate(chips):
            for cp in copies(4 + j, (*chip, 1 - c), me):
                cp.wait_recv()
        for cp in first + passed:
            cp.wait_send()
        for cp in mine:
            cp.wait()

    return _pcall(
        body, name=name,
        out_shape=[jax.ShapeDtypeStruct((N_DEV,) + q.shape, q.dtype) for q in shards],
        in_specs=[ANY] * n, out_specs=[ANY] * n,
        scratch_shapes=_sem_arrays(7 * n) + [pltpu.SemaphoreType.DMA((n,))],
    )(*shards)


def _swap_with_sibling(bufs, name):
    n = len(bufs)

    def body(*refs):
        buf_refs, recv_refs = refs[:n], refs[n:2 * n]
        send_sems, recv_sems = refs[2 * n:]
        x, y, c = _my_place()
        copies = [
            pltpu.make_async_remote_copy(
                src_ref=buf_refs[a].at[k, 1 - c], dst_ref=recv_refs[a].at[k],
                send_sem=send_sems.at[k * n + a], recv_sem=recv_sems.at[k * n + a],
                device_id=(x, y, 1 - c), device_id_type=MESH)
            for k in range(4) for a in range(n)]
        for cp in copies:
            cp.start()
        for cp in copies:
            cp.wait()

    return _pcall(
        body, name=name,
        out_shape=[jax.ShapeDtypeStruct((4,) + q.shape[2:], q.dtype) for q in bufs],
        in_specs=[ANY] * n, out_specs=[ANY] * n,
        scratch_shapes=_sem_arrays(4 * n),
    )(*bufs)


def _exchange_chips(parts, name):
    n = len(parts)

    def body(*refs):
        s_refs, recv_refs = refs[:n], refs[n:2 * n]
        send_sems, recv_sems, local_sems = refs[2 * n:]
        x, y, c = _my_place()
        mychip = 2 * x + y
        chips = [(1 - x, y), (x, 1 - y), (1 - x, 1 - y)]
        mine = [pltpu.make_async_copy(s_refs[a].at[mychip], recv_refs[a].at[mychip], local_sems.at[a])
                for a in range(n)]
        for cp in mine:
            cp.start()
        copies = [
            pltpu.make_async_remote_copy(
                src_ref=s_refs[a].at[2 * px + py], dst_ref=recv_refs[a].at[mychip],
                send_sem=send_sems.at[j * n + a], recv_sem=recv_sems.at[j * n + a],
                device_id=(px, py, c), device_id_type=MESH)
            for j, (px, py) in enumerate(chips) for a in range(n)]
        for cp in copies:
            cp.start()
        for cp in copies:
            cp.wait()
        for cp in mine:
            cp.wait()

    return _pcall(
        body, name=name,
        out_shape=[jax.ShapeDtypeStruct(q.shape, q.dtype) for q in parts],
        in_specs=[ANY] * n, out_specs=[ANY] * n,
        scratch_shapes=_sem_arrays(3 * n) + [pltpu.SemaphoreType.DMA((n,))],
    )(*parts)


def _row_tile(a):
    for t in range(256, 7, -8):
        if a % t == 0:
            return t
    return a


def _add_sibling_part(buf, recv, name):
    _, _, nl, a, b = buf.shape
    ta = _row_tile(a)
    core = lax.axis_index("c").astype(jnp.int32).reshape(1)

    def body(core_ref, a_ref, b_ref, o_ref):
        o_ref[...] = a_ref[...] + b_ref[...]

    grid_spec = pltpu.PrefetchScalarGridSpec(
        num_scalar_prefetch=1, grid=(4, nl, a // ta),
        in_specs=[pl.BlockSpec((None, None, None, ta, b), lambda k, l, r, cr: (k, cr[0], l, r, 0)),
                  pl.BlockSpec((None, None, ta, b), lambda k, l, r, cr: (k, l, r, 0))],
        out_specs=pl.BlockSpec((None, None, ta, b), lambda k, l, r, cr: (k, l, r, 0)))
    return _pcall(body, name=name, grid_spec=grid_spec,
                  out_shape=jax.ShapeDtypeStruct((4, nl, a, b), F32),
                  compiler_params=_cp("parallel", "parallel", "parallel"))(core, buf, recv)


def _sum_adamw(stack, w, m, v, name):
    ns, nl, a, b = stack.shape
    ta = _row_tile(a)
    c1 = 1.0 - ADAM_B1 ** ADAM_STEP
    c2 = 1.0 - ADAM_B2 ** ADAM_STEP

    def body(s_ref, w_ref, m_ref, v_ref, g_out, d_out, m_out, v_out):
        g = s_ref[0]
        for k in range(1, ns):
            g = g + s_ref[k]
        m_new = ADAM_B1 * m_ref[...] + (1.0 - ADAM_B1) * g
        v_new = ADAM_B2 * v_ref[...] + (1.0 - ADAM_B2) * (g * g)
        m_hat = m_new / c1
        v_hat = v_new / c2
        g_out[...] = g
        d_out[...] = -ADAM_LR * (m_hat / (jnp.sqrt(v_hat) + ADAM_EPS) + ADAM_WD * w_ref[...])
        m_out[...] = m_new
        v_out[...] = v_new

    blk = pl.BlockSpec((None, ta, b), lambda l, i: (l, i, 0))
    return _pcall(
        body, name=name, grid=(nl, a // ta),
        in_specs=[pl.BlockSpec((ns, None, ta, b), lambda l, i: (0, l, i, 0)), blk, blk, blk],
        out_specs=[blk] * 4,
        out_shape=[jax.ShapeDtypeStruct((nl, a, b), F32)] * 4,
        compiler_params=_cp("parallel", "parallel"))(stack, w, m, v)


def _mod_fwd(cvec, wmod, bmod, name):
    _, d = cvec.shape

    def body(c_ref, w_ref, b_ref, o_ref):
        cv = c_ref[...]
        cs = cv * _sig(cv)
        o_ref[...] = _dot(cs.astype(BF16), w_ref[...]) + b_ref[...]

    return _pcall(
        body, name=name, grid=(3,),
        in_specs=[_full((8, d)), pl.BlockSpec((d, d), lambda n: (0, n)), pl.BlockSpec((1, d), lambda n: (0, n))],
        out_specs=pl.BlockSpec((8, d), lambda n: (0, n)),
        out_shape=jax.ShapeDtypeStruct((8, 3 * d), F32),
        compiler_params=_cp("parallel"))(cvec, wmod, bmod)


def _mod_bwd(dmod, cvec, cvec_t, wmod, name):
    _, d = cvec.shape

    def body(dm_ref, c_ref, ct_ref, w_ref, dw_ref, db_ref, dc_ref):
        n = pl.program_id(0)
        dm = dm_ref[...]
        ct = ct_ref[...]
        cs_t = _bf_round(ct * _sig(ct))
        d0 = _bf_round(dm[0:1, :])
        d1 = _bf_round(dm[1:2, :])
        dw_ref[...] = cs_t[:, 0:1] * d0 + cs_t[:, 1:2] * d1
        db_ref[...] = dm[0:1, :] + dm[1:2, :]

        @pl.when(n == 0)
        def _():
            dc_ref[...] = jnp.zeros_like(dc_ref)

        dc_ref[...] += _dot_nt(dm.astype(BF16), w_ref[...])

        @pl.when(n == 2)
        def _():
            cv = c_ref[...]
            s = _sig(cv)
            dc_ref[...] = dc_ref[...] * (s * (1.0 + cv * (1.0 - s)))

    return _pcall(
        body, name=name, grid=(3,),
        in_specs=[pl.BlockSpec((8, d), lambda n: (0, n)), _full((8, d)), _full((d, 128)),
                  pl.BlockSpec((d, d), lambda n: (0, n))],
        out_specs=[pl.BlockSpec((d, d), lambda n: (0, n)), pl.BlockSpec((1, d), lambda n: (0, n)),
                   _full((8, d))],
        out_shape=[jax.ShapeDtypeStruct((d, 3 * d), F32), jax.ShapeDtypeStruct((1, 3 * d), F32),
                   jax.ShapeDtypeStruct((8, d), F32)],
        compiler_params=_cp("arbitrary"))(dmod, cvec, cvec_t, wmod)


def _seg_rows(mod_ref, lo, hi, is_ctx):
    return jnp.where(is_ctx, mod_ref[1:2, lo:hi], mod_ref[0:1, lo:hi])


def _inproj(xa, modv, gpre, wp, n_lat, tm, name):
    na, d = xa.shape
    wcols = wp.shape[1]
    tn = d // 2

    def body(x_ref, mod_ref, g_ref, w_ref, p_ref, h_ref, h_s):
        i = pl.program_id(0)

        @pl.when(pl.program_id(1) == 0)
        def _():
            x = x_ref[...]
            r = lax.rsqrt(_mean(x * x) + EPS)
            row = i * tm + lax.broadcasted_iota(jnp.int32, (tm, 1), 0)
            is_ctx = row >= n_lat
            sh = _seg_rows(mod_ref, 0, d, is_ctx)
            sc = _seg_rows(mod_ref, d, 2 * d, is_ctx)
            hb = ((x * r * g_ref[...]) * (1.0 + sc) + sh).astype(BF16)
            h_s[...] = hb
            h_ref[...] = hb

        p_ref[...] = _dot(h_s[...], w_ref[...])

    return _pcall(
        body, name=name, grid=(na // tm, wcols // tn),
        in_specs=[pl.BlockSpec((tm, d), lambda i, j: (i, 0)), _full((8, 3 * d)), _full((1, d)),
                  pl.BlockSpec((d, tn), lambda i, j: (0, j))],
        out_specs=[pl.BlockSpec((tm, tn), lambda i, j: (i, j)), pl.BlockSpec((tm, d), lambda i, j: (i, 0))],
        out_shape=[jax.ShapeDtypeStruct((na, wcols), F32), jax.ShapeDtypeStruct((na, d), BF16)],
        scratch_shapes=[pltpu.VMEM((tm, d), BF16)],
        compiler_params=_cp("parallel", "arbitrary"))(xa, modv, gpre, wp)


def _lane_tile(t, width):
    if width >= 128:
        return jnp.tile(t, (1, width // 128))
    return t[:, :width]


def _partner(x):
    w = x.shape[-1]
    lane = lax.broadcasted_iota(jnp.int32, x.shape, 1)
    low = (lane % HEAD_DIM) < (HEAD_DIM // 2)
    return jnp.where(low, pltpu.roll(x, w - HEAD_DIM // 2, 1), pltpu.roll(x, HEAD_DIM // 2, 1))


def _qknorm_fwd(p, cos_t, sin_t, gq, gk, bd, tm, name):
    na = p.shape[0]
    d = gq.shape[1]
    kw = d // GROUP

    def body(q_ref, kv_ref, cos_ref, sin_ref, gq_ref, gk_ref, bd_ref, qo_ref, ko_ref, vo_ref):
        cos = cos_ref[...]
        sin = sin_ref[...]

        def norm_rope(xh, g, w):
            ms = _split_dot(xh * xh, bd_ref[0:w, 0:w]) * (1.0 / HEAD_DIM)
            xn = xh * lax.rsqrt(ms + EPS) * g
            return xn * _lane_tile(cos, w) + _partner(xn) * _lane_tile(sin, w)

        qo_ref[...] = (norm_rope(q_ref[...], gq_ref[...], d) * ATTN_SCALE).astype(BF16)
        kv = kv_ref[...]
        ko_ref[...] = norm_rope(kv[:, 0:kw], gk_ref[...], kw).astype(BF16)
        vo_ref[...] = kv[:, kw:2 * kw].astype(BF16)

    return _pcall(
        body, name=name, grid=(na // tm,),
        in_specs=[_rows(tm, d, 3), _rows(tm, d // 2, 14), _rows(tm, 128), _rows(tm, 128),
                  _full((1, d)), _full((1, kw)), _full((d, d))],
        out_specs=[_rows(tm, d), _rows(tm, kw), _rows(tm, kw)],
        out_shape=[jax.ShapeDtypeStruct((na, d), BF16), jax.ShapeDtypeStruct((na, kw), BF16),
                   jax.ShapeDtypeStruct((na, kw), BF16)],
        compiler_params=_cp("parallel"))(p, p, cos_t, sin_t, gq, gk, bd)


def _attn_fwd(q_hm, k_hm, vt_hm, n_lat, tq, tk, name):
    h, na, hd = q_hm.shape
    kv = k_hm.shape[0]
    nq, nk = na // tq, na // tk
    nql, nkl = n_lat // tq, n_lat // tk
    cols = GROUP * tq

    def body(q_ref, k_ref, vt_ref, o_ref, lse_ref, m_s, l_s, acc_s):
        i = pl.program_id(1)
        j = pl.program_id(2)

        @pl.when(j == 0)
        def _():
            m_s[...] = jnp.full_like(m_s, NEG_BIG)
            l_s[...] = jnp.zeros_like(l_s)
            acc_s[...] = jnp.zeros_like(acc_s)

        @pl.when(jnp.logical_or(i < nql, j >= nkl))
        def _():
            q = q_ref[...].reshape(cols, hd)
            s_t = _dot_nt(k_ref[...], q)
            m_prev = m_s[...]
            m_new = jnp.maximum(m_prev, jnp.max(s_t, axis=0, keepdims=True))
            alpha = jnp.exp(m_prev - m_new)
            p_t = jnp.exp(s_t - m_new)
            l_s[...] = alpha * l_s[...] + jnp.sum(p_t, axis=0, keepdims=True)
            acc_s[...] = alpha * acc_s[...] + _dot(vt_ref[...], p_t.astype(BF16))
            m_s[...] = m_new

        @pl.when(j == nk - 1)
        def _():
            l = l_s[...]
            o_t = acc_s[...] / l
            lse = m_s[...] + jnp.log(l)
            for a in range(GROUP):
                o_ref[a] = o_t[:, a * tq:(a + 1) * tq]
                lse_ref[a] = lse[:, a * tq:(a + 1) * tq]

    return _pcall(
        body, name=name, grid=(kv, nq, nk),
        in_specs=[pl.BlockSpec((GROUP, tq, hd), lambda g, i, j: (g, i, 0)),
                  pl.BlockSpec((None, tk, hd), lambda g, i, j: (g, j, 0)),
                  pl.BlockSpec((None, hd, tk), lambda g, i, j: (g, 0, j))],
        out_specs=[pl.BlockSpec((GROUP, hd, tq), lambda g, i, j: (g, 0, i)),
                   pl.BlockSpec((GROUP, 1, tq), lambda g, i, j: (g, 0, i))],
        out_shape=[jax.ShapeDtypeStruct((h, hd, na), F32), jax.ShapeDtypeStruct((h, 1, na), F32)],
        scratch_shapes=[pltpu.VMEM((1, cols), F32), pltpu.VMEM((1, cols), F32), pltpu.VMEM((hd, cols), F32)],
        compiler_params=_cp("parallel", "parallel", "arbitrary"))(q_hm, k_hm, vt_hm)


def _window(win_ref, prev, cur, nxt, first, last, tm):
    win_ref[0:HALO, :] = jnp.where(first, 0.0, prev)
    win_ref[HALO:HALO + tm, :] = cur
    win_ref[HALO + tm:HALO + tm + HALO, :] = jnp.where(last, 0.0, nxt)


def _halo_specs(tm, d, na, colblk):
    per = tm // HALO
    last_blk = na // HALO - 1
    prev = pl.BlockSpec((HALO, d), lambda i: (jnp.maximum(i * per - 1, 0), colblk))
    nxt = pl.BlockSpec((HALO, d), lambda i: (jnp.minimum((i + 1) * per, last_blk), colblk))
    return prev, nxt


def _seq_ends(i, n_lat, na, tm):
    first = jnp.logical_or(i == 0, i == n_lat // tm)
    last = jnp.logical_or(i == n_lat // tm - 1, i == na // tm - 1)
    return first, last


def _conv_fwd(p, conv_w, conv_b, ln_g, ln_b, n_lat, tm, name):
    na = p.shape[0]
    ktaps, d = conv_w.shape
    pad = ktaps // 2

    def body(a_ref, ap_ref, an_ref, g_ref, gp_ref, gn_ref, ga_ref, w_ref, cb_ref, lg_ref, lb_ref,
             y5_ref, y2_ref, win):
        i = pl.program_id(0)
        first, last = _seq_ends(i, n_lat, na, tm)
        _window(win, ap_ref[...] * _sig(gp_ref[...]), a_ref[...] * _sig(g_ref[...]),
                an_ref[...] * _sig(gn_ref[...]), first, last, tm)
        acc = jnp.zeros((tm, d), F32)
        for k in range(ktaps):
            acc = acc + w_ref[k:k + 1, :] * win[pl.ds(HALO - pad + k, tm), :]
        y2 = acc + cb_ref[...]
        y2_ref[...] = y2
        xc = y2 - _mean(y2)
        y3 = xc * lax.rsqrt(_mean(xc * xc) + EPS) * lg_ref[...] + lb_ref[...]
        gate = ga_ref[...]
        y5_ref[...] = ((y3 * _sig(y3)) * (gate * _sig(gate))).astype(BF16)

    ap, an = _halo_specs(tm, d, na, 0)
    gp, gn = _halo_specs(tm, d, na, 1)
    return _pcall(
        body, name=name, grid=(na // tm,),
        in_specs=[_rows(tm, d, 0), ap, an, _rows(tm, d, 1), gp, gn, _rows(tm, d, 2),
                  _full((ktaps, d)), _full((1, d)), _full((1, d)), _full((1, d))],
        out_specs=[_rows(tm, d), _rows(tm, d)],
        out_shape=[jax.ShapeDtypeStruct((na, d), BF16), jax.ShapeDtypeStruct((na, d), F32)],
        scratch_shapes=[pltpu.VMEM((tm + 2 * HALO, d), F32)],
        compiler_params=_cp("parallel"))(p, p, p, p, p, p, p, conv_w, conv_b, ln_g, ln_b)


def _merge_fwd(y5, o_tm, p, xa, modv, gpost, wc, wa, wo, n_lat, tm, name):
    na, d = xa.shape

    def body(y5_ref, o_ref, gb_ref, ma_ref, mb_ref, x_ref, mod_ref, gp_ref, wc_ref, wa_ref, wo_ref,
             xn_ref, yc_ref, ya_ref, out_ref, z_ref, og_ref):
        is_ctx = pl.program_id(0) >= n_lat // tm
        gate_b = gb_ref[...]
        og = (o_ref[...] * (gate_b * _sig(gate_b))).astype(BF16)
        og_ref[...] = og
        yc = _dot(y5_ref[...], wc_ref[...])
        ya = _dot(og, wa_ref[...])
        yc_ref[...] = yc
        ya_ref[...] = ya
        z = (_sig(ma_ref[...]) * yc + _sig(mb_ref[...]) * ya).astype(BF16)
        z_ref[...] = z
        out = _dot(z, wo_ref[...])
        out_ref[...] = out
        gt = _seg_rows(mod_ref, 2 * d, 3 * d, is_ctx)
        xn_ref[...] = x_ref[...] + gt * (out * lax.rsqrt(_mean(out * out) + EPS) * gp_ref[...])

    f32o = jax.ShapeDtypeStruct((na, d), F32)
    bfo = jax.ShapeDtypeStruct((na, d), BF16)
    return _pcall(
        body, name=name, grid=(na // tm,),
        in_specs=[_rows(tm, d), _rows(tm, d), _rows(tm, d, 4), _rows(tm, d, 5), _rows(tm, d, 6), _rows(tm, d),
                  _full((8, 3 * d)), _full((1, d)), _full((d, d)), _full((d, d)), _full((d, d))],
        out_specs=[_rows(tm, d)] * 6,
        out_shape=[f32o, f32o, f32o, f32o, bfo, bfo],
        compiler_params=_cp("parallel"))(y5, o_tm, p, p, p, xa, modv, gpost, wc, wa, wo)


def _loss_grad(xa, target, n_lat, tm, name):
    na, d = xa.shape
    nlt = n_lat // tm

    def body(x_ref, t_ref, dx_ref, loss_ref):
        i = pl.program_id(0)

        @pl.when(i == 0)
        def _():
            loss_ref[...] = jnp.zeros_like(loss_ref)

        @pl.when(i < nlt)
        def _():
            err = x_ref[...] - t_ref[...]
            dx_ref[...] = err * (1.0 / d)
            loss_ref[...] += 0.5 * jnp.sum(_mean(err * err))

        @pl.when(i >= nlt)
        def _():
            dx_ref[...] = jnp.zeros_like(dx_ref)

    return _pcall(
        body, name=name, grid=(na // tm,),
        in_specs=[_rows(tm, d), pl.BlockSpec((tm, d), lambda i: (jnp.minimum(i, nlt - 1), 0))],
        out_specs=[_rows(tm, d), _full((8, 128))],
        out_shape=[jax.ShapeDtypeStruct((na, d), F32), jax.ShapeDtypeStruct((8, 128), F32)],
        compiler_params=_cp("arbitrary"))(xa, target)


def _merge_bwd(dxn, out, yc, ya, o_tm, p, y2, modv, gpost, ln_g, ln_b, wo_t, wc_t, wa_t, esel, n_lat, tm, name):
    na, d = dxn.shape

    def body(dx_ref, out_ref, yc_ref, ya_ref, o_ref, gb_ref, ma_ref, mb_ref, gta_ref, y2_ref,
             mod_ref, gp_ref, lg_ref, lb_ref, wot_ref, wct_ref, wat_ref, es_ref,
             dgb_ref, dma_ref, dmb_ref, dgta_ref, do_ref, dout_ref, dyc_ref, dya_ref, dy2_ref, dl_ref, acc_ref):
        i = pl.program_id(0)
        ctx_tile = i >= n_lat // tm

        @pl.when(i == 0)
        def _():
            acc_ref[...] = jnp.zeros_like(acc_ref)

        gt = _seg_rows(mod_ref, 2 * d, 3 * d, ctx_tile)
        gp = gp_ref[...]
        dx = dx_ref[...]
        out = out_ref[...]
        r2 = lax.rsqrt(_mean(out * out) + EPS)
        n2 = out * r2
        dgt = _colsum(dx * (n2 * gp))

        @pl.when(ctx_tile)
        def _():
            acc_ref[1:2, :] += dgt

        @pl.when(jnp.logical_not(ctx_tile))
        def _():
            acc_ref[0:1, :] += dgt

        acc_ref[2:3, :] += _colsum(dx * gt * n2)
        dn2 = dx * gt * gp
        dout = (r2 * (dn2 - n2 * _mean(dn2 * n2))).astype(BF16)
        dout_ref[...] = dout
        dz = _dot(dout, wot_ref[...])
        sa = _sig(ma_ref[...])
        sb = _sig(mb_ref[...])
        dyc = (dz * sa).astype(BF16)
        dya = (dz * sb).astype(BF16)
        dyc_ref[...] = dyc
        dya_ref[...] = dya
        dma_ref[...] = (dz * yc_ref[...] * sa * (1.0 - sa)).astype(BF16)
        dmb_ref[...] = (dz * ya_ref[...] * sb * (1.0 - sb)).astype(BF16)
        dy5 = _dot(dyc, wct_ref[...])
        dog = _dot(dya, wat_ref[...])

        gate_b = gb_ref[...]
        sgb = _sig(gate_b)
        o = o_ref[...]
        do = dog * (gate_b * sgb)
        do_ref[...] = do.astype(BF16)
        dgb_ref[...] = (dog * o * (sgb * (1.0 + gate_b * (1.0 - sgb)))).astype(BF16)
        dl_ref[...] = _split_dot(do * o, es_ref[...])

        y2 = y2_ref[...]
        xc = y2 - _mean(y2)
        rstd = lax.rsqrt(_mean(xc * xc) + EPS)
        xhat = xc * rstd
        lg = lg_ref[...]
        y3 = xhat * lg + lb_ref[...]
        s3 = _sig(y3)
        gate_a = gta_ref[...]
        sga = _sig(gate_a)
        dgta_ref[...] = (dy5 * (y3 * s3) * (sga * (1.0 + gate_a * (1.0 - sga)))).astype(BF16)
        dy3 = dy5 * (gate_a * sga) * (s3 * (1.0 + y3 * (1.0 - s3)))
        acc_ref[3:4, :] += _colsum(dy3 * xhat)
        acc_ref[4:5, :] += _colsum(dy3)
        dxh = dy3 * lg
        dy2 = rstd * (dxh - _mean(dxh) - xhat * _mean(dxh * xhat))
        dy2_ref[...] = dy2
        acc_ref[5:6, :] += _colsum(dy2)

    f32o = jax.ShapeDtypeStruct((na, d), F32)
    bfo = jax.ShapeDtypeStruct((na, d), BF16)
    r = _rows(tm, d)
    return _pcall(
        body, name=name, grid=(na // tm,),
        in_specs=[r, r, r, r, r, _rows(tm, d, 4), _rows(tm, d, 5), _rows(tm, d, 6), _rows(tm, d, 2), r,
                  _full((8, 3 * d)), _full((1, d)), _full((1, d)), _full((1, d)),
                  _full((d, d)), _full((d, d)), _full((d, d)), _full((d, 128))],
        out_specs=[r] * 9 + [_rows(tm, 128), _full((8, d))],
        out_shape=[bfo] * 8 + [f32o, jax.ShapeDtypeStruct((na, 128), F32), jax.ShapeDtypeStruct((8, d), F32)],
        compiler_params=_cp("arbitrary"),
    )(dxn, out, yc, ya, o_tm, p, p, p, p, y2, modv, gpost, ln_g, ln_b, wo_t, wc_t, wa_t, esel)


def _attn_bwd(q_hm, k_hm, kt_hm, v_hm, do_hm, lse_r, dl_r, n_lat, tq, tk, name):
    h, na, hd = q_hm.shape
    kv = k_hm.shape[0]
    nq, nk = na // tq, na // tk
    nql, nkl = n_lat // tq, n_lat // tk
    cols = GROUP * tq

    def body(q_ref, k_ref, kt_ref, v_ref, do_ref, lse_ref, dl_ref, dq_hbm, dk_ref, dv_ref,
             dq_acc, dk_acc, dv_acc, sem):
        g = pl.program_id(0)
        j = pl.program_id(1)
        i = pl.program_id(2)

        @pl.when(jnp.logical_and(j == 0, i == 0))
        def _():
            dq_acc[...] = jnp.zeros_like(dq_acc)

        @pl.when(i == 0)
        def _():
            dk_acc[...] = jnp.zeros_like(dk_acc)
            dv_acc[...] = jnp.zeros_like(dv_acc)

        @pl.when(jnp.logical_or(i < nql, j >= nkl))
        def _():
            q = q_ref[...].reshape(cols, hd)
            do = do_ref[...].reshape(cols, hd)
            lse = jnp.concatenate([lse_ref[a] for a in range(GROUP)], axis=1)
            dl = jnp.concatenate([dl_ref[a] for a in range(GROUP)], axis=1)
            p_t = jnp.exp(_dot_nt(k_ref[...], q) - lse)
            ds_t = (p_t * (_dot_nt(v_ref[...], do) - dl)).astype(BF16)
            dv_acc[...] += _dot(p_t.astype(BF16), do)
            dk_acc[...] += _dot(ds_t, q)
            dq_t = _dot(kt_ref[...], ds_t)
            at = pl.multiple_of(i * tq, tq)
            for a in range(GROUP):
                dq_acc[a, :, pl.ds(at, tq)] += dq_t[:, a * tq:(a + 1) * tq]

        @pl.when(i == nq - 1)
        def _():
            dk_ref[...] = dk_acc[...]
            dv_ref[...] = dv_acc[...]

        @pl.when(jnp.logical_and(j == nk - 1, i == nq - 1))
        def _():
            cp = pltpu.make_async_copy(dq_acc, dq_hbm.at[pl.ds(g * GROUP, GROUP)], sem)
            cp.start()
            cp.wait()

    qspec = pl.BlockSpec((GROUP, tq, hd), lambda g, j, i: (g, i, 0))
    kspec = pl.BlockSpec((None, tk, hd), lambda g, j, i: (g, j, 0))
    rspec = pl.BlockSpec((GROUP, 1, tq), lambda g, j, i: (g, 0, i))
    return _pcall(
        body, name=name, grid=(kv, nk, nq),
        in_specs=[qspec, kspec, pl.BlockSpec((None, hd, tk), lambda g, j, i: (g, 0, j)), kspec, qspec, rspec, rspec],
        out_specs=[ANY, kspec, kspec],
        out_shape=[jax.ShapeDtypeStruct((h, hd, na), F32), jax.ShapeDtypeStruct((kv, na, hd), F32),
                   jax.ShapeDtypeStruct((kv, na, hd), F32)],
        scratch_shapes=[pltpu.VMEM((GROUP, hd, na), F32), pltpu.VMEM((tk, hd), F32), pltpu.VMEM((tk, hd), F32),
                        pltpu.SemaphoreType.DMA],
        compiler_params=_cp("arbitrary", "arbitrary", "arbitrary"),
    )(q_hm, k_hm, kt_hm, v_hm, do_hm, lse_r, dl_r)


def _qknorm_bwd(dq_tm, dk_tm, dv_tm, p, cos_t, sin_t, gq, gk, bd, tm, name):
    na = p.shape[0]
    d = gq.shape[1]
    kw = d // GROUP

    def body(dq_ref, dk_ref, dv_ref, q_ref, kv_ref, cos_ref, sin_ref, gq_ref, gk_ref, bd_ref,
             dqo_ref, dkvo_ref, acc_ref):
        @pl.when(pl.program_id(0) == 0)
        def _():
            acc_ref[...] = jnp.zeros_like(acc_ref)

        cos = cos_ref[...]
        sin = sin_ref[...]

        def back(dy, xh, g, w):
            bdw = bd_ref[0:w, 0:w]
            dn = dy * _lane_tile(cos, w) - _partner(dy) * _lane_tile(sin, w)
            rs = lax.rsqrt(_split_dot(xh * xh, bdw) * (1.0 / HEAD_DIM) + EPS)
            y = xh * rs
            dg = _colsum(dn * y)
            dyn = dn * g
            dx = rs * (dyn - y * (_split_dot(dyn * y, bdw) * (1.0 / HEAD_DIM)))
            return dx, dg

        dq, dgq = back(dq_ref[...] * ATTN_SCALE, q_ref[...], gq_ref[...], d)
        dqo_ref[...] = dq.astype(BF16)
        acc_ref[0:1, :] += dgq
        kv = kv_ref[...]
        dk, dgk = back(dk_ref[...], kv[:, 0:kw], gk_ref[...], kw)
        acc_ref[1:2, 0:kw] += dgk
        dkvo_ref[:, 0:kw] = dk.astype(BF16)
        dkvo_ref[:, kw:2 * kw] = dv_ref[...].astype(BF16)

    return _pcall(
        body, name=name, grid=(na // tm,),
        in_specs=[_rows(tm, d), _rows(tm, kw), _rows(tm, kw), _rows(tm, d, 3), _rows(tm, d // 2, 14),
                  _rows(tm, 128), _rows(tm, 128), _full((1, d)), _full((1, kw)), _full((d, d))],
        out_specs=[_rows(tm, d), _rows(tm, d // 2), _full((8, d))],
        out_shape=[jax.ShapeDtypeStruct((na, d), BF16), jax.ShapeDtypeStruct((na, d // 2), BF16),
                   jax.ShapeDtypeStruct((8, d), F32)],
        compiler_params=_cp("arbitrary"))(dq_tm, dk_tm, dv_tm, p, p, cos_t, sin_t, gq, gk, bd)


def _conv_bwd(dy2, p, conv_w, n_lat, tm, name):
    na = p.shape[0]
    ktaps, d = conv_w.shape
    pad = ktaps // 2

    def body(dy_ref, dyp_ref, dyn_ref, a_ref, ap_ref, an_ref, g_ref, gp_ref, gn_ref, w_ref,
             da_ref, dg_ref, dw_ref, dwin, ywin):
        i = pl.program_id(0)

        @pl.when(i == 0)
        def _():
            dw_ref[...] = jnp.zeros_like(dw_ref)

        first, last = _seq_ends(i, n_lat, na, tm)
        dy = dy_ref[...]
        a = a_ref[...]
        sg = _sig(g_ref[...])
        _window(dwin, dyp_ref[...], dy, dyn_ref[...], first, last, tm)
        _window(ywin, ap_ref[...] * _sig(gp_ref[...]), a * sg, an_ref[...] * _sig(gn_ref[...]), first, last, tm)
        dy1 = jnp.zeros((tm, d), F32)
        for k in range(ktaps):
            dy1 = dy1 + w_ref[k:k + 1, :] * dwin[pl.ds(HALO + pad - k, tm), :]
            dw_ref[k:k + 1, :] += _colsum(dy * ywin[pl.ds(HALO - pad + k, tm), :])
        da_ref[...] = (dy1 * sg).astype(BF16)
        dg_ref[...] = (dy1 * a * sg * (1.0 - sg)).astype(BF16)

    dyp, dyn = _halo_specs(tm, d, na, 0)
    ap, an = _halo_specs(tm, d, na, 0)
    gp, gn = _halo_specs(tm, d, na, 1)
    bfo = jax.ShapeDtypeStruct((na, d), BF16)
    return _pcall(
        body, name=name, grid=(na // tm,),
        in_specs=[_rows(tm, d), dyp, dyn, _rows(tm, d, 0), ap, an, _rows(tm, d, 1), gp, gn, _full((ktaps, d))],
        out_specs=[_rows(tm, d), _rows(tm, d), _full((ktaps, d))],
        out_shape=[bfo, bfo, jax.ShapeDtypeStruct((ktaps, d), F32)],
        scratch_shapes=[pltpu.VMEM((tm + 2 * HALO, d), F32), pltpu.VMEM((tm + 2 * HALO, d), F32)],
        compiler_params=_cp("arbitrary"))(dy2, dy2, dy2, p, p, p, p, p, p, conv_w)


def _inproj_bwd(segs, dkv, xa, dxn, modv, gpre, wp_t, n_lat, tm, name):
    na, d = xa.shape
    nseg = len(segs)
    wrows = wp_t.shape[0]

    def body(*refs):
        seg_refs = refs[:nseg]
        dkv_ref, x_ref, dxn_ref, mod_ref, g_ref, wt_hbm, dx_ref, acc_ref, wt, sem = refs[nseg:]
        i = pl.program_id(0)
        ctx_tile = i >= n_lat // tm

        @pl.when(i == 0)
        def _():
            cp = pltpu.make_async_copy(wt_hbm, wt, sem)
            cp.start()
            cp.wait()
            acc_ref[...] = jnp.zeros_like(acc_ref)

        dh = _dot(dkv_ref[...], wt[nseg * d:wrows, :])
        for s in range(nseg):
            dh = dh + _dot(seg_refs[s][...], wt[s * d:(s + 1) * d, :])
        x = x_ref[...]
        r = lax.rsqrt(_mean(x * x) + EPS)
        xn = x * r
        g = g_ref[...]
        sc1 = 1.0 + _seg_rows(mod_ref, d, 2 * d, ctx_tile)
        dsh = _colsum(dh)
        dsc = _colsum(dh * (xn * g))

        @pl.when(ctx_tile)
        def _():
            acc_ref[1:2, :] += dsh
            acc_ref[3:4, :] += dsc

        @pl.when(jnp.logical_not(ctx_tile))
        def _():
            acc_ref[0:1, :] += dsh
            acc_ref[2:3, :] += dsc

        acc_ref[4:5, :] += _colsum(dh * xn * sc1)
        dxh = dh * g * sc1
        dx_ref[...] = dxn_ref[...] + r * (dxh - xn * _mean(dxh * xn))

    r_ = _rows(tm, d)
    return _pcall(
        body, name=name, grid=(na // tm,),
        in_specs=[r_] * nseg + [_rows(tm, d // 2), r_, r_, _full((8, 3 * d)), _full((1, d)), ANY],
        out_specs=[r_, _full((8, d))],
        out_shape=[jax.ShapeDtypeStruct((na, d), F32), jax.ShapeDtypeStruct((8, d), F32)],
        scratch_shapes=[pltpu.VMEM(wp_t.shape, BF16), pltpu.SemaphoreType.DMA],
        compiler_params=_cp("arbitrary"))(*segs, dkv, xa, dxn, modv, gpre, wp_t)


def _grad_matmul(a, b, tk, name):
    na, ka = a.shape
    nb = b.shape[1]
    tn = min(nb, 512)

    def body(a_ref, b_ref, o_ref):
        @pl.when(pl.program_id(1) == 0)
        def _():
            o_ref[...] = jnp.zeros_like(o_ref)

        o_ref[...] += _dot_tn(a_ref[...], b_ref[...])

    return _pcall(
        body, name=name, grid=(nb // tn, na // tk),
        in_specs=[pl.BlockSpec((tk, ka), lambda n, k: (k, 0)), pl.BlockSpec((tk, tn), lambda n, k: (k, n))],
        out_specs=pl.BlockSpec((ka, tn), lambda n, k: (0, n)),
        out_shape=jax.ShapeDtypeStruct((ka, nb), F32),
        compiler_params=_cp("parallel", "arbitrary"))(a, b)


def _pack(parts, cols, row_mult):
    flat = jnp.concatenate([q.astype(F32).reshape(-1) for q in parts])
    rows = -(-flat.shape[0] // (cols * row_mult)) * row_mult
    return jnp.pad(flat, (0, rows * cols - flat.shape[0])).reshape(1, rows, cols)


def _unpack(flat, shapes):
    out, off = [], 0
    for s in shapes:
        n = math.prod(s)
        out.append(flat[off:off + n].reshape(tuple(s)))
        off += n
    return out


def _cols_by_dest(g):
    l, a, w8 = g.shape
    return g.reshape(l, a, N_DEV, w8 // N_DEV).transpose(2, 0, 1, 3)


def _rows_by_dest(g):
    l, r8, b = g.shape
    return g.reshape(l, N_DEV, r8 // N_DEV, b).transpose(1, 0, 2, 3)


def _cols_from_src(s):
    n, l, a, w = s.shape
    return s.transpose(1, 2, 0, 3).reshape(l, a, n * w)


def _rows_from_src(s):
    n, l, r, b = s.shape
    return s.transpose(1, 0, 2, 3).reshape(l, n * r, b)


def _rope_tables(n_lat, n_ctx):
    half = HEAD_DIM // 2
    rows = n_lat // GRID_W
    row = jnp.repeat(jnp.arange(rows, dtype=F32), GRID_W)
    col = jnp.tile(jnp.arange(GRID_W, dtype=F32), rows)
    inv_freq = ROPE_THETA ** (-jnp.arange(0, half, 2, dtype=F32) / half)
    ang = jnp.concatenate([row[:, None] * inv_freq, col[:, None] * inv_freq], axis=-1)
    cos, sin = jnp.cos(ang), jnp.sin(ang)
    cos_t = jnp.concatenate([jnp.tile(cos, (1, 4)), jnp.ones((n_ctx, 128), F32)], axis=0)
    sin_t = jnp.concatenate([jnp.tile(jnp.concatenate([-sin, sin], axis=-1), (1, 2)),
                             jnp.zeros((n_ctx, 128), F32)], axis=0)
    return cos_t, sin_t


def _to_heads(t):
    na, w = t.shape
    return t.reshape(na, w // HEAD_DIM, HEAD_DIM).transpose(1, 0, 2)


def _from_heads(t):
    nh, na, hd = t.shape
    return t.transpose(1, 0, 2).reshape(na, nh * hd)


def kernel(x, c, ctx, c_ctx, w_mod, b_mod, g_pre, g_post, w_in, conv_w, conv_b, ln_g, ln_b, w_conv_out, q_norm_g, k_norm_g, w_attn_out, w_out, loss_target, m_c_ctx, m_w_mod, m_b_mod, m_g_pre, m_g_post, m_w_in, m_conv_w, m_conv_b, m_ln_g, m_ln_b, m_w_conv_out, m_q_norm_g, m_k_norm_g, m_w_attn_out, m_w_out, v_c_ctx, v_w_mod, v_b_mod, v_g_pre, v_g_post, v_w_in, v_conv_w, v_conv_b, v_ln_g, v_ln_b, v_w_conv_out, v_q_norm_g, v_k_norm_g, v_w_attn_out, v_w_out):
    depth, d, _ = w_mod.shape
    n_lat, n_ctx = x.shape[1], ctx.shape[1]
    na = n_lat + n_ctx
    heads = d // HEAD_DIM
    kw = d // GROUP
    ktaps = conv_w.shape[1]
    tm = n_ctx
    tm_half = tm // 2
    tbig = 3 * tm if na % (3 * tm) == 0 else tm

    s_mod, s_in, s_co, s_ao, s_oo, s_cw = _all_gather(
        [w_mod.astype(BF16), w_in.astype(BF16), w_conv_out.astype(BF16), w_attn_out.astype(BF16),
         w_out.astype(BF16), conv_w], "gather_weights")
    wmod_f = _cols_from_src(s_mod)
    win_f = _cols_from_src(s_in)
    wc_f, wa_f, wo_f = _rows_from_src(s_co), _rows_from_src(s_ao), _rows_from_src(s_oo)
    wp_f = jnp.concatenate([win_f[:, :, :4 * d], win_f[:, :, 4 * d + 2 * kw:], win_f[:, :, 4 * d:4 * d + 2 * kw]], axis=2)
    convw_f = _cols_from_src(s_cw)

    cos_t, sin_t = _rope_tables(n_lat, n_ctx)
    lane = jnp.arange(d)
    bd = (lane[:, None] // HEAD_DIM == lane[None, :] // HEAD_DIM).astype(BF16)
    esel = (lane[:, None] // HEAD_DIM == jnp.arange(128)[None, :]).astype(BF16)
    cvec = jnp.zeros((8, d), F32).at[0].set(c[0]).at[1].set(c_ctx)
    cvec_t = jnp.zeros((d, 128), F32).at[:, 0].set(c[0]).at[:, 1].set(c_ctx)

    xa = jnp.concatenate([x[0], ctx[0]], axis=0)

    saved = []
    for l in range(depth):
        tag = f"_l{l}"
        gq = jnp.tile(q_norm_g[l], heads)[None, :]
        gk = jnp.tile(k_norm_g[l], heads // GROUP)[None, :]
        modv = _mod_fwd(cvec, wmod_f[l], b_mod[l][None, :], "mod_fwd" + tag)
        p, hb = _inproj(xa, modv, g_pre[l][None, :], wp_f[l], n_lat, tbig, "inproj" + tag)
        qr, kr, vb = _qknorm_fwd(p, cos_t, sin_t, gq, gk, bd, tm, "qknorm_fwd" + tag)
        q_hm, k_hm, v_hm = _to_heads(qr), _to_heads(kr), _to_heads(vb)
        ot_hm, lse = _attn_fwd(q_hm, k_hm, v_hm.transpose(0, 2, 1), n_lat, tm, tm, "attn_fwd" + tag)
        o_tm = ot_hm.transpose(2, 0, 1).reshape(na, d)
        y5, y2 = _conv_fwd(p, convw_f[l], conv_b[l][None, :], ln_g[l][None, :], ln_b[l][None, :], n_lat, tm,
                           "conv_fwd" + tag)
        xa_new, yc, ya, out, z, og = _merge_fwd(y5, o_tm, p, xa, modv, g_post[l][None, :], wc_f[l], wa_f[l], wo_f[l],
                                                n_lat, tm, "merge_fwd" + tag)
        saved.append(dict(xa=xa, modv=modv, p=p, hb=hb, q_hm=q_hm, k_hm=k_hm, v_hm=v_hm, o_tm=o_tm, lse=lse,
                          y5=y5, y2=y2, yc=yc, ya=ya, out=out, z=z, og=og, gq=gq, gk=gk))
        xa = xa_new

    dxa, loss_blk = _loss_grad(xa, loss_target[0], n_lat, tm, "loss_grad")

    g_wmod, g_win, g_convw, g_wc, g_wa, g_wo = [], [], [], [], [], []
    g_bmod, g_gpre, g_gpost, g_convb, g_lng, g_lnb, g_qg, g_kg = [], [], [], [], [], [], [], []
    g_cctx = jnp.zeros((d,), F32)
    for l in reversed(range(depth)):
        tag = f"_l{l}"
        s = saved[l]
        p = s["p"]
        (dgb, dma, dmb, dgta, do_tm, dout, dyc, dya, dy2, dl128, acc_m) = _merge_bwd(
            dxa, s["out"], s["yc"], s["ya"], s["o_tm"], p, s["y2"], s["modv"], g_post[l][None, :],
            ln_g[l][None, :], ln_b[l][None, :], wo_f[l].T, wc_f[l].T, wa_f[l].T, esel, n_lat, tm_half,
            "merge_bwd" + tag)
        lse_r = s["lse"].reshape(heads, 1, na)
        dl_r = dl128[:, :heads].T.reshape(heads, 1, na)
        kt_hm = s["k_hm"].transpose(0, 2, 1)
        dqt_hm, dk_hm, dv_hm = _attn_bwd(s["q_hm"], s["k_hm"], kt_hm, s["v_hm"], _to_heads(do_tm), lse_r, dl_r,
                                         n_lat, tm, tm, "attn_bwd" + tag)
        dq_tm = dqt_hm.transpose(2, 0, 1).reshape(na, d)
        dq, dkv, acc_q = _qknorm_bwd(dq_tm, _from_heads(dk_hm), _from_heads(dv_hm), p, cos_t, sin_t,
                                     s["gq"], s["gk"], bd, tm, "qknorm_bwd" + tag)
        da, dg, dconvw = _conv_bwd(dy2, p, convw_f[l], n_lat, tm, "conv_bwd" + tag)
        segs = [da, dg, dgta, dq, dgb, dma, dmb]
        dxa, acc_h = _inproj_bwd(segs, dkv, s["xa"], dxa, s["modv"], g_pre[l][None, :], wp_f[l].T, n_lat, tm,
                                 "inproj_bwd" + tag)

        dwp = [_grad_matmul(s["hb"], sg, tbig, f"grad_w_in{k}" + tag) for k, sg in enumerate(segs + [dkv])]
        g_win.append(jnp.concatenate(dwp[:4] + [dwp[7]] + dwp[4:7], axis=1))
        g_wc.append(_grad_matmul(s["y5"], dyc, tbig, "grad_w_conv_out" + tag))
        g_wa.append(_grad_matmul(s["og"], dya, tbig, "grad_w_attn_out" + tag))
        g_wo.append(_grad_matmul(s["z"], dout, tbig, "grad_w_out" + tag))
        g_convw.append(dconvw)

        dmod = jnp.zeros((8, 3 * d), F32)
        dmod = dmod.at[0].set(jnp.concatenate([acc_h[0], acc_h[2], acc_m[0]]))
        dmod = dmod.at[1].set(jnp.concatenate([acc_h[1], acc_h[3], acc_m[1]]))
        dwm, dbm, dcv = _mod_bwd(dmod, cvec, cvec_t, wmod_f[l], "mod_bwd" + tag)
        g_wmod.append(dwm)
        g_bmod.append(dbm[0])
        g_cctx = g_cctx + dcv[1]
        g_gpre.append(acc_h[4])
        g_gpost.append(acc_m[2])
        g_lng.append(acc_m[3])
        g_lnb.append(acc_m[4])
        g_convb.append(acc_m[5])
        g_qg.append(acc_q[0].reshape(heads, HEAD_DIM).sum(0))
        g_kg.append(acc_q[1, :kw].reshape(heads // GROUP, HEAD_DIM).sum(0))

    grad_x = dxa[:n_lat][None]

    def stack(lst):
        return jnp.stack(lst[::-1])

    big_names = ["w_mod", "w_in", "w_conv_out", "w_attn_out", "w_out", "conv_w"]
    big_w = dict(w_mod=w_mod, w_in=w_in, w_conv_out=w_conv_out, w_attn_out=w_attn_out, w_out=w_out, conv_w=conv_w)
    big_m = dict(w_mod=m_w_mod, w_in=m_w_in, w_conv_out=m_w_conv_out, w_attn_out=m_w_attn_out, w_out=m_w_out,
                 conv_w=m_conv_w)
    big_v = dict(w_mod=v_w_mod, w_in=v_w_in, w_conv_out=v_w_conv_out, w_attn_out=v_w_attn_out, w_out=v_w_out,
                 conv_w=v_conv_w)
    by_dest = [_cols_by_dest(stack(g_wmod)), _cols_by_dest(stack(g_win)), _rows_by_dest(stack(g_wc)),
               _rows_by_dest(stack(g_wa)), _rows_by_dest(stack(g_wo)), _cols_by_dest(stack(g_convw))]
    send = [q.reshape((4, 2) + q.shape[1:]) for q in by_dest]
    from_sibling = _swap_with_sibling(send, "reduce_sibling")
    chip_part = [_add_sibling_part(sb, rc, "reduce_sibling_add_" + n)
                 for n, sb, rc in zip(big_names, send, from_sibling)]
    from_chips = _exchange_chips(chip_part, "reduce_chips")
    big_g, big_d, big_nm, big_nv = {}, {}, {}, {}
    for n, st in zip(big_names, from_chips):
        big_g[n], big_d[n], big_nm[n], big_nv[n] = _sum_adamw(st, big_w[n], big_m[n], big_v[n], "adamw_" + n)

    small_names = ["c_ctx", "b_mod", "g_pre", "g_post", "conv_b", "ln_g", "ln_b", "q_norm_g", "k_norm_g", "loss"]
    zero1 = jnp.zeros((1,), F32)
    small_w = dict(c_ctx=c_ctx, b_mod=b_mod, g_pre=g_pre, g_post=g_post, conv_b=conv_b, ln_g=ln_g, ln_b=ln_b,
                   q_norm_g=q_norm_g, k_norm_g=k_norm_g, loss=zero1)
    small_m = dict(c_ctx=m_c_ctx, b_mod=m_b_mod, g_pre=m_g_pre, g_post=m_g_post, conv_b=m_conv_b, ln_g=m_ln_g,
                   ln_b=m_ln_b, q_norm_g=m_q_norm_g, k_norm_g=m_k_norm_g, loss=zero1)
    small_v = dict(c_ctx=v_c_ctx, b_mod=v_b_mod, g_pre=v_g_pre, g_post=v_g_post, conv_b=v_conv_b, ln_g=v_ln_g,
                   ln_b=v_ln_b, q_norm_g=v_q_norm_g, k_norm_g=v_k_norm_g, loss=zero1)
    small_g = dict(c_ctx=g_cctx, b_mod=stack(g_bmod), g_pre=stack(g_gpre), g_post=stack(g_gpost),
                   conv_b=stack(g_convb), ln_g=stack(g_lng), ln_b=stack(g_lnb), q_norm_g=stack(g_qg),
                   k_norm_g=stack(g_kg), loss=loss_blk[0, 0:1])
    small_shapes = [small_w[n].shape for n in small_names]

    def pack_small(tree):
        return _pack([tree[n] for n in small_names], d, 8)

    small_parts, = _all_gather([pack_small(small_g)], "gather_small_grads")
    small_out = _sum_adamw(small_parts, pack_small(small_w), pack_small(small_m), pack_small(small_v),
                           "adamw_replicated")
    sm_g, sm_d, sm_nm, sm_nv = [dict(zip(small_names, _unpack(o.reshape(-1), small_shapes))) for o in small_out]
    loss = sm_g["loss"].reshape(())

    order = ["c_ctx", "w_mod", "b_mod", "g_pre", "g_post", "w_in", "conv_w", "conv_b", "ln_g", "ln_b",
             "w_conv_out", "q_norm_g", "k_norm_g", "w_attn_out", "w_out"]

    def pick(big, small):
        return [big[n] if n in big else small[n] for n in order]

    return (loss, grad_x, *pick(big_g, sm_g), *pick(big_d, sm_d), *pick(big_nm, sm_nm), *pick(big_nv, sm_nv))
```

```python
import math

import jax
import jax.numpy as jnp
from jax import lax
from jax.experimental import pallas as pl
from jax.experimental.pallas import tpu as pltpu

F32 = jnp.float32
BF16 = jnp.bfloat16

HEAD_DIM = 64
GROUP = 4
GRID_W = 64
ROPE_THETA = 10000.0
EPS = 1e-6
ATTN_SCALE = HEAD_DIM ** -0.5
HALO = 16
ATTN_AHEAD = 3

ADAM_LR = 0.001
ADAM_B1 = 0.9
ADAM_B2 = 0.999
ADAM_EPS = 1e-08
ADAM_WD = 0.01
ADAM_STEP = 10

N_DEV = 8
MESH_AXES = ("x", "y", "c")
V7X_VMEM_LIMIT = 56 * 1024 * 1024
NEG_BIG = -1e30

MESH = pl.DeviceIdType.MESH
ANY = pl.BlockSpec(memory_space=pl.ANY)


def _pcall(body, **kw):
    return pl.pallas_call(body, **kw)


def _cp(*sem):
    return pltpu.CompilerParams(dimension_semantics=sem, vmem_limit_bytes=V7X_VMEM_LIMIT)


def _sig(x):
    return 1.0 / (1.0 + jnp.exp(-x))


def _mean(x):
    return jnp.mean(x, axis=-1, keepdims=True)


def _colsum(x):
    return jnp.sum(x, axis=0, keepdims=True)


def _bf_round(x):
    return x.astype(BF16).astype(F32)


def _dot(a, b):
    return jnp.dot(a, b, preferred_element_type=F32)


def _dot_nt(a, b):
    return lax.dot_general(a, b, (((1,), (1,)), ((), ())), preferred_element_type=F32)


def _dot_tn(a, b):
    return lax.dot_general(a, b, (((0,), (0,)), ((), ())), preferred_element_type=F32)


def _split_dot(x, m):
    hi = x.astype(BF16)
    lo = (x - hi.astype(F32)).astype(BF16)
    return _dot(hi, m) + _dot(lo, m)


def _full(shape):
    nd = len(shape)
    return pl.BlockSpec(shape, lambda *_: (0,) * nd)


def _rows(tm, width, colblk=0):
    return pl.BlockSpec((tm, width), lambda i: (i, colblk))


def _my_place():
    return lax.axis_index("x"), lax.axis_index("y"), lax.axis_index("c")


def _sem_arrays(n):
    return [pltpu.SemaphoreType.DMA((n,)), pltpu.SemaphoreType.DMA((n,))]


def _all_gather(shards, name):
    n = len(shards)

    def body(*refs):
        x_refs, out_refs = refs[:n], refs[n:2 * n]
        send_sems, recv_sems, local_sems = refs[2 * n:]
        x, y, c = _my_place()
        me, sibling = (x, y, c), (x, y, 1 - c)
        chips = [(1 - x, y), (x, 1 - y), (1 - x, 1 - y)]

        def slab(a, px, py, pc):
            return out_refs[a].at[4 * px + 2 * py + pc]

        def copies(k, block, to, from_input=False):
            return [pltpu.make_async_remote_copy(
                src_ref=x_refs[a] if from_input else slab(a, *block), dst_ref=slab(a, *block),
                send_sem=send_sems.at[k * n + a], recv_sem=recv_sems.at[k * n + a],
                device_id=to, device_id_type=MESH) for a in range(n)]

        mine = [pltpu.make_async_copy(x_refs[a], slab(a, *me), local_sems.at[a]) for a in range(n)]
        for cp in mine:
            cp.start()
        first = copies(0, me, sibling, True)
        for j, chip in enumerate(chips):
            first += copies(1 + j, me, (*chip, c), True)
        for cp in first:
            cp.start()
        passed = []
        for j, chip in enumerate(chips):
            for cp in copies(1 + j, (*chip, c), me):
                cp.wait_recv()
            onward = copies(4 + j, (*chip, c), sibling)
            for cp in onward:
                cp.start()
            passed += onward
        for cp in copies(0, sibling, me):
            cp.wait_recv()
        for j, chip in enumerate(chips):
            for cp in copies(4 + j, (*chip, 1 - c), me):
                cp.wait_recv()
        for cp in first + passed:
            cp.wait_send()
        for cp in mine:
            cp.wait()

    return _pcall(
        body, name=name,
        out_shape=[jax.ShapeDtypeStruct((N_DEV,) + q.shape, q.dtype) for q in shards],
        in_specs=[ANY] * n, out_specs=[ANY] * n,
        scratch_shapes=_sem_arrays(7 * n) + [pltpu.SemaphoreType.DMA((n,))],
    )(*shards)


def _swap_with_sibling(bufs, name):
    n = len(bufs)

    def body(*refs):
        buf_refs, recv_refs = refs[:n], refs[n:2 * n]
        send_sems, recv_sems = refs[2 * n:]
        x, y, c = _my_place()
        copies = [
            pltpu.make_async_remote_copy(
                src_ref=buf_refs[a].at[k, 1 - c], dst_ref=recv_refs[a].at[k],
                send_sem=send_sems.at[k * n + a], recv_sem=recv_sems.at[k * n + a],
                device_id=(x, y, 1 - c), device_id_type=MESH)
            for k in range(4) for a in range(n)]
        for cp in copies:
            cp.start()
        for cp in copies:
            cp.wait()

    return _pcall(
        body, name=name,
        out_shape=[jax.ShapeDtypeStruct((4,) + q.shape[2:], q.dtype) for q in bufs],
        in_specs=[ANY] * n, out_specs=[ANY] * n,
        scratch_shapes=_sem_arrays(4 * n),
    )(*bufs)


def _exchange_chips(parts, name):
    n = len(parts)

    def body(*refs):
        s_refs, recv_refs = refs[:n], refs[n:2 * n]
        send_sems, recv_sems, local_sems = refs[2 * n:]
        x, y, c = _my_place()
        mychip = 2 * x + y
        chips = [(1 - x, y), (x, 1 - y), (1 - x, 1 - y)]
        mine = [pltpu.make_async_copy(s_refs[a].at[mychip], recv_refs[a].at[mychip], local_sems.at[a])
                for a in range(n)]
        for cp in mine:
            cp.start()
        copies = [
            pltpu.make_async_remote_copy(
                src_ref=s_refs[a].at[2 * px + py], dst_ref=recv_refs[a].at[mychip],
                send_sem=send_sems.at[j * n + a], recv_sem=recv_sems.at[j * n + a],
                device_id=(px, py, c), device_id_type=MESH)
            for j, (px, py) in enumerate(chips) for a in range(n)]
        for cp in copies:
            cp.start()
        for cp in copies:
            cp.wait()
        for cp in mine:
            cp.wait()

    return _pcall(
        body, name=name,
        out_shape=[jax.ShapeDtypeStruct(q.shape, q.dtype) for q in parts],
        in_specs=[ANY] * n, out_specs=[ANY] * n,
        scratch_shapes=_sem_arrays(3 * n) + [pltpu.SemaphoreType.DMA((n,))],
    )(*parts)


def _row_tile(a):
    for t in range(256, 7, -8):
        if a % t == 0:
            return t
    return a


def _add_sibling_part(buf, recv, name):
    _, _, nl, a, b = buf.shape
    ta = _row_tile(a)
    core = lax.axis_index("c").astype(jnp.int32).reshape(1)

    def body(core_ref, a_ref, b_ref, o_ref):
        o_ref[...] = a_ref[...] + b_ref[...]

    grid_spec = pltpu.PrefetchScalarGridSpec(
        num_scalar_prefetch=1, grid=(4, nl, a // ta),
        in_specs=[pl.BlockSpec((None, None, None, ta, b), lambda k, l, r, cr: (k, cr[0], l, r, 0)),
                  pl.BlockSpec((None, None, ta, b), lambda k, l, r, cr: (k, l, r, 0))],
        out_specs=pl.BlockSpec((None, None, ta, b), lambda k, l, r, cr: (k, l, r, 0)))
    return _pcall(body, name=name, grid_spec=grid_spec,
                  out_shape=jax.ShapeDtypeStruct((4, nl, a, b), F32),
                  compiler_params=_cp("parallel", "parallel", "parallel"))(core, buf, recv)


def _sum_adamw(stack, w, m, v, name):
    ns, nl, a, b = stack.shape
    ta = _row_tile(a)
    c1 = 1.0 - ADAM_B1 ** ADAM_STEP
    c2 = 1.0 - ADAM_B2 ** ADAM_STEP

    def body(s_ref, w_ref, m_ref, v_ref, g_out, d_out, m_out, v_out):
        g = s_ref[0]
        for k in range(1, ns):
            g = g + s_ref[k]
        m_new = ADAM_B1 * m_ref[...] + (1.0 - ADAM_B1) * g
        v_new = ADAM_B2 * v_ref[...] + (1.0 - ADAM_B2) * (g * g)
        m_hat = m_new / c1
        v_hat = v_new / c2
        g_out[...] = g
        d_out[...] = -ADAM_LR * (m_hat / (jnp.sqrt(v_hat) + ADAM_EPS) + ADAM_WD * w_ref[...])
        m_out[...] = m_new
        v_out[...] = v_new

    blk = pl.BlockSpec((None, ta, b), lambda l, i: (l, i, 0))
    return _pcall(
        body, name=name, grid=(nl, a // ta),
        in_specs=[pl.BlockSpec((ns, None, ta, b), lambda l, i: (0, l, i, 0)), blk, blk, blk],
        out_specs=[blk] * 4,
        out_shape=[jax.ShapeDtypeStruct((nl, a, b), F32)] * 4,
        compiler_params=_cp("parallel", "parallel"))(stack, w, m, v)


def _mod_fwd(cvec, wmod, bmod, name):
    _, d = cvec.shape

    def body(c_ref, w_ref, b_ref, o_ref):
        cv = c_ref[...]
        cs = cv * _sig(cv)
        o_ref[...] = _dot(cs.astype(BF16), w_ref[...]) + b_ref[...]

    return _pcall(
        body, name=name, grid=(3,),
        in_specs=[_full((8, d)), pl.BlockSpec((d, d), lambda n: (0, n)), pl.BlockSpec((1, d), lambda n: (0, n))],
        out_specs=pl.BlockSpec((8, d), lambda n: (0, n)),
        out_shape=jax.ShapeDtypeStruct((8, 3 * d), F32),
        compiler_params=_cp("parallel"))(cvec, wmod, bmod)


def _mod_bwd(dmod, cvec, cvec_t, wmod, name):
    _, d = cvec.shape

    def body(dm_ref, c_ref, ct_ref, w_ref, dw_ref, db_ref, dc_ref):
        n = pl.program_id(0)
        dm = dm_ref[...]
        ct = ct_ref[...]
        cs_t = _bf_round(ct * _sig(ct))
        d0 = _bf_round(dm[0:1, :])
        d1 = _bf_round(dm[1:2, :])
        dw_ref[...] = cs_t[:, 0:1] * d0 + cs_t[:, 1:2] * d1
        db_ref[...] = dm[0:1, :] + dm[1:2, :]

        @pl.when(n == 0)
        def _():
            dc_ref[...] = jnp.zeros_like(dc_ref)

        dc_ref[...] += _dot_nt(dm.astype(BF16), w_ref[...])

        @pl.when(n == 2)
        def _():
            cv = c_ref[...]
            s = _sig(cv)
            dc_ref[...] = dc_ref[...] * (s * (1.0 + cv * (1.0 - s)))

    return _pcall(
        body, name=name, grid=(3,),
        in_specs=[pl.BlockSpec((8, d), lambda n: (0, n)), _full((8, d)), _full((d, 128)),
                  pl.BlockSpec((d, d), lambda n: (0, n))],
        out_specs=[pl.BlockSpec((d, d), lambda n: (0, n)), pl.BlockSpec((1, d), lambda n: (0, n)),
                   _full((8, d))],
        out_shape=[jax.ShapeDtypeStruct((d, 3 * d), F32), jax.ShapeDtypeStruct((1, 3 * d), F32),
                   jax.ShapeDtypeStruct((8, d), F32)],
        compiler_params=_cp("arbitrary"))(dmod, cvec, cvec_t, wmod)


def _seg_rows(mod_ref, lo, hi, is_ctx):
    return jnp.where(is_ctx, mod_ref[1:2, lo:hi], mod_ref[0:1, lo:hi])


def _inproj(xa, modv, gpre, wp, n_lat, tm, name):
    na, d = xa.shape
    wcols = wp.shape[1]
    tn = d // 2

    def body(x_ref, mod_ref, g_ref, w_ref, p_ref, h_ref, h_s):
        i = pl.program_id(0)

        @pl.when(pl.program_id(1) == 0)
        def _():
            x = x_ref[...]
            r = lax.rsqrt(_mean(x * x) + EPS)
            row = i * tm + lax.broadcasted_iota(jnp.int32, (tm, 1), 0)
            is_ctx = row >= n_lat
            sh = _seg_rows(mod_ref, 0, d, is_ctx)
            sc = _seg_rows(mod_ref, d, 2 * d, is_ctx)
            hb = ((x * r * g_ref[...]) * (1.0 + sc) + sh).astype(BF16)
            h_s[...] = hb
            h_ref[...] = hb

        p_ref[...] = _dot(h_s[...], w_ref[...])

    return _pcall(
        body, name=name, grid=(na // tm, wcols // tn),
        in_specs=[pl.BlockSpec((tm, d), lambda i, j: (i, 0)), _full((8, 3 * d)), _full((1, d)),
                  pl.BlockSpec((d, tn), lambda i, j: (0, j))],
        out_specs=[pl.BlockSpec((tm, tn), lambda i, j: (i, j)), pl.BlockSpec((tm, d), lambda i, j: (i, 0))],
        out_shape=[jax.ShapeDtypeStruct((na, wcols), F32), jax.ShapeDtypeStruct((na, d), BF16)],
        scratch_shapes=[pltpu.VMEM((tm, d), BF16)],
        compiler_params=_cp("parallel", "arbitrary"))(xa, modv, gpre, wp)


def _lane_tile(t, width):
    if width >= 128:
        return jnp.tile(t, (1, width // 128))
    return t[:, :width]


def _partner(x):
    w = x.shape[-1]
    lane = lax.broadcasted_iota(jnp.int32, x.shape, 1)
    low = (lane % HEAD_DIM) < (HEAD_DIM // 2)
    return jnp.where(low, pltpu.roll(x, w - HEAD_DIM // 2, 1), pltpu.roll(x, HEAD_DIM // 2, 1))


def _qknorm_fwd(p, cos_t, sin_t, gq, gk, bd, tm, name):
    na = p.shape[0]
    d = gq.shape[1]
    kw = d // GROUP

    def body(q_ref, kv_ref, cos_ref, sin_ref, gq_ref, gk_ref, bd_ref, qo_ref, ko_ref, vo_ref):
        cos = cos_ref[...]
        sin = sin_ref[...]

        def norm_rope(xh, g, w):
            ms = _split_dot(xh * xh, bd_ref[0:w, 0:w]) * (1.0 / HEAD_DIM)
            xn = xh * lax.rsqrt(ms + EPS) * g
            return xn * _lane_tile(cos, w) + _partner(xn) * _lane_tile(sin, w)

        qo_ref[...] = (norm_rope(q_ref[...], gq_ref[...], d) * ATTN_SCALE).astype(BF16)
        kv = kv_ref[...]
        ko_ref[...] = norm_rope(kv[:, 0:kw], gk_ref[...], kw).astype(BF16)
        vo_ref[...] = kv[:, kw:2 * kw].astype(BF16)

    return _pcall(
        body, name=name, grid=(na // tm,),
        in_specs=[_rows(tm, d, 3), _rows(tm, d // 2, 14), _rows(tm, 128), _rows(tm, 128),
                  _full((1, d)), _full((1, kw)), _full((d, d))],
        out_specs=[_rows(tm, d), _rows(tm, kw), _rows(tm, kw)],
        out_shape=[jax.ShapeDtypeStruct((na, d), BF16), jax.ShapeDtypeStruct((na, kw), BF16),
                   jax.ShapeDtypeStruct((na, kw), BF16)],
        compiler_params=_cp("parallel"))(p, p, cos_t, sin_t, gq, gk, bd)


def _attn_fwd(q_hm, k_hm, vt_hm, n_lat, tq, tk, cw, name):
    h, na, hd = q_hm.shape
    kv = k_hm.shape[0]
    vrows = vt_hm.shape[1]
    nq, nk = na // tq, na // tk
    nsub = tq // cw
    chains = [(a, u) for a in range(GROUP) for u in range(nsub)]
    first_ctx_block, first_ctx_chunk = n_lat // cw, n_lat // tk
    ctx_subs = {b % nsub for b in range(first_ctx_block, na // cw)}

    def body(q_ref, k_ref, vt_ref, o_ref, lse_ref, m_s, acc_s):
        i = pl.program_id(1)
        j = pl.program_id(2)

        @pl.when(j == 0)
        def _():
            m_s[...] = jnp.full_like(m_s, NEG_BIG)
            acc_s[...] = jnp.zeros_like(acc_s)

        k = k_ref[...]
        vt = vt_ref[...]

        def scores(n):
            a, u = chains[n]
            s_t = _dot_nt(k, q_ref[a, u * cw:(u + 1) * cw, :])
            if u not in ctx_subs:
                return s_t
            hidden = jnp.logical_and(i * nsub + u >= first_ctx_block, j < first_ctx_chunk)
            return s_t + jnp.where(hidden, NEG_BIG, 0.0)

        s_tiles = {n: scores(n) for n in range(min(ATTN_AHEAD, len(chains)))}
        pending = None
        for n, (a, u) in enumerate(chains):
            cols = slice(u * cw, (u + 1) * cw)
            s_t = s_tiles.pop(n)
            m_prev = m_s[a, :, cols]
            m_new = jnp.maximum(m_prev, jnp.max(s_t, axis=0, keepdims=True))
            m_s[a, :, cols] = m_new
            pv = _dot(vt, jnp.exp(s_t - m_new).astype(BF16))
            if pending is not None:
                pa, pcols, palpha, ppv = pending
                acc_s[pa, :, pcols] = palpha * acc_s[pa, :, pcols] + ppv
            pending = (a, cols, jnp.exp(m_prev - m_new), pv)
            if n + ATTN_AHEAD < len(chains):
                s_tiles[n + ATTN_AHEAD] = scores(n + ATTN_AHEAD)
        pa, pcols, palpha, ppv = pending
        acc_s[pa, :, pcols] = palpha * acc_s[pa, :, pcols] + ppv

        @pl.when(j == nk - 1)
        def _():
            for a in range(GROUP):
                acc = acc_s[a]
                l = acc[hd:hd + 1, :]
                o_ref[a] = acc[0:hd, :] / l
                lse_ref[a] = m_s[a] + jnp.log(l)

    return _pcall(
        body, name=name, grid=(kv, nq, nk),
        in_specs=[pl.BlockSpec((GROUP, tq, hd), lambda g, i, j: (g, i, 0)),
                  pl.BlockSpec((None, tk, hd), lambda g, i, j: (g, j, 0)),
                  pl.BlockSpec((None, vrows, tk), lambda g, i, j: (g, 0, j))],
        out_specs=[pl.BlockSpec((GROUP, hd, tq), lambda g, i, j: (g, 0, i)),
                   pl.BlockSpec((GROUP, 1, tq), lambda g, i, j: (g, 0, i))],
        out_shape=[jax.ShapeDtypeStruct((h, hd, na), F32), jax.ShapeDtypeStruct((h, 1, na), F32)],
        scratch_shapes=[pltpu.VMEM((GROUP, 1, tq), F32), pltpu.VMEM((GROUP, vrows, tq), F32)],
        compiler_params=_cp("parallel", "parallel", "arbitrary"))(q_hm, k_hm, vt_hm)


def _window(win_ref, prev, cur, nxt, first, last, tm):
    win_ref[0:HALO, :] = jnp.where(first, 0.0, prev)
    win_ref[HALO:HALO + tm, :] = cur
    win_ref[HALO + tm:HALO + tm + HALO, :] = jnp.where(last, 0.0, nxt)


def _halo_specs(tm, d, na, colblk):
    per = tm // HALO
    last_blk = na // HALO - 1
    prev = pl.BlockSpec((HALO, d), lambda i: (jnp.maximum(i * per - 1, 0), colblk))
    nxt = pl.BlockSpec((HALO, d), lambda i: (jnp.minimum((i + 1) * per, last_blk), colblk))
    return prev, nxt


def _seq_ends(i, n_lat, na, tm):
    first = jnp.logical_or(i == 0, i == n_lat // tm)
    last = jnp.logical_or(i == n_lat // tm - 1, i == na // tm - 1)
    return first, last


def _conv_fwd(p, conv_w, conv_b, ln_g, ln_b, n_lat, tm, name):
    na = p.shape[0]
    ktaps, d = conv_w.shape
    pad = ktaps // 2

    def body(a_ref, ap_ref, an_ref, g_ref, gp_ref, gn_ref, ga_ref, w_ref, cb_ref, lg_ref, lb_ref,
             y5_ref, y2_ref, win):
        i = pl.program_id(0)
        first, last = _seq_ends(i, n_lat, na, tm)
        _window(win, ap_ref[...] * _sig(gp_ref[...]), a_ref[...] * _sig(g_ref[...]),
                an_ref[...] * _sig(gn_ref[...]), first, last, tm)
        acc = jnp.zeros((tm, d), F32)
        for k in range(ktaps):
            acc = acc + w_ref[k:k + 1, :] * win[pl.ds(HALO - pad + k, tm), :]
        y2 = acc + cb_ref[...]
        y2_ref[...] = y2
        xc = y2 - _mean(y2)
        y3 = xc * lax.rsqrt(_mean(xc * xc) + EPS) * lg_ref[...] + lb_ref[...]
        gate = ga_ref[...]
        y5_ref[...] = ((y3 * _sig(y3)) * (gate * _sig(gate))).astype(BF16)

    ap, an = _halo_specs(tm, d, na, 0)
    gp, gn = _halo_specs(tm, d, na, 1)
    return _pcall(
        body, name=name, grid=(na // tm,),
        in_specs=[_rows(tm, d, 0), ap, an, _rows(tm, d, 1), gp, gn, _rows(tm, d, 2),
                  _full((ktaps, d)), _full((1, d)), _full((1, d)), _full((1, d))],
        out_specs=[_rows(tm, d), _rows(tm, d)],
        out_shape=[jax.ShapeDtypeStruct((na, d), BF16), jax.ShapeDtypeStruct((na, d), F32)],
        scratch_shapes=[pltpu.VMEM((tm + 2 * HALO, d), F32)],
        compiler_params=_cp("parallel"))(p, p, p, p, p, p, p, conv_w, conv_b, ln_g, ln_b)


def _merge_fwd(y5, o_tm, p, xa, modv, gpost, wc, wa, wo, n_lat, tm, name):
    na, d = xa.shape

    def body(y5_ref, o_ref, gb_ref, ma_ref, mb_ref, x_ref, mod_ref, gp_ref, wc_ref, wa_ref, wo_ref,
             xn_ref, yc_ref, ya_ref, out_ref, z_ref, og_ref):
        is_ctx = pl.program_id(0) >= n_lat // tm
        gate_b = gb_ref[...]
        og = (o_ref[...] * (gate_b * _sig(gate_b))).astype(BF16)
        og_ref[...] = og
        yc = _dot(y5_ref[...], wc_ref[...])
        ya = _dot(og, wa_ref[...])
        yc_ref[...] = yc
        ya_ref[...] = ya
        z = (_sig(ma_ref[...]) * yc + _sig(mb_ref[...]) * ya).astype(BF16)
        z_ref[...] = z
        out = _dot(z, wo_ref[...])
        out_ref[...] = out
        gt = _seg_rows(mod_ref, 2 * d, 3 * d, is_ctx)
        xn_ref[...] = x_ref[...] + gt * (out * lax.rsqrt(_mean(out * out) + EPS) * gp_ref[...])

    f32o = jax.ShapeDtypeStruct((na, d), F32)
    bfo = jax.ShapeDtypeStruct((na, d), BF16)
    return _pcall(
        body, name=name, grid=(na // tm,),
        in_specs=[_rows(tm, d), _rows(tm, d), _rows(tm, d, 4), _rows(tm, d, 5), _rows(tm, d, 6), _rows(tm, d),
                  _full((8, 3 * d)), _full((1, d)), _full((d, d)), _full((d, d)), _full((d, d))],
        out_specs=[_rows(tm, d)] * 6,
        out_shape=[f32o, f32o, f32o, f32o, bfo, bfo],
        compiler_params=_cp("parallel"))(y5, o_tm, p, p, p, xa, modv, gpost, wc, wa, wo)


def _loss_grad(xa, target, n_lat, tm, name):
    na, d = xa.shape
    nlt = n_lat // tm

    def body(x_ref, t_ref, dx_ref, loss_ref):
        i = pl.program_id(0)

        @pl.when(i == 0)
        def _():
            loss_ref[...] = jnp.zeros_like(loss_ref)

        @pl.when(i < nlt)
        def _():
            err = x_ref[...] - t_ref[...]
            dx_ref[...] = err * (1.0 / d)
            loss_ref[...] += 0.5 * jnp.sum(_mean(err * err))

        @pl.when(i >= nlt)
        def _():
            dx_ref[...] = jnp.zeros_like(dx_ref)

    return _pcall(
        body, name=name, grid=(na // tm,),
        in_specs=[_rows(tm, d), pl.BlockSpec((tm, d), lambda i: (jnp.minimum(i, nlt - 1), 0))],
        out_specs=[_rows(tm, d), _full((8, 128))],
        out_shape=[jax.ShapeDtypeStruct((na, d), F32), jax.ShapeDtypeStruct((8, 128), F32)],
        compiler_params=_cp("arbitrary"))(xa, target)


def _merge_bwd(dxn, out, yc, ya, o_tm, p, y2, modv, gpost, ln_g, ln_b, wo_t, wc_t, wa_t, esel, n_lat, tm, name):
    na, d = dxn.shape

    def body(dx_ref, out_ref, yc_ref, ya_ref, o_ref, gb_ref, ma_ref, mb_ref, gta_ref, y2_ref,
             mod_ref, gp_ref, lg_ref, lb_ref, wot_ref, wct_ref, wat_ref, es_ref,
             dgb_ref, dma_ref, dmb_ref, dgta_ref, do_ref, dout_ref, dyc_ref, dya_ref, dy2_ref, dl_ref, acc_ref):
        i = pl.program_id(0)
        ctx_tile = i >= n_lat // tm

        @pl.when(i == 0)
        def _():
            acc_ref[...] = jnp.zeros_like(acc_ref)

        gt = _seg_rows(mod_ref, 2 * d, 3 * d, ctx_tile)
        gp = gp_ref[...]
        dx = dx_ref[...]
        out = out_ref[...]
        r2 = lax.rsqrt(_mean(out * out) + EPS)
        n2 = out * r2
        dgt = _colsum(dx * (n2 * gp))

        @pl.when(ctx_tile)
        def _():
            acc_ref[1:2, :] += dgt

        @pl.when(jnp.logical_not(ctx_tile))
        def _():
            acc_ref[0:1, :] += dgt

        acc_ref[2:3, :] += _colsum(dx * gt * n2)
        dn2 = dx * gt * gp
        dout = (r2 * (dn2 - n2 * _mean(dn2 * n2))).astype(BF16)
        dout_ref[...] = dout
        dz = _dot(dout, wot_ref[...])
        sa = _sig(ma_ref[...])
        sb = _sig(mb_ref[...])
        dyc = (dz * sa).astype(BF16)
        dya = (dz * sb).astype(BF16)
        dyc_ref[...] = dyc
        dya_ref[...] = dya
        dma_ref[...] = (dz * yc_ref[...] * sa * (1.0 - sa)).astype(BF16)
        dmb_ref[...] = (dz * ya_ref[...] * sb * (1.0 - sb)).astype(BF16)
        dy5 = _dot(dyc, wct_ref[...])
        dog = _dot(dya, wat_ref[...])

        gate_b = gb_ref[...]
        sgb = _sig(gate_b)
        o = o_ref[...]
        do = dog * (gate_b * sgb)
        do_ref[...] = do.astype(BF16)
        dgb_ref[...] = (dog * o * (sgb * (1.0 + gate_b * (1.0 - sgb)))).astype(BF16)
        dl_ref[...] = _split_dot(do * o, es_ref[...])

        y2 = y2_ref[...]
        xc = y2 - _mean(y2)
        rstd = lax.rsqrt(_mean(xc * xc) + EPS)
        xhat = xc * rstd
        lg = lg_ref[...]
        y3 = xhat * lg + lb_ref[...]
        s3 = _sig(y3)
        gate_a = gta_ref[...]
        sga = _sig(gate_a)
        dgta_ref[...] = (dy5 * (y3 * s3) * (sga * (1.0 + gate_a * (1.0 - sga)))).astype(BF16)
        dy3 = dy5 * (gate_a * sga) * (s3 * (1.0 + y3 * (1.0 - s3)))
        acc_ref[3:4, :] += _colsum(dy3 * xhat)
        acc_ref[4:5, :] += _colsum(dy3)
        dxh = dy3 * lg
        dy2 = rstd * (dxh - _mean(dxh) - xhat * _mean(dxh * xhat))
        dy2_ref[...] = dy2
        acc_ref[5:6, :] += _colsum(dy2)

    f32o = jax.ShapeDtypeStruct((na, d), F32)
    bfo = jax.ShapeDtypeStruct((na, d), BF16)
    r = _rows(tm, d)
    return _pcall(
        body, name=name, grid=(na // tm,),
        in_specs=[r, r, r, r, r, _rows(tm, d, 4), _rows(tm, d, 5), _rows(tm, d, 6), _rows(tm, d, 2), r,
                  _full((8, 3 * d)), _full((1, d)), _full((1, d)), _full((1, d)),
                  _full((d, d)), _full((d, d)), _full((d, d)), _full((d, 128))],
        out_specs=[r] * 9 + [_rows(tm, 128), _full((8, d))],
        out_shape=[bfo] * 8 + [f32o, jax.ShapeDtypeStruct((na, 128), F32), jax.ShapeDtypeStruct((8, d), F32)],
        compiler_params=_cp("arbitrary"),
    )(dxn, out, yc, ya, o_tm, p, p, p, p, y2, modv, gpost, ln_g, ln_b, wo_t, wc_t, wa_t, esel)


def _attn_bwd(q_hm, qt_hm, k_hm, kt_hm, v_hm, do_hm, dot_hm, lse_r, dl_r, n_lat, tq, tk, cw, name):
    h, na, hd = q_hm.shape
    kv = k_hm.shape[0]
    nq, nk = na // tq, na // tk
    nsub = tq // cw
    chains = [(a, u) for a in range(GROUP) for u in range(nsub)]
    first_ctx_block, first_ctx_chunk = n_lat // cw, n_lat // tk
    ctx_subs = {b % nsub for b in range(first_ctx_block, na // cw)}

    def body(q_ref, qt_ref, k_ref, kt_ref, v_ref, do_ref, dot_ref, lse_ref, dl_ref, dq_hbm, dk_ref, dv_ref,
             dq_acc, dk_acc, dv_acc, sem):
        g = pl.program_id(0)
        j = pl.program_id(1)
        i = pl.program_id(2)

        @pl.when(jnp.logical_and(j == 0, i == 0))
        def _():
            dq_acc[...] = jnp.zeros_like(dq_acc)

        @pl.when(i == 0)
        def _():
            dk_acc[...] = jnp.zeros_like(dk_acc)
            dv_acc[...] = jnp.zeros_like(dv_acc)

        k = k_ref[...]
        v = v_ref[...]
        kt = kt_ref[...]

        def products(n):
            a, u = chains[n]
            rows = slice(u * cw, (u + 1) * cw)
            return _dot_nt(k, q_ref[a, rows, :]), _dot_nt(v, do_ref[a, rows, :])

        def accumulate(done):
            a, u, dv_t, dk_t, dq_t = done
            dv_acc[...] += dv_t
            dk_acc[...] += dk_t
            at = pl.multiple_of(i * tq + u * cw, cw)
            dq_acc[a, :, pl.ds(at, cw)] += dq_t

        tiles = {n: products(n) for n in range(min(ATTN_AHEAD, len(chains)))}
        pending = None
        for n, (a, u) in enumerate(chains):
            cols = slice(u * cw, (u + 1) * cw)
            s_t, dp_t = tiles.pop(n)
            shift = lse_ref[a, :, cols]
            if u in ctx_subs:
                hidden = jnp.logical_and(i * nsub + u >= first_ctx_block, j < first_ctx_chunk)
                shift = shift + jnp.where(hidden, -NEG_BIG, 0.0)
            p_t = jnp.exp(s_t - shift)
            ds_b = (p_t * (dp_t - dl_ref[a, :, cols])).astype(BF16)
            p_b = p_t.astype(BF16)
            dv_t = _dot_nt(dot_ref[a, :, cols], p_b)
            dk_t = _dot_nt(qt_ref[a, :, cols], ds_b)
            dq_t = _dot(kt, ds_b)
            if pending is not None:
                accumulate(pending)
            pending = (a, u, dv_t, dk_t, dq_t)
            if n + ATTN_AHEAD < len(chains):
                tiles[n + ATTN_AHEAD] = products(n + ATTN_AHEAD)
        accumulate(pending)

        @pl.when(i == nq - 1)
        def _():
            dk_ref[...] = dk_acc[...]
            dv_ref[...] = dv_acc[...]

        @pl.when(jnp.logical_and(j == nk - 1, i == nq - 1))
        def _():
            cp = pltpu.make_async_copy(dq_acc, dq_hbm.at[pl.ds(g * GROUP, GROUP)], sem)
            cp.start()
            cp.wait()

    qspec = pl.BlockSpec((GROUP, tq, hd), lambda g, j, i: (g, i, 0))
    qtspec = pl.BlockSpec((GROUP, hd, tq), lambda g, j, i: (g, 0, i))
    kspec = pl.BlockSpec((None, tk, hd), lambda g, j, i: (g, j, 0))
    ktspec = pl.BlockSpec((None, hd, tk), lambda g, j, i: (g, 0, j))
    rspec = pl.BlockSpec((GROUP, 1, tq), lambda g, j, i: (g, 0, i))
    return _pcall(
        body, name=name, grid=(kv, nk, nq),
        in_specs=[qspec, qtspec, kspec, ktspec, kspec, qspec, qtspec, rspec, rspec],
        out_specs=[ANY, ktspec, ktspec],
        out_shape=[jax.ShapeDtypeStruct((h, hd, na), F32), jax.ShapeDtypeStruct((kv, hd, na), F32),
                   jax.ShapeDtypeStruct((kv, hd, na), F32)],
        scratch_shapes=[pltpu.VMEM((GROUP, hd, na), F32), pltpu.VMEM((hd, tk), F32), pltpu.VMEM((hd, tk), F32),
                        pltpu.SemaphoreType.DMA],
        compiler_params=_cp("arbitrary", "arbitrary", "arbitrary"),
    )(q_hm, qt_hm, k_hm, kt_hm, v_hm, do_hm, dot_hm, lse_r, dl_r)


def _qknorm_bwd(dq_tm, dk_tm, dv_tm, p, cos_t, sin_t, gq, gk, bd, tm, name):
    na = p.shape[0]
    d = gq.shape[1]
    kw = d // GROUP

    def body(dq_ref, dk_ref, dv_ref, q_ref, kv_ref, cos_ref, sin_ref, gq_ref, gk_ref, bd_ref,
             dqo_ref, dkvo_ref, acc_ref):
        @pl.when(pl.program_id(0) == 0)
        def _():
            acc_ref[...] = jnp.zeros_like(acc_ref)

        cos = cos_ref[...]
        sin = sin_ref[...]

        def back(dy, xh, g, w):
            bdw = bd_ref[0:w, 0:w]
            dn = dy * _lane_tile(cos, w) - _partner(dy) * _lane_tile(sin, w)
            rs = lax.rsqrt(_split_dot(xh * xh, bdw) * (1.0 / HEAD_DIM) + EPS)
            y = xh * rs
            dg = _colsum(dn * y)
            dyn = dn * g
            dx = rs * (dyn - y * (_split_dot(dyn * y, bdw) * (1.0 / HEAD_DIM)))
            return dx, dg

        dq, dgq = back(dq_ref[...] * ATTN_SCALE, q_ref[...], gq_ref[...], d)
        dqo_ref[...] = dq.astype(BF16)
        acc_ref[0:1, :] += dgq
        kv = kv_ref[...]
        dk, dgk = back(dk_ref[...], kv[:, 0:kw], gk_ref[...], kw)
        acc_ref[1:2, 0:kw] += dgk
        dkvo_ref[:, 0:kw] = dk.astype(BF16)
        dkvo_ref[:, kw:2 * kw] = dv_ref[...].astype(BF16)

    return _pcall(
        body, name=name, grid=(na // tm,),
        in_specs=[_rows(tm, d), _rows(tm, kw), _rows(tm, kw), _rows(tm, d, 3), _rows(tm, d // 2, 14),
                  _rows(tm, 128), _rows(tm, 128), _full((1, d)), _full((1, kw)), _full((d, d))],
        out_specs=[_rows(tm, d), _rows(tm, d // 2), _full((8, d))],
        out_shape=[jax.ShapeDtypeStruct((na, d), BF16), jax.ShapeDtypeStruct((na, d // 2), BF16),
                   jax.ShapeDtypeStruct((8, d), F32)],
        compiler_params=_cp("arbitrary"))(dq_tm, dk_tm, dv_tm, p, p, cos_t, sin_t, gq, gk, bd)


def _conv_bwd(dy2, p, conv_w, n_lat, tm, name):
    na = p.shape[0]
    ktaps, d = conv_w.shape
    pad = ktaps // 2

    def body(dy_ref, dyp_ref, dyn_ref, a_ref, ap_ref, an_ref, g_ref, gp_ref, gn_ref, w_ref,
             da_ref, dg_ref, dw_ref, dwin, ywin):
        i = pl.program_id(0)

        @pl.when(i == 0)
        def _():
            dw_ref[...] = jnp.zeros_like(dw_ref)

        first, last = _seq_ends(i, n_lat, na, tm)
        dy = dy_ref[...]
        a = a_ref[...]
        sg = _sig(g_ref[...])
        _window(dwin, dyp_ref[...], dy, dyn_ref[...], first, last, tm)
        _window(ywin, ap_ref[...] * _sig(gp_ref[...]), a * sg, an_ref[...] * _sig(gn_ref[...]), first, last, tm)
        dy1 = jnp.zeros((tm, d), F32)
        for k in range(ktaps):
            dy1 = dy1 + w_ref[k:k + 1, :] * dwin[pl.ds(HALO + pad - k, tm), :]
            dw_ref[k:k + 1, :] += _colsum(dy * ywin[pl.ds(HALO - pad + k, tm), :])
        da_ref[...] = (dy1 * sg).astype(BF16)
        dg_ref[...] = (dy1 * a * sg * (1.0 - sg)).astype(BF16)

    dyp, dyn = _halo_specs(tm, d, na, 0)
    ap, an = _halo_specs(tm, d, na, 0)
    gp, gn = _halo_specs(tm, d, na, 1)
    bfo = jax.ShapeDtypeStruct((na, d), BF16)
    return _pcall(
        body, name=name, grid=(na // tm,),
        in_specs=[_rows(tm, d), dyp, dyn, _rows(tm, d, 0), ap, an, _rows(tm, d, 1), gp, gn, _full((ktaps, d))],
        out_specs=[_rows(tm, d), _rows(tm, d), _full((ktaps, d))],
        out_shape=[bfo, bfo, jax.ShapeDtypeStruct((ktaps, d), F32)],
        scratch_shapes=[pltpu.VMEM((tm + 2 * HALO, d), F32), pltpu.VMEM((tm + 2 * HALO, d), F32)],
        compiler_params=_cp("arbitrary"))(dy2, dy2, dy2, p, p, p, p, p, p, conv_w)


def _inproj_bwd(segs, dkv, xa, dxn, modv, gpre, wp_t, n_lat, tm, name):
    na, d = xa.shape
    nseg = len(segs)
    wrows = wp_t.shape[0]

    def body(*refs):
        seg_refs = refs[:nseg]
        dkv_ref, x_ref, dxn_ref, mod_ref, g_ref, wt_hbm, dx_ref, acc_ref, wt, sem = refs[nseg:]
        i = pl.program_id(0)
        ctx_tile = i >= n_lat // tm

        @pl.when(i == 0)
        def _():
            cp = pltpu.make_async_copy(wt_hbm, wt, sem)
            cp.start()
            cp.wait()
            acc_ref[...] = jnp.zeros_like(acc_ref)

        dh = _dot(dkv_ref[...], wt[nseg * d:wrows, :])
        for s in range(nseg):
            dh = dh + _dot(seg_refs[s][...], wt[s * d:(s + 1) * d, :])
        x = x_ref[...]
        r = lax.rsqrt(_mean(x * x) + EPS)
        xn = x * r
        g = g_ref[...]
        sc1 = 1.0 + _seg_rows(mod_ref, d, 2 * d, ctx_tile)
        dsh = _colsum(dh)
        dsc = _colsum(dh * (xn * g))

        @pl.when(ctx_tile)
        def _():
            acc_ref[1:2, :] += dsh
            acc_ref[3:4, :] += dsc

        @pl.when(jnp.logical_not(ctx_tile))
        def _():
            acc_ref[0:1, :] += dsh
            acc_ref[2:3, :] += dsc

        acc_ref[4:5, :] += _colsum(dh * xn * sc1)
        dxh = dh * g * sc1
        dx_ref[...] = dxn_ref[...] + r * (dxh - xn * _mean(dxh * xn))

    r_ = _rows(tm, d)
    return _pcall(
        body, name=name, grid=(na // tm,),
        in_specs=[r_] * nseg + [_rows(tm, d // 2), r_, r_, _full((8, 3 * d)), _full((1, d)), ANY],
        out_specs=[r_, _full((8, d))],
        out_shape=[jax.ShapeDtypeStruct((na, d), F32), jax.ShapeDtypeStruct((8, d), F32)],
        scratch_shapes=[pltpu.VMEM(wp_t.shape, BF16), pltpu.SemaphoreType.DMA],
        compiler_params=_cp("arbitrary"))(*segs, dkv, xa, dxn, modv, gpre, wp_t)


def _grad_matmul(a, b, tk, name):
    na, ka = a.shape
    nb = b.shape[1]
    tn = min(nb, 512)

    def body(a_ref, b_ref, o_ref):
        @pl.when(pl.program_id(1) == 0)
        def _():
            o_ref[...] = jnp.zeros_like(o_ref)

        o_ref[...] += _dot_tn(a_ref[...], b_ref[...])

    return _pcall(
        body, name=name, grid=(nb // tn, na // tk),
        in_specs=[pl.BlockSpec((tk, ka), lambda n, k: (k, 0)), pl.BlockSpec((tk, tn), lambda n, k: (k, n))],
        out_specs=pl.BlockSpec((ka, tn), lambda n, k: (0, n)),
        out_shape=jax.ShapeDtypeStruct((ka, nb), F32),
        compiler_params=_cp("parallel", "arbitrary"))(a, b)


def _pack(parts, cols, row_mult):
    flat = jnp.concatenate([q.astype(F32).reshape(-1) for q in parts])
    rows = -(-flat.shape[0] // (cols * row_mult)) * row_mult
    return jnp.pad(flat, (0, rows * cols - flat.shape[0])).reshape(1, rows, cols)


def _unpack(flat, shapes):
    out, off = [], 0
    for s in shapes:
        n = math.prod(s)
        out.append(flat[off:off + n].reshape(tuple(s)))
        off += n
    return out


def _cols_by_dest(g):
    l, a, w8 = g.shape
    return g.reshape(l, a, N_DEV, w8 // N_DEV).transpose(2, 0, 1, 3)


def _rows_by_dest(g):
    l, r8, b = g.shape
    return g.reshape(l, N_DEV, r8 // N_DEV, b).transpose(1, 0, 2, 3)


def _cols_from_src(s):
    n, l, a, w = s.shape
    return s.transpose(1, 2, 0, 3).reshape(l, a, n * w)


def _rows_from_src(s):
    n, l, r, b = s.shape
    return s.transpose(1, 0, 2, 3).reshape(l, n * r, b)


def _rope_tables(n_lat, n_ctx):
    half = HEAD_DIM // 2
    rows = n_lat // GRID_W
    row = jnp.repeat(jnp.arange(rows, dtype=F32), GRID_W)
    col = jnp.tile(jnp.arange(GRID_W, dtype=F32), rows)
    inv_freq = ROPE_THETA ** (-jnp.arange(0, half, 2, dtype=F32) / half)
    ang = jnp.concatenate([row[:, None] * inv_freq, col[:, None] * inv_freq], axis=-1)
    cos, sin = jnp.cos(ang), jnp.sin(ang)
    cos_t = jnp.concatenate([jnp.tile(cos, (1, 4)), jnp.ones((n_ctx, 128), F32)], axis=0)
    sin_t = jnp.concatenate([jnp.tile(jnp.concatenate([-sin, sin], axis=-1), (1, 2)),
                             jnp.zeros((n_ctx, 128), F32)], axis=0)
    return cos_t, sin_t


def _to_heads(t):
    na, w = t.shape
    return t.reshape(na, w // HEAD_DIM, HEAD_DIM).transpose(1, 0, 2)


def _from_heads(t):
    nh, na, hd = t.shape
    return t.transpose(1, 0, 2).reshape(na, nh * hd)


def kernel(x, c, ctx, c_ctx, w_mod, b_mod, g_pre, g_post, w_in, conv_w, conv_b, ln_g, ln_b, w_conv_out, q_norm_g, k_norm_g, w_attn_out, w_out, loss_target, m_c_ctx, m_w_mod, m_b_mod, m_g_pre, m_g_post, m_w_in, m_conv_w, m_conv_b, m_ln_g, m_ln_b, m_w_conv_out, m_q_norm_g, m_k_norm_g, m_w_attn_out, m_w_out, v_c_ctx, v_w_mod, v_b_mod, v_g_pre, v_g_post, v_w_in, v_conv_w, v_conv_b, v_ln_g, v_ln_b, v_w_conv_out, v_q_norm_g, v_k_norm_g, v_w_attn_out, v_w_out):
    depth, d, _ = w_mod.shape
    n_lat, n_ctx = x.shape[1], ctx.shape[1]
    na = n_lat + n_ctx
    heads = d // HEAD_DIM
    kw = d // GROUP
    ktaps = conv_w.shape[1]
    tm = n_ctx
    tm_half = tm // 2
    tbig = 3 * tm if na % (3 * tm) == 0 else tm

    s_mod, s_in, s_co, s_ao, s_oo, s_cw = _all_gather(
        [w_mod.astype(BF16), w_in.astype(BF16), w_conv_out.astype(BF16), w_attn_out.astype(BF16),
         w_out.astype(BF16), conv_w], "gather_weights")
    wmod_f = _cols_from_src(s_mod)
    win_f = _cols_from_src(s_in)
    wc_f, wa_f, wo_f = _rows_from_src(s_co), _rows_from_src(s_ao), _rows_from_src(s_oo)
    wp_f = jnp.concatenate([win_f[:, :, :4 * d], win_f[:, :, 4 * d + 2 * kw:], win_f[:, :, 4 * d:4 * d + 2 * kw]], axis=2)
    convw_f = _cols_from_src(s_cw)

    cos_t, sin_t = _rope_tables(n_lat, n_ctx)
    lane = jnp.arange(d)
    bd = (lane[:, None] // HEAD_DIM == lane[None, :] // HEAD_DIM).astype(BF16)
    esel = (lane[:, None] // HEAD_DIM == jnp.arange(128)[None, :]).astype(BF16)
    cvec = jnp.zeros((8, d), F32).at[0].set(c[0]).at[1].set(c_ctx)
    cvec_t = jnp.zeros((d, 128), F32).at[:, 0].set(c[0]).at[:, 1].set(c_ctx)

    xa = jnp.concatenate([x[0], ctx[0]], axis=0)

    saved = []
    for l in range(depth):
        tag = f"_l{l}"
        gq = jnp.tile(q_norm_g[l], heads)[None, :]
        gk = jnp.tile(k_norm_g[l], heads // GROUP)[None, :]
        modv = _mod_fwd(cvec, wmod_f[l], b_mod[l][None, :], "mod_fwd" + tag)
        p, hb = _inproj(xa, modv, g_pre[l][None, :], wp_f[l], n_lat, tbig, "inproj" + tag)
        qr, kr, vb = _qknorm_fwd(p, cos_t, sin_t, gq, gk, bd, tm, "qknorm_fwd" + tag)
        q_hm, k_hm, v_hm = _to_heads(qr), _to_heads(kr), _to_heads(vb)
        vt_ones = jnp.concatenate([v_hm.transpose(0, 2, 1), jnp.ones((heads // GROUP, 16, na), BF16)], axis=1)
        ot_hm, lse = _attn_fwd(q_hm, k_hm, vt_ones, n_lat, tbig, tm, tm, "attn_fwd" + tag)
        o_tm = ot_hm.transpose(2, 0, 1).reshape(na, d)
        y5, y2 = _conv_fwd(p, convw_f[l], conv_b[l][None, :], ln_g[l][None, :], ln_b[l][None, :], n_lat, tm,
                           "conv_fwd" + tag)
        xa_new, yc, ya, out, z, og = _merge_fwd(y5, o_tm, p, xa, modv, g_post[l][None, :], wc_f[l], wa_f[l], wo_f[l],
                                                n_lat, tm, "merge_fwd" + tag)
        saved.append(dict(xa=xa, modv=modv, p=p, hb=hb, q_hm=q_hm, k_hm=k_hm, v_hm=v_hm, o_tm=o_tm, lse=lse,
                          qt_hm=qr.T.reshape(heads, HEAD_DIM, na), kt_hm=kr.T.reshape(heads // GROUP, HEAD_DIM, na),
                          y5=y5, y2=y2, yc=yc, ya=ya, out=out, z=z, og=og, gq=gq, gk=gk))
        xa = xa_new

    dxa, loss_blk = _loss_grad(xa, loss_target[0], n_lat, tm, "loss_grad")

    g_wmod, g_win, g_convw, g_wc, g_wa, g_wo = [], [], [], [], [], []
    g_bmod, g_gpre, g_gpost, g_convb, g_lng, g_lnb, g_qg, g_kg = [], [], [], [], [], [], [], []
    g_cctx = jnp.zeros((d,), F32)
    for l in reversed(range(depth)):
        tag = f"_l{l}"
        s = saved[l]
        p = s["p"]
        (dgb, dma, dmb, dgta, do_tm, dout, dyc, dya, dy2, dl128, acc_m) = _merge_bwd(
            dxa, s["out"], s["yc"], s["ya"], s["o_tm"], p, s["y2"], s["modv"], g_post[l][None, :],
            ln_g[l][None, :], ln_b[l][None, :], wo_f[l].T, wc_f[l].T, wa_f[l].T, esel, n_lat, tm_half,
            "merge_bwd" + tag)
        lse_r = s["lse"].reshape(heads, 1, na)
        dl_r = dl128[:, :heads].T.reshape(heads, 1, na)
        dqt_hm, dkt_hm, dvt_hm = _attn_bwd(
            s["q_hm"], s["qt_hm"], s["k_hm"], s["kt_hm"], s["v_hm"], _to_heads(do_tm),
            do_tm.T.reshape(heads, HEAD_DIM, na), lse_r, dl_r, n_lat, tbig, tm, tm, "attn_bwd" + tag)
        dq_tm = dqt_hm.reshape(d, na).T
        dq, dkv, acc_q = _qknorm_bwd(dq_tm, dkt_hm.reshape(kw, na).T, dvt_hm.reshape(kw, na).T, p, cos_t, sin_t,
                                     s["gq"], s["gk"], bd, tm, "qknorm_bwd" + tag)
        da, dg, dconvw = _conv_bwd(dy2, p, convw_f[l], n_lat, tm, "conv_bwd" + tag)
        segs = [da, dg, dgta, dq, dgb, dma, dmb]
        dxa, acc_h = _inproj_bwd(segs, dkv, s["xa"], dxa, s["modv"], g_pre[l][None, :], wp_f[l].T, n_lat, tm,
                                 "inproj_bwd" + tag)

        dwp = [_grad_matmul(s["hb"], sg, tbig, f"grad_w_in{k}" + tag) for k, sg in enumerate(segs + [dkv])]
        g_win.append(jnp.concatenate(dwp[:4] + [dwp[7]] + dwp[4:7], axis=1))
        g_wc.append(_grad_matmul(s["y5"], dyc, tbig, "grad_w_conv_out" + tag))
        g_wa.append(_grad_matmul(s["og"], dya, tbig, "grad_w_attn_out" + tag))
        g_wo.append(_grad_matmul(s["z"], dout, tbig, "grad_w_out" + tag))
        g_convw.append(dconvw)

        dmod = jnp.zeros((8, 3 * d), F32)
        dmod = dmod.at[0].set(jnp.concatenate([acc_h[0], acc_h[2], acc_m[0]]))
        dmod = dmod.at[1].set(jnp.concatenate([acc_h[1], acc_h[3], acc_m[1]]))
        dwm, dbm, dcv = _mod_bwd(dmod, cvec, cvec_t, wmod_f[l], "mod_bwd" + tag)
        g_wmod.append(dwm)
        g_bmod.append(dbm[0])
        g_cctx = g_cctx + dcv[1]
        g_gpre.append(acc_h[4])
        g_gpost.append(acc_m[2])
        g_lng.append(acc_m[3])
        g_lnb.append(acc_m[4])
        g_convb.append(acc_m[5])
        g_qg.append(acc_q[0].reshape(heads, HEAD_DIM).sum(0))
        g_kg.append(acc_q[1, :kw].reshape(heads // GROUP, HEAD_DIM).sum(0))

    grad_x = dxa[:n_lat][None]

    def stack(lst):
        return jnp.stack(lst[::-1])

    big_names = ["w_mod", "w_in", "w_conv_out", "w_attn_out", "w_out", "conv_w"]
    big_w = dict(w_mod=w_mod, w_in=w_in, w_conv_out=w_conv_out, w_attn_out=w_attn_out, w_out=w_out, conv_w=conv_w)
    big_m = dict(w_mod=m_w_mod, w_in=m_w_in, w_conv_out=m_w_conv_out, w_attn_out=m_w_attn_out, w_out=m_w_out,
                 conv_w=m_conv_w)
    big_v = dict(w_mod=v_w_mod, w_in=v_w_in, w_conv_out=v_w_conv_out, w_attn_out=v_w_attn_out, w_out=v_w_out,
                 conv_w=v_conv_w)
    by_dest = [_cols_by_dest(stack(g_wmod)), _cols_by_dest(stack(g_win)), _rows_by_dest(stack(g_wc)),
               _rows_by_dest(stack(g_wa)), _rows_by_dest(stack(g_wo)), _cols_by_dest(stack(g_convw))]
    send = [q.reshape((4, 2) + q.shape[1:]) for q in by_dest]
    from_sibling = _swap_with_sibling(send, "reduce_sibling")
    chip_part = [_add_sibling_part(sb, rc, "reduce_sibling_add_" + n)
                 for n, sb, rc in zip(big_names, send, from_sibling)]
    from_chips = _exchange_chips(chip_part, "reduce_chips")
    big_g, big_d, big_nm, big_nv = {}, {}, {}, {}
    for n, st in zip(big_names, from_chips):
        big_g[n], big_d[n], big_nm[n], big_nv[n] = _sum_adamw(st, big_w[n], big_m[n], big_v[n], "adamw_" + n)

    small_names = ["c_ctx", "b_mod", "g_pre", "g_post", "conv_b", "ln_g", "ln_b", "q_norm_g", "k_norm_g", "loss"]
    zero1 = jnp.zeros((1,), F32)
    small_w = dict(c_ctx=c_ctx, b_mod=b_mod, g_pre=g_pre, g_post=g_post, conv_b=conv_b, ln_g=ln_g, ln_b=ln_b,
                   q_norm_g=q_norm_g, k_norm_g=k_norm_g, loss=zero1)
    small_m = dict(c_ctx=m_c_ctx, b_mod=m_b_mod, g_pre=m_g_pre, g_post=m_g_post, conv_b=m_conv_b, ln_g=m_ln_g,
                   ln_b=m_ln_b, q_norm_g=m_q_norm_g, k_norm_g=m_k_norm_g, loss=zero1)
    small_v = dict(c_ctx=v_c_ctx, b_mod=v_b_mod, g_pre=v_g_pre, g_post=v_g_post, conv_b=v_conv_b, ln_g=v_ln_g,
                   ln_b=v_ln_b, q_norm_g=v_q_norm_g, k_norm_g=v_k_norm_g, loss=zero1)
    small_g = dict(c_ctx=g_cctx, b_mod=stack(g_bmod), g_pre=stack(g_gpre), g_post=stack(g_gpost),
                   conv_b=stack(g_convb), ln_g=stack(g_lng), ln_b=stack(g_lnb), q_norm_g=stack(g_qg),
                   k_norm_g=stack(g_kg), loss=loss_blk[0, 0:1])
    small_shapes = [small_w[n].shape for n in small_names]

    def pack_small(tree):
        return _pack([tree[n] for n in small_names], d, 8)

    small_parts, = _all_gather([pack_small(small_g)], "gather_small_grads")
    small_out = _sum_adamw(small_parts, pack_small(small_w), pack_small(small_m), pack_small(small_v),
                           "adamw_replicated")
    sm_g, sm_d, sm_nm, sm_nv = [dict(zip(small_names, _unpack(o.reshape(-1), small_shapes))) for o in small_out]
    loss = sm_g["loss"].reshape(())

    order = ["c_ctx", "w_mod", "b_mod", "g_pre", "g_post", "w_in", "conv_w", "conv_b", "ln_g", "ln_b",
             "w_conv_out", "q_norm_g", "k_norm_g", "w_attn_out", "w_out"]

    def pick(big, small):
        return [big[n] if n in big else small[n] for n in order]

    return (loss, grad_x, *pick(big_g, sm_g), *pick(big_d, sm_d), *pick(big_nm, sm_nm), *pick(big_nv, sm_nv))
```

```python
import math

import jax
import jax.numpy as jnp
from jax import lax
from jax.experimental import pallas as pl
from jax.experimental.pallas import tpu as pltpu

F32 = jnp.float32
BF16 = jnp.bfloat16

HEAD_DIM = 64
GROUP = 4
GRID_W = 64
ROPE_THETA = 10000.0
EPS = 1e-6
ATTN_SCALE = HEAD_DIM ** -0.5
HALO = 16
ATTN_AHEAD = 3

ADAM_LR = 0.001
ADAM_B1 = 0.9
ADAM_B2 = 0.999
ADAM_EPS = 1e-08
ADAM_WD = 0.01
ADAM_STEP = 10

N_DEV = 8
MESH_AXES = ("x", "y", "c")
V7X_VMEM_LIMIT = 56 * 1024 * 1024
NEG_BIG = -1e30

MESH = pl.DeviceIdType.MESH
ANY = pl.BlockSpec(memory_space=pl.ANY)


def _pcall(body, **kw):
    return pl.pallas_call(body, **kw)


def _cp(*sem):
    return pltpu.CompilerParams(dimension_semantics=sem, vmem_limit_bytes=V7X_VMEM_LIMIT)


def _sig(x):
    return 1.0 / (1.0 + jnp.exp(-x))


def _mean(x):
    return jnp.mean(x, axis=-1, keepdims=True)


def _colsum(x):
    return jnp.sum(x, axis=0, keepdims=True)


def _bf_round(x):
    return x.astype(BF16).astype(F32)


def _dot(a, b):
    return jnp.dot(a, b, preferred_element_type=F32)


def _dot_nt(a, b):
    return lax.dot_general(a, b, (((1,), (1,)), ((), ())), preferred_element_type=F32)


def _dot_tn(a, b):
    return lax.dot_general(a, b, (((0,), (0,)), ((), ())), preferred_element_type=F32)


def _split_dot(x, m):
    hi = x.astype(BF16)
    lo = (x - hi.astype(F32)).astype(BF16)
    return _dot(hi, m) + _dot(lo, m)


def _full(shape):
    nd = len(shape)
    return pl.BlockSpec(shape, lambda *_: (0,) * nd)


def _rows(tm, width, colblk=0):
    return pl.BlockSpec((tm, width), lambda i: (i, colblk))


def _my_place():
    return lax.axis_index("x"), lax.axis_index("y"), lax.axis_index("c")


def _sem_arrays(n):
    return [pltpu.SemaphoreType.DMA((n,)), pltpu.SemaphoreType.DMA((n,))]


def _all_gather(shards, name):
    n = len(shards)

    def body(*refs):
        x_refs, out_refs = refs[:n], refs[n:2 * n]
        send_sems, recv_sems, local_sems = refs[2 * n:]
        x, y, c = _my_place()
        me, sibling = (x, y, c), (x, y, 1 - c)
        chips = [(1 - x, y), (x, 1 - y), (1 - x, 1 - y)]

        def slab(a, px, py, pc):
            return out_refs[a].at[4 * px + 2 * py + pc]

        def copies(k, block, to, from_input=False):
            return [pltpu.make_async_remote_copy(
                src_ref=x_refs[a] if from_input else slab(a, *block), dst_ref=slab(a, *block),
                send_sem=send_sems.at[k * n + a], recv_sem=recv_sems.at[k * n + a],
                device_id=to, device_id_type=MESH) for a in range(n)]

        mine = [pltpu.make_async_copy(x_refs[a], slab(a, *me), local_sems.at[a]) for a in range(n)]
        for cp in mine:
            cp.start()
        first = copies(0, me, sibling, True)
        for j, chip in enumerate(chips):
            first += copies(1 + j, me, (*chip, c), True)
        for cp in first:
            cp.start()
        passed = []
        for j, chip in enumerate(chips):
            for cp in copies(1 + j, (*chip, c), me):
                cp.wait_recv()
            onward = copies(4 + j, (*chip, c), sibling)
            for cp in onward:
                cp.start()
            passed += onward
        for cp in copies(0, sibling, me):
            cp.wait_recv()
        for j, chip in enumerate(chips):
            for cp in copies(4 + j, (*chip, 1 - c), me):
                cp.wait_recv()
        for cp in first + passed:
            cp.wait_send()
        for cp in mine:
            cp.wait()

    return _pcall(
        body, name=name,
        out_shape=[jax.ShapeDtypeStruct((N_DEV,) + q.shape, q.dtype) for q in shards],
        in_specs=[ANY] * n, out_specs=[ANY] * n,
        scratch_shapes=_sem_arrays(7 * n) + [pltpu.SemaphoreType.DMA((n,))],
    )(*shards)


def _swap_with_sibling(bufs, name):
    n = len(bufs)

    def body(*refs):
        buf_refs, recv_refs = refs[:n], refs[n:2 * n]
        send_sems, recv_sems = refs[2 * n:]
        x, y, c = _my_place()
        copies = [
            pltpu.make_async_remote_copy(
                src_ref=buf_refs[a].at[k, 1 - c], dst_ref=recv_refs[a].at[k],
                send_sem=send_sems.at[k * n + a], recv_sem=recv_sems.at[k * n + a],
                device_id=(x, y, 1 - c), device_id_type=MESH)
            for k in range(4) for a in range(n)]
        for cp in copies:
            cp.start()
        for cp in copies:
            cp.wait()

    return _pcall(
        body, name=name,
        out_shape=[jax.ShapeDtypeStruct((4,) + q.shape[2:], q.dtype) for q in bufs],
        in_specs=[ANY] * n, out_specs=[ANY] * n,
        scratch_shapes=_sem_arrays(4 * n),
    )(*bufs)


def _exchange_chips(parts, name):
    n = len(parts)

    def body(*refs):
        s_refs, recv_refs = refs[:n], refs[n:2 * n]
        send_sems, recv_sems, local_sems = refs[2 * n:]
        x, y, c = _my_place()
        mychip = 2 * x + y
        chips = [(1 - x, y), (x, 1 - y), (1 - x, 1 - y)]
        mine = [pltpu.make_async_copy(s_refs[a].at[mychip], recv_refs[a].at[mychip], local_sems.at[a])
                for a in range(n)]
        for cp in mine:
            cp.start()
        copies = [
            pltpu.make_async_remote_copy(
                src_ref=s_refs[a].at[2 * px + py], dst_ref=recv_refs[a].at[mychip],
                send_sem=send_sems.at[j * n + a], recv_sem=recv_sems.at[j * n + a],
                device_id=(px, py, c), device_id_type=MESH)
            for j, (px, py) in enumerate(chips) for a in range(n)]
        for cp in copies:
            cp.start()
        for cp in copies:
            cp.wait()
        for cp in mine:
            cp.wait()

    return _pcall(
        body, name=name,
        out_shape=[jax.ShapeDtypeStruct(q.shape, q.dtype) for q in parts],
        in_specs=[ANY] * n, out_specs=[ANY] * n,
        scratch_shapes=_sem_arrays(3 * n) + [pltpu.SemaphoreType.DMA((n,))],
    )(*parts)


def _row_tile(a):
    for t in range(256, 7, -8):
        if a % t == 0:
            return t
    return a


def _add_sibling_part(buf, recv, name):
    _, _, nl, a, b = buf.shape
    ta = _row_tile(a)
    core = lax.axis_index("c").astype(jnp.int32).reshape(1)

    def body(core_ref, a_ref, b_ref, o_ref):
        o_ref[...] = a_ref[...] + b_ref[...]

    grid_spec = pltpu.PrefetchScalarGridSpec(
        num_scalar_prefetch=1, grid=(4, nl, a // ta),
        in_specs=[pl.BlockSpec((None, None, None, ta, b), lambda k, l, r, cr: (k, cr[0], l, r, 0)),
                  pl.BlockSpec((None, None, ta, b), lambda k, l, r, cr: (k, l, r, 0))],
        out_specs=pl.BlockSpec((None, None, ta, b), lambda k, l, r, cr: (k, l, r, 0)))
    return _pcall(body, name=name, grid_spec=grid_spec,
                  out_shape=jax.ShapeDtypeStruct((4, nl, a, b), F32),
                  compiler_params=_cp("parallel", "parallel", "parallel"))(core, buf, recv)


def _sum_adamw(stack, w, m, v, name):
    ns, nl, a, b = stack.shape
    ta = _row_tile(a)
    c1 = 1.0 - ADAM_B1 ** ADAM_STEP
    c2 = 1.0 - ADAM_B2 ** ADAM_STEP

    def body(s_ref, w_ref, m_ref, v_ref, g_out, d_out, m_out, v_out):
        g = s_ref[0]
        for k in range(1, ns):
            g = g + s_ref[k]
        m_new = ADAM_B1 * m_ref[...] + (1.0 - ADAM_B1) * g
        v_new = ADAM_B2 * v_ref[...] + (1.0 - ADAM_B2) * (g * g)
        m_hat = m_new / c1
        v_hat = v_new / c2
        g_out[...] = g
        d_out[...] = -ADAM_LR * (m_hat / (jnp.sqrt(v_hat) + ADAM_EPS) + ADAM_WD * w_ref[...])
        m_out[...] = m_new
        v_out[...] = v_new

    blk = pl.BlockSpec((None, ta, b), lambda l, i: (l, i, 0))
    return _pcall(
        body, name=name, grid=(nl, a // ta),
        in_specs=[pl.BlockSpec((ns, None, ta, b), lambda l, i: (0, l, i, 0)), blk, blk, blk],
        out_specs=[blk] * 4,
        out_shape=[jax.ShapeDtypeStruct((nl, a, b), F32)] * 4,
        compiler_params=_cp("parallel", "parallel"))(stack, w, m, v)


def _mod_fwd(cvec, wmod, bmod, name):
    _, d = cvec.shape

    def body(c_ref, w_ref, b_ref, o_ref):
        cv = c_ref[...]
        cs = cv * _sig(cv)
        o_ref[...] = _dot(cs.astype(BF16), w_ref[...]) + b_ref[...]

    return _pcall(
        body, name=name, grid=(3,),
        in_specs=[_full((8, d)), pl.BlockSpec((d, d), lambda n: (0, n)), pl.BlockSpec((1, d), lambda n: (0, n))],
        out_specs=pl.BlockSpec((8, d), lambda n: (0, n)),
        out_shape=jax.ShapeDtypeStruct((8, 3 * d), F32),
        compiler_params=_cp("parallel"))(cvec, wmod, bmod)


def _mod_bwd(dmod, cvec, cvec_t, wmod, name):
    _, d = cvec.shape

    def body(dm_ref, c_ref, ct_ref, w_ref, dw_ref, db_ref, dc_ref):
        n = pl.program_id(0)
        dm = dm_ref[...]
        ct = ct_ref[...]
        cs_t = _bf_round(ct * _sig(ct))
        d0 = _bf_round(dm[0:1, :])
        d1 = _bf_round(dm[1:2, :])
        dw_ref[...] = cs_t[:, 0:1] * d0 + cs_t[:, 1:2] * d1
        db_ref[...] = dm[0:1, :] + dm[1:2, :]

        @pl.when(n == 0)
        def _():
            dc_ref[...] = jnp.zeros_like(dc_ref)

        dc_ref[...] += _dot_nt(dm.astype(BF16), w_ref[...])

        @pl.when(n == 2)
        def _():
            cv = c_ref[...]
            s = _sig(cv)
            dc_ref[...] = dc_ref[...] * (s * (1.0 + cv * (1.0 - s)))

    return _pcall(
        body, name=name, grid=(3,),
        in_specs=[pl.BlockSpec((8, d), lambda n: (0, n)), _full((8, d)), _full((d, 128)),
                  pl.BlockSpec((d, d), lambda n: (0, n))],
        out_specs=[pl.BlockSpec((d, d), lambda n: (0, n)), pl.BlockSpec((1, d), lambda n: (0, n)),
                   _full((8, d))],
        out_shape=[jax.ShapeDtypeStruct((d, 3 * d), F32), jax.ShapeDtypeStruct((1, 3 * d), F32),
                   jax.ShapeDtypeStruct((8, d), F32)],
        compiler_params=_cp("arbitrary"))(dmod, cvec, cvec_t, wmod)


def _seg_rows(mod_ref, lo, hi, is_ctx):
    return jnp.where(is_ctx, mod_ref[1:2, lo:hi], mod_ref[0:1, lo:hi])


def _inproj(xa, modv, gpre, wp, n_lat, tm, name):
    na, d = xa.shape
    wcols = wp.shape[1]
    tn = d // 2

    def body(x_ref, mod_ref, g_ref, w_ref, p_ref, h_ref, h_s):
        i = pl.program_id(0)

        @pl.when(pl.program_id(1) == 0)
        def _():
            x = x_ref[...]
            r = lax.rsqrt(_mean(x * x) + EPS)
            row = i * tm + lax.broadcasted_iota(jnp.int32, (tm, 1), 0)
            is_ctx = row >= n_lat
            sh = _seg_rows(mod_ref, 0, d, is_ctx)
            sc = _seg_rows(mod_ref, d, 2 * d, is_ctx)
            hb = ((x * r * g_ref[...]) * (1.0 + sc) + sh).astype(BF16)
            h_s[...] = hb
            h_ref[...] = hb

        p_ref[...] = _dot(h_s[...], w_ref[...])

    return _pcall(
        body, name=name, grid=(na // tm, wcols // tn),
        in_specs=[pl.BlockSpec((tm, d), lambda i, j: (i, 0)), _full((8, 3 * d)), _full((1, d)),
                  pl.BlockSpec((d, tn), lambda i, j: (0, j))],
        out_specs=[pl.BlockSpec((tm, tn), lambda i, j: (i, j)), pl.BlockSpec((tm, d), lambda i, j: (i, 0))],
        out_shape=[jax.ShapeDtypeStruct((na, wcols), F32), jax.ShapeDtypeStruct((na, d), BF16)],
        scratch_shapes=[pltpu.VMEM((tm, d), BF16)],
        compiler_params=_cp("parallel", "arbitrary"))(xa, modv, gpre, wp)


def _lane_tile(t, width):
    if width >= 128:
        return jnp.tile(t, (1, width // 128))
    return t[:, :width]


def _partner(x):
    w = x.shape[-1]
    lane = lax.broadcasted_iota(jnp.int32, x.shape, 1)
    low = (lane % HEAD_DIM) < (HEAD_DIM // 2)
    return jnp.where(low, pltpu.roll(x, w - HEAD_DIM // 2, 1), pltpu.roll(x, HEAD_DIM // 2, 1))


def _qknorm_fwd(p, cos_t, sin_t, gq, gk, bd, tm, name):
    na = p.shape[0]
    d = gq.shape[1]
    kw = d // GROUP

    def body(q_ref, kv_ref, cos_ref, sin_ref, gq_ref, gk_ref, bd_ref, qo_ref, ko_ref, vo_ref):
        cos = cos_ref[...]
        sin = sin_ref[...]

        def norm_rope(xh, g, w):
            ms = _split_dot(xh * xh, bd_ref[0:w, 0:w]) * (1.0 / HEAD_DIM)
            xn = xh * lax.rsqrt(ms + EPS) * g
            return xn * _lane_tile(cos, w) + _partner(xn) * _lane_tile(sin, w)

        qo_ref[...] = (norm_rope(q_ref[...], gq_ref[...], d) * ATTN_SCALE).astype(BF16)
        kv = kv_ref[...]
        ko_ref[...] = norm_rope(kv[:, 0:kw], gk_ref[...], kw).astype(BF16)
        vo_ref[...] = kv[:, kw:2 * kw].astype(BF16)

    return _pcall(
        body, name=name, grid=(na // tm,),
        in_specs=[_rows(tm, d, 3), _rows(tm, d // 2, 14), _rows(tm, 128), _rows(tm, 128),
                  _full((1, d)), _full((1, kw)), _full((d, d))],
        out_specs=[_rows(tm, d), _rows(tm, kw), _rows(tm, kw)],
        out_shape=[jax.ShapeDtypeStruct((na, d), BF16), jax.ShapeDtypeStruct((na, kw), BF16),
                   jax.ShapeDtypeStruct((na, kw), BF16)],
        compiler_params=_cp("parallel"))(p, p, cos_t, sin_t, gq, gk, bd)


def _attn_fwd(q_hm, k_hm, vt_hm, na, tq, tk, cw, name):
    h, _, hd = q_hm.shape
    kv, nkeys, _ = k_hm.shape
    vrows = vt_hm.shape[1]
    nq, nk = na // tq, nkeys // tk
    nsub = tq // cw
    chains = [(a, u) for a in range(GROUP) for u in range(nsub)]

    def body(q_ref, k_ref, vt_ref, o_ref, lse_ref, m_s, acc_s):
        j = pl.program_id(2)

        @pl.when(j == 0)
        def _():
            m_s[...] = jnp.full_like(m_s, NEG_BIG)
            acc_s[...] = jnp.zeros_like(acc_s)

        k = k_ref[...]
        vt = vt_ref[...]

        def scores(n):
            a, u = chains[n]
            return _dot_nt(k, q_ref[a, u * cw:(u + 1) * cw, :])

        s_tiles = {n: scores(n) for n in range(min(ATTN_AHEAD, len(chains)))}
        pending = None
        for n, (a, u) in enumerate(chains):
            cols = slice(u * cw, (u + 1) * cw)
            s_t = s_tiles.pop(n)
            m_prev = m_s[a, :, cols]
            m_new = jnp.maximum(m_prev, jnp.max(s_t, axis=0, keepdims=True))
            m_s[a, :, cols] = m_new
            pv = _dot(vt, jnp.exp(s_t - m_new).astype(BF16))
            if pending is not None:
                pa, pcols, palpha, ppv = pending
                acc_s[pa, :, pcols] = palpha * acc_s[pa, :, pcols] + ppv
            pending = (a, cols, jnp.exp(m_prev - m_new), pv)
            if n + ATTN_AHEAD < len(chains):
                s_tiles[n + ATTN_AHEAD] = scores(n + ATTN_AHEAD)
        pa, pcols, palpha, ppv = pending
        acc_s[pa, :, pcols] = palpha * acc_s[pa, :, pcols] + ppv

        @pl.when(j == nk - 1)
        def _():
            for a in range(GROUP):
                acc = acc_s[a]
                l = acc[hd:hd + 1, :]
                o_ref[a] = acc[0:hd, :] / l
                lse_ref[a] = m_s[a] + jnp.log(l)

    return _pcall(
        body, name=name, grid=(kv, nq, nk),
        in_specs=[pl.BlockSpec((GROUP, tq, hd), lambda g, i, j: (g, i, 0)),
                  pl.BlockSpec((None, tk, hd), lambda g, i, j: (g, j, 0)),
                  pl.BlockSpec((None, vrows, tk), lambda g, i, j: (g, 0, j))],
        out_specs=[pl.BlockSpec((GROUP, hd, tq), lambda g, i, j: (g, 0, i)),
                   pl.BlockSpec((GROUP, 1, tq), lambda g, i, j: (g, 0, i))],
        out_shape=[jax.ShapeDtypeStruct((h, hd, na), F32), jax.ShapeDtypeStruct((h, 1, na), F32)],
        scratch_shapes=[pltpu.VMEM((GROUP, 1, tq), F32), pltpu.VMEM((GROUP, vrows, tq), F32)],
        compiler_params=_cp("parallel", "parallel", "arbitrary"))(q_hm, k_hm, vt_hm)


def _window(win_ref, prev, cur, nxt, first, last, tm):
    win_ref[0:HALO, :] = jnp.where(first, 0.0, prev)
    win_ref[HALO:HALO + tm, :] = cur
    win_ref[HALO + tm:HALO + tm + HALO, :] = jnp.where(last, 0.0, nxt)


def _halo_specs(tm, d, na, colblk):
    per = tm // HALO
    last_blk = na // HALO - 1
    prev = pl.BlockSpec((HALO, d), lambda i: (jnp.maximum(i * per - 1, 0), colblk))
    nxt = pl.BlockSpec((HALO, d), lambda i: (jnp.minimum((i + 1) * per, last_blk), colblk))
    return prev, nxt


def _seq_ends(i, n_lat, na, tm):
    first = jnp.logical_or(i == 0, i == n_lat // tm)
    last = jnp.logical_or(i == n_lat // tm - 1, i == na // tm - 1)
    return first, last


def _conv_fwd(p, conv_w, conv_b, ln_g, ln_b, n_lat, tm, name):
    na = p.shape[0]
    ktaps, d = conv_w.shape
    pad = ktaps // 2

    def body(a_ref, ap_ref, an_ref, g_ref, gp_ref, gn_ref, ga_ref, w_ref, cb_ref, lg_ref, lb_ref,
             y5_ref, y2_ref, win):
        i = pl.program_id(0)
        first, last = _seq_ends(i, n_lat, na, tm)
        _window(win, ap_ref[...] * _sig(gp_ref[...]), a_ref[...] * _sig(g_ref[...]),
                an_ref[...] * _sig(gn_ref[...]), first, last, tm)
        acc = jnp.zeros((tm, d), F32)
        for k in range(ktaps):
            acc = acc + w_ref[k:k + 1, :] * win[pl.ds(HALO - pad + k, tm), :]
        y2 = acc + cb_ref[...]
        y2_ref[...] = y2
        xc = y2 - _mean(y2)
        y3 = xc * lax.rsqrt(_mean(xc * xc) + EPS) * lg_ref[...] + lb_ref[...]
        gate = ga_ref[...]
        y5_ref[...] = ((y3 * _sig(y3)) * (gate * _sig(gate))).astype(BF16)

    ap, an = _halo_specs(tm, d, na, 0)
    gp, gn = _halo_specs(tm, d, na, 1)
    return _pcall(
        body, name=name, grid=(na // tm,),
        in_specs=[_rows(tm, d, 0), ap, an, _rows(tm, d, 1), gp, gn, _rows(tm, d, 2),
                  _full((ktaps, d)), _full((1, d)), _full((1, d)), _full((1, d))],
        out_specs=[_rows(tm, d), _rows(tm, d)],
        out_shape=[jax.ShapeDtypeStruct((na, d), BF16), jax.ShapeDtypeStruct((na, d), F32)],
        scratch_shapes=[pltpu.VMEM((tm + 2 * HALO, d), F32)],
        compiler_params=_cp("parallel"))(p, p, p, p, p, p, p, conv_w, conv_b, ln_g, ln_b)


def _merge_fwd(y5, o_tm, p, xa, modv, gpost, wc, wa, wo, n_lat, tm, name):
    na, d = xa.shape

    def body(y5_ref, o_ref, gb_ref, ma_ref, mb_ref, x_ref, mod_ref, gp_ref, wc_ref, wa_ref, wo_ref,
             xn_ref, yc_ref, ya_ref, out_ref, z_ref, og_ref):
        is_ctx = pl.program_id(0) >= n_lat // tm
        gate_b = gb_ref[...]
        og = (o_ref[...] * (gate_b * _sig(gate_b))).astype(BF16)
        og_ref[...] = og
        yc = _dot(y5_ref[...], wc_ref[...])
        ya = _dot(og, wa_ref[...])
        yc_ref[...] = yc
        ya_ref[...] = ya
        z = (_sig(ma_ref[...]) * yc + _sig(mb_ref[...]) * ya).astype(BF16)
        z_ref[...] = z
        out = _dot(z, wo_ref[...])
        out_ref[...] = out
        gt = _seg_rows(mod_ref, 2 * d, 3 * d, is_ctx)
        xn_ref[...] = x_ref[...] + gt * (out * lax.rsqrt(_mean(out * out) + EPS) * gp_ref[...])

    f32o = jax.ShapeDtypeStruct((na, d), F32)
    bfo = jax.ShapeDtypeStruct((na, d), BF16)
    return _pcall(
        body, name=name, grid=(na // tm,),
        in_specs=[_rows(tm, d), _rows(tm, d), _rows(tm, d, 4), _rows(tm, d, 5), _rows(tm, d, 6), _rows(tm, d),
                  _full((8, 3 * d)), _full((1, d)), _full((d, d)), _full((d, d)), _full((d, d))],
        out_specs=[_rows(tm, d)] * 6,
        out_shape=[f32o, f32o, f32o, f32o, bfo, bfo],
        compiler_params=_cp("parallel"))(y5, o_tm, p, p, p, xa, modv, gpost, wc, wa, wo)


def _loss_grad(xa, target, n_lat, tm, name):
    na, d = xa.shape
    nlt = n_lat // tm

    def body(x_ref, t_ref, dx_ref, loss_ref):
        i = pl.program_id(0)

        @pl.when(i == 0)
        def _():
            loss_ref[...] = jnp.zeros_like(loss_ref)

        @pl.when(i < nlt)
        def _():
            err = x_ref[...] - t_ref[...]
            dx_ref[...] = err * (1.0 / d)
            loss_ref[...] += 0.5 * jnp.sum(_mean(err * err))

        @pl.when(i >= nlt)
        def _():
            dx_ref[...] = jnp.zeros_like(dx_ref)

    return _pcall(
        body, name=name, grid=(na // tm,),
        in_specs=[_rows(tm, d), pl.BlockSpec((tm, d), lambda i: (jnp.minimum(i, nlt - 1), 0))],
        out_specs=[_rows(tm, d), _full((8, 128))],
        out_shape=[jax.ShapeDtypeStruct((na, d), F32), jax.ShapeDtypeStruct((8, 128), F32)],
        compiler_params=_cp("arbitrary"))(xa, target)


def _merge_bwd(dxn, out, yc, ya, o_tm, p, y2, modv, gpost, ln_g, ln_b, wo_t, wc_t, wa_t, esel, n_lat, tm, name):
    na, d = dxn.shape

    def body(dx_ref, out_ref, yc_ref, ya_ref, o_ref, gb_ref, ma_ref, mb_ref, gta_ref, y2_ref,
             mod_ref, gp_ref, lg_ref, lb_ref, wot_ref, wct_ref, wat_ref, es_ref,
             dgb_ref, dma_ref, dmb_ref, dgta_ref, do_ref, dout_ref, dyc_ref, dya_ref, dy2_ref, dl_ref, acc_ref):
        i = pl.program_id(0)
        ctx_tile = i >= n_lat // tm

        @pl.when(i == 0)
        def _():
            acc_ref[...] = jnp.zeros_like(acc_ref)

        gt = _seg_rows(mod_ref, 2 * d, 3 * d, ctx_tile)
        gp = gp_ref[...]
        dx = dx_ref[...]
        out = out_ref[...]
        r2 = lax.rsqrt(_mean(out * out) + EPS)
        n2 = out * r2
        dgt = _colsum(dx * (n2 * gp))

        @pl.when(ctx_tile)
        def _():
            acc_ref[1:2, :] += dgt

        @pl.when(jnp.logical_not(ctx_tile))
        def _():
            acc_ref[0:1, :] += dgt

        acc_ref[2:3, :] += _colsum(dx * gt * n2)
        dn2 = dx * gt * gp
        dout = (r2 * (dn2 - n2 * _mean(dn2 * n2))).astype(BF16)
        dout_ref[...] = dout
        dz = _dot(dout, wot_ref[...])
        sa = _sig(ma_ref[...])
        sb = _sig(mb_ref[...])
        dyc = (dz * sa).astype(BF16)
        dya = (dz * sb).astype(BF16)
        dyc_ref[...] = dyc
        dya_ref[...] = dya
        dma_ref[...] = (dz * yc_ref[...] * sa * (1.0 - sa)).astype(BF16)
        dmb_ref[...] = (dz * ya_ref[...] * sb * (1.0 - sb)).astype(BF16)
        dy5 = _dot(dyc, wct_ref[...])
        dog = _dot(dya, wat_ref[...])

        gate_b = gb_ref[...]
        sgb = _sig(gate_b)
        o = o_ref[...]
        do = dog * (gate_b * sgb)
        do_ref[...] = do.astype(BF16)
        dgb_ref[...] = (dog * o * (sgb * (1.0 + gate_b * (1.0 - sgb)))).astype(BF16)
        dl_ref[...] = _split_dot(do * o, es_ref[...])

        y2 = y2_ref[...]
        xc = y2 - _mean(y2)
        rstd = lax.rsqrt(_mean(xc * xc) + EPS)
        xhat = xc * rstd
        lg = lg_ref[...]
        y3 = xhat * lg + lb_ref[...]
        s3 = _sig(y3)
        gate_a = gta_ref[...]
        sga = _sig(gate_a)
        dgta_ref[...] = (dy5 * (y3 * s3) * (sga * (1.0 + gate_a * (1.0 - sga)))).astype(BF16)
        dy3 = dy5 * (gate_a * sga) * (s3 * (1.0 + y3 * (1.0 - s3)))
        acc_ref[3:4, :] += _colsum(dy3 * xhat)
        acc_ref[4:5, :] += _colsum(dy3)
        dxh = dy3 * lg
        dy2 = rstd * (dxh - _mean(dxh) - xhat * _mean(dxh * xhat))
        dy2_ref[...] = dy2
        acc_ref[5:6, :] += _colsum(dy2)

    f32o = jax.ShapeDtypeStruct((na, d), F32)
    bfo = jax.ShapeDtypeStruct((na, d), BF16)
    r = _rows(tm, d)
    return _pcall(
        body, name=name, grid=(na // tm,),
        in_specs=[r, r, r, r, r, _rows(tm, d, 4), _rows(tm, d, 5), _rows(tm, d, 6), _rows(tm, d, 2), r,
                  _full((8, 3 * d)), _full((1, d)), _full((1, d)), _full((1, d)),
                  _full((d, d)), _full((d, d)), _full((d, d)), _full((d, 128))],
        out_specs=[r] * 9 + [_rows(tm, 128), _full((8, d))],
        out_shape=[bfo] * 8 + [f32o, jax.ShapeDtypeStruct((na, 128), F32), jax.ShapeDtypeStruct((8, d), F32)],
        compiler_params=_cp("arbitrary"),
    )(dxn, out, yc, ya, o_tm, p, p, p, p, y2, modv, gpost, ln_g, ln_b, wo_t, wc_t, wa_t, esel)


def _attn_bwd(q_hm, qt_hm, k_hm, kt_hm, v_hm, do_hm, dot_hm, lse_r, dl_r, na, tq, tk, cw, name):
    h, _, hd = q_hm.shape
    kv, nkeys, _ = k_hm.shape
    nq, nk = na // tq, nkeys // tk
    nsub = tq // cw
    chains = [(a, u) for a in range(GROUP) for u in range(nsub)]

    def body(q_ref, qt_ref, k_ref, kt_ref, v_ref, do_ref, dot_ref, lse_ref, dl_ref, dq_hbm, dk_ref, dv_ref,
             dq_acc, dk_acc, dv_acc, sem):
        g = pl.program_id(0)
        j = pl.program_id(1)
        i = pl.program_id(2)

        @pl.when(jnp.logical_and(j == 0, i == 0))
        def _():
            dq_acc[...] = jnp.zeros_like(dq_acc)

        @pl.when(i == 0)
        def _():
            dk_acc[...] = jnp.zeros_like(dk_acc)
            dv_acc[...] = jnp.zeros_like(dv_acc)

        k = k_ref[...]
        v = v_ref[...]
        kt = kt_ref[...]

        def products(n):
            a, u = chains[n]
            rows = slice(u * cw, (u + 1) * cw)
            return _dot_nt(k, q_ref[a, rows, :]), _dot_nt(v, do_ref[a, rows, :])

        def accumulate(done):
            a, u, dv_t, dk_t, dq_t = done
            dv_acc[...] += dv_t
            dk_acc[...] += dk_t
            at = pl.multiple_of(i * tq + u * cw, cw)
            dq_acc[a, :, pl.ds(at, cw)] += dq_t

        tiles = {n: products(n) for n in range(min(ATTN_AHEAD, len(chains)))}
        pending = None
        for n, (a, u) in enumerate(chains):
            cols = slice(u * cw, (u + 1) * cw)
            s_t, dp_t = tiles.pop(n)
            p_t = jnp.exp(s_t - lse_ref[a, :, cols])
            ds_b = (p_t * (dp_t - dl_ref[a, :, cols])).astype(BF16)
            p_b = p_t.astype(BF16)
            dv_t = _dot_nt(dot_ref[a, :, cols], p_b)
            dk_t = _dot_nt(qt_ref[a, :, cols], ds_b)
            dq_t = _dot(kt, ds_b)
            if pending is not None:
                accumulate(pending)
            pending = (a, u, dv_t, dk_t, dq_t)
            if n + ATTN_AHEAD < len(chains):
                tiles[n + ATTN_AHEAD] = products(n + ATTN_AHEAD)
        accumulate(pending)

        @pl.when(i == nq - 1)
        def _():
            dk_ref[...] = dk_acc[...]
            dv_ref[...] = dv_acc[...]

        @pl.when(jnp.logical_and(j == nk - 1, i == nq - 1))
        def _():
            cp = pltpu.make_async_copy(dq_acc, dq_hbm.at[pl.ds(g * GROUP, GROUP)], sem)
            cp.start()
            cp.wait()

    qspec = pl.BlockSpec((GROUP, tq, hd), lambda g, j, i: (g, i, 0))
    qtspec = pl.BlockSpec((GROUP, hd, tq), lambda g, j, i: (g, 0, i))
    kspec = pl.BlockSpec((None, tk, hd), lambda g, j, i: (g, j, 0))
    ktspec = pl.BlockSpec((None, hd, tk), lambda g, j, i: (g, 0, j))
    rspec = pl.BlockSpec((GROUP, 1, tq), lambda g, j, i: (g, 0, i))
    return _pcall(
        body, name=name, grid=(kv, nk, nq),
        in_specs=[qspec, qtspec, kspec, ktspec, kspec, qspec, qtspec, rspec, rspec],
        out_specs=[ANY, ktspec, ktspec],
        out_shape=[jax.ShapeDtypeStruct((h, hd, na), F32), jax.ShapeDtypeStruct((kv, hd, nkeys), F32),
                   jax.ShapeDtypeStruct((kv, hd, nkeys), F32)],
        scratch_shapes=[pltpu.VMEM((GROUP, hd, na), F32), pltpu.VMEM((hd, tk), F32), pltpu.VMEM((hd, tk), F32),
                        pltpu.SemaphoreType.DMA],
        compiler_params=_cp("arbitrary", "arbitrary", "arbitrary"),
    )(q_hm, qt_hm, k_hm, kt_hm, v_hm, do_hm, dot_hm, lse_r, dl_r)


def _qknorm_bwd(dq_tm, dk_tm, dv_tm, p, cos_t, sin_t, gq, gk, bd, tm, name):
    na = p.shape[0]
    d = gq.shape[1]
    kw = d // GROUP

    def body(dq_ref, dk_ref, dv_ref, q_ref, kv_ref, cos_ref, sin_ref, gq_ref, gk_ref, bd_ref,
             dqo_ref, dkvo_ref, acc_ref):
        @pl.when(pl.program_id(0) == 0)
        def _():
            acc_ref[...] = jnp.zeros_like(acc_ref)

        cos = cos_ref[...]
        sin = sin_ref[...]

        def back(dy, xh, g, w):
            bdw = bd_ref[0:w, 0:w]
            dn = dy * _lane_tile(cos, w) - _partner(dy) * _lane_tile(sin, w)
            rs = lax.rsqrt(_split_dot(xh * xh, bdw) * (1.0 / HEAD_DIM) + EPS)
            y = xh * rs
            dg = _colsum(dn * y)
            dyn = dn * g
            dx = rs * (dyn - y * (_split_dot(dyn * y, bdw) * (1.0 / HEAD_DIM)))
            return dx, dg

        dq, dgq = back(dq_ref[...] * ATTN_SCALE, q_ref[...], gq_ref[...], d)
        dqo_ref[...] = dq.astype(BF16)
        acc_ref[0:1, :] += dgq
        kv = kv_ref[...]
        dk, dgk = back(dk_ref[...], kv[:, 0:kw], gk_ref[...], kw)
        acc_ref[1:2, 0:kw] += dgk
        dkvo_ref[:, 0:kw] = dk.astype(BF16)
        dkvo_ref[:, kw:2 * kw] = dv_ref[...].astype(BF16)

    return _pcall(
        body, name=name, grid=(na // tm,),
        in_specs=[_rows(tm, d), _rows(tm, kw), _rows(tm, kw), _rows(tm, d, 3), _rows(tm, d // 2, 14),
                  _rows(tm, 128), _rows(tm, 128), _full((1, d)), _full((1, kw)), _full((d, d))],
        out_specs=[_rows(tm, d), _rows(tm, d // 2), _full((8, d))],
        out_shape=[jax.ShapeDtypeStruct((na, d), BF16), jax.ShapeDtypeStruct((na, d // 2), BF16),
                   jax.ShapeDtypeStruct((8, d), F32)],
        compiler_params=_cp("arbitrary"))(dq_tm, dk_tm, dv_tm, p, p, cos_t, sin_t, gq, gk, bd)


def _conv_bwd(dy2, p, conv_w, n_lat, tm, name):
    na = p.shape[0]
    ktaps, d = conv_w.shape
    pad = ktaps // 2

    def body(dy_ref, dyp_ref, dyn_ref, a_ref, ap_ref, an_ref, g_ref, gp_ref, gn_ref, w_ref,
             da_ref, dg_ref, dw_ref, dwin, ywin):
        i = pl.program_id(0)

        @pl.when(i == 0)
        def _():
            dw_ref[...] = jnp.zeros_like(dw_ref)

        first, last = _seq_ends(i, n_lat, na, tm)
        dy = dy_ref[...]
        a = a_ref[...]
        sg = _sig(g_ref[...])
        _window(dwin, dyp_ref[...], dy, dyn_ref[...], first, last, tm)
        _window(ywin, ap_ref[...] * _sig(gp_ref[...]), a * sg, an_ref[...] * _sig(gn_ref[...]), first, last, tm)
        dy1 = jnp.zeros((tm, d), F32)
        for k in range(ktaps):
            dy1 = dy1 + w_ref[k:k + 1, :] * dwin[pl.ds(HALO + pad - k, tm), :]
            dw_ref[k:k + 1, :] += _colsum(dy * ywin[pl.ds(HALO - pad + k, tm), :])
        da_ref[...] = (dy1 * sg).astype(BF16)
        dg_ref[...] = (dy1 * a * sg * (1.0 - sg)).astype(BF16)

    dyp, dyn = _halo_specs(tm, d, na, 0)
    ap, an = _halo_specs(tm, d, na, 0)
    gp, gn = _halo_specs(tm, d, na, 1)
    bfo = jax.ShapeDtypeStruct((na, d), BF16)
    return _pcall(
        body, name=name, grid=(na // tm,),
        in_specs=[_rows(tm, d), dyp, dyn, _rows(tm, d, 0), ap, an, _rows(tm, d, 1), gp, gn, _full((ktaps, d))],
        out_specs=[_rows(tm, d), _rows(tm, d), _full((ktaps, d))],
        out_shape=[bfo, bfo, jax.ShapeDtypeStruct((ktaps, d), F32)],
        scratch_shapes=[pltpu.VMEM((tm + 2 * HALO, d), F32), pltpu.VMEM((tm + 2 * HALO, d), F32)],
        compiler_params=_cp("arbitrary"))(dy2, dy2, dy2, p, p, p, p, p, p, conv_w)


def _inproj_bwd(segs, dkv, xa, dxn, modv, gpre, wp_t, n_lat, tm, name):
    na, d = xa.shape
    nseg = len(segs)
    wrows = wp_t.shape[0]

    def body(*refs):
        seg_refs = refs[:nseg]
        dkv_ref, x_ref, dxn_ref, mod_ref, g_ref, wt_hbm, dx_ref, acc_ref, wt, sem = refs[nseg:]
        i = pl.program_id(0)
        ctx_tile = i >= n_lat // tm

        @pl.when(i == 0)
        def _():
            cp = pltpu.make_async_copy(wt_hbm, wt, sem)
            cp.start()
            cp.wait()
            acc_ref[...] = jnp.zeros_like(acc_ref)

        dh = _dot(dkv_ref[...], wt[nseg * d:wrows, :])
        for s in range(nseg):
            dh = dh + _dot(seg_refs[s][...], wt[s * d:(s + 1) * d, :])
        x = x_ref[...]
        r = lax.rsqrt(_mean(x * x) + EPS)
        xn = x * r
        g = g_ref[...]
        sc1 = 1.0 + _seg_rows(mod_ref, d, 2 * d, ctx_tile)
        dsh = _colsum(dh)
        dsc = _colsum(dh * (xn * g))

        @pl.when(ctx_tile)
        def _():
            acc_ref[1:2, :] += dsh
            acc_ref[3:4, :] += dsc

        @pl.when(jnp.logical_not(ctx_tile))
        def _():
            acc_ref[0:1, :] += dsh
            acc_ref[2:3, :] += dsc

        acc_ref[4:5, :] += _colsum(dh * xn * sc1)
        dxh = dh * g * sc1
        dx_ref[...] = dxn_ref[...] + r * (dxh - xn * _mean(dxh * xn))

    r_ = _rows(tm, d)
    return _pcall(
        body, name=name, grid=(na // tm,),
        in_specs=[r_] * nseg + [_rows(tm, d // 2), r_, r_, _full((8, 3 * d)), _full((1, d)), ANY],
        out_specs=[r_, _full((8, d))],
        out_shape=[jax.ShapeDtypeStruct((na, d), F32), jax.ShapeDtypeStruct((8, d), F32)],
        scratch_shapes=[pltpu.VMEM(wp_t.shape, BF16), pltpu.SemaphoreType.DMA],
        compiler_params=_cp("arbitrary"))(*segs, dkv, xa, dxn, modv, gpre, wp_t)


def _grad_matmul(a, b, tk, name):
    na, ka = a.shape
    nb = b.shape[1]
    tn = min(nb, 512)

    def body(a_ref, b_ref, o_ref):
        @pl.when(pl.program_id(1) == 0)
        def _():
            o_ref[...] = jnp.zeros_like(o_ref)

        o_ref[...] += _dot_tn(a_ref[...], b_ref[...])

    return _pcall(
        body, name=name, grid=(nb // tn, na // tk),
        in_specs=[pl.BlockSpec((tk, ka), lambda n, k: (k, 0)), pl.BlockSpec((tk, tn), lambda n, k: (k, n))],
        out_specs=pl.BlockSpec((ka, tn), lambda n, k: (0, n)),
        out_shape=jax.ShapeDtypeStruct((ka, nb), F32),
        compiler_params=_cp("parallel", "arbitrary"))(a, b)


def _pack(parts, cols, row_mult):
    flat = jnp.concatenate([q.astype(F32).reshape(-1) for q in parts])
    rows = -(-flat.shape[0] // (cols * row_mult)) * row_mult
    return jnp.pad(flat, (0, rows * cols - flat.shape[0])).reshape(1, rows, cols)


def _unpack(flat, shapes):
    out, off = [], 0
    for s in shapes:
        n = math.prod(s)
        out.append(flat[off:off + n].reshape(tuple(s)))
        off += n
    return out


def _cols_by_dest(g):
    l, a, w8 = g.shape
    return g.reshape(l, a, N_DEV, w8 // N_DEV).transpose(2, 0, 1, 3)


def _rows_by_dest(g):
    l, r8, b = g.shape
    return g.reshape(l, N_DEV, r8 // N_DEV, b).transpose(1, 0, 2, 3)


def _cols_from_src(s):
    n, l, a, w = s.shape
    return s.transpose(1, 2, 0, 3).reshape(l, a, n * w)


def _rows_from_src(s):
    n, l, r, b = s.shape
    return s.transpose(1, 0, 2, 3).reshape(l, n * r, b)


def _rope_tables(n_lat, n_ctx):
    half = HEAD_DIM // 2
    rows = n_lat // GRID_W
    row = jnp.repeat(jnp.arange(rows, dtype=F32), GRID_W)
    col = jnp.tile(jnp.arange(GRID_W, dtype=F32), rows)
    inv_freq = ROPE_THETA ** (-jnp.arange(0, half, 2, dtype=F32) / half)
    ang = jnp.concatenate([row[:, None] * inv_freq, col[:, None] * inv_freq], axis=-1)
    cos, sin = jnp.cos(ang), jnp.sin(ang)
    cos_t = jnp.concatenate([jnp.tile(cos, (1, 4)), jnp.ones((n_ctx, 128), F32)], axis=0)
    sin_t = jnp.concatenate([jnp.tile(jnp.concatenate([-sin, sin], axis=-1), (1, 2)),
                             jnp.zeros((n_ctx, 128), F32)], axis=0)
    return cos_t, sin_t


def _to_heads(t):
    na, w = t.shape
    return t.reshape(na, w // HEAD_DIM, HEAD_DIM).transpose(1, 0, 2)


def _from_heads(t):
    nh, na, hd = t.shape
    return t.transpose(1, 0, 2).reshape(na, nh * hd)


def kernel(x, c, ctx, c_ctx, w_mod, b_mod, g_pre, g_post, w_in, conv_w, conv_b, ln_g, ln_b, w_conv_out, q_norm_g, k_norm_g, w_attn_out, w_out, loss_target, m_c_ctx, m_w_mod, m_b_mod, m_g_pre, m_g_post, m_w_in, m_conv_w, m_conv_b, m_ln_g, m_ln_b, m_w_conv_out, m_q_norm_g, m_k_norm_g, m_w_attn_out, m_w_out, v_c_ctx, v_w_mod, v_b_mod, v_g_pre, v_g_post, v_w_in, v_conv_w, v_conv_b, v_ln_g, v_ln_b, v_w_conv_out, v_q_norm_g, v_k_norm_g, v_w_attn_out, v_w_out):
    depth, d, _ = w_mod.shape
    n_lat, n_ctx = x.shape[1], ctx.shape[1]
    na = n_lat + n_ctx
    heads = d // HEAD_DIM
    kw = d // GROUP
    ktaps = conv_w.shape[1]
    tm = n_ctx
    tm_half = tm // 2
    tbig = 3 * tm if na % (3 * tm) == 0 else tm
    tq_lat = 2 * tm if n_lat % (2 * tm) == 0 else tm
    tq_bwd = 4 * tm if n_lat % (4 * tm) == 0 else tm

    s_mod, s_in, s_co, s_ao, s_oo, s_cw = _all_gather(
        [w_mod.astype(BF16), w_in.astype(BF16), w_conv_out.astype(BF16), w_attn_out.astype(BF16),
         w_out.astype(BF16), conv_w], "gather_weights")
    wmod_f = _cols_from_src(s_mod)
    win_f = _cols_from_src(s_in)
    wc_f, wa_f, wo_f = _rows_from_src(s_co), _rows_from_src(s_ao), _rows_from_src(s_oo)
    wp_f = jnp.concatenate([win_f[:, :, :4 * d], win_f[:, :, 4 * d + 2 * kw:], win_f[:, :, 4 * d:4 * d + 2 * kw]], axis=2)
    convw_f = _cols_from_src(s_cw)

    cos_t, sin_t = _rope_tables(n_lat, n_ctx)
    lane = jnp.arange(d)
    bd = (lane[:, None] // HEAD_DIM == lane[None, :] // HEAD_DIM).astype(BF16)
    esel = (lane[:, None] // HEAD_DIM == jnp.arange(128)[None, :]).astype(BF16)
    cvec = jnp.zeros((8, d), F32).at[0].set(c[0]).at[1].set(c_ctx)
    cvec_t = jnp.zeros((d, 128), F32).at[:, 0].set(c[0]).at[:, 1].set(c_ctx)

    xa = jnp.concatenate([x[0], ctx[0]], axis=0)

    saved = []
    for l in range(depth):
        tag = f"_l{l}"
        gq = jnp.tile(q_norm_g[l], heads)[None, :]
        gk = jnp.tile(k_norm_g[l], heads // GROUP)[None, :]
        modv = _mod_fwd(cvec, wmod_f[l], b_mod[l][None, :], "mod_fwd" + tag)
        p, hb = _inproj(xa, modv, g_pre[l][None, :], wp_f[l], n_lat, tbig, "inproj" + tag)
        qr, kr, vb = _qknorm_fwd(p, cos_t, sin_t, gq, gk, bd, tm, "qknorm_fwd" + tag)
        q_hm, k_hm, v_hm = _to_heads(qr), _to_heads(kr), _to_heads(vb)
        vt_ones = jnp.concatenate([v_hm.transpose(0, 2, 1), jnp.ones((heads // GROUP, 16, na), BF16)], axis=1)
        ot_lat, lse_lat = _attn_fwd(q_hm, k_hm, vt_ones, n_lat, tq_lat, tbig, tm, "attn_fwd" + tag)
        ot_ctx, lse_ctx = _attn_fwd(q_hm[:, n_lat:], k_hm[:, n_lat:], vt_ones[:, :, n_lat:], n_ctx, tm, tm, tm,
                                    "attn_ctx_fwd" + tag)
        lse = jnp.concatenate([lse_lat, lse_ctx], axis=2)
        o_tm = jnp.concatenate([ot_lat, ot_ctx], axis=2).reshape(d, na).T
        y5, y2 = _conv_fwd(p, convw_f[l], conv_b[l][None, :], ln_g[l][None, :], ln_b[l][None, :], n_lat, tm,
                           "conv_fwd" + tag)
        xa_new, yc, ya, out, z, og = _merge_fwd(y5, o_tm, p, xa, modv, g_post[l][None, :], wc_f[l], wa_f[l], wo_f[l],
                                                n_lat, tm, "merge_fwd" + tag)
        saved.append(dict(xa=xa, modv=modv, p=p, hb=hb, q_hm=q_hm, k_hm=k_hm, v_hm=v_hm, o_tm=o_tm, lse=lse,
                          qt_hm=qr.T.reshape(heads, HEAD_DIM, na), kt_hm=kr.T.reshape(heads // GROUP, HEAD_DIM, na),
                          y5=y5, y2=y2, yc=yc, ya=ya, out=out, z=z, og=og, gq=gq, gk=gk))
        xa = xa_new

    dxa, loss_blk = _loss_grad(xa, loss_target[0], n_lat, tm, "loss_grad")

    g_wmod, g_win, g_convw, g_wc, g_wa, g_wo = [], [], [], [], [], []
    g_bmod, g_gpre, g_gpost, g_convb, g_lng, g_lnb, g_qg, g_kg = [], [], [], [], [], [], [], []
    g_cctx = jnp.zeros((d,), F32)
    for l in reversed(range(depth)):
        tag = f"_l{l}"
        s = saved[l]
        p = s["p"]
        (dgb, dma, dmb, dgta, do_tm, dout, dyc, dya, dy2, dl128, acc_m) = _merge_bwd(
            dxa, s["out"], s["yc"], s["ya"], s["o_tm"], p, s["y2"], s["modv"], g_post[l][None, :],
            ln_g[l][None, :], ln_b[l][None, :], wo_f[l].T, wc_f[l].T, wa_f[l].T, esel, n_lat, tm_half,
            "merge_bwd" + tag)
        dl_r = dl128[:, :heads].T.reshape(heads, 1, na)
        do_hm, dot_hm = _to_heads(do_tm), do_tm.T.reshape(heads, HEAD_DIM, na)
        q_hm, qt_hm, k_hm, kt_hm, v_hm, lse = s["q_hm"], s["qt_hm"], s["k_hm"], s["kt_hm"], s["v_hm"], s["lse"]
        dqt_lat, dkt_hm, dvt_hm = _attn_bwd(q_hm, qt_hm, k_hm, kt_hm, v_hm, do_hm, dot_hm, lse, dl_r,
                                            n_lat, tq_bwd, tm, tm, "attn_bwd" + tag)
        dqt_ctx, dkt_ctx, dvt_ctx = _attn_bwd(
            q_hm[:, n_lat:], qt_hm[:, :, n_lat:], k_hm[:, n_lat:], kt_hm[:, :, n_lat:], v_hm[:, n_lat:],
            do_hm[:, n_lat:], dot_hm[:, :, n_lat:], lse[:, :, n_lat:], dl_r[:, :, n_lat:],
            n_ctx, tm, tm, tm, "attn_ctx_bwd" + tag)
        dkt_hm = dkt_hm.at[:, :, n_lat:].add(dkt_ctx)
        dvt_hm = dvt_hm.at[:, :, n_lat:].add(dvt_ctx)
        dq_tm = jnp.concatenate([dqt_lat, dqt_ctx], axis=2).reshape(d, na).T
        dq, dkv, acc_q = _qknorm_bwd(dq_tm, dkt_hm.reshape(kw, na).T, dvt_hm.reshape(kw, na).T, p, cos_t, sin_t,
                                     s["gq"], s["gk"], bd, tm, "qknorm_bwd" + tag)
        da, dg, dconvw = _conv_bwd(dy2, p, convw_f[l], n_lat, tm, "conv_bwd" + tag)
        segs = [da, dg, dgta, dq, dgb, dma, dmb]
        dxa, acc_h = _inproj_bwd(segs, dkv, s["xa"], dxa, s["modv"], g_pre[l][None, :], wp_f[l].T, n_lat, tm,
                                 "inproj_bwd" + tag)

        dwp = [_grad_matmul(s["hb"], sg, tbig, f"grad_w_in{k}" + tag) for k, sg in enumerate(segs + [dkv])]
        g_win.append(jnp.concatenate(dwp[:4] + [dwp[7]] + dwp[4:7], axis=1))
        g_wc.append(_grad_matmul(s["y5"], dyc, tbig, "grad_w_conv_out" + tag))
        g_wa.append(_grad_matmul(s["og"], dya, tbig, "grad_w_attn_out" + tag))
        g_wo.append(_grad_matmul(s["z"], dout, tbig, "grad_w_out" + tag))
        g_convw.append(dconvw)

        dmod = jnp.zeros((8, 3 * d), F32)
        dmod = dmod.at[0].set(jnp.concatenate([acc_h[0], acc_h[2], acc_m[0]]))
        dmod = dmod.at[1].set(jnp.concatenate([acc_h[1], acc_h[3], acc_m[1]]))
        dwm, dbm, dcv = _mod_bwd(dmod, cvec, cvec_t, wmod_f[l], "mod_bwd" + tag)
        g_wmod.append(dwm)
        g_bmod.append(dbm[0])
        g_cctx = g_cctx + dcv[1]
        g_gpre.append(acc_h[4])
        g_gpost.append(acc_m[2])
        g_lng.append(acc_m[3])
        g_lnb.append(acc_m[4])
        g_convb.append(acc_m[5])
        g_qg.append(acc_q[0].reshape(heads, HEAD_DIM).sum(0))
        g_kg.append(acc_q[1, :kw].reshape(heads // GROUP, HEAD_DIM).sum(0))

    grad_x = dxa[:n_lat][None]

    def stack(lst):
        return jnp.stack(lst[::-1])

    big_names = ["w_mod", "w_in", "w_conv_out", "w_attn_out", "w_out", "conv_w"]
    big_w = dict(w_mod=w_mod, w_in=w_in, w_conv_out=w_conv_out, w_attn_out=w_attn_out, w_out=w_out, conv_w=conv_w)
    big_m = dict(w_mod=m_w_mod, w_in=m_w_in, w_conv_out=m_w_conv_out, w_attn_out=m_w_attn_out, w_out=m_w_out,
                 conv_w=m_conv_w)
    big_v = dict(w_mod=v_w_mod, w_in=v_w_in, w_conv_out=v_w_conv_out, w_attn_out=v_w_attn_out, w_out=v_w_out,
                 conv_w=v_conv_w)
    by_dest = [_cols_by_dest(stack(g_wmod)), _cols_by_dest(stack(g_win)), _rows_by_dest(stack(g_wc)),
               _rows_by_dest(stack(g_wa)), _rows_by_dest(stack(g_wo)), _cols_by_dest(stack(g_convw))]
    send = [q.reshape((4, 2) + q.shape[1:]) for q in by_dest]
    from_sibling = _swap_with_sibling(send, "reduce_sibling")
    chip_part = [_add_sibling_part(sb, rc, "reduce_sibling_add_" + n)
                 for n, sb, rc in zip(big_names, send, from_sibling)]
    from_chips = _exchange_chips(chip_part, "reduce_chips")
    big_g, big_d, big_nm, big_nv = {}, {}, {}, {}
    for n, st in zip(big_names, from_chips):
        big_g[n], big_d[n], big_nm[n], big_nv[n] = _sum_adamw(st, big_w[n], big_m[n], big_v[n], "adamw_" + n)

    small_names = ["c_ctx", "b_mod", "g_pre", "g_post", "conv_b", "ln_g", "ln_b", "q_norm_g", "k_norm_g", "loss"]
    zero1 = jnp.zeros((1,), F32)
    small_w = dict(c_ctx=c_ctx, b_mod=b_mod, g_pre=g_pre, g_post=g_post, conv_b=conv_b, ln_g=ln_g, ln_b=ln_b,
                   q_norm_g=q_norm_g, k_norm_g=k_norm_g, loss=zero1)
    small_m = dict(c_ctx=m_c_ctx, b_mod=m_b_mod, g_pre=m_g_pre, g_post=m_g_post, conv_b=m_conv_b, ln_g=m_ln_g,
                   ln_b=m_ln_b, q_norm_g=m_q_norm_g, k_norm_g=m_k_norm_g, loss=zero1)
    small_v = dict(c_ctx=v_c_ctx, b_mod=v_b_mod, g_pre=v_g_pre, g_post=v_g_post, conv_b=v_conv_b, ln_g=v_ln_g,
                   ln_b=v_ln_b, q_norm_g=v_q_norm_g, k_norm_g=v_k_norm_g, loss=zero1)
    small_g = dict(c_ctx=g_cctx, b_mod=stack(g_bmod), g_pre=stack(g_gpre), g_post=stack(g_gpost),
                   conv_b=stack(g_convb), ln_g=stack(g_lng), ln_b=stack(g_lnb), q_norm_g=stack(g_qg),
                   k_norm_g=stack(g_kg), loss=loss_blk[0, 0:1])
    small_shapes = [small_w[n].shape for n in small_names]

    def pack_small(tree):
        return _pack([tree[n] for n in small_names], d, 8)

    small_parts, = _all_gather([pack_small(small_g)], "gather_small_grads")
    small_out = _sum_adamw(small_parts, pack_small(small_w), pack_small(small_m), pack_small(small_v),
                           "adamw_replicated")
    sm_g, sm_d, sm_nm, sm_nv = [dict(zip(small_names, _unpack(o.reshape(-1), small_shapes))) for o in small_out]
    loss = sm_g["loss"].reshape(())

    order = ["c_ctx", "w_mod", "b_mod", "g_pre", "g_post", "w_in", "conv_w", "conv_b", "ln_g", "ln_b",
             "w_conv_out", "q_norm_g", "k_norm_g", "w_attn_out", "w_out"]

    def pick(big, small):
        return [big[n] if n in big else small[n] for n in order]

    return (loss, grad_x, *pick(big_g, sm_g), *pick(big_d, sm_d), *pick(big_nm, sm_nm), *pick(big_nv, sm_nv))
```

```python
import math

import jax
import jax.numpy as jnp
from jax import lax
from jax.experimental import pallas as pl
from jax.experimental.pallas import tpu as pltpu

F32 = jnp.float32
BF16 = jnp.bfloat16

HEAD_DIM = 64
GROUP = 4
GRID_W = 64
ROPE_THETA = 10000.0
EPS = 1e-6
ATTN_SCALE = HEAD_DIM ** -0.5
HALO = 16
CONV_ROWS = 64
ATTN_AHEAD = 3

ADAM_LR = 0.001
ADAM_B1 = 0.9
ADAM_B2 = 0.999
ADAM_EPS = 1e-08
ADAM_WD = 0.01
ADAM_STEP = 10

N_DEV = 8
MESH_AXES = ("x", "y", "c")
V7X_VMEM_LIMIT = 56 * 1024 * 1024
NEG_BIG = -1e30

MESH = pl.DeviceIdType.MESH
ANY = pl.BlockSpec(memory_space=pl.ANY)


def _pcall(body, **kw):
    return pl.pallas_call(body, **kw)


def _cp(*sem):
    return pltpu.CompilerParams(dimension_semantics=sem, vmem_limit_bytes=V7X_VMEM_LIMIT)


def _sig(x):
    return 1.0 / (1.0 + jnp.exp(-x))


def _mean(x):
    return jnp.mean(x, axis=-1, keepdims=True)


def _colsum(x):
    return jnp.sum(x, axis=0, keepdims=True)


def _bf_round(x):
    return x.astype(BF16).astype(F32)


def _dot(a, b):
    return jnp.dot(a, b, preferred_element_type=F32)


def _dot_nt(a, b):
    return lax.dot_general(a, b, (((1,), (1,)), ((), ())), preferred_element_type=F32)


def _dot_tn(a, b):
    return lax.dot_general(a, b, (((0,), (0,)), ((), ())), preferred_element_type=F32)


def _split_dot(x, m):
    hi = x.astype(BF16)
    lo = (x - hi.astype(F32)).astype(BF16)
    return _dot(hi, m) + _dot(lo, m)


def _full(shape):
    nd = len(shape)
    return pl.BlockSpec(shape, lambda *_: (0,) * nd)


def _rows(tm, width, colblk=0):
    return pl.BlockSpec((tm, width), lambda i: (i, colblk))


def _my_place():
    return lax.axis_index("x"), lax.axis_index("y"), lax.axis_index("c")


def _sem_arrays(n):
    return [pltpu.SemaphoreType.DMA((n,)), pltpu.SemaphoreType.DMA((n,))]


def _all_gather(shards, name):
    n = len(shards)

    def body(*refs):
        x_refs, out_refs = refs[:n], refs[n:2 * n]
        send_sems, recv_sems, local_sems = refs[2 * n:]
        x, y, c = _my_place()
        me, sibling = (x, y, c), (x, y, 1 - c)
        chips = [(1 - x, y), (x, 1 - y), (1 - x, 1 - y)]

        def slab(a, px, py, pc):
            return out_refs[a].at[4 * px + 2 * py + pc]

        def copies(k, block, to, from_input=False):
            return [pltpu.make_async_remote_copy(
                src_ref=x_refs[a] if from_input else slab(a, *block), dst_ref=slab(a, *block),
                send_sem=send_sems.at[k * n + a], recv_sem=recv_sems.at[k * n + a],
                device_id=to, device_id_type=MESH) for a in range(n)]

        mine = [pltpu.make_async_copy(x_refs[a], slab(a, *me), local_sems.at[a]) for a in range(n)]
        for cp in mine:
            cp.start()
        first = copies(0, me, sibling, True)
        for j, chip in enumerate(chips):
            first += copies(1 + j, me, (*chip, c), True)
        for cp in first:
            cp.start()
        passed = []
        for j, chip in enumerate(chips):
            for cp in copies(1 + j, (*chip, c), me):
                cp.wait_recv()
            onward = copies(4 + j, (*chip, c), sibling)
            for cp in onward:
                cp.start()
            passed += onward
        for cp in copies(0, sibling, me):
            cp.wait_recv()
        for j, chip in enumerate(chips):
            for cp in copies(4 + j, (*chip, 1 - c), me):
                cp.wait_recv()
        for cp in first + passed:
            cp.wait_send()
        for cp in mine:
            cp.wait()

    return _pcall(
        body, name=name,
        out_shape=[jax.ShapeDtypeStruct((N_DEV,) + q.shape, q.dtype) for q in shards],
        in_specs=[ANY] * n, out_specs=[ANY] * n,
        scratch_shapes=_sem_arrays(7 * n) + [pltpu.SemaphoreType.DMA((n,))],
    )(*shards)


def _swap_with_sibling(bufs, name):
    n = len(bufs)

    def body(*refs):
        buf_refs, recv_refs = refs[:n], refs[n:2 * n]
        send_sems, recv_sems = refs[2 * n:]
        x, y, c = _my_place()
        copies = [
            pltpu.make_async_remote_copy(
                src_ref=buf_refs[a].at[k, 1 - c], dst_ref=recv_refs[a].at[k],
                send_sem=send_sems.at[k * n + a], recv_sem=recv_sems.at[k * n + a],
                device_id=(x, y, 1 - c), device_id_type=MESH)
            for k in range(4) for a in range(n)]
        for cp in copies:
            cp.start()
        for cp in copies:
            cp.wait()

    return _pcall(
        body, name=name,
        out_shape=[jax.ShapeDtypeStruct((4,) + q.shape[2:], q.dtype) for q in bufs],
        in_specs=[ANY] * n, out_specs=[ANY] * n,
        scratch_shapes=_sem_arrays(4 * n),
    )(*bufs)


def _exchange_chips(parts, name):
    n = len(parts)

    def body(*refs):
        s_refs, recv_refs = refs[:n], refs[n:2 * n]
        send_sems, recv_sems, local_sems = refs[2 * n:]
        x, y, c = _my_place()
        mychip = 2 * x + y
        chips = [(1 - x, y), (x, 1 - y), (1 - x, 1 - y)]
        mine = [pltpu.make_async_copy(s_refs[a].at[mychip], recv_refs[a].at[mychip], local_sems.at[a])
                for a in range(n)]
        for cp in mine:
            cp.start()
        copies = [
            pltpu.make_async_remote_copy(
                src_ref=s_refs[a].at[2 * px + py], dst_ref=recv_refs[a].at[mychip],
                send_sem=send_sems.at[j * n + a], recv_sem=recv_sems.at[j * n + a],
                device_id=(px, py, c), device_id_type=MESH)
            for j, (px, py) in enumerate(chips) for a in range(n)]
        for cp in copies:
            cp.start()
        for cp in copies:
            cp.wait()
        for cp in mine:
            cp.wait()

    return _pcall(
        body, name=name,
        out_shape=[jax.ShapeDtypeStruct(q.shape, q.dtype) for q in parts],
        in_specs=[ANY] * n, out_specs=[ANY] * n,
        scratch_shapes=_sem_arrays(3 * n) + [pltpu.SemaphoreType.DMA((n,))],
    )(*parts)


def _row_tile(a):
    for t in range(256, 7, -8):
        if a % t == 0:
            return t
    return a


def _add_sibling_part(buf, recv, out_dtype, name):
    _, _, nl, a, b = buf.shape
    ta = _row_tile(a)
    core = lax.axis_index("c").astype(jnp.int32).reshape(1)

    def body(core_ref, a_ref, b_ref, o_ref):
        o_ref[...] = (a_ref[...] + b_ref[...]).astype(o_ref.dtype)

    grid_spec = pltpu.PrefetchScalarGridSpec(
        num_scalar_prefetch=1, grid=(4, nl, a // ta),
        in_specs=[pl.BlockSpec((None, None, None, ta, b), lambda k, l, r, cr: (k, cr[0], l, r, 0)),
                  pl.BlockSpec((None, None, ta, b), lambda k, l, r, cr: (k, l, r, 0))],
        out_specs=pl.BlockSpec((None, None, ta, b), lambda k, l, r, cr: (k, l, r, 0)))
    return _pcall(body, name=name, grid_spec=grid_spec,
                  out_shape=jax.ShapeDtypeStruct((4, nl, a, b), out_dtype),
                  compiler_params=_cp("parallel", "parallel", "parallel"))(core, buf, recv)


def _sum_adamw(stack, w, m, v, name):
    ns, nl, a, b = stack.shape
    ta = _row_tile(a)
    c1 = 1.0 - ADAM_B1 ** ADAM_STEP
    c2 = 1.0 - ADAM_B2 ** ADAM_STEP

    def body(s_ref, w_ref, m_ref, v_ref, g_out, d_out, m_out, v_out):
        g = s_ref[0].astype(F32)
        for k in range(1, ns):
            g = g + s_ref[k].astype(F32)
        m_new = ADAM_B1 * m_ref[...] + (1.0 - ADAM_B1) * g
        v_new = ADAM_B2 * v_ref[...] + (1.0 - ADAM_B2) * (g * g)
        m_hat = m_new / c1
        v_hat = v_new / c2
        g_out[...] = g
        d_out[...] = -ADAM_LR * (m_hat / (jnp.sqrt(v_hat) + ADAM_EPS) + ADAM_WD * w_ref[...])
        m_out[...] = m_new
        v_out[...] = v_new

    blk = pl.BlockSpec((None, ta, b), lambda l, i: (l, i, 0))
    return _pcall(
        body, name=name, grid=(nl, a // ta),
        in_specs=[pl.BlockSpec((ns, None, ta, b), lambda l, i: (0, l, i, 0)), blk, blk, blk],
        out_specs=[blk] * 4,
        out_shape=[jax.ShapeDtypeStruct((nl, a, b), F32)] * 4,
        compiler_params=_cp("parallel", "parallel"))(stack, w, m, v)


def _mod_fwd(cvec, wmod, bmod, name):
    _, d = cvec.shape

    def body(c_ref, w_ref, b_ref, o_ref):
        cv = c_ref[...]
        cs = cv * _sig(cv)
        o_ref[...] = _dot(cs.astype(BF16), w_ref[...]) + b_ref[...]

    return _pcall(
        body, name=name, grid=(3,),
        in_specs=[_full((8, d)), pl.BlockSpec((d, d), lambda n: (0, n)), pl.BlockSpec((1, d), lambda n: (0, n))],
        out_specs=pl.BlockSpec((8, d), lambda n: (0, n)),
        out_shape=jax.ShapeDtypeStruct((8, 3 * d), F32),
        compiler_params=_cp("parallel"))(cvec, wmod, bmod)


def _mod_bwd(dmod, cvec, cvec_t, wmod, name):
    _, d = cvec.shape

    def body(dm_ref, c_ref, ct_ref, w_ref, dw_ref, db_ref, dc_ref):
        n = pl.program_id(0)
        dm = dm_ref[...]
        ct = ct_ref[...]
        cs_t = _bf_round(ct * _sig(ct))
        d0 = _bf_round(dm[0:1, :])
        d1 = _bf_round(dm[1:2, :])
        dw_ref[...] = cs_t[:, 0:1] * d0 + cs_t[:, 1:2] * d1
        db_ref[...] = dm[0:1, :] + dm[1:2, :]

        @pl.when(n == 0)
        def _():
            dc_ref[...] = jnp.zeros_like(dc_ref)

        dc_ref[...] += _dot_nt(dm.astype(BF16), w_ref[...])

        @pl.when(n == 2)
        def _():
            cv = c_ref[...]
            s = _sig(cv)
            dc_ref[...] = dc_ref[...] * (s * (1.0 + cv * (1.0 - s)))

    return _pcall(
        body, name=name, grid=(3,),
        in_specs=[pl.BlockSpec((8, d), lambda n: (0, n)), _full((8, d)), _full((d, 128)),
                  pl.BlockSpec((d, d), lambda n: (0, n))],
        out_specs=[pl.BlockSpec((d, d), lambda n: (0, n)), pl.BlockSpec((1, d), lambda n: (0, n)),
                   _full((8, d))],
        out_shape=[jax.ShapeDtypeStruct((d, 3 * d), F32), jax.ShapeDtypeStruct((1, 3 * d), F32),
                   jax.ShapeDtypeStruct((8, d), F32)],
        compiler_params=_cp("arbitrary"))(dmod, cvec, cvec_t, wmod)


def _seg_rows(mod_ref, lo, hi, is_ctx):
    return jnp.where(is_ctx, mod_ref[1:2, lo:hi], mod_ref[0:1, lo:hi])


def _inproj(xa, modv, gpre, wp, n_lat, tm, name):
    na, d = xa.shape
    wcols = wp.shape[1]
    tn = d // 2

    def body(x_ref, mod_ref, g_ref, w_ref, p_ref, h_ref, h_s):
        i = pl.program_id(0)

        @pl.when(pl.program_id(1) == 0)
        def _():
            x = x_ref[...]
            r = lax.rsqrt(_mean(x * x) + EPS)
            row = i * tm + lax.broadcasted_iota(jnp.int32, (tm, 1), 0)
            is_ctx = row >= n_lat
            sh = _seg_rows(mod_ref, 0, d, is_ctx)
            sc = _seg_rows(mod_ref, d, 2 * d, is_ctx)
            hb = ((x * r * g_ref[...]) * (1.0 + sc) + sh).astype(BF16)
            h_s[...] = hb
            h_ref[...] = hb

        p_ref[...] = _dot(h_s[...], w_ref[...])

    return _pcall(
        body, name=name, grid=(na // tm, wcols // tn),
        in_specs=[pl.BlockSpec((tm, d), lambda i, j: (i, 0)), _full((8, 3 * d)), _full((1, d)),
                  pl.BlockSpec((d, tn), lambda i, j: (0, j))],
        out_specs=[pl.BlockSpec((tm, tn), lambda i, j: (i, j)), pl.BlockSpec((tm, d), lambda i, j: (i, 0))],
        out_shape=[jax.ShapeDtypeStruct((na, wcols), F32), jax.ShapeDtypeStruct((na, d), BF16)],
        scratch_shapes=[pltpu.VMEM((tm, d), BF16)],
        compiler_params=_cp("parallel", "arbitrary"))(xa, modv, gpre, wp)


def _lane_tile(t, width):
    if width >= 128:
        return jnp.tile(t, (1, width // 128))
    return t[:, :width]


def _partner(x):
    w = x.shape[-1]
    lane = lax.broadcasted_iota(jnp.int32, x.shape, 1)
    low = (lane % HEAD_DIM) < (HEAD_DIM // 2)
    return jnp.where(low, pltpu.roll(x, w - HEAD_DIM // 2, 1), pltpu.roll(x, HEAD_DIM // 2, 1))


def _qknorm_fwd(p, cos_t, sin_t, gq, gk, bd, tm, name):
    na = p.shape[0]
    d = gq.shape[1]
    kw = d // GROUP

    def body(q_ref, kv_ref, cos_ref, sin_ref, gq_ref, gk_ref, bd_ref, qo_ref, ko_ref, vo_ref):
        cos = cos_ref[...]
        sin = sin_ref[...]

        def norm_rope(xh, g, w):
            ms = _split_dot(xh * xh, bd_ref[0:w, 0:w]) * (1.0 / HEAD_DIM)
            xn = xh * lax.rsqrt(ms + EPS) * g
            return xn * _lane_tile(cos, w) + _partner(xn) * _lane_tile(sin, w)

        qo_ref[...] = (norm_rope(q_ref[...], gq_ref[...], d) * ATTN_SCALE).astype(BF16)
        kv = kv_ref[...]
        ko_ref[...] = norm_rope(kv[:, 0:kw], gk_ref[...], kw).astype(BF16)
        vo_ref[...] = kv[:, kw:2 * kw].astype(BF16)

    return _pcall(
        body, name=name, grid=(na // tm,),
        in_specs=[_rows(tm, d, 3), _rows(tm, d // 2, 14), _rows(tm, 128), _rows(tm, 128),
                  _full((1, d)), _full((1, kw)), _full((d, d))],
        out_specs=[_rows(tm, d), _rows(tm, kw), _rows(tm, kw)],
        out_shape=[jax.ShapeDtypeStruct((na, d), BF16), jax.ShapeDtypeStruct((na, kw), BF16),
                   jax.ShapeDtypeStruct((na, kw), BF16)],
        compiler_params=_cp("parallel"))(p, p, cos_t, sin_t, gq, gk, bd)


def _attn_fwd(q_hm, k_hm, vt_hm, na, tq, tk, cw, name):
    h, _, hd = q_hm.shape
    kv, nkeys, _ = k_hm.shape
    vrows = vt_hm.shape[1]
    nq, nk = na // tq, nkeys // tk
    nsub = tq // cw
    chains = [(a, u) for a in range(GROUP) for u in range(nsub)]

    def body(q_ref, k_ref, vt_ref, o_ref, lse_ref, m_s, acc_s):
        j = pl.program_id(2)

        @pl.when(j == 0)
        def _():
            m_s[...] = jnp.full_like(m_s, NEG_BIG)
            acc_s[...] = jnp.zeros_like(acc_s)

        k = k_ref[...]
        vt = vt_ref[...]

        def scores(n):
            a, u = chains[n]
            return _dot_nt(k, q_ref[a, u * cw:(u + 1) * cw, :])

        s_tiles = {n: scores(n) for n in range(min(ATTN_AHEAD, len(chains)))}
        pending = None
        for n, (a, u) in enumerate(chains):
            cols = slice(u * cw, (u + 1) * cw)
            s_t = s_tiles.pop(n)
            m_prev = m_s[a, :, cols]
            m_new = jnp.maximum(m_prev, jnp.max(s_t, axis=0, keepdims=True))
            m_s[a, :, cols] = m_new
            pv = _dot(vt, jnp.exp(s_t - m_new).astype(BF16))
            if pending is not None:
                pa, pcols, palpha, ppv = pending
                acc_s[pa, :, pcols] = palpha * acc_s[pa, :, pcols] + ppv
            pending = (a, cols, jnp.exp(m_prev - m_new), pv)
            if n + ATTN_AHEAD < len(chains):
                s_tiles[n + ATTN_AHEAD] = scores(n + ATTN_AHEAD)
        pa, pcols, palpha, ppv = pending
        acc_s[pa, :, pcols] = palpha * acc_s[pa, :, pcols] + ppv

        @pl.when(j == nk - 1)
        def _():
            for a in range(GROUP):
                acc = acc_s[a]
                l = acc[hd:hd + 1, :]
                o_ref[a] = acc[0:hd, :] / l
                lse_ref[a] = m_s[a] + jnp.log(l)

    return _pcall(
        body, name=name, grid=(kv, nq, nk),
        in_specs=[pl.BlockSpec((GROUP, tq, hd), lambda g, i, j: (g, i, 0)),
                  pl.BlockSpec((None, tk, hd), lambda g, i, j: (g, j, 0)),
                  pl.BlockSpec((None, vrows, tk), lambda g, i, j: (g, 0, j))],
        out_specs=[pl.BlockSpec((GROUP, hd, tq), lambda g, i, j: (g, 0, i)),
                   pl.BlockSpec((GROUP, 1, tq), lambda g, i, j: (g, 0, i))],
        out_shape=[jax.ShapeDtypeStruct((h, hd, na), F32), jax.ShapeDtypeStruct((h, 1, na), F32)],
        scratch_shapes=[pltpu.VMEM((GROUP, 1, tq), F32), pltpu.VMEM((GROUP, vrows, tq), F32)],
        compiler_params=_cp("parallel", "parallel", "arbitrary"))(q_hm, k_hm, vt_hm)


def _window(win_ref, prev, cur, nxt, first, last, tm):
    win_ref[0:HALO, :] = jnp.where(first, 0.0, prev)
    win_ref[HALO:HALO + tm, :] = cur
    win_ref[HALO + tm:HALO + tm + HALO, :] = jnp.where(last, 0.0, nxt)


def _lane_blocks(d):
    return [slice(b, b + 128) for b in range(0, d, 128)]


def _sublane_shifts(slab):
    n = slab.shape[0]
    for b in range(8):
        sh = slab if b == 0 else pltpu.roll(slab, n - b, 0)
        for a8 in range(0, 2 * HALO, 8):
            yield a8 + b, sh[a8:a8 + CONV_ROWS, :]


def _halo_specs(tm, d, na, colblk):
    per = tm // HALO
    last_blk = na // HALO - 1
    prev = pl.BlockSpec((HALO, d), lambda i: (jnp.maximum(i * per - 1, 0), colblk))
    nxt = pl.BlockSpec((HALO, d), lambda i: (jnp.minimum((i + 1) * per, last_blk), colblk))
    return prev, nxt


def _seq_ends(i, n_lat, na, tm):
    first = jnp.logical_or(i == 0, i == n_lat // tm)
    last = jnp.logical_or(i == n_lat // tm - 1, i == na // tm - 1)
    return first, last


def _conv_fwd(p, conv_w, conv_b, ln_g, ln_b, n_lat, tm, name):
    na = p.shape[0]
    ktaps, d = conv_w.shape
    pad = ktaps // 2

    def body(a_ref, ap_ref, an_ref, g_ref, gp_ref, gn_ref, ga_ref, w_ref, cb_ref, lg_ref, lb_ref,
             y5_ref, y2_ref, win):
        i = pl.program_id(0)
        first, last = _seq_ends(i, n_lat, na, tm)
        _window(win, ap_ref[...] * _sig(gp_ref[...]), a_ref[...] * _sig(g_ref[...]),
                an_ref[...] * _sig(gn_ref[...]), first, last, tm)

        def chunk(c, carry):
            r = pl.multiple_of(c * CONV_ROWS, CONV_ROWS)
            for lanes in _lane_blocks(d):
                acc = jnp.zeros((CONV_ROWS, 128), F32)
                for off, sh in _sublane_shifts(win[pl.ds(r, CONV_ROWS + 2 * HALO), lanes]):
                    k = off - (HALO - pad)
                    if 0 <= k < ktaps:
                        acc = acc + w_ref[k:k + 1, lanes] * sh
                y2_ref[pl.ds(r, CONV_ROWS), lanes] = acc + cb_ref[:, lanes]
            return carry

        lax.fori_loop(0, tm // CONV_ROWS, chunk, 0)
        y2 = y2_ref[...]
        xc = y2 - _mean(y2)
        y3 = xc * lax.rsqrt(_mean(xc * xc) + EPS) * lg_ref[...] + lb_ref[...]
        gate = ga_ref[...]
        y5_ref[...] = ((y3 * _sig(y3)) * (gate * _sig(gate))).astype(BF16)

    ap, an = _halo_specs(tm, d, na, 0)
    gp, gn = _halo_specs(tm, d, na, 1)
    return _pcall(
        body, name=name, grid=(na // tm,),
        in_specs=[_rows(tm, d, 0), ap, an, _rows(tm, d, 1), gp, gn, _rows(tm, d, 2),
                  _full((ktaps, d)), _full((1, d)), _full((1, d)), _full((1, d))],
        out_specs=[_rows(tm, d), _rows(tm, d)],
        out_shape=[jax.ShapeDtypeStruct((na, d), BF16), jax.ShapeDtypeStruct((na, d), F32)],
        scratch_shapes=[pltpu.VMEM((tm + 2 * HALO, d), F32)],
        compiler_params=_cp("parallel"))(p, p, p, p, p, p, p, conv_w, conv_b, ln_g, ln_b)


def _merge_fwd(y5, o_tm, p, xa, modv, gpost, wc, wa, wo, n_lat, tm, name):
    na, d = xa.shape

    def body(y5_ref, o_ref, gb_ref, ma_ref, mb_ref, x_ref, mod_ref, gp_ref, wc_ref, wa_ref, wo_ref,
             xn_ref, yc_ref, ya_ref, out_ref, z_ref, og_ref):
        is_ctx = pl.program_id(0) >= n_lat // tm
        gate_b = gb_ref[...]
        og = (o_ref[...] * (gate_b * _sig(gate_b))).astype(BF16)
        og_ref[...] = og
        yc = _dot(y5_ref[...], wc_ref[...])
        ya = _dot(og, wa_ref[...])
        yc_ref[...] = yc
        ya_ref[...] = ya
        z = (_sig(ma_ref[...]) * yc + _sig(mb_ref[...]) * ya).astype(BF16)
        z_ref[...] = z
        out = _dot(z, wo_ref[...])
        out_ref[...] = out
        gt = _seg_rows(mod_ref, 2 * d, 3 * d, is_ctx)
        xn_ref[...] = x_ref[...] + gt * (out * lax.rsqrt(_mean(out * out) + EPS) * gp_ref[...])

    f32o = jax.ShapeDtypeStruct((na, d), F32)
    bfo = jax.ShapeDtypeStruct((na, d), BF16)
    return _pcall(
        body, name=name, grid=(na // tm,),
        in_specs=[_rows(tm, d), _rows(tm, d), _rows(tm, d, 4), _rows(tm, d, 5), _rows(tm, d, 6), _rows(tm, d),
                  _full((8, 3 * d)), _full((1, d)), _full((d, d)), _full((d, d)), _full((d, d))],
        out_specs=[_rows(tm, d)] * 6,
        out_shape=[f32o, f32o, f32o, f32o, bfo, bfo],
        compiler_params=_cp("parallel"))(y5, o_tm, p, p, p, xa, modv, gpost, wc, wa, wo)


def _loss_grad(xa, target, n_lat, tm, name):
    na, d = xa.shape
    nlt = n_lat // tm

    def body(x_ref, t_ref, dx_ref, loss_ref):
        i = pl.program_id(0)

        @pl.when(i == 0)
        def _():
            loss_ref[...] = jnp.zeros_like(loss_ref)

        @pl.when(i < nlt)
        def _():
            err = x_ref[...] - t_ref[...]
            dx_ref[...] = err * (1.0 / d)
            loss_ref[...] += 0.5 * jnp.sum(_mean(err * err))

        @pl.when(i >= nlt)
        def _():
            dx_ref[...] = jnp.zeros_like(dx_ref)

    return _pcall(
        body, name=name, grid=(na // tm,),
        in_specs=[_rows(tm, d), pl.BlockSpec((tm, d), lambda i: (jnp.minimum(i, nlt - 1), 0))],
        out_specs=[_rows(tm, d), _full((8, 128))],
        out_shape=[jax.ShapeDtypeStruct((na, d), F32), jax.ShapeDtypeStruct((8, 128), F32)],
        compiler_params=_cp("arbitrary"))(xa, target)


def _merge_bwd(dxn, out, yc, ya, o_tm, p, y2, modv, gpost, ln_g, ln_b, wo_t, wc_t, wa_t, esel, n_lat, tm, name):
    na, d = dxn.shape

    def body(dx_ref, out_ref, yc_ref, ya_ref, o_ref, gb_ref, ma_ref, mb_ref, gta_ref, y2_ref,
             mod_ref, gp_ref, lg_ref, lb_ref, wot_ref, wct_ref, wat_ref, es_ref,
             dgb_ref, dma_ref, dmb_ref, dgta_ref, do_ref, dout_ref, dyc_ref, dya_ref, dy2_ref, dl_ref, acc_ref):
        i = pl.program_id(0)
        ctx_tile = i >= n_lat // tm

        @pl.when(i == 0)
        def _():
            acc_ref[...] = jnp.zeros_like(acc_ref)

        gt = _seg_rows(mod_ref, 2 * d, 3 * d, ctx_tile)
        gp = gp_ref[...]
        dx = dx_ref[...]
        out = out_ref[...]
        r2 = lax.rsqrt(_mean(out * out) + EPS)
        n2 = out * r2
        dgt = _colsum(dx * (n2 * gp))

        @pl.when(ctx_tile)
        def _():
            acc_ref[1:2, :] += dgt

        @pl.when(jnp.logical_not(ctx_tile))
        def _():
            acc_ref[0:1, :] += dgt

        acc_ref[2:3, :] += _colsum(dx * gt * n2)
        dn2 = dx * gt * gp
        dout = (r2 * (dn2 - n2 * _mean(dn2 * n2))).astype(BF16)
        dout_ref[...] = dout
        dz = _dot(dout, wot_ref[...])
        sa = _sig(ma_ref[...])
        sb = _sig(mb_ref[...])
        dyc = (dz * sa).astype(BF16)
        dya = (dz * sb).astype(BF16)
        dyc_ref[...] = dyc
        dya_ref[...] = dya
        dma_ref[...] = (dz * yc_ref[...] * sa * (1.0 - sa)).astype(BF16)
        dmb_ref[...] = (dz * ya_ref[...] * sb * (1.0 - sb)).astype(BF16)
        dy5 = _dot(dyc, wct_ref[...])
        dog = _dot(dya, wat_ref[...])

        gate_b = gb_ref[...]
        sgb = _sig(gate_b)
        o = o_ref[...]
        do = dog * (gate_b * sgb)
        do_ref[...] = do.astype(BF16)
        dgb_ref[...] = (dog * o * (sgb * (1.0 + gate_b * (1.0 - sgb)))).astype(BF16)
        dl_ref[...] = _split_dot(do * o, es_ref[...])

        y2 = y2_ref[...]
        xc = y2 - _mean(y2)
        rstd = lax.rsqrt(_mean(xc * xc) + EPS)
        xhat = xc * rstd
        lg = lg_ref[...]
        y3 = xhat * lg + lb_ref[...]
        s3 = _sig(y3)
        gate_a = gta_ref[...]
        sga = _sig(gate_a)
        dgta_ref[...] = (dy5 * (y3 * s3) * (sga * (1.0 + gate_a * (1.0 - sga)))).astype(BF16)
        dy3 = dy5 * (gate_a * sga) * (s3 * (1.0 + y3 * (1.0 - s3)))
        acc_ref[3:4, :] += _colsum(dy3 * xhat)
        acc_ref[4:5, :] += _colsum(dy3)
        dxh = dy3 * lg
        dy2 = rstd * (dxh - _mean(dxh) - xhat * _mean(dxh * xhat))
        dy2_ref[...] = dy2
        acc_ref[5:6, :] += _colsum(dy2)

    f32o = jax.ShapeDtypeStruct((na, d), F32)
    bfo = jax.ShapeDtypeStruct((na, d), BF16)
    r = _rows(tm, d)
    return _pcall(
        body, name=name, grid=(na // tm,),
        in_specs=[r, r, r, r, r, _rows(tm, d, 4), _rows(tm, d, 5), _rows(tm, d, 6), _rows(tm, d, 2), r,
                  _full((8, 3 * d)), _full((1, d)), _full((1, d)), _full((1, d)),
                  _full((d, d)), _full((d, d)), _full((d, d)), _full((d, 128))],
        out_specs=[r] * 9 + [_rows(tm, 128), _full((8, d))],
        out_shape=[bfo] * 8 + [f32o, jax.ShapeDtypeStruct((na, 128), F32), jax.ShapeDtypeStruct((8, d), F32)],
        compiler_params=_cp("arbitrary"),
    )(dxn, out, yc, ya, o_tm, p, p, p, p, y2, modv, gpost, ln_g, ln_b, wo_t, wc_t, wa_t, esel)


def _attn_bwd(q_hm, qt_hm, k_hm, kt_hm, v_hm, do_hm, dot_hm, lse_r, dl_r, na, tq, tk, cw, name):
    h, _, hd = q_hm.shape
    kv, nkeys, _ = k_hm.shape
    nq, nk = na // tq, nkeys // tk
    nsub = tq // cw
    chains = [(a, u) for a in range(GROUP) for u in range(nsub)]

    def body(q_ref, qt_ref, k_ref, kt_ref, v_ref, do_ref, dot_ref, lse_ref, dl_ref, dq_hbm, dk_ref, dv_ref,
             dq_acc, dk_acc, dv_acc, sem):
        g = pl.program_id(0)
        j = pl.program_id(1)
        i = pl.program_id(2)

        @pl.when(jnp.logical_and(j == 0, i == 0))
        def _():
            dq_acc[...] = jnp.zeros_like(dq_acc)

        @pl.when(i == 0)
        def _():
            dk_acc[...] = jnp.zeros_like(dk_acc)
            dv_acc[...] = jnp.zeros_like(dv_acc)

        k = k_ref[...]
        v = v_ref[...]
        kt = kt_ref[...]

        def products(n):
            a, u = chains[n]
            rows = slice(u * cw, (u + 1) * cw)
            return _dot_nt(k, q_ref[a, rows, :]), _dot_nt(v, do_ref[a, rows, :])

        def accumulate(done):
            a, u, dv_t, dk_t, dq_t = done
            dv_acc[...] += dv_t
            dk_acc[...] += dk_t
            at = pl.multiple_of(i * tq + u * cw, cw)
            dq_acc[a, :, pl.ds(at, cw)] += dq_t

        tiles = {n: products(n) for n in range(min(ATTN_AHEAD, len(chains)))}
        pending = None
        for n, (a, u) in enumerate(chains):
            cols = slice(u * cw, (u + 1) * cw)
            s_t, dp_t = tiles.pop(n)
            p_t = jnp.exp(s_t - lse_ref[a, :, cols])
            ds_b = (p_t * (dp_t - dl_ref[a, :, cols])).astype(BF16)
            p_b = p_t.astype(BF16)
            dv_t = _dot_nt(dot_ref[a, :, cols], p_b)
            dk_t = _dot_nt(qt_ref[a, :, cols], ds_b)
            dq_t = _dot(kt, ds_b)
            if pending is not None:
                accumulate(pending)
            pending = (a, u, dv_t, dk_t, dq_t)
            if n + ATTN_AHEAD < len(chains):
                tiles[n + ATTN_AHEAD] = products(n + ATTN_AHEAD)
        accumulate(pending)

        @pl.when(i == nq - 1)
        def _():
            dk_ref[...] = dk_acc[...]
            dv_ref[...] = dv_acc[...]

        @pl.when(jnp.logical_and(j == nk - 1, i == nq - 1))
        def _():
            cp = pltpu.make_async_copy(dq_acc, dq_hbm.at[pl.ds(g * GROUP, GROUP)], sem)
            cp.start()
            cp.wait()

    qspec = pl.BlockSpec((GROUP, tq, hd), lambda g, j, i: (g, i, 0))
    qtspec = pl.BlockSpec((GROUP, hd, tq), lambda g, j, i: (g, 0, i))
    kspec = pl.BlockSpec((None, tk, hd), lambda g, j, i: (g, j, 0))
    ktspec = pl.BlockSpec((None, hd, tk), lambda g, j, i: (g, 0, j))
    rspec = pl.BlockSpec((GROUP, 1, tq), lambda g, j, i: (g, 0, i))
    return _pcall(
        body, name=name, grid=(kv, nk, nq),
        in_specs=[qspec, qtspec, kspec, ktspec, kspec, qspec, qtspec, rspec, rspec],
        out_specs=[ANY, ktspec, ktspec],
        out_shape=[jax.ShapeDtypeStruct((h, hd, na), F32), jax.ShapeDtypeStruct((kv, hd, nkeys), F32),
                   jax.ShapeDtypeStruct((kv, hd, nkeys), F32)],
        scratch_shapes=[pltpu.VMEM((GROUP, hd, na), F32), pltpu.VMEM((hd, tk), F32), pltpu.VMEM((hd, tk), F32),
                        pltpu.SemaphoreType.DMA],
        compiler_params=_cp("arbitrary", "arbitrary", "arbitrary"),
    )(q_hm, qt_hm, k_hm, kt_hm, v_hm, do_hm, dot_hm, lse_r, dl_r)


def _qknorm_bwd(dq_tm, dk_tm, dv_tm, p, cos_t, sin_t, gq, gk, bd, tm, name):
    na = p.shape[0]
    d = gq.shape[1]
    kw = d // GROUP

    def body(dq_ref, dk_ref, dv_ref, q_ref, kv_ref, cos_ref, sin_ref, gq_ref, gk_ref, bd_ref,
             dqo_ref, dkvo_ref, acc_ref):
        @pl.when(pl.program_id(0) == 0)
        def _():
            acc_ref[...] = jnp.zeros_like(acc_ref)

        cos = cos_ref[...]
        sin = sin_ref[...]

        def back(dy, xh, g, w):
            bdw = bd_ref[0:w, 0:w]
            dn = dy * _lane_tile(cos, w) - _partner(dy) * _lane_tile(sin, w)
            rs = lax.rsqrt(_split_dot(xh * xh, bdw) * (1.0 / HEAD_DIM) + EPS)
            y = xh * rs
            dg = _colsum(dn * y)
            dyn = dn * g
            dx = rs * (dyn - y * (_split_dot(dyn * y, bdw) * (1.0 / HEAD_DIM)))
            return dx, dg

        dq, dgq = back(dq_ref[...] * ATTN_SCALE, q_ref[...], gq_ref[...], d)
        dqo_ref[...] = dq.astype(BF16)
        acc_ref[0:1, :] += dgq
        kv = kv_ref[...]
        dk, dgk = back(dk_ref[...], kv[:, 0:kw], gk_ref[...], kw)
        acc_ref[1:2, 0:kw] += dgk
        dkvo_ref[:, 0:kw] = dk.astype(BF16)
        dkvo_ref[:, kw:2 * kw] = dv_ref[...].astype(BF16)

    return _pcall(
        body, name=name, grid=(na // tm,),
        in_specs=[_rows(tm, d), _rows(tm, kw), _rows(tm, kw), _rows(tm, d, 3), _rows(tm, d // 2, 14),
                  _rows(tm, 128), _rows(tm, 128), _full((1, d)), _full((1, kw)), _full((d, d))],
        out_specs=[_rows(tm, d), _rows(tm, d // 2), _full((8, d))],
        out_shape=[jax.ShapeDtypeStruct((na, d), BF16), jax.ShapeDtypeStruct((na, d // 2), BF16),
                   jax.ShapeDtypeStruct((8, d), F32)],
        compiler_params=_cp("arbitrary"))(dq_tm, dk_tm, dv_tm, p, p, cos_t, sin_t, gq, gk, bd)


def _conv_bwd(dy2, p, conv_w, n_lat, tm, name):
    na = p.shape[0]
    ktaps, d = conv_w.shape
    pad = ktaps // 2

    def body(dy_ref, dyp_ref, dyn_ref, a_ref, ap_ref, an_ref, g_ref, gp_ref, gn_ref, w_ref,
             da_ref, dg_ref, dw_ref, dwin, ywin, dy1_s, part):
        i = pl.program_id(0)

        @pl.when(i == 0)
        def _():
            part[...] = jnp.zeros_like(part)

        first, last = _seq_ends(i, n_lat, na, tm)
        a = a_ref[...]
        sg = _sig(g_ref[...])
        _window(dwin, dyp_ref[...], dy_ref[...], dyn_ref[...], first, last, tm)
        _window(ywin, ap_ref[...] * _sig(gp_ref[...]), a * sg, an_ref[...] * _sig(gn_ref[...]), first, last, tm)

        def chunk(c, carry):
            r = pl.multiple_of(c * CONV_ROWS, CONV_ROWS)
            for lanes in _lane_blocks(d):
                dy = dy_ref[pl.ds(r, CONV_ROWS), lanes]
                acc = jnp.zeros((CONV_ROWS, 128), F32)
                for off, sh in _sublane_shifts(dwin[pl.ds(r, CONV_ROWS + 2 * HALO), lanes]):
                    k = HALO + pad - off
                    if 0 <= k < ktaps:
                        acc = acc + w_ref[k:k + 1, lanes] * sh
                dy1_s[pl.ds(r, CONV_ROWS), lanes] = acc
                for off, sh in _sublane_shifts(ywin[pl.ds(r, CONV_ROWS + 2 * HALO), lanes]):
                    k = off - (HALO - pad)
                    if 0 <= k < ktaps:
                        part[k, :, lanes] += jnp.sum((dy * sh).reshape(CONV_ROWS // 8, 8, 128), axis=0)
            return carry

        lax.fori_loop(0, tm // CONV_ROWS, chunk, 0)
        dy1 = dy1_s[...]
        da_ref[...] = (dy1 * sg).astype(BF16)
        dg_ref[...] = (dy1 * a * sg * (1.0 - sg)).astype(BF16)

        @pl.when(i == na // tm - 1)
        def _():
            dw_ref[...] = jnp.sum(part[...], axis=1)

    dyp, dyn = _halo_specs(tm, d, na, 0)
    ap, an = _halo_specs(tm, d, na, 0)
    gp, gn = _halo_specs(tm, d, na, 1)
    bfo = jax.ShapeDtypeStruct((na, d), BF16)
    return _pcall(
        body, name=name, grid=(na // tm,),
        in_specs=[_rows(tm, d), dyp, dyn, _rows(tm, d, 0), ap, an, _rows(tm, d, 1), gp, gn, _full((ktaps, d))],
        out_specs=[_rows(tm, d), _rows(tm, d), _full((ktaps, d))],
        out_shape=[bfo, bfo, jax.ShapeDtypeStruct((ktaps, d), F32)],
        scratch_shapes=[pltpu.VMEM((tm + 2 * HALO, d), F32), pltpu.VMEM((tm + 2 * HALO, d), F32),
                        pltpu.VMEM((tm, d), F32), pltpu.VMEM((ktaps, 8, d), F32)],
        compiler_params=_cp("arbitrary"))(dy2, dy2, dy2, p, p, p, p, p, p, conv_w)


def _inproj_bwd(segs, dkv, xa, dxn, modv, gpre, wp_t, n_lat, tm, name):
    na, d = xa.shape
    nseg = len(segs)
    wrows = wp_t.shape[0]

    def body(*refs):
        seg_refs = refs[:nseg]
        dkv_ref, x_ref, dxn_ref, mod_ref, g_ref, wt_hbm, dx_ref, acc_ref, wt, sem = refs[nseg:]
        i = pl.program_id(0)
        ctx_tile = i >= n_lat // tm

        @pl.when(i == 0)
        def _():
            cp = pltpu.make_async_copy(wt_hbm, wt, sem)
            cp.start()
            cp.wait()
            acc_ref[...] = jnp.zeros_like(acc_ref)

        dh = _dot(dkv_ref[...], wt[nseg * d:wrows, :])
        for s in range(nseg):
            dh = dh + _dot(seg_refs[s][...], wt[s * d:(s + 1) * d, :])
        x = x_ref[...]
        r = lax.rsqrt(_mean(x * x) + EPS)
        xn = x * r
        g = g_ref[...]
        sc1 = 1.0 + _seg_rows(mod_ref, d, 2 * d, ctx_tile)
        dsh = _colsum(dh)
        dsc = _colsum(dh * (xn * g))

        @pl.when(ctx_tile)
        def _():
            acc_ref[1:2, :] += dsh
            acc_ref[3:4, :] += dsc

        @pl.when(jnp.logical_not(ctx_tile))
        def _():
            acc_ref[0:1, :] += dsh
            acc_ref[2:3, :] += dsc

        acc_ref[4:5, :] += _colsum(dh * xn * sc1)
        dxh = dh * g * sc1
        dx_ref[...] = dxn_ref[...] + r * (dxh - xn * _mean(dxh * xn))

    r_ = _rows(tm, d)
    return _pcall(
        body, name=name, grid=(na // tm,),
        in_specs=[r_] * nseg + [_rows(tm, d // 2), r_, r_, _full((8, 3 * d)), _full((1, d)), ANY],
        out_specs=[r_, _full((8, d))],
        out_shape=[jax.ShapeDtypeStruct((na, d), F32), jax.ShapeDtypeStruct((8, d), F32)],
        scratch_shapes=[pltpu.VMEM(wp_t.shape, BF16), pltpu.SemaphoreType.DMA],
        compiler_params=_cp("arbitrary"))(*segs, dkv, xa, dxn, modv, gpre, wp_t)


def _grad_matmul(a, b, tk, name):
    na, ka = a.shape
    nb = b.shape[1]
    tn = min(nb, 1024)

    def body(a_ref, b_ref, o_ref):
        @pl.when(pl.program_id(1) == 0)
        def _():
            o_ref[...] = jnp.zeros_like(o_ref)

        o_ref[...] += _dot_tn(a_ref[...], b_ref[...])

    return _pcall(
        body, name=name, grid=(nb // tn, na // tk),
        in_specs=[pl.BlockSpec((tk, ka), lambda n, k: (k, 0)), pl.BlockSpec((tk, tn), lambda n, k: (k, n))],
        out_specs=pl.BlockSpec((ka, tn), lambda n, k: (0, n)),
        out_shape=jax.ShapeDtypeStruct((ka, nb), F32),
        compiler_params=_cp("parallel", "arbitrary"))(a, b)


def _pack(parts, cols, row_mult):
    flat = jnp.concatenate([q.astype(F32).reshape(-1) for q in parts])
    rows = -(-flat.shape[0] // (cols * row_mult)) * row_mult
    return jnp.pad(flat, (0, rows * cols - flat.shape[0])).reshape(1, rows, cols)


def _unpack(flat, shapes):
    out, off = [], 0
    for s in shapes:
        n = math.prod(s)
        out.append(flat[off:off + n].reshape(tuple(s)))
        off += n
    return out


def _cols_by_dest(g):
    l, a, w8 = g.shape
    return g.reshape(l, a, N_DEV, w8 // N_DEV).transpose(2, 0, 1, 3)


def _rows_by_dest(g):
    l, r8, b = g.shape
    return g.reshape(l, N_DEV, r8 // N_DEV, b).transpose(1, 0, 2, 3)


def _cols_from_src(s):
    n, l, a, w = s.shape
    return s.transpose(1, 2, 0, 3).reshape(l, a, n * w)


def _rows_from_src(s):
    n, l, r, b = s.shape
    return s.transpose(1, 0, 2, 3).reshape(l, n * r, b)


def _rope_tables(n_lat, n_ctx):
    half = HEAD_DIM // 2
    rows = n_lat // GRID_W
    row = jnp.repeat(jnp.arange(rows, dtype=F32), GRID_W)
    col = jnp.tile(jnp.arange(GRID_W, dtype=F32), rows)
    inv_freq = ROPE_THETA ** (-jnp.arange(0, half, 2, dtype=F32) / half)
    ang = jnp.concatenate([row[:, None] * inv_freq, col[:, None] * inv_freq], axis=-1)
    cos, sin = jnp.cos(ang), jnp.sin(ang)
    cos_t = jnp.concatenate([jnp.tile(cos, (1, 4)), jnp.ones((n_ctx, 128), F32)], axis=0)
    sin_t = jnp.concatenate([jnp.tile(jnp.concatenate([-sin, sin], axis=-1), (1, 2)),
                             jnp.zeros((n_ctx, 128), F32)], axis=0)
    return cos_t, sin_t


def _to_heads(t):
    na, w = t.shape
    return t.reshape(na, w // HEAD_DIM, HEAD_DIM).transpose(1, 0, 2)


def _from_heads(t):
    nh, na, hd = t.shape
    return t.transpose(1, 0, 2).reshape(na, nh * hd)


def kernel(x, c, ctx, c_ctx, w_mod, b_mod, g_pre, g_post, w_in, conv_w, conv_b, ln_g, ln_b, w_conv_out, q_norm_g, k_norm_g, w_attn_out, w_out, loss_target, m_c_ctx, m_w_mod, m_b_mod, m_g_pre, m_g_post, m_w_in, m_conv_w, m_conv_b, m_ln_g, m_ln_b, m_w_conv_out, m_q_norm_g, m_k_norm_g, m_w_attn_out, m_w_out, v_c_ctx, v_w_mod, v_b_mod, v_g_pre, v_g_post, v_w_in, v_conv_w, v_conv_b, v_ln_g, v_ln_b, v_w_conv_out, v_q_norm_g, v_k_norm_g, v_w_attn_out, v_w_out):
    depth, d, _ = w_mod.shape
    n_lat, n_ctx = x.shape[1], ctx.shape[1]
    na = n_lat + n_ctx
    heads = d // HEAD_DIM
    kw = d // GROUP
    ktaps = conv_w.shape[1]
    tm = n_ctx
    tm_half = tm // 2
    tbig = 3 * tm if na % (3 * tm) == 0 else tm
    tq_lat = 2 * tm if n_lat % (2 * tm) == 0 else tm
    tq_bwd = 4 * tm if n_lat % (4 * tm) == 0 else tm

    s_mod, s_in, s_co, s_ao, s_oo, s_cw = _all_gather(
        [w_mod.astype(BF16), w_in.astype(BF16), w_conv_out.astype(BF16), w_attn_out.astype(BF16),
         w_out.astype(BF16), conv_w], "gather_weights")
    wmod_f = _cols_from_src(s_mod)
    win_f = _cols_from_src(s_in)
    wc_f, wa_f, wo_f = _rows_from_src(s_co), _rows_from_src(s_ao), _rows_from_src(s_oo)
    wp_f = jnp.concatenate([win_f[:, :, :4 * d], win_f[:, :, 4 * d + 2 * kw:], win_f[:, :, 4 * d:4 * d + 2 * kw]], axis=2)
    convw_f = _cols_from_src(s_cw)

    cos_t, sin_t = _rope_tables(n_lat, n_ctx)
    lane = jnp.arange(d)
    bd = (lane[:, None] // HEAD_DIM == lane[None, :] // HEAD_DIM).astype(BF16)
    esel = (lane[:, None] // HEAD_DIM == jnp.arange(128)[None, :]).astype(BF16)
    cvec = jnp.zeros((8, d), F32).at[0].set(c[0]).at[1].set(c_ctx)
    cvec_t = jnp.zeros((d, 128), F32).at[:, 0].set(c[0]).at[:, 1].set(c_ctx)

    xa = jnp.concatenate([x[0], ctx[0]], axis=0)

    saved = []
    for l in range(depth):
        tag = f"_l{l}"
        gq = jnp.tile(q_norm_g[l], heads)[None, :]
        gk = jnp.tile(k_norm_g[l], heads // GROUP)[None, :]
        modv = _mod_fwd(cvec, wmod_f[l], b_mod[l][None, :], "mod_fwd" + tag)
        p, hb = _inproj(xa, modv, g_pre[l][None, :], wp_f[l], n_lat, tbig, "inproj" + tag)
        qr, kr, vb = _qknorm_fwd(p, cos_t, sin_t, gq, gk, bd, tm, "qknorm_fwd" + tag)
        q_hm, k_hm, v_hm = _to_heads(qr), _to_heads(kr), _to_heads(vb)
        vt_ones = jnp.concatenate([v_hm.transpose(0, 2, 1), jnp.ones((heads // GROUP, 16, na), BF16)], axis=1)
        ot_lat, lse_lat = _attn_fwd(q_hm, k_hm, vt_ones, n_lat, tq_lat, tbig, tm, "attn_fwd" + tag)
        ot_ctx, lse_ctx = _attn_fwd(q_hm[:, n_lat:], k_hm[:, n_lat:], vt_ones[:, :, n_lat:], n_ctx, tm, tm, tm,
                                    "attn_ctx_fwd" + tag)
        lse = jnp.concatenate([lse_lat, lse_ctx], axis=2)
        o_tm = jnp.concatenate([ot_lat, ot_ctx], axis=2).reshape(d, na).T
        y5, y2 = _conv_fwd(p, convw_f[l], conv_b[l][None, :], ln_g[l][None, :], ln_b[l][None, :], n_lat, tm,
                           "conv_fwd" + tag)
        xa_new, yc, ya, out, z, og = _merge_fwd(y5, o_tm, p, xa, modv, g_post[l][None, :], wc_f[l], wa_f[l], wo_f[l],
                                                n_lat, tm, "merge_fwd" + tag)
        saved.append(dict(xa=xa, modv=modv, p=p, hb=hb, q_hm=q_hm, k_hm=k_hm, v_hm=v_hm, o_tm=o_tm, lse=lse,
                          qt_hm=qr.T.reshape(heads, HEAD_DIM, na), kt_hm=kr.T.reshape(heads // GROUP, HEAD_DIM, na),
                          y5=y5, y2=y2, yc=yc, ya=ya, out=out, z=z, og=og, gq=gq, gk=gk))
        xa = xa_new

    dxa, loss_blk = _loss_grad(xa, loss_target[0], n_lat, tm, "loss_grad")

    g_wmod, g_win, g_convw, g_wc, g_wa, g_wo = [], [], [], [], [], []
    g_bmod, g_gpre, g_gpost, g_convb, g_lng, g_lnb, g_qg, g_kg = [], [], [], [], [], [], [], []
    g_cctx = jnp.zeros((d,), F32)
    for l in reversed(range(depth)):
        tag = f"_l{l}"
        s = saved[l]
        p = s["p"]
        (dgb, dma, dmb, dgta, do_tm, dout, dyc, dya, dy2, dl128, acc_m) = _merge_bwd(
            dxa, s["out"], s["yc"], s["ya"], s["o_tm"], p, s["y2"], s["modv"], g_post[l][None, :],
            ln_g[l][None, :], ln_b[l][None, :], wo_f[l].T, wc_f[l].T, wa_f[l].T, esel, n_lat, tm_half,
            "merge_bwd" + tag)
        dl_r = dl128[:, :heads].T.reshape(heads, 1, na)
        do_hm, dot_hm = _to_heads(do_tm), do_tm.T.reshape(heads, HEAD_DIM, na)
        q_hm, qt_hm, k_hm, kt_hm, v_hm, lse = s["q_hm"], s["qt_hm"], s["k_hm"], s["kt_hm"], s["v_hm"], s["lse"]
        dqt_lat, dkt_hm, dvt_hm = _attn_bwd(q_hm, qt_hm, k_hm, kt_hm, v_hm, do_hm, dot_hm, lse, dl_r,
                                            n_lat, tq_bwd, tm, tm, "attn_bwd" + tag)
        dqt_ctx, dkt_ctx, dvt_ctx = _attn_bwd(
            q_hm[:, n_lat:], qt_hm[:, :, n_lat:], k_hm[:, n_lat:], kt_hm[:, :, n_lat:], v_hm[:, n_lat:],
            do_hm[:, n_lat:], dot_hm[:, :, n_lat:], lse[:, :, n_lat:], dl_r[:, :, n_lat:],
            n_ctx, tm, tm, tm, "attn_ctx_bwd" + tag)
        dkt_hm = dkt_hm.at[:, :, n_lat:].add(dkt_ctx)
        dvt_hm = dvt_hm.at[:, :, n_lat:].add(dvt_ctx)
        dq_tm = jnp.concatenate([dqt_lat, dqt_ctx], axis=2).reshape(d, na).T
        dq, dkv, acc_q = _qknorm_bwd(dq_tm, dkt_hm.reshape(kw, na).T, dvt_hm.reshape(kw, na).T, p, cos_t, sin_t,
                                     s["gq"], s["gk"], bd, tm, "qknorm_bwd" + tag)
        da, dg, dconvw = _conv_bwd(dy2, p, convw_f[l], n_lat, tm, "conv_bwd" + tag)
        segs = [da, dg, dgta, dq, dgb, dma, dmb]
        dxa, acc_h = _inproj_bwd(segs, dkv, s["xa"], dxa, s["modv"], g_pre[l][None, :], wp_f[l].T, n_lat, tm,
                                 "inproj_bwd" + tag)

        dwp = [_grad_matmul(s["hb"], sg, tbig, f"grad_w_in{k}" + tag) for k, sg in enumerate(segs + [dkv])]
        g_win.append(jnp.concatenate(dwp[:4] + [dwp[7]] + dwp[4:7], axis=1))
        g_wc.append(_grad_matmul(s["y5"], dyc, tbig, "grad_w_conv_out" + tag))
        g_wa.append(_grad_matmul(s["og"], dya, tbig, "grad_w_attn_out" + tag))
        g_wo.append(_grad_matmul(s["z"], dout, tbig, "grad_w_out" + tag))
        g_convw.append(dconvw)

        dmod = jnp.zeros((8, 3 * d), F32)
        dmod = dmod.at[0].set(jnp.concatenate([acc_h[0], acc_h[2], acc_m[0]]))
        dmod = dmod.at[1].set(jnp.concatenate([acc_h[1], acc_h[3], acc_m[1]]))
        dwm, dbm, dcv = _mod_bwd(dmod, cvec, cvec_t, wmod_f[l], "mod_bwd" + tag)
        g_wmod.append(dwm)
        g_bmod.append(dbm[0])
        g_cctx = g_cctx + dcv[1]
        g_gpre.append(acc_h[4])
        g_gpost.append(acc_m[2])
        g_lng.append(acc_m[3])
        g_lnb.append(acc_m[4])
        g_convb.append(acc_m[5])
        g_qg.append(acc_q[0].reshape(heads, HEAD_DIM).sum(0))
        g_kg.append(acc_q[1, :kw].reshape(heads // GROUP, HEAD_DIM).sum(0))

    grad_x = dxa[:n_lat][None]

    def stack(lst):
        return jnp.stack(lst[::-1])

    big_names = ["w_mod", "w_in", "w_conv_out", "w_attn_out", "w_out", "conv_w"]
    big_w = dict(w_mod=w_mod, w_in=w_in, w_conv_out=w_conv_out, w_attn_out=w_attn_out, w_out=w_out, conv_w=conv_w)
    big_m = dict(w_mod=m_w_mod, w_in=m_w_in, w_conv_out=m_w_conv_out, w_attn_out=m_w_attn_out, w_out=m_w_out,
                 conv_w=m_conv_w)
    big_v = dict(w_mod=v_w_mod, w_in=v_w_in, w_conv_out=v_w_conv_out, w_attn_out=v_w_attn_out, w_out=v_w_out,
                 conv_w=v_conv_w)
    by_dest = [_cols_by_dest(stack(g_wmod)), _cols_by_dest(stack(g_win)), _rows_by_dest(stack(g_wc)),
               _rows_by_dest(stack(g_wa)), _rows_by_dest(stack(g_wo)), _cols_by_dest(stack(g_convw))]
    send = [q.reshape((4, 2) + q.shape[1:]) for q in by_dest]
    from_sibling = _swap_with_sibling(send, "reduce_sibling")
    chip_part = [_add_sibling_part(sb, rc, BF16, "reduce_sibling_add_" + n)
                 for n, sb, rc in zip(big_names, send, from_sibling)]
    from_chips = _exchange_chips(chip_part, "reduce_chips")
    big_g, big_d, big_nm, big_nv = {}, {}, {}, {}
    for n, st in zip(big_names, from_chips):
        big_g[n], big_d[n], big_nm[n], big_nv[n] = _sum_adamw(st, big_w[n], big_m[n], big_v[n], "adamw_" + n)

    small_names = ["c_ctx", "b_mod", "g_pre", "g_post", "conv_b", "ln_g", "ln_b", "q_norm_g", "k_norm_g", "loss"]
    zero1 = jnp.zeros((1,), F32)
    small_w = dict(c_ctx=c_ctx, b_mod=b_mod, g_pre=g_pre, g_post=g_post, conv_b=conv_b, ln_g=ln_g, ln_b=ln_b,
                   q_norm_g=q_norm_g, k_norm_g=k_norm_g, loss=zero1)
    small_m = dict(c_ctx=m_c_ctx, b_mod=m_b_mod, g_pre=m_g_pre, g_post=m_g_post, conv_b=m_conv_b, ln_g=m_ln_g,
                   ln_b=m_ln_b, q_norm_g=m_q_norm_g, k_norm_g=m_k_norm_g, loss=zero1)
    small_v = dict(c_ctx=v_c_ctx, b_mod=v_b_mod, g_pre=v_g_pre, g_post=v_g_post, conv_b=v_conv_b, ln_g=v_ln_g,
                   ln_b=v_ln_b, q_norm_g=v_q_norm_g, k_norm_g=v_k_norm_g, loss=zero1)
    small_g = dict(c_ctx=g_cctx, b_mod=stack(g_bmod), g_pre=stack(g_gpre), g_post=stack(g_gpost),
                   conv_b=stack(g_convb), ln_g=stack(g_lng), ln_b=stack(g_lnb), q_norm_g=stack(g_qg),
                   k_norm_g=stack(g_kg), loss=loss_blk[0, 0:1])
    small_shapes = [small_w[n].shape for n in small_names]

    def pack_small(tree):
        return _pack([tree[n] for n in small_names], d, 8)

    small_parts, = _all_gather([pack_small(small_g)], "gather_small_grads")
    small_out = _sum_adamw(small_parts, pack_small(small_w), pack_small(small_m), pack_small(small_v),
                           "adamw_replicated")
    sm_g, sm_d, sm_nm, sm_nv = [dict(zip(small_names, _unpack(o.reshape(-1), small_shapes))) for o in small_out]
    loss = sm_g["loss"].reshape(())

    order = ["c_ctx", "w_mod", "b_mod", "g_pre", "g_post", "w_in", "conv_w", "conv_b", "ln_g", "ln_b",
             "w_conv_out", "q_norm_g", "k_norm_g", "w_attn_out", "w_out"]

    def pick(big, small):
        return [big[n] if n in big else small[n] for n in order]

    return (loss, grad_x, *pick(big_g, sm_g), *pick(big_d, sm_d), *pick(big_nm, sm_nm), *pick(big_nv, sm_nv))
```

```python
import math

import jax
import jax.numpy as jnp
from jax import lax
from jax.experimental import pallas as pl
from jax.experimental.pallas import tpu as pltpu

F32 = jnp.float32
BF16 = jnp.bfloat16

HEAD_DIM = 64
GROUP = 4
GRID_W = 64
ROPE_THETA = 10000.0
EPS = 1e-6
ATTN_SCALE = HEAD_DIM ** -0.5
LOG2E = 1.4426950408889634
Q_PRESCALE = ATTN_SCALE * LOG2E
HALO = 16
CONV_ROWS = 64
ATTN_FWD_AHEAD = 16
ATTN_BWD_AHEAD = 3

ADAM_LR = 0.001
ADAM_B1 = 0.9
ADAM_B2 = 0.999
ADAM_EPS = 1e-08
ADAM_WD = 0.01
ADAM_STEP = 10

N_DEV = 8
MESH_AXES = ("x", "y", "c")
V7X_VMEM_LIMIT = 56 * 1024 * 1024
NEG_BIG = -1e30

MESH = pl.DeviceIdType.MESH
ANY = pl.BlockSpec(memory_space=pl.ANY)


def _pcall(body, **kw):
    return pl.pallas_call(body, **kw)


def _cp(*sem):
    return pltpu.CompilerParams(dimension_semantics=sem, vmem_limit_bytes=V7X_VMEM_LIMIT)


def _sig(x):
    return 0.5 * jnp.tanh(0.5 * x) + 0.5


def _mean(x):
    return jnp.mean(x, axis=-1, keepdims=True)


def _colsum(x):
    return jnp.sum(x, axis=0, keepdims=True)


def _bf_round(x):
    return x.astype(BF16).astype(F32)


def _dot(a, b):
    return jnp.dot(a, b, preferred_element_type=F32)


def _dot_nt(a, b):
    return lax.dot_general(a, b, (((1,), (1,)), ((), ())), preferred_element_type=F32)


def _dot_tn(a, b):
    return lax.dot_general(a, b, (((0,), (0,)), ((), ())), preferred_element_type=F32)


def _split_dot(x, m):
    hi = x.astype(BF16)
    lo = (x - hi.astype(F32)).astype(BF16)
    return _dot(hi, m) + _dot(lo, m)


def _full(shape):
    nd = len(shape)
    return pl.BlockSpec(shape, lambda *_: (0,) * nd)


def _rows(tm, width, colblk=0):
    return pl.BlockSpec((tm, width), lambda i: (i, colblk))


def _my_place():
    return lax.axis_index("x"), lax.axis_index("y"), lax.axis_index("c")


def _sem_arrays(n):
    return [pltpu.SemaphoreType.DMA((n,)), pltpu.SemaphoreType.DMA((n,))]


def _all_gather(shards, name):
    n = len(shards)

    def body(*refs):
        x_refs, out_refs = refs[:n], refs[n:2 * n]
        send_sems, recv_sems, local_sems = refs[2 * n:]
        x, y, c = _my_place()
        me, sibling = (x, y, c), (x, y, 1 - c)
        chips = [(1 - x, y), (x, 1 - y), (1 - x, 1 - y)]

        def slab(a, px, py, pc):
            return out_refs[a].at[4 * px + 2 * py + pc]

        def copies(k, block, to, from_input=False):
            return [pltpu.make_async_remote_copy(
                src_ref=x_refs[a] if from_input else slab(a, *block), dst_ref=slab(a, *block),
                send_sem=send_sems.at[k * n + a], recv_sem=recv_sems.at[k * n + a],
                device_id=to, device_id_type=MESH) for a in range(n)]

        mine = [pltpu.make_async_copy(x_refs[a], slab(a, *me), local_sems.at[a]) for a in range(n)]
        for cp in mine:
            cp.start()
        first = copies(0, me, sibling, True)
        for j, chip in enumerate(chips):
            first += copies(1 + j, me, (*chip, c), True)
        for cp in first:
            cp.start()
        passed = []
        for j, chip in enumerate(chips):
            for cp in copies(1 + j, (*chip, c), me):
                cp.wait_recv()
            onward = copies(4 + j, (*chip, c), sibling)
            for cp in onward:
                cp.start()
            passed += onward
        for cp in copies(0, sibling, me):
            cp.wait_recv()
        for j, chip in enumerate(chips):
            for cp in copies(4 + j, (*chip, 1 - c), me):
                cp.wait_recv()
        for cp in first + passed:
            cp.wait_send()
        for cp in mine:
            cp.wait()

    return _pcall(
        body, name=name,
        out_shape=[jax.ShapeDtypeStruct((N_DEV,) + q.shape, q.dtype) for q in shards],
        in_specs=[ANY] * n, out_specs=[ANY] * n,
        scratch_shapes=_sem_arrays(7 * n) + [pltpu.SemaphoreType.DMA((n,))],
    )(*shards)


def _swap_with_sibling(bufs, name):
    n = len(bufs)

    def body(*refs):
        buf_refs, recv_refs = refs[:n], refs[n:2 * n]
        send_sems, recv_sems = refs[2 * n:]
        x, y, c = _my_place()
        copies = [
            pltpu.make_async_remote_copy(
                src_ref=buf_refs[a].at[k, 1 - c], dst_ref=recv_refs[a].at[k],
                send_sem=send_sems.at[k * n + a], recv_sem=recv_sems.at[k * n + a],
                device_id=(x, y, 1 - c), device_id_type=MESH)
            for k in range(4) for a in range(n)]
        for cp in copies:
            cp.start()
        for cp in copies:
            cp.wait()

    return _pcall(
        body, name=name,
        out_shape=[jax.ShapeDtypeStruct((4,) + q.shape[2:], q.dtype) for q in bufs],
        in_specs=[ANY] * n, out_specs=[ANY] * n,
        scratch_shapes=_sem_arrays(4 * n),
    )(*bufs)


def _exchange_chips(parts, name):
    n = len(parts)

    def body(*refs):
        s_refs, recv_refs = refs[:n], refs[n:2 * n]
        send_sems, recv_sems, local_sems = refs[2 * n:]
        x, y, c = _my_place()
        mychip = 2 * x + y
        chips = [(1 - x, y), (x, 1 - y), (1 - x, 1 - y)]
        mine = [pltpu.make_async_copy(s_refs[a].at[mychip], recv_refs[a].at[mychip], local_sems.at[a])
                for a in range(n)]
        for cp in mine:
            cp.start()
        copies = [
            pltpu.make_async_remote_copy(
                src_ref=s_refs[a].at[2 * px + py], dst_ref=recv_refs[a].at[mychip],
                send_sem=send_sems.at[j * n + a], recv_sem=recv_sems.at[j * n + a],
                device_id=(px, py, c), device_id_type=MESH)
            for j, (px, py) in enumerate(chips) for a in range(n)]
        for cp in copies:
            cp.start()
        for cp in copies:
            cp.wait()
        for cp in mine:
            cp.wait()

    return _pcall(
        body, name=name,
        out_shape=[jax.ShapeDtypeStruct(q.shape, q.dtype) for q in parts],
        in_specs=[ANY] * n, out_specs=[ANY] * n,
        scratch_shapes=_sem_arrays(3 * n) + [pltpu.SemaphoreType.DMA((n,))],
    )(*parts)


def _row_tile(a):
    for t in range(256, 7, -8):
        if a % t == 0:
            return t
    return a


def _add_sibling_part(buf, recv, out_dtype, name):
    _, _, nl, a, b = buf.shape
    ta = _row_tile(a)
    core = lax.axis_index("c").astype(jnp.int32).reshape(1)

    def body(core_ref, a_ref, b_ref, o_ref):
        o_ref[...] = (a_ref[...] + b_ref[...]).astype(o_ref.dtype)

    grid_spec = pltpu.PrefetchScalarGridSpec(
        num_scalar_prefetch=1, grid=(4, nl, a // ta),
        in_specs=[pl.BlockSpec((None, None, None, ta, b), lambda k, l, r, cr: (k, cr[0], l, r, 0)),
                  pl.BlockSpec((None, None, ta, b), lambda k, l, r, cr: (k, l, r, 0))],
        out_specs=pl.BlockSpec((None, None, ta, b), lambda k, l, r, cr: (k, l, r, 0)))
    return _pcall(body, name=name, grid_spec=grid_spec,
                  out_shape=jax.ShapeDtypeStruct((4, nl, a, b), out_dtype),
                  compiler_params=_cp("parallel", "parallel", "parallel"))(core, buf, recv)


def _sum_adamw(stack, w, m, v, name):
    ns, nl, a, b = stack.shape
    ta = _row_tile(a)
    c1 = 1.0 - ADAM_B1 ** ADAM_STEP
    c2 = 1.0 - ADAM_B2 ** ADAM_STEP

    def body(s_ref, w_ref, m_ref, v_ref, g_out, d_out, m_out, v_out):
        g = s_ref[0].astype(F32)
        for k in range(1, ns):
            g = g + s_ref[k].astype(F32)
        m_new = ADAM_B1 * m_ref[...] + (1.0 - ADAM_B1) * g
        v_new = ADAM_B2 * v_ref[...] + (1.0 - ADAM_B2) * (g * g)
        m_hat = m_new / c1
        v_hat = v_new / c2
        g_out[...] = g
        d_out[...] = -ADAM_LR * (m_hat / (jnp.sqrt(v_hat) + ADAM_EPS) + ADAM_WD * w_ref[...])
        m_out[...] = m_new
        v_out[...] = v_new

    blk = pl.BlockSpec((None, ta, b), lambda l, i: (l, i, 0))
    return _pcall(
        body, name=name, grid=(nl, a // ta),
        in_specs=[pl.BlockSpec((ns, None, ta, b), lambda l, i: (0, l, i, 0)), blk, blk, blk],
        out_specs=[blk] * 4,
        out_shape=[jax.ShapeDtypeStruct((nl, a, b), F32)] * 4,
        compiler_params=_cp("parallel", "parallel"))(stack, w, m, v)


def _mod_fwd(cvec, wmod, bmod, name):
    _, d = cvec.shape

    def body(c_ref, w_ref, b_ref, o_ref):
        cv = c_ref[...]
        cs = cv * _sig(cv)
        o_ref[...] = _dot(cs.astype(BF16), w_ref[...]) + b_ref[...]

    return _pcall(
        body, name=name, grid=(3,),
        in_specs=[_full((8, d)), pl.BlockSpec((d, d), lambda n: (0, n)), pl.BlockSpec((1, d), lambda n: (0, n))],
        out_specs=pl.BlockSpec((8, d), lambda n: (0, n)),
        out_shape=jax.ShapeDtypeStruct((8, 3 * d), F32),
        compiler_params=_cp("parallel"))(cvec, wmod, bmod)


def _mod_bwd(dmod, cvec, cvec_t, wmod, name):
    _, d = cvec.shape

    def body(dm_ref, c_ref, ct_ref, w_ref, dw_ref, db_ref, dc_ref):
        n = pl.program_id(0)
        dm = dm_ref[...]
        ct = ct_ref[...]
        cs_t = _bf_round(ct * _sig(ct))
        d0 = _bf_round(dm[0:1, :])
        d1 = _bf_round(dm[1:2, :])
        dw_ref[...] = cs_t[:, 0:1] * d0 + cs_t[:, 1:2] * d1
        db_ref[...] = dm[0:1, :] + dm[1:2, :]

        @pl.when(n == 0)
        def _():
            dc_ref[...] = jnp.zeros_like(dc_ref)

        dc_ref[...] += _dot_nt(dm.astype(BF16), w_ref[...])

        @pl.when(n == 2)
        def _():
            cv = c_ref[...]
            s = _sig(cv)
            dc_ref[...] = dc_ref[...] * (s * (1.0 + cv * (1.0 - s)))

    return _pcall(
        body, name=name, grid=(3,),
        in_specs=[pl.BlockSpec((8, d), lambda n: (0, n)), _full((8, d)), _full((d, 128)),
                  pl.BlockSpec((d, d), lambda n: (0, n))],
        out_specs=[pl.BlockSpec((d, d), lambda n: (0, n)), pl.BlockSpec((1, d), lambda n: (0, n)),
                   _full((8, d))],
        out_shape=[jax.ShapeDtypeStruct((d, 3 * d), F32), jax.ShapeDtypeStruct((1, 3 * d), F32),
                   jax.ShapeDtypeStruct((8, d), F32)],
        compiler_params=_cp("arbitrary"))(dmod, cvec, cvec_t, wmod)


def _seg_rows(mod_ref, lo, hi, is_ctx):
    return jnp.where(is_ctx, mod_ref[1:2, lo:hi], mod_ref[0:1, lo:hi])


def _inproj(xa, modv, gpre, wp, n_lat, tm, name):
    na, d = xa.shape
    wcols = wp.shape[1]
    tn = 5 * d // 4 if (5 * d // 4) % 128 == 0 and wcols % (5 * d // 4) == 0 else d // 2

    def body(x_ref, mod_ref, g_ref, w_ref, p_ref, h_ref, h_s):
        i = pl.program_id(0)

        @pl.when(pl.program_id(1) == 0)
        def _():
            x = x_ref[...]
            r = lax.rsqrt(_mean(x * x) + EPS)
            row = i * tm + lax.broadcasted_iota(jnp.int32, (tm, 1), 0)
            is_ctx = row >= n_lat
            sh = _seg_rows(mod_ref, 0, d, is_ctx)
            sc = _seg_rows(mod_ref, d, 2 * d, is_ctx)
            hb = ((x * r * g_ref[...]) * (1.0 + sc) + sh).astype(BF16)
            h_s[...] = hb
            h_ref[...] = hb

        p_ref[...] = _dot(h_s[...], w_ref[...])

    return _pcall(
        body, name=name, grid=(na // tm, wcols // tn),
        in_specs=[pl.BlockSpec((tm, d), lambda i, j: (i, 0)), _full((8, 3 * d)), _full((1, d)),
                  pl.BlockSpec((d, tn), lambda i, j: (0, j))],
        out_specs=[pl.BlockSpec((tm, tn), lambda i, j: (i, j)), pl.BlockSpec((tm, d), lambda i, j: (i, 0))],
        out_shape=[jax.ShapeDtypeStruct((na, wcols), F32), jax.ShapeDtypeStruct((na, d), BF16)],
        scratch_shapes=[pltpu.VMEM((tm, d), BF16)],
        compiler_params=_cp("parallel", "arbitrary"))(xa, modv, gpre, wp)


def _lane_tile(t, width):
    if width >= 128:
        return jnp.tile(t, (1, width // 128))
    return t[:, :width]


def _partner(x):
    w = x.shape[-1]
    lane = lax.broadcasted_iota(jnp.int32, x.shape, 1)
    low = (lane % HEAD_DIM) < (HEAD_DIM // 2)
    return jnp.where(low, pltpu.roll(x, w - HEAD_DIM // 2, 1), pltpu.roll(x, HEAD_DIM // 2, 1))


def _qknorm_fwd(p, cos_t, sin_t, gq, gk, bd, tm, name):
    na = p.shape[0]
    d = gq.shape[1]
    kw = d // GROUP

    def body(q_ref, kv_ref, cos_ref, sin_ref, gq_ref, gk_ref, bd_ref, qo_ref, ko_ref, vo_ref):
        cos = cos_ref[...]
        sin = sin_ref[...]

        def norm_rope(xh, g, w):
            ms = _split_dot(xh * xh, bd_ref[0:w, 0:w]) * (1.0 / HEAD_DIM)
            xn = xh * lax.rsqrt(ms + EPS) * g
            return xn * _lane_tile(cos, w) + _partner(xn) * _lane_tile(sin, w)

        qo_ref[...] = (norm_rope(q_ref[...], gq_ref[...], d) * Q_PRESCALE).astype(BF16)
        kv = kv_ref[...]
        ko_ref[...] = norm_rope(kv[:, 0:kw], gk_ref[...], kw).astype(BF16)
        vo_ref[...] = kv[:, kw:2 * kw].astype(BF16)

    return _pcall(
        body, name=name, grid=(na // tm,),
        in_specs=[_rows(tm, d, 3), _rows(tm, d // 2, 14), _rows(tm, 128), _rows(tm, 128),
                  _full((1, d)), _full((1, kw)), _full((d, d))],
        out_specs=[_rows(tm, d), _rows(tm, kw), _rows(tm, kw)],
        out_shape=[jax.ShapeDtypeStruct((na, d), BF16), jax.ShapeDtypeStruct((na, kw), BF16),
                   jax.ShapeDtypeStruct((na, kw), BF16)],
        compiler_params=_cp("parallel"))(p, p, cos_t, sin_t, gq, gk, bd)


def _attn_fwd(q_hm, k_hm, vt_hm, na, tq, tk, cw, name):
    h, _, hd = q_hm.shape
    kv, nkeys, _ = k_hm.shape
    vrows = vt_hm.shape[1]
    nq, nk = na // tq, nkeys // tk
    nsub = tq // cw
    chains = [(a, u) for a in range(GROUP) for u in range(nsub)]

    def body(q_ref, k_ref, vt_ref, o_ref, lse_ref, m_s, acc_s):
        j = pl.program_id(2)

        @pl.when(j == 0)
        def _():
            m_s[...] = jnp.full_like(m_s, NEG_BIG)
            acc_s[...] = jnp.zeros_like(acc_s)

        k = k_ref[...]
        vt = vt_ref[...]

        def scores(n):
            a, u = chains[n]
            return _dot_nt(k, q_ref[a, u * cw:(u + 1) * cw, :])

        s_tiles = {n: scores(n) for n in range(min(ATTN_FWD_AHEAD, len(chains)))}
        pending = None
        for n, (a, u) in enumerate(chains):
            cols = slice(u * cw, (u + 1) * cw)
            s_t = s_tiles.pop(n)
            m_prev = m_s[a, :, cols]
            m_new = jnp.maximum(m_prev, jnp.max(s_t, axis=0, keepdims=True))
            m_s[a, :, cols] = m_new
            pv = _dot(vt, jnp.exp2(s_t - m_new).astype(BF16))
            if pending is not None:
                pa, pcols, palpha, ppv = pending
                acc_s[pa, :, pcols] = palpha * acc_s[pa, :, pcols] + ppv
            pending = (a, cols, jnp.exp2(m_prev - m_new), pv)
            if n + ATTN_FWD_AHEAD < len(chains):
                s_tiles[n + ATTN_FWD_AHEAD] = scores(n + ATTN_FWD_AHEAD)
        pa, pcols, palpha, ppv = pending
        acc_s[pa, :, pcols] = palpha * acc_s[pa, :, pcols] + ppv

        @pl.when(j == nk - 1)
        def _():
            for a in range(GROUP):
                acc = acc_s[a]
                l = acc[hd:hd + 1, :]
                o_ref[a] = acc[0:hd, :] / l
                lse_ref[a] = m_s[a] + jnp.log2(l)

    return _pcall(
        body, name=name, grid=(kv, nq, nk),
        in_specs=[pl.BlockSpec((GROUP, tq, hd), lambda g, i, j: (g, i, 0)),
                  pl.BlockSpec((None, tk, hd), lambda g, i, j: (g, j, 0)),
                  pl.BlockSpec((None, vrows, tk), lambda g, i, j: (g, 0, j))],
        out_specs=[pl.BlockSpec((GROUP, hd, tq), lambda g, i, j: (g, 0, i)),
                   pl.BlockSpec((GROUP, 1, tq), lambda g, i, j: (g, 0, i))],
        out_shape=[jax.ShapeDtypeStruct((h, hd, na), F32), jax.ShapeDtypeStruct((h, 1, na), F32)],
        scratch_shapes=[pltpu.VMEM((GROUP, 1, tq), F32), pltpu.VMEM((GROUP, vrows, tq), F32)],
        compiler_params=_cp("parallel", "parallel", "arbitrary"))(q_hm, k_hm, vt_hm)


def _window(win_ref, prev, cur, nxt, first, last, tm):
    win_ref[0:HALO, :] = jnp.where(first, 0.0, prev)
    win_ref[HALO:HALO + tm, :] = cur
    win_ref[HALO + tm:HALO + tm + HALO, :] = jnp.where(last, 0.0, nxt)


def _lane_blocks(d):
    return [slice(b, b + 128) for b in range(0, d, 128)]


def _sublane_shifts(slab):
    n = slab.shape[0]
    for b in range(8):
        sh = slab if b == 0 else pltpu.roll(slab, n - b, 0)
        for a8 in range(0, 2 * HALO, 8):
            yield a8 + b, sh[a8:a8 + CONV_ROWS, :]


def _halo_specs(tm, d, na, colblk):
    per = tm // HALO
    last_blk = na // HALO - 1
    prev = pl.BlockSpec((HALO, d), lambda i: (jnp.maximum(i * per - 1, 0), colblk))
    nxt = pl.BlockSpec((HALO, d), lambda i: (jnp.minimum((i + 1) * per, last_blk), colblk))
    return prev, nxt


def _seq_ends(i, n_lat, na, tm):
    first = jnp.logical_or(i == 0, i == n_lat // tm)
    last = jnp.logical_or(i == n_lat // tm - 1, i == na // tm - 1)
    return first, last


def _conv_fwd(p, conv_w, conv_b, ln_g, ln_b, n_lat, tm, name):
    na = p.shape[0]
    ktaps, d = conv_w.shape
    pad = ktaps // 2

    def body(a_ref, ap_ref, an_ref, g_ref, gp_ref, gn_ref, ga_ref, w_ref, cb_ref, lg_ref, lb_ref,
             y5_ref, y2_ref, win):
        i = pl.program_id(0)
        first, last = _seq_ends(i, n_lat, na, tm)
        _window(win, ap_ref[...] * _sig(gp_ref[...]), a_ref[...] * _sig(g_ref[...]),
                an_ref[...] * _sig(gn_ref[...]), first, last, tm)

        def chunk(c, carry):
            r = pl.multiple_of(c * CONV_ROWS, CONV_ROWS)
            for lanes in _lane_blocks(d):
                acc = jnp.zeros((CONV_ROWS, 128), F32)
                for off, sh in _sublane_shifts(win[pl.ds(r, CONV_ROWS + 2 * HALO), lanes]):
                    k = off - (HALO - pad)
                    if 0 <= k < ktaps:
                        acc = acc + w_ref[k:k + 1, lanes] * sh
                y2_ref[pl.ds(r, CONV_ROWS), lanes] = acc + cb_ref[:, lanes]
            return carry

        lax.fori_loop(0, tm // CONV_ROWS, chunk, 0)
        y2 = y2_ref[...]
        xc = y2 - _mean(y2)
        y3 = xc * lax.rsqrt(_mean(xc * xc) + EPS) * lg_ref[...] + lb_ref[...]
        gate = ga_ref[...]
        y5_ref[...] = ((y3 * _sig(y3)) * (gate * _sig(gate))).astype(BF16)

    ap, an = _halo_specs(tm, d, na, 0)
    gp, gn = _halo_specs(tm, d, na, 1)
    return _pcall(
        body, name=name, grid=(na // tm,),
        in_specs=[_rows(tm, d, 0), ap, an, _rows(tm, d, 1), gp, gn, _rows(tm, d, 2),
                  _full((ktaps, d)), _full((1, d)), _full((1, d)), _full((1, d))],
        out_specs=[_rows(tm, d), _rows(tm, d)],
        out_shape=[jax.ShapeDtypeStruct((na, d), BF16), jax.ShapeDtypeStruct((na, d), F32)],
        scratch_shapes=[pltpu.VMEM((tm + 2 * HALO, d), F32)],
        compiler_params=_cp("parallel"))(p, p, p, p, p, p, p, conv_w, conv_b, ln_g, ln_b)


def _merge_fwd(y5, o_tm, p, xa, modv, gpost, wc, wa, wo, n_lat, tm, name):
    na, d = xa.shape

    def body(y5_ref, o_ref, gb_ref, ma_ref, mb_ref, x_ref, mod_ref, gp_ref, wc_ref, wa_ref, wo_ref,
             xn_ref, yc_ref, ya_ref, out_ref, z_ref, og_ref):
        is_ctx = pl.program_id(0) >= n_lat // tm
        gate_b = gb_ref[...]
        og = (o_ref[...] * (gate_b * _sig(gate_b))).astype(BF16)
        og_ref[...] = og
        yc = _dot(y5_ref[...], wc_ref[...])
        ya = _dot(og, wa_ref[...])
        yc_ref[...] = yc
        ya_ref[...] = ya
        z = (_sig(ma_ref[...]) * yc + _sig(mb_ref[...]) * ya).astype(BF16)
        z_ref[...] = z
        out = _dot(z, wo_ref[...])
        out_ref[...] = out
        gt = _seg_rows(mod_ref, 2 * d, 3 * d, is_ctx)
        xn_ref[...] = x_ref[...] + gt * (out * lax.rsqrt(_mean(out * out) + EPS) * gp_ref[...])

    f32o = jax.ShapeDtypeStruct((na, d), F32)
    bfo = jax.ShapeDtypeStruct((na, d), BF16)
    return _pcall(
        body, name=name, grid=(na // tm,),
        in_specs=[_rows(tm, d), _rows(tm, d), _rows(tm, d, 4), _rows(tm, d, 5), _rows(tm, d, 6), _rows(tm, d),
                  _full((8, 3 * d)), _full((1, d)), _full((d, d)), _full((d, d)), _full((d, d))],
        out_specs=[_rows(tm, d)] * 6,
        out_shape=[f32o, f32o, f32o, f32o, bfo, bfo],
        compiler_params=_cp("parallel"))(y5, o_tm, p, p, p, xa, modv, gpost, wc, wa, wo)


def _loss_grad(xa, target, n_lat, tm, name):
    na, d = xa.shape
    nlt = n_lat // tm

    def body(x_ref, t_ref, dx_ref, loss_ref):
        i = pl.program_id(0)

        @pl.when(i == 0)
        def _():
            loss_ref[...] = jnp.zeros_like(loss_ref)

        @pl.when(i < nlt)
        def _():
            err = x_ref[...] - t_ref[...]
            dx_ref[...] = err * (1.0 / d)
            loss_ref[...] += 0.5 * jnp.sum(_mean(err * err))

        @pl.when(i >= nlt)
        def _():
            dx_ref[...] = jnp.zeros_like(dx_ref)

    return _pcall(
        body, name=name, grid=(na // tm,),
        in_specs=[_rows(tm, d), pl.BlockSpec((tm, d), lambda i: (jnp.minimum(i, nlt - 1), 0))],
        out_specs=[_rows(tm, d), _full((8, 128))],
        out_shape=[jax.ShapeDtypeStruct((na, d), F32), jax.ShapeDtypeStruct((8, 128), F32)],
        compiler_params=_cp("arbitrary"))(xa, target)


def _merge_bwd(dxn, out, yc, ya, o_tm, p, y2, modv, gpost, ln_g, ln_b, wo_t, wc_t, wa_t, esel, n_lat, tm, name):
    na, d = dxn.shape

    def body(dx_ref, out_ref, yc_ref, ya_ref, o_ref, gb_ref, ma_ref, mb_ref, gta_ref, y2_ref,
             mod_ref, gp_ref, lg_ref, lb_ref, wot_ref, wct_ref, wat_ref, es_ref,
             dgb_ref, dma_ref, dmb_ref, dgta_ref, do_ref, dout_ref, dyc_ref, dya_ref, dy2_ref, dl_ref, acc_ref):
        i = pl.program_id(0)
        ctx_tile = i >= n_lat // tm

        @pl.when(i == 0)
        def _():
            acc_ref[...] = jnp.zeros_like(acc_ref)

        gt = _seg_rows(mod_ref, 2 * d, 3 * d, ctx_tile)
        gp = gp_ref[...]
        dx = dx_ref[...]
        out = out_ref[...]
        r2 = lax.rsqrt(_mean(out * out) + EPS)
        n2 = out * r2
        dgt = _colsum(dx * (n2 * gp))

        @pl.when(ctx_tile)
        def _():
            acc_ref[1:2, :] += dgt

        @pl.when(jnp.logical_not(ctx_tile))
        def _():
            acc_ref[0:1, :] += dgt

        acc_ref[2:3, :] += _colsum(dx * gt * n2)
        dn2 = dx * gt * gp
        dout = (r2 * (dn2 - n2 * _mean(dn2 * n2))).astype(BF16)
        dout_ref[...] = dout
        dz = _dot(dout, wot_ref[...])
        sa = _sig(ma_ref[...])
        sb = _sig(mb_ref[...])
        dyc = (dz * sa).astype(BF16)
        dya = (dz * sb).astype(BF16)
        dyc_ref[...] = dyc
        dya_ref[...] = dya
        dma_ref[...] = (dz * yc_ref[...] * sa * (1.0 - sa)).astype(BF16)
        dmb_ref[...] = (dz * ya_ref[...] * sb * (1.0 - sb)).astype(BF16)
        dy5 = _dot(dyc, wct_ref[...])
        dog = _dot(dya, wat_ref[...])

        gate_b = gb_ref[...]
        sgb = _sig(gate_b)
        o = o_ref[...]
        do = dog * (gate_b * sgb)
        do_ref[...] = do.astype(BF16)
        dgb_ref[...] = (dog * o * (sgb * (1.0 + gate_b * (1.0 - sgb)))).astype(BF16)
        dl_ref[...] = _split_dot(do * o, es_ref[...])

        y2 = y2_ref[...]
        xc = y2 - _mean(y2)
        rstd = lax.rsqrt(_mean(xc * xc) + EPS)
        xhat = xc * rstd
        lg = lg_ref[...]
        y3 = xhat * lg + lb_ref[...]
        s3 = _sig(y3)
        gate_a = gta_ref[...]
        sga = _sig(gate_a)
        dgta_ref[...] = (dy5 * (y3 * s3) * (sga * (1.0 + gate_a * (1.0 - sga)))).astype(BF16)
        dy3 = dy5 * (gate_a * sga) * (s3 * (1.0 + y3 * (1.0 - s3)))
        acc_ref[3:4, :] += _colsum(dy3 * xhat)
        acc_ref[4:5, :] += _colsum(dy3)
        dxh = dy3 * lg
        dy2 = rstd * (dxh - _mean(dxh) - xhat * _mean(dxh * xhat))
        dy2_ref[...] = dy2
        acc_ref[5:6, :] += _colsum(dy2)

    f32o = jax.ShapeDtypeStruct((na, d), F32)
    bfo = jax.ShapeDtypeStruct((na, d), BF16)
    r = _rows(tm, d)
    return _pcall(
        body, name=name, grid=(na // tm,),
        in_specs=[r, r, r, r, r, _rows(tm, d, 4), _rows(tm, d, 5), _rows(tm, d, 6), _rows(tm, d, 2), r,
                  _full((8, 3 * d)), _full((1, d)), _full((1, d)), _full((1, d)),
                  _full((d, d)), _full((d, d)), _full((d, d)), _full((d, 128))],
        out_specs=[r] * 9 + [_rows(tm, 128), _full((8, d))],
        out_shape=[bfo] * 8 + [f32o, jax.ShapeDtypeStruct((na, 128), F32), jax.ShapeDtypeStruct((8, d), F32)],
        compiler_params=_cp("arbitrary"),
    )(dxn, out, yc, ya, o_tm, p, p, p, p, y2, modv, gpost, ln_g, ln_b, wo_t, wc_t, wa_t, esel)


def _attn_bwd(q_hm, qt_hm, k_hm, kt_hm, v_hm, do_hm, dot_hm, lse_r, dl_r, na, tq, tk, cw, name):
    h, _, hd = q_hm.shape
    kv, nkeys, _ = k_hm.shape
    nq, nk = na // tq, nkeys // tk
    nsub = tq // cw
    chains = [(a, u) for a in range(GROUP) for u in range(nsub)]

    def body(q_ref, qt_ref, k_ref, kt_ref, v_ref, do_ref, dot_ref, lse_ref, dl_ref, dq_hbm, dk_ref, dv_ref,
             dq_acc, dk_acc, dv_acc, sem):
        g = pl.program_id(0)
        j = pl.program_id(1)
        i = pl.program_id(2)

        @pl.when(jnp.logical_and(j == 0, i == 0))
        def _():
            dq_acc[...] = jnp.zeros_like(dq_acc)

        @pl.when(i == 0)
        def _():
            dk_acc[...] = jnp.zeros_like(dk_acc)
            dv_acc[...] = jnp.zeros_like(dv_acc)

        k = k_ref[...]
        v = v_ref[...]
        kt = kt_ref[...]

        def products(n):
            a, u = chains[n]
            rows = slice(u * cw, (u + 1) * cw)
            return _dot_nt(k, q_ref[a, rows, :]), _dot_nt(v, do_ref[a, rows, :])

        def accumulate(done):
            a, u, dv_t, dk_t, dq_t = done
            dv_acc[...] += dv_t
            dk_acc[...] += dk_t
            at = pl.multiple_of(i * tq + u * cw, cw)
            dq_acc[a, :, pl.ds(at, cw)] += dq_t

        tiles = {n: products(n) for n in range(min(ATTN_BWD_AHEAD, len(chains)))}
        pending = None
        for n, (a, u) in enumerate(chains):
            cols = slice(u * cw, (u + 1) * cw)
            s_t, dp_t = tiles.pop(n)
            p_t = jnp.exp2(s_t - lse_ref[a, :, cols])
            ds_b = (p_t * (dp_t - dl_ref[a, :, cols])).astype(BF16)
            p_b = p_t.astype(BF16)
            dv_t = _dot_nt(dot_ref[a, :, cols], p_b)
            dk_t = _dot_nt(qt_ref[a, :, cols], ds_b)
            dq_t = _dot(kt, ds_b)
            if pending is not None:
                accumulate(pending)
            pending = (a, u, dv_t, dk_t, dq_t)
            if n + ATTN_BWD_AHEAD < len(chains):
                tiles[n + ATTN_BWD_AHEAD] = products(n + ATTN_BWD_AHEAD)
        accumulate(pending)

        @pl.when(i == nq - 1)
        def _():
            dk_ref[...] = dk_acc[...]
            dv_ref[...] = dv_acc[...]

        @pl.when(jnp.logical_and(j == nk - 1, i == nq - 1))
        def _():
            cp = pltpu.make_async_copy(dq_acc, dq_hbm.at[pl.ds(g * GROUP, GROUP)], sem)
            cp.start()
            cp.wait()

    qspec = pl.BlockSpec((GROUP, tq, hd), lambda g, j, i: (g, i, 0))
    qtspec = pl.BlockSpec((GROUP, hd, tq), lambda g, j, i: (g, 0, i))
    kspec = pl.BlockSpec((None, tk, hd), lambda g, j, i: (g, j, 0))
    ktspec = pl.BlockSpec((None, hd, tk), lambda g, j, i: (g, 0, j))
    rspec = pl.BlockSpec((GROUP, 1, tq), lambda g, j, i: (g, 0, i))
    return _pcall(
        body, name=name, grid=(kv, nk, nq),
        in_specs=[qspec, qtspec, kspec, ktspec, kspec, qspec, qtspec, rspec, rspec],
        out_specs=[ANY, ktspec, ktspec],
        out_shape=[jax.ShapeDtypeStruct((h, hd, na), F32), jax.ShapeDtypeStruct((kv, hd, nkeys), F32),
                   jax.ShapeDtypeStruct((kv, hd, nkeys), F32)],
        scratch_shapes=[pltpu.VMEM((GROUP, hd, na), F32), pltpu.VMEM((hd, tk), F32), pltpu.VMEM((hd, tk), F32),
                        pltpu.SemaphoreType.DMA],
        compiler_params=_cp("arbitrary", "arbitrary", "arbitrary"),
    )(q_hm, qt_hm, k_hm, kt_hm, v_hm, do_hm, dot_hm, lse_r, dl_r)


def _qknorm_bwd(dq_tm, dk_tm, dv_tm, p, cos_t, sin_t, gq, gk, bd, tm, name):
    na = p.shape[0]
    d = gq.shape[1]
    kw = d // GROUP

    def body(dq_ref, dk_ref, dv_ref, q_ref, kv_ref, cos_ref, sin_ref, gq_ref, gk_ref, bd_ref,
             dqo_ref, dkvo_ref, acc_ref):
        @pl.when(pl.program_id(0) == 0)
        def _():
            acc_ref[...] = jnp.zeros_like(acc_ref)

        cos = cos_ref[...]
        sin = sin_ref[...]

        def back(dy, xh, g, w):
            bdw = bd_ref[0:w, 0:w]
            dn = dy * _lane_tile(cos, w) - _partner(dy) * _lane_tile(sin, w)
            rs = lax.rsqrt(_split_dot(xh * xh, bdw) * (1.0 / HEAD_DIM) + EPS)
            y = xh * rs
            dg = _colsum(dn * y)
            dyn = dn * g
            dx = rs * (dyn - y * (_split_dot(dyn * y, bdw) * (1.0 / HEAD_DIM)))
            return dx, dg

        dq, dgq = back(dq_ref[...] * ATTN_SCALE, q_ref[...], gq_ref[...], d)
        dqo_ref[...] = dq.astype(BF16)
        acc_ref[0:1, :] += dgq
        kv = kv_ref[...]
        dk, dgk = back(dk_ref[...] * (1.0 / LOG2E), kv[:, 0:kw], gk_ref[...], kw)
        acc_ref[1:2, 0:kw] += dgk
        dkvo_ref[:, 0:kw] = dk.astype(BF16)
        dkvo_ref[:, kw:2 * kw] = dv_ref[...].astype(BF16)

    return _pcall(
        body, name=name, grid=(na // tm,),
        in_specs=[_rows(tm, d), _rows(tm, kw), _rows(tm, kw), _rows(tm, d, 3), _rows(tm, d // 2, 14),
                  _rows(tm, 128), _rows(tm, 128), _full((1, d)), _full((1, kw)), _full((d, d))],
        out_specs=[_rows(tm, d), _rows(tm, d // 2), _full((8, d))],
        out_shape=[jax.ShapeDtypeStruct((na, d), BF16), jax.ShapeDtypeStruct((na, d // 2), BF16),
                   jax.ShapeDtypeStruct((8, d), F32)],
        compiler_params=_cp("arbitrary"))(dq_tm, dk_tm, dv_tm, p, p, cos_t, sin_t, gq, gk, bd)


def _conv_bwd(dy2, p, conv_w, n_lat, tm, name):
    na = p.shape[0]
    ktaps, d = conv_w.shape
    pad = ktaps // 2

    def body(dy_ref, dyp_ref, dyn_ref, a_ref, ap_ref, an_ref, g_ref, gp_ref, gn_ref, w_ref,
             da_ref, dg_ref, dw_ref, dwin, ywin, dy1_s, part):
        i = pl.program_id(0)

        @pl.when(i == 0)
        def _():
            part[...] = jnp.zeros_like(part)

        first, last = _seq_ends(i, n_lat, na, tm)
        a = a_ref[...]
        sg = _sig(g_ref[...])
        _window(dwin, dyp_ref[...], dy_ref[...], dyn_ref[...], first, last, tm)
        _window(ywin, ap_ref[...] * _sig(gp_ref[...]), a * sg, an_ref[...] * _sig(gn_ref[...]), first, last, tm)

        def chunk(c, carry):
            r = pl.multiple_of(c * CONV_ROWS, CONV_ROWS)
            for lanes in _lane_blocks(d):
                dy = dy_ref[pl.ds(r, CONV_ROWS), lanes]
                acc = jnp.zeros((CONV_ROWS, 128), F32)
                for off, sh in _sublane_shifts(dwin[pl.ds(r, CONV_ROWS + 2 * HALO), lanes]):
                    k = HALO + pad - off
                    if 0 <= k < ktaps:
                        acc = acc + w_ref[k:k + 1, lanes] * sh
                dy1_s[pl.ds(r, CONV_ROWS), lanes] = acc
                for off, sh in _sublane_shifts(ywin[pl.ds(r, CONV_ROWS + 2 * HALO), lanes]):
                    k = off - (HALO - pad)
                    if 0 <= k < ktaps:
                        part[k, :, lanes] += jnp.sum((dy * sh).reshape(CONV_ROWS // 8, 8, 128), axis=0)
            return carry

        lax.fori_loop(0, tm // CONV_ROWS, chunk, 0)
        dy1 = dy1_s[...]
        da_ref[...] = (dy1 * sg).astype(BF16)
        dg_ref[...] = (dy1 * a * sg * (1.0 - sg)).astype(BF16)

        @pl.when(i == na // tm - 1)
        def _():
            dw_ref[...] = jnp.sum(part[...], axis=1)

    dyp, dyn = _halo_specs(tm, d, na, 0)
    ap, an = _halo_specs(tm, d, na, 0)
    gp, gn = _halo_specs(tm, d, na, 1)
    bfo = jax.ShapeDtypeStruct((na, d), BF16)
    return _pcall(
        body, name=name, grid=(na // tm,),
        in_specs=[_rows(tm, d), dyp, dyn, _rows(tm, d, 0), ap, an, _rows(tm, d, 1), gp, gn, _full((ktaps, d))],
        out_specs=[_rows(tm, d), _rows(tm, d), _full((ktaps, d))],
        out_shape=[bfo, bfo, jax.ShapeDtypeStruct((ktaps, d), F32)],
        scratch_shapes=[pltpu.VMEM((tm + 2 * HALO, d), F32), pltpu.VMEM((tm + 2 * HALO, d), F32),
                        pltpu.VMEM((tm, d), F32), pltpu.VMEM((ktaps, 8, d), F32)],
        compiler_params=_cp("arbitrary"))(dy2, dy2, dy2, p, p, p, p, p, p, conv_w)


def _inproj_bwd(segs, dkv, xa, dxn, modv, gpre, wp_t, n_lat, out_rows, tm, name):
    na, d = xa.shape
    nseg = len(segs)
    wrows = wp_t.shape[0]

    def body(*refs):
        seg_refs = refs[:nseg]
        dkv_ref, x_ref, dxn_ref, mod_ref, g_ref, wt_hbm, dx_ref, acc_ref, wt, sem = refs[nseg:]
        i = pl.program_id(0)
        ctx_tile = i >= n_lat // tm

        @pl.when(i == 0)
        def _():
            cp = pltpu.make_async_copy(wt_hbm, wt, sem)
            cp.start()
            cp.wait()
            acc_ref[...] = jnp.zeros_like(acc_ref)

        dh = _dot(dkv_ref[...], wt[nseg * d:wrows, :])
        for s in range(nseg):
            dh = dh + _dot(seg_refs[s][...], wt[s * d:(s + 1) * d, :])
        x = x_ref[...]
        r = lax.rsqrt(_mean(x * x) + EPS)
        xn = x * r
        g = g_ref[...]
        sc1 = 1.0 + _seg_rows(mod_ref, d, 2 * d, ctx_tile)
        dsh = _colsum(dh)
        dsc = _colsum(dh * (xn * g))

        @pl.when(ctx_tile)
        def _():
            acc_ref[1:2, :] += dsh
            acc_ref[3:4, :] += dsc

        @pl.when(jnp.logical_not(ctx_tile))
        def _():
            acc_ref[0:1, :] += dsh
            acc_ref[2:3, :] += dsc

        acc_ref[4:5, :] += _colsum(dh * xn * sc1)
        dxh = dh * g * sc1

        @pl.when(i < out_tiles)
        def _():
            dx_ref[...] = dxn_ref[...] + r * (dxh - xn * _mean(dxh * xn))

    r_ = _rows(tm, d)
    out_tiles = out_rows // tm
    return _pcall(
        body, name=name, grid=(na // tm,),
        in_specs=[r_] * nseg + [_rows(tm, d // 2), r_, r_, _full((8, 3 * d)), _full((1, d)), ANY],
        out_specs=[pl.BlockSpec((tm, d), lambda i: (jnp.minimum(i, out_tiles - 1), 0)), _full((8, d))],
        out_shape=[jax.ShapeDtypeStruct((out_rows, d), F32), jax.ShapeDtypeStruct((8, d), F32)],
        scratch_shapes=[pltpu.VMEM(wp_t.shape, BF16), pltpu.SemaphoreType.DMA],
        compiler_params=_cp("arbitrary"))(*segs, dkv, xa, dxn, modv, gpre, wp_t)


def _grad_matmul(a, b, tk, name):
    na, ka = a.shape
    nb = b.shape[1]
    tn = min(nb, 1024)

    def body(a_ref, b_ref, o_ref):
        @pl.when(pl.program_id(1) == 0)
        def _():
            o_ref[...] = jnp.zeros_like(o_ref)

        o_ref[...] += _dot_tn(a_ref[...], b_ref[...])

    return _pcall(
        body, name=name, grid=(nb // tn, na // tk),
        in_specs=[pl.BlockSpec((tk, ka), lambda n, k: (k, 0)), pl.BlockSpec((tk, tn), lambda n, k: (k, n))],
        out_specs=pl.BlockSpec((ka, tn), lambda n, k: (0, n)),
        out_shape=jax.ShapeDtypeStruct((ka, nb), F32),
        compiler_params=_cp("parallel", "arbitrary"))(a, b)


def _pack(parts, cols, row_mult):
    flat = jnp.concatenate([q.astype(F32).reshape(-1) for q in parts])
    rows = -(-flat.shape[0] // (cols * row_mult)) * row_mult
    return jnp.pad(flat, (0, rows * cols - flat.shape[0])).reshape(1, rows, cols)


def _unpack(flat, shapes):
    out, off = [], 0
    for s in shapes:
        n = math.prod(s)
        out.append(flat[off:off + n].reshape(tuple(s)))
        off += n
    return out


def _cols_by_dest(g):
    l, a, w8 = g.shape
    return g.reshape(l, a, N_DEV, w8 // N_DEV).transpose(2, 0, 1, 3)


def _rows_by_dest(g):
    l, r8, b = g.shape
    return g.reshape(l, N_DEV, r8 // N_DEV, b).transpose(1, 0, 2, 3)


def _cols_from_src(s):
    n, l, a, w = s.shape
    return s.transpose(1, 2, 0, 3).reshape(l, a, n * w)


def _rows_from_src(s):
    n, l, r, b = s.shape
    return s.transpose(1, 0, 2, 3).reshape(l, n * r, b)


def _rope_tables(n_lat, n_ctx):
    half = HEAD_DIM // 2
    rows = n_lat // GRID_W
    row = jnp.repeat(jnp.arange(rows, dtype=F32), GRID_W)
    col = jnp.tile(jnp.arange(GRID_W, dtype=F32), rows)
    inv_freq = ROPE_THETA ** (-jnp.arange(0, half, 2, dtype=F32) / half)
    ang = jnp.concatenate([row[:, None] * inv_freq, col[:, None] * inv_freq], axis=-1)
    cos, sin = jnp.cos(ang), jnp.sin(ang)
    cos_t = jnp.concatenate([jnp.tile(cos, (1, 4)), jnp.ones((n_ctx, 128), F32)], axis=0)
    sin_t = jnp.concatenate([jnp.tile(jnp.concatenate([-sin, sin], axis=-1), (1, 2)),
                             jnp.zeros((n_ctx, 128), F32)], axis=0)
    return cos_t, sin_t


def _to_heads(t):
    na, w = t.shape
    return t.reshape(na, w // HEAD_DIM, HEAD_DIM).transpose(1, 0, 2)


def _from_heads(t):
    nh, na, hd = t.shape
    return t.transpose(1, 0, 2).reshape(na, nh * hd)


def kernel(x, c, ctx, c_ctx, w_mod, b_mod, g_pre, g_post, w_in, conv_w, conv_b, ln_g, ln_b, w_conv_out, q_norm_g, k_norm_g, w_attn_out, w_out, loss_target, m_c_ctx, m_w_mod, m_b_mod, m_g_pre, m_g_post, m_w_in, m_conv_w, m_conv_b, m_ln_g, m_ln_b, m_w_conv_out, m_q_norm_g, m_k_norm_g, m_w_attn_out, m_w_out, v_c_ctx, v_w_mod, v_b_mod, v_g_pre, v_g_post, v_w_in, v_conv_w, v_conv_b, v_ln_g, v_ln_b, v_w_conv_out, v_q_norm_g, v_k_norm_g, v_w_attn_out, v_w_out):
    depth, d, _ = w_mod.shape
    n_lat, n_ctx = x.shape[1], ctx.shape[1]
    na = n_lat + n_ctx
    heads = d // HEAD_DIM
    kw = d // GROUP
    ktaps = conv_w.shape[1]
    tm = n_ctx
    tm_half = tm // 2
    tbig = 3 * tm if na % (3 * tm) == 0 else tm
    tq_lat = 4 * tm if n_lat % (4 * tm) == 0 else tm
    tq_bwd = 4 * tm if n_lat % (4 * tm) == 0 else tm

    s_mod, s_in, s_co, s_ao, s_oo, s_cw = _all_gather(
        [w_mod.astype(BF16), w_in.astype(BF16), w_conv_out.astype(BF16), w_attn_out.astype(BF16),
         w_out.astype(BF16), conv_w], "gather_weights")
    wmod_f = _cols_from_src(s_mod)
    win_f = _cols_from_src(s_in)
    wc_f, wa_f, wo_f = _rows_from_src(s_co), _rows_from_src(s_ao), _rows_from_src(s_oo)
    wp_f = jnp.concatenate([win_f[:, :, :4 * d], win_f[:, :, 4 * d + 2 * kw:], win_f[:, :, 4 * d:4 * d + 2 * kw]], axis=2)
    convw_f = _cols_from_src(s_cw)

    cos_t, sin_t = _rope_tables(n_lat, n_ctx)
    lane = jnp.arange(d)
    bd = (lane[:, None] // HEAD_DIM == lane[None, :] // HEAD_DIM).astype(BF16)
    esel = (lane[:, None] // HEAD_DIM == jnp.arange(128)[None, :]).astype(BF16)
    cvec = jnp.zeros((8, d), F32).at[0].set(c[0]).at[1].set(c_ctx)
    cvec_t = jnp.zeros((d, 128), F32).at[:, 0].set(c[0]).at[:, 1].set(c_ctx)

    xa = jnp.concatenate([x[0], ctx[0]], axis=0)

    saved = []
    for l in range(depth):
        tag = f"_l{l}"
        gq = jnp.tile(q_norm_g[l], heads)[None, :]
        gk = jnp.tile(k_norm_g[l], heads // GROUP)[None, :]
        modv = _mod_fwd(cvec, wmod_f[l], b_mod[l][None, :], "mod_fwd" + tag)
        p, hb = _inproj(xa, modv, g_pre[l][None, :], wp_f[l], n_lat, tbig, "inproj" + tag)
        qr, kr, vb = _qknorm_fwd(p, cos_t, sin_t, gq, gk, bd, tm, "qknorm_fwd" + tag)
        q_hm, k_hm, v_hm = _to_heads(qr), _to_heads(kr), _to_heads(vb)
        vt_ones = jnp.concatenate([v_hm.transpose(0, 2, 1), jnp.ones((heads // GROUP, 16, na), BF16)], axis=1)
        ot_lat, lse_lat = _attn_fwd(q_hm, k_hm, vt_ones, n_lat, tq_lat, tbig, tm, "attn_fwd" + tag)
        ot_ctx, lse_ctx = _attn_fwd(q_hm[:, n_lat:], k_hm[:, n_lat:], vt_ones[:, :, n_lat:], n_ctx, tm, tm, tm,
                                    "attn_ctx_fwd" + tag)
        lse = jnp.concatenate([lse_lat, lse_ctx], axis=2)
        o_tm = jnp.concatenate([ot_lat, ot_ctx], axis=2).reshape(d, na).T
        y5, y2 = _conv_fwd(p, convw_f[l], conv_b[l][None, :], ln_g[l][None, :], ln_b[l][None, :], n_lat, tm,
                           "conv_fwd" + tag)
        xa_new, yc, ya, out, z, og = _merge_fwd(y5, o_tm, p, xa, modv, g_post[l][None, :], wc_f[l], wa_f[l], wo_f[l],
                                                n_lat, tm, "merge_fwd" + tag)
        saved.append(dict(xa=xa, modv=modv, p=p, hb=hb, q_hm=q_hm, k_hm=k_hm, v_hm=v_hm, o_tm=o_tm, lse=lse,
                          qt_hm=qr.T.reshape(heads, HEAD_DIM, na), kt_hm=kr.T.reshape(heads // GROUP, HEAD_DIM, na),
                          y5=y5, y2=y2, yc=yc, ya=ya, out=out, z=z, og=og, gq=gq, gk=gk))
        xa = xa_new

    dxa, loss_blk = _loss_grad(xa, loss_target[0], n_lat, tm, "loss_grad")

    g_wmod, g_win, g_convw, g_wc, g_wa, g_wo = [], [], [], [], [], []
    g_bmod, g_gpre, g_gpost, g_convb, g_lng, g_lnb, g_qg, g_kg = [], [], [], [], [], [], [], []
    g_cctx = jnp.zeros((d,), F32)
    for l in reversed(range(depth)):
        tag = f"_l{l}"
        s = saved[l]
        p = s["p"]
        (dgb, dma, dmb, dgta, do_tm, dout, dyc, dya, dy2, dl128, acc_m) = _merge_bwd(
            dxa, s["out"], s["yc"], s["ya"], s["o_tm"], p, s["y2"], s["modv"], g_post[l][None, :],
            ln_g[l][None, :], ln_b[l][None, :], wo_f[l].T, wc_f[l].T, wa_f[l].T, esel, n_lat, tm_half,
            "merge_bwd" + tag)
        dl_r = dl128[:, :heads].T.reshape(heads, 1, na)
        do_hm, dot_hm = _to_heads(do_tm), do_tm.T.reshape(heads, HEAD_DIM, na)
        q_hm, qt_hm, k_hm, kt_hm, v_hm, lse = s["q_hm"], s["qt_hm"], s["k_hm"], s["kt_hm"], s["v_hm"], s["lse"]
        dqt_lat, dkt_hm, dvt_hm = _attn_bwd(q_hm, qt_hm, k_hm, kt_hm, v_hm, do_hm, dot_hm, lse, dl_r,
                                            n_lat, tq_bwd, tm, tm, "attn_bwd" + tag)
        dqt_ctx, dkt_ctx, dvt_ctx = _attn_bwd(
            q_hm[:, n_lat:], qt_hm[:, :, n_lat:], k_hm[:, n_lat:], kt_hm[:, :, n_lat:], v_hm[:, n_lat:],
            do_hm[:, n_lat:], dot_hm[:, :, n_lat:], lse[:, :, n_lat:], dl_r[:, :, n_lat:],
            n_ctx, tm, tm, tm, "attn_ctx_bwd" + tag)
        dkt_hm = dkt_hm.at[:, :, n_lat:].add(dkt_ctx)
        dvt_hm = dvt_hm.at[:, :, n_lat:].add(dvt_ctx)
        dq_tm = jnp.concatenate([dqt_lat, dqt_ctx], axis=2).reshape(d, na).T
        dq, dkv, acc_q = _qknorm_bwd(dq_tm, dkt_hm.reshape(kw, na).T, dvt_hm.reshape(kw, na).T, p, cos_t, sin_t,
                                     s["gq"], s["gk"], bd, tm, "qknorm_bwd" + tag)
        da, dg, dconvw = _conv_bwd(dy2, p, convw_f[l], n_lat, tm, "conv_bwd" + tag)
        segs = [da, dg, dgta, dq, dgb, dma, dmb]
        dxa, acc_h = _inproj_bwd(segs, dkv, s["xa"], dxa, s["modv"], g_pre[l][None, :], wp_f[l].T, n_lat,
                                 na if l else n_lat, tm, "inproj_bwd" + tag)

        dwp = [_grad_matmul(s["hb"], sg, tbig, f"grad_w_in{k}" + tag) for k, sg in enumerate(segs + [dkv])]
        g_win.append(jnp.concatenate(dwp[:4] + [dwp[7]] + dwp[4:7], axis=1))
        g_wc.append(_grad_matmul(s["y5"], dyc, tbig, "grad_w_conv_out" + tag))
        g_wa.append(_grad_matmul(s["og"], dya, tbig, "grad_w_attn_out" + tag))
        g_wo.append(_grad_matmul(s["z"], dout, tbig, "grad_w_out" + tag))
        g_convw.append(dconvw)

        dmod = jnp.zeros((8, 3 * d), F32)
        dmod = dmod.at[0].set(jnp.concatenate([acc_h[0], acc_h[2], acc_m[0]]))
        dmod = dmod.at[1].set(jnp.concatenate([acc_h[1], acc_h[3], acc_m[1]]))
        dwm, dbm, dcv = _mod_bwd(dmod, cvec, cvec_t, wmod_f[l], "mod_bwd" + tag)
        g_wmod.append(dwm)
        g_bmod.append(dbm[0])
        g_cctx = g_cctx + dcv[1]
        g_gpre.append(acc_h[4])
        g_gpost.append(acc_m[2])
        g_lng.append(acc_m[3])
        g_lnb.append(acc_m[4])
        g_convb.append(acc_m[5])
        g_qg.append(acc_q[0].reshape(heads, HEAD_DIM).sum(0))
        g_kg.append(acc_q[1, :kw].reshape(heads // GROUP, HEAD_DIM).sum(0))

    grad_x = dxa[None]

    def stack(lst):
        return jnp.stack(lst[::-1])

    big_names = ["w_mod", "w_in", "w_conv_out", "w_attn_out", "w_out", "conv_w"]
    big_w = dict(w_mod=w_mod, w_in=w_in, w_conv_out=w_conv_out, w_attn_out=w_attn_out, w_out=w_out, conv_w=conv_w)
    big_m = dict(w_mod=m_w_mod, w_in=m_w_in, w_conv_out=m_w_conv_out, w_attn_out=m_w_attn_out, w_out=m_w_out,
                 conv_w=m_conv_w)
    big_v = dict(w_mod=v_w_mod, w_in=v_w_in, w_conv_out=v_w_conv_out, w_attn_out=v_w_attn_out, w_out=v_w_out,
                 conv_w=v_conv_w)
    by_dest = [_cols_by_dest(stack(g_wmod)), _cols_by_dest(stack(g_win)), _rows_by_dest(stack(g_wc)),
               _rows_by_dest(stack(g_wa)), _rows_by_dest(stack(g_wo)), _cols_by_dest(stack(g_convw))]
    send = [q.reshape((4, 2) + q.shape[1:]) for q in by_dest]
    from_sibling = _swap_with_sibling(send, "reduce_sibling")
    chip_part = [_add_sibling_part(sb, rc, BF16, "reduce_sibling_add_" + n)
                 for n, sb, rc in zip(big_names, send, from_sibling)]
    from_chips = _exchange_chips(chip_part, "reduce_chips")
    big_g, big_d, big_nm, big_nv = {}, {}, {}, {}
    for n, st in zip(big_names, from_chips):
        big_g[n], big_d[n], big_nm[n], big_nv[n] = _sum_adamw(st, big_w[n], big_m[n], big_v[n], "adamw_" + n)

    small_names = ["c_ctx", "b_mod", "g_pre", "g_post", "conv_b", "ln_g", "ln_b", "q_norm_g", "k_norm_g", "loss"]
    zero1 = jnp.zeros((1,), F32)
    small_w = dict(c_ctx=c_ctx, b_mod=b_mod, g_pre=g_pre, g_post=g_post, conv_b=conv_b, ln_g=ln_g, ln_b=ln_b,
                   q_norm_g=q_norm_g, k_norm_g=k_norm_g, loss=zero1)
    small_m = dict(c_ctx=m_c_ctx, b_mod=m_b_mod, g_pre=m_g_pre, g_post=m_g_post, conv_b=m_conv_b, ln_g=m_ln_g,
                   ln_b=m_ln_b, q_norm_g=m_q_norm_g, k_norm_g=m_k_norm_g, loss=zero1)
    small_v = dict(c_ctx=v_c_ctx, b_mod=v_b_mod, g_pre=v_g_pre, g_post=v_g_post, conv_b=v_conv_b, ln_g=v_ln_g,
                   ln_b=v_ln_b, q_norm_g=v_q_norm_g, k_norm_g=v_k_norm_g, loss=zero1)
    small_g = dict(c_ctx=g_cctx, b_mod=stack(g_bmod), g_pre=stack(g_gpre), g_post=stack(g_gpost),
                   conv_b=stack(g_convb), ln_g=stack(g_lng), ln_b=stack(g_lnb), q_norm_g=stack(g_qg),
                   k_norm_g=stack(g_kg), loss=loss_blk[0, 0:1])
    small_shapes = [small_w[n].shape for n in small_names]

    def pack_small(tree):
        return _pack([tree[n] for n in small_names], d, 8)

    small_parts, = _all_gather([pack_small(small_g)], "gather_small_grads")
    small_out = _sum_adamw(small_parts, pack_small(small_w), pack_small(small_m), pack_small(small_v),
                           "adamw_replicated")
    sm_g, sm_d, sm_nm, sm_nv = [dict(zip(small_names, _unpack(o.reshape(-1), small_shapes))) for o in small_out]
    loss = sm_g["loss"].reshape(())

    order = ["c_ctx", "w_mod", "b_mod", "g_pre", "g_post", "w_in", "conv_w", "conv_b", "ln_g", "ln_b",
             "w_conv_out", "q_norm_g", "k_norm_g", "w_attn_out", "w_out"]

    def pick(big, small):
        return [big[n] if n in big else small[n] for n in order]

    return (loss, grad_x, *pick(big_g, sm_g), *pick(big_d, sm_d), *pick(big_nm, sm_nm), *pick(big_nv, sm_nv))
```

```python
import math

import jax
import jax.numpy as jnp
from jax import lax
from jax.experimental import pallas as pl
from jax.experimental.pallas import tpu as pltpu

F32 = jnp.float32
BF16 = jnp.bfloat16

HEAD_DIM = 64
GROUP = 4
GRID_W = 64
ROPE_THETA = 10000.0
EPS = 1e-6
ATTN_SCALE = HEAD_DIM ** -0.5
LOG2E = 1.4426950408889634
Q_PRESCALE = ATTN_SCALE * LOG2E
HALO = 16
CONV_ROWS = 64
ATTN_FWD_AHEAD = 16
ATTN_BWD_AHEAD = 3

ADAM_LR = 0.001
ADAM_B1 = 0.9
ADAM_B2 = 0.999
ADAM_EPS = 1e-08
ADAM_WD = 0.01
ADAM_STEP = 10

N_DEV = 8
MESH_AXES = ("x", "y", "c")
V7X_VMEM_LIMIT = 56 * 1024 * 1024
NEG_BIG = -1e30

MESH = pl.DeviceIdType.MESH
ANY = pl.BlockSpec(memory_space=pl.ANY)


def _pcall(body, **kw):
    return pl.pallas_call(body, **kw)


def _cp(*sem):
    return pltpu.CompilerParams(dimension_semantics=sem, vmem_limit_bytes=V7X_VMEM_LIMIT)


def _sig(x):
    return 0.5 * jnp.tanh(0.5 * x) + 0.5


def _mean(x):
    return jnp.mean(x, axis=-1, keepdims=True)


def _colsum(x):
    return jnp.sum(x, axis=0, keepdims=True)


def _bf_round(x):
    return x.astype(BF16).astype(F32)


def _dot(a, b):
    return jnp.dot(a, b, preferred_element_type=F32)


def _dot_nt(a, b):
    return lax.dot_general(a, b, (((1,), (1,)), ((), ())), preferred_element_type=F32)


def _split_dot(x, m):
    hi = x.astype(BF16)
    lo = (x - hi.astype(F32)).astype(BF16)
    return _dot(hi, m) + _dot(lo, m)


def _full(shape):
    nd = len(shape)
    return pl.BlockSpec(shape, lambda *_: (0,) * nd)


def _rows(tm, width, colblk=0):
    return pl.BlockSpec((tm, width), lambda i: (i, colblk))


def _cols(height, tm):
    return pl.BlockSpec((height, tm), lambda i: (0, i))


def _split_cols(height, tm, n_lat):
    nl = n_lat // tm
    return (pl.BlockSpec((height, tm), lambda i: (0, jnp.minimum(i, nl - 1))),
            pl.BlockSpec((height, tm), lambda i: (0, jnp.maximum(i - nl, 0))))


def _my_place():
    return lax.axis_index("x"), lax.axis_index("y"), lax.axis_index("c")


def _sem_arrays(n):
    return [pltpu.SemaphoreType.DMA((n,)), pltpu.SemaphoreType.DMA((n,))]


def _all_gather(shards, name):
    n = len(shards)

    def body(*refs):
        x_refs, out_refs = refs[:n], refs[n:2 * n]
        send_sems, recv_sems, local_sems = refs[2 * n:]
        x, y, c = _my_place()
        me, sibling = (x, y, c), (x, y, 1 - c)
        chips = [(1 - x, y), (x, 1 - y), (1 - x, 1 - y)]

        def slab(a, px, py, pc):
            return out_refs[a].at[4 * px + 2 * py + pc]

        def copies(k, block, to, from_input=False):
            return [pltpu.make_async_remote_copy(
                src_ref=x_refs[a] if from_input else slab(a, *block), dst_ref=slab(a, *block),
                send_sem=send_sems.at[k * n + a], recv_sem=recv_sems.at[k * n + a],
                device_id=to, device_id_type=MESH) for a in range(n)]

        mine = [pltpu.make_async_copy(x_refs[a], slab(a, *me), local_sems.at[a]) for a in range(n)]
        for cp in mine:
            cp.start()
        first = copies(0, me, sibling, True)
        for j, chip in enumerate(chips):
            first += copies(1 + j, me, (*chip, c), True)
        for cp in first:
            cp.start()
        passed = []
        for j, chip in enumerate(chips):
            for cp in copies(1 + j, (*chip, c), me):
                cp.wait_recv()
            onward = copies(4 + j, (*chip, c), sibling)
            for cp in onward:
                cp.start()
            passed += onward
        for cp in copies(0, sibling, me):
            cp.wait_recv()
        for j, chip in enumerate(chips):
            for cp in copies(4 + j, (*chip, 1 - c), me):
                cp.wait_recv()
        for cp in first + passed:
            cp.wait_send()
        for cp in mine:
            cp.wait()

    return _pcall(
        body, name=name,
        out_shape=[jax.ShapeDtypeStruct((N_DEV,) + q.shape, q.dtype) for q in shards],
        in_specs=[ANY] * n, out_specs=[ANY] * n,
        scratch_shapes=_sem_arrays(7 * n) + [pltpu.SemaphoreType.DMA((n,))],
    )(*shards)


def _swap_with_sibling(bufs, name):
    n = len(bufs)

    def body(*refs):
        buf_refs, recv_refs = refs[:n], refs[n:2 * n]
        send_sems, recv_sems = refs[2 * n:]
        x, y, c = _my_place()
        copies = [
            pltpu.make_async_remote_copy(
                src_ref=buf_refs[a].at[k, 1 - c], dst_ref=recv_refs[a].at[k],
                send_sem=send_sems.at[k * n + a], recv_sem=recv_sems.at[k * n + a],
                device_id=(x, y, 1 - c), device_id_type=MESH)
            for k in range(4) for a in range(n)]
        for cp in copies:
            cp.start()
        for cp in copies:
            cp.wait()

    return _pcall(
        body, name=name,
        out_shape=[jax.ShapeDtypeStruct((4,) + q.shape[2:], q.dtype) for q in bufs],
        in_specs=[ANY] * n, out_specs=[ANY] * n,
        scratch_shapes=_sem_arrays(4 * n),
    )(*bufs)


def _exchange_chips(parts, name):
    n = len(parts)

    def body(*refs):
        s_refs, recv_refs = refs[:n], refs[n:2 * n]
        send_sems, recv_sems, local_sems = refs[2 * n:]
        x, y, c = _my_place()
        mychip = 2 * x + y
        chips = [(1 - x, y), (x, 1 - y), (1 - x, 1 - y)]
        mine = [pltpu.make_async_copy(s_refs[a].at[mychip], recv_refs[a].at[mychip], local_sems.at[a])
                for a in range(n)]
        for cp in mine:
            cp.start()
        copies = [
            pltpu.make_async_remote_copy(
                src_ref=s_refs[a].at[2 * px + py], dst_ref=recv_refs[a].at[mychip],
                send_sem=send_sems.at[j * n + a], recv_sem=recv_sems.at[j * n + a],
                device_id=(px, py, c), device_id_type=MESH)
            for j, (px, py) in enumerate(chips) for a in range(n)]
        for cp in copies:
            cp.start()
        for cp in copies:
            cp.wait()
        for cp in mine:
            cp.wait()

    return _pcall(
        body, name=name,
        out_shape=[jax.ShapeDtypeStruct(q.shape, q.dtype) for q in parts],
        in_specs=[ANY] * n, out_specs=[ANY] * n,
        scratch_shapes=_sem_arrays(3 * n) + [pltpu.SemaphoreType.DMA((n,))],
    )(*parts)


def _row_tile(a):
    for t in range(256, 7, -8):
        if a % t == 0:
            return t
    return a


def _add_sibling_part(buf, recv, out_dtype, name):
    _, _, nl, a, b = buf.shape
    ta = _row_tile(a)
    core = lax.axis_index("c").astype(jnp.int32).reshape(1)

    def body(core_ref, a_ref, b_ref, o_ref):
        o_ref[...] = (a_ref[...] + b_ref[...]).astype(o_ref.dtype)

    grid_spec = pltpu.PrefetchScalarGridSpec(
        num_scalar_prefetch=1, grid=(4, nl, a // ta),
        in_specs=[pl.BlockSpec((None, None, None, ta, b), lambda k, l, r, cr: (k, cr[0], l, r, 0)),
                  pl.BlockSpec((None, None, ta, b), lambda k, l, r, cr: (k, l, r, 0))],
        out_specs=pl.BlockSpec((None, None, ta, b), lambda k, l, r, cr: (k, l, r, 0)))
    return _pcall(body, name=name, grid_spec=grid_spec,
                  out_shape=jax.ShapeDtypeStruct((4, nl, a, b), out_dtype),
                  compiler_params=_cp("parallel", "parallel", "parallel"))(core, buf, recv)


def _sum_adamw(stack, w, m, v, name):
    ns, nl, a, b = stack.shape
    ta = _row_tile(a)
    c1 = 1.0 - ADAM_B1 ** ADAM_STEP
    c2 = 1.0 - ADAM_B2 ** ADAM_STEP

    def body(s_ref, w_ref, m_ref, v_ref, g_out, d_out, m_out, v_out):
        g = s_ref[0].astype(F32)
        for k in range(1, ns):
            g = g + s_ref[k].astype(F32)
        m_new = ADAM_B1 * m_ref[...] + (1.0 - ADAM_B1) * g
        v_new = ADAM_B2 * v_ref[...] + (1.0 - ADAM_B2) * (g * g)
        m_hat = m_new / c1
        v_hat = v_new / c2
        g_out[...] = g
        d_out[...] = -ADAM_LR * (m_hat / (jnp.sqrt(v_hat) + ADAM_EPS) + ADAM_WD * w_ref[...])
        m_out[...] = m_new
        v_out[...] = v_new

    blk = pl.BlockSpec((None, ta, b), lambda l, i: (l, i, 0))
    return _pcall(
        body, name=name, grid=(nl, a // ta),
        in_specs=[pl.BlockSpec((ns, None, ta, b), lambda l, i: (0, l, i, 0)), blk, blk, blk],
        out_specs=[blk] * 4,
        out_shape=[jax.ShapeDtypeStruct((nl, a, b), F32)] * 4,
        compiler_params=_cp("parallel", "parallel"))(stack, w, m, v)


def _mod_fwd(cvec, wmod, bmod, name):
    _, d = cvec.shape

    def body(c_ref, w_ref, b_ref, o_ref):
        cv = c_ref[...]
        cs = cv * _sig(cv)
        o_ref[...] = _dot(cs.astype(BF16), w_ref[...]) + b_ref[...]

    return _pcall(
        body, name=name, grid=(3,),
        in_specs=[_full((8, d)), pl.BlockSpec((d, d), lambda n: (0, n)), pl.BlockSpec((1, d), lambda n: (0, n))],
        out_specs=pl.BlockSpec((8, d), lambda n: (0, n)),
        out_shape=jax.ShapeDtypeStruct((8, 3 * d), F32),
        compiler_params=_cp("parallel"))(cvec, wmod, bmod)


def _mod_bwd(dmod, cvec, cvec_t, wmod, name):
    _, d = cvec.shape

    def body(dm_ref, c_ref, ct_ref, w_ref, dw_ref, db_ref, dc_ref):
        n = pl.program_id(0)
        dm = dm_ref[...]
        ct = ct_ref[...]
        cs_t = _bf_round(ct * _sig(ct))
        d0 = _bf_round(dm[0:1, :])
        d1 = _bf_round(dm[1:2, :])
        dw_ref[...] = cs_t[:, 0:1] * d0 + cs_t[:, 1:2] * d1
        db_ref[...] = dm[0:1, :] + dm[1:2, :]

        @pl.when(n == 0)
        def _():
            dc_ref[...] = jnp.zeros_like(dc_ref)

        dc_ref[...] += _dot_nt(dm.astype(BF16), w_ref[...])

        @pl.when(n == 2)
        def _():
            cv = c_ref[...]
            s = _sig(cv)
            dc_ref[...] = dc_ref[...] * (s * (1.0 + cv * (1.0 - s)))

    return _pcall(
        body, name=name, grid=(3,),
        in_specs=[pl.BlockSpec((8, d), lambda n: (0, n)), _full((8, d)), _full((d, 128)),
                  pl.BlockSpec((d, d), lambda n: (0, n))],
        out_specs=[pl.BlockSpec((d, d), lambda n: (0, n)), pl.BlockSpec((1, d), lambda n: (0, n)),
                   _full((8, d))],
        out_shape=[jax.ShapeDtypeStruct((d, 3 * d), F32), jax.ShapeDtypeStruct((1, 3 * d), F32),
                   jax.ShapeDtypeStruct((8, d), F32)],
        compiler_params=_cp("arbitrary"))(dmod, cvec, cvec_t, wmod)


def _seg_rows(mod_ref, lo, hi, is_ctx):
    return jnp.where(is_ctx, mod_ref[1:2, lo:hi], mod_ref[0:1, lo:hi])


def _inproj(xa, modv, gpre, wp, n_lat, tm, name):
    na, d = xa.shape
    wcols = wp.shape[1]
    tn = 5 * d // 4 if (5 * d // 4) % 128 == 0 and wcols % (5 * d // 4) == 0 else d // 2

    def body(x_ref, mod_ref, g_ref, w_ref, p_ref, ht_ref, h_s):
        i = pl.program_id(0)

        @pl.when(pl.program_id(1) == 0)
        def _():
            x = x_ref[...]
            r = lax.rsqrt(_mean(x * x) + EPS)
            row = i * tm + lax.broadcasted_iota(jnp.int32, (tm, 1), 0)
            is_ctx = row >= n_lat
            sh = _seg_rows(mod_ref, 0, d, is_ctx)
            sc = _seg_rows(mod_ref, d, 2 * d, is_ctx)
            h = (x * r * g_ref[...]) * (1.0 + sc) + sh
            h_s[...] = h.astype(BF16)
            ht_ref[...] = h.T.astype(BF16)

        p_ref[...] = _dot(h_s[...], w_ref[...])

    return _pcall(
        body, name=name, grid=(na // tm, wcols // tn),
        in_specs=[pl.BlockSpec((tm, d), lambda i, j: (i, 0)), _full((8, 3 * d)), _full((1, d)),
                  pl.BlockSpec((d, tn), lambda i, j: (0, j))],
        out_specs=[pl.BlockSpec((tm, tn), lambda i, j: (i, j)), pl.BlockSpec((d, tm), lambda i, j: (0, i))],
        out_shape=[jax.ShapeDtypeStruct((na, wcols), F32), jax.ShapeDtypeStruct((d, na), BF16)],
        scratch_shapes=[pltpu.VMEM((tm, d), BF16)],
        compiler_params=_cp("parallel", "arbitrary"))(xa, modv, gpre, wp)


def _lane_tile(t, width):
    if width >= 128:
        return jnp.tile(t, (1, width // 128))
    return t[:, :width]


def _partner(x):
    w = x.shape[-1]
    lane = lax.broadcasted_iota(jnp.int32, x.shape, 1)
    low = (lane % HEAD_DIM) < (HEAD_DIM // 2)
    return jnp.where(low, pltpu.roll(x, w - HEAD_DIM // 2, 1), pltpu.roll(x, HEAD_DIM // 2, 1))


def _qknorm_fwd(p, cos_t, sin_t, gq, gk, bd, tm, name):
    na = p.shape[0]
    d = gq.shape[1]
    kw = d // GROUP

    def body(q_ref, kv_ref, cos_ref, sin_ref, gq_ref, gk_ref, bd_ref, qt_ref, ko_ref, vo_ref, kt_ref, vt_ref):
        cos = cos_ref[...]
        sin = sin_ref[...]

        def norm_rope(xh, g, w):
            ms = _split_dot(xh * xh, bd_ref[0:w, 0:w]) * (1.0 / HEAD_DIM)
            xn = xh * lax.rsqrt(ms + EPS) * g
            return xn * _lane_tile(cos, w) + _partner(xn) * _lane_tile(sin, w)

        qt_ref[...] = (norm_rope(q_ref[...], gq_ref[...], d) * Q_PRESCALE).T.astype(BF16)
        kv = kv_ref[...]
        k = norm_rope(kv[:, 0:kw], gk_ref[...], kw)
        v = kv[:, kw:2 * kw]
        ko_ref[...] = k.astype(BF16)
        vo_ref[...] = v.astype(BF16)
        kt_ref[...] = k.T.astype(BF16)
        vt_ref[...] = v.T.astype(BF16)

    cols = lambda w: pl.BlockSpec((w, tm), lambda i: (0, i))
    return _pcall(
        body, name=name, grid=(na // tm,),
        in_specs=[_rows(tm, d, 3), _rows(tm, d // 2, 14), _rows(tm, 128), _rows(tm, 128),
                  _full((1, d)), _full((1, kw)), _full((d, d))],
        out_specs=[cols(d), _rows(tm, kw), _rows(tm, kw), cols(kw), cols(kw)],
        out_shape=[jax.ShapeDtypeStruct((d, na), BF16), jax.ShapeDtypeStruct((na, kw), BF16),
                   jax.ShapeDtypeStruct((na, kw), BF16), jax.ShapeDtypeStruct((kw, na), BF16),
                   jax.ShapeDtypeStruct((kw, na), BF16)],
        compiler_params=_cp("parallel"))(p, p, cos_t, sin_t, gq, gk, bd)


def _attn_fwd(qt_hm, k_hm, vt_hm, na, tq, tk, cw, name):
    h, hd, _ = qt_hm.shape
    kv, nkeys, _ = k_hm.shape
    vrows = vt_hm.shape[1]
    nq, nk = na // tq, nkeys // tk
    nsub = tq // cw
    chains = [(a, u) for a in range(GROUP) for u in range(nsub)]

    def body(q_ref, k_ref, vt_ref, o_ref, lse_ref, m_s, acc_s):
        j = pl.program_id(2)

        @pl.when(j == 0)
        def _():
            m_s[...] = jnp.full_like(m_s, NEG_BIG)
            acc_s[...] = jnp.zeros_like(acc_s)

        k = k_ref[...]
        vt = vt_ref[...]

        def scores(n):
            a, u = chains[n]
            return _dot(k, q_ref[a, :, u * cw:(u + 1) * cw])

        s_tiles = {n: scores(n) for n in range(min(ATTN_FWD_AHEAD, len(chains)))}
        pending = None
        for n, (a, u) in enumerate(chains):
            cols = slice(u * cw, (u + 1) * cw)
            s_t = s_tiles.pop(n)
            m_prev = m_s[a, :, cols]
            m_new = jnp.maximum(m_prev, jnp.max(s_t, axis=0, keepdims=True))
            m_s[a, :, cols] = m_new
            pv = _dot(vt, jnp.exp2(s_t - m_new).astype(BF16))
            if pending is not None:
                pa, pcols, palpha, ppv = pending
                acc_s[pa, :, pcols] = palpha * acc_s[pa, :, pcols] + ppv
            pending = (a, cols, jnp.exp2(m_prev - m_new), pv)
            if n + ATTN_FWD_AHEAD < len(chains):
                s_tiles[n + ATTN_FWD_AHEAD] = scores(n + ATTN_FWD_AHEAD)
        pa, pcols, palpha, ppv = pending
        acc_s[pa, :, pcols] = palpha * acc_s[pa, :, pcols] + ppv

        @pl.when(j == nk - 1)
        def _():
            for a in range(GROUP):
                acc = acc_s[a]
                l = acc[hd:hd + 1, :]
                o_ref[a] = acc[0:hd, :] / l
                lse_ref[a] = m_s[a] + jnp.log2(l)

    return _pcall(
        body, name=name, grid=(kv, nq, nk),
        in_specs=[pl.BlockSpec((GROUP, hd, tq), lambda g, i, j: (g, 0, i)),
                  pl.BlockSpec((None, tk, hd), lambda g, i, j: (g, j, 0)),
                  pl.BlockSpec((None, vrows, tk), lambda g, i, j: (g, 0, j))],
        out_specs=[pl.BlockSpec((GROUP, hd, tq), lambda g, i, j: (g, 0, i)),
                   pl.BlockSpec((GROUP, 1, tq), lambda g, i, j: (g, 0, i))],
        out_shape=[jax.ShapeDtypeStruct((h, hd, na), F32), jax.ShapeDtypeStruct((h, 1, na), F32)],
        scratch_shapes=[pltpu.VMEM((GROUP, 1, tq), F32), pltpu.VMEM((GROUP, vrows, tq), F32)],
        compiler_params=_cp("parallel", "parallel", "arbitrary"))(qt_hm, k_hm, vt_hm)


def _window(win_ref, prev, cur, nxt, first, last, tm):
    win_ref[0:HALO, :] = jnp.where(first, 0.0, prev)
    win_ref[HALO:HALO + tm, :] = cur
    win_ref[HALO + tm:HALO + tm + HALO, :] = jnp.where(last, 0.0, nxt)


def _lane_blocks(d):
    return [slice(b, b + 128) for b in range(0, d, 128)]


def _sublane_shifts(slab):
    n = slab.shape[0]
    for b in range(8):
        sh = slab if b == 0 else pltpu.roll(slab, n - b, 0)
        for a8 in range(0, 2 * HALO, 8):
            yield a8 + b, sh[a8:a8 + CONV_ROWS, :]


def _halo_specs(tm, d, na, colblk):
    per = tm // HALO
    last_blk = na // HALO - 1
    prev = pl.BlockSpec((HALO, d), lambda i: (jnp.maximum(i * per - 1, 0), colblk))
    nxt = pl.BlockSpec((HALO, d), lambda i: (jnp.minimum((i + 1) * per, last_blk), colblk))
    return prev, nxt


def _seq_ends(i, n_lat, na, tm):
    first = jnp.logical_or(i == 0, i == n_lat // tm)
    last = jnp.logical_or(i == n_lat // tm - 1, i == na // tm - 1)
    return first, last


def _conv_fwd(p, conv_w, conv_b, ln_g, ln_b, n_lat, tm, name):
    na = p.shape[0]
    ktaps, d = conv_w.shape
    pad = ktaps // 2

    def body(a_ref, ap_ref, an_ref, g_ref, gp_ref, gn_ref, ga_ref, w_ref, cb_ref, lg_ref, lb_ref,
             y5_ref, y5t_ref, y2_ref, win):
        i = pl.program_id(0)
        first, last = _seq_ends(i, n_lat, na, tm)
        _window(win, ap_ref[...] * _sig(gp_ref[...]), a_ref[...] * _sig(g_ref[...]),
                an_ref[...] * _sig(gn_ref[...]), first, last, tm)

        def chunk(c, carry):
            r = pl.multiple_of(c * CONV_ROWS, CONV_ROWS)
            for lanes in _lane_blocks(d):
                acc = jnp.zeros((CONV_ROWS, 128), F32)
                for off, sh in _sublane_shifts(win[pl.ds(r, CONV_ROWS + 2 * HALO), lanes]):
                    k = off - (HALO - pad)
                    if 0 <= k < ktaps:
                        acc = acc + w_ref[k:k + 1, lanes] * sh
                y2_ref[pl.ds(r, CONV_ROWS), lanes] = acc + cb_ref[:, lanes]
            return carry

        lax.fori_loop(0, tm // CONV_ROWS, chunk, 0)
        y2 = y2_ref[...]
        xc = y2 - _mean(y2)
        y3 = xc * lax.rsqrt(_mean(xc * xc) + EPS) * lg_ref[...] + lb_ref[...]
        gate = ga_ref[...]
        y5 = (y3 * _sig(y3)) * (gate * _sig(gate))
        y5_ref[...] = y5.astype(BF16)
        y5t_ref[...] = y5.T.astype(BF16)

    ap, an = _halo_specs(tm, d, na, 0)
    gp, gn = _halo_specs(tm, d, na, 1)
    return _pcall(
        body, name=name, grid=(na // tm,),
        in_specs=[_rows(tm, d, 0), ap, an, _rows(tm, d, 1), gp, gn, _rows(tm, d, 2),
                  _full((ktaps, d)), _full((1, d)), _full((1, d)), _full((1, d))],
        out_specs=[_rows(tm, d), _cols(d, tm), _rows(tm, d)],
        out_shape=[jax.ShapeDtypeStruct((na, d), BF16), jax.ShapeDtypeStruct((d, na), BF16),
                   jax.ShapeDtypeStruct((na, d), F32)],
        scratch_shapes=[pltpu.VMEM((tm + 2 * HALO, d), F32)],
        compiler_params=_cp("parallel"))(p, p, p, p, p, p, p, conv_w, conv_b, ln_g, ln_b)


def _merge_fwd(y5, ot_lat, ot_ctx, p, xa, modv, gpost, wc, wa, wo, n_lat, tm, name):
    na, d = xa.shape

    def body(y5_ref, ol_ref, oc_ref, gb_ref, ma_ref, mb_ref, x_ref, mod_ref, gp_ref, wc_ref, wa_ref, wo_ref,
             xn_ref, yc_ref, ya_ref, out_ref, zt_ref, ogt_ref):
        is_ctx = pl.program_id(0) >= n_lat // tm
        gate_b = gb_ref[...]
        o = jnp.where(is_ctx, oc_ref[...], ol_ref[...]).T
        og = o * (gate_b * _sig(gate_b))
        ogt_ref[...] = og.T.astype(BF16)
        yc = _dot(y5_ref[...], wc_ref[...])
        ya = _dot(og.astype(BF16), wa_ref[...])
        yc_ref[...] = yc
        ya_ref[...] = ya
        z = _sig(ma_ref[...]) * yc + _sig(mb_ref[...]) * ya
        zt_ref[...] = z.T.astype(BF16)
        out = _dot(z.astype(BF16), wo_ref[...])
        out_ref[...] = out
        gt = _seg_rows(mod_ref, 2 * d, 3 * d, is_ctx)
        xn_ref[...] = x_ref[...] + gt * (out * lax.rsqrt(_mean(out * out) + EPS) * gp_ref[...])

    f32o = jax.ShapeDtypeStruct((na, d), F32)
    bft = jax.ShapeDtypeStruct((d, na), BF16)
    ol, oc = _split_cols(d, tm, n_lat)
    return _pcall(
        body, name=name, grid=(na // tm,),
        in_specs=[_rows(tm, d), ol, oc, _rows(tm, d, 4), _rows(tm, d, 5), _rows(tm, d, 6), _rows(tm, d),
                  _full((8, 3 * d)), _full((1, d)), _full((d, d)), _full((d, d)), _full((d, d))],
        out_specs=[_rows(tm, d)] * 4 + [_cols(d, tm)] * 2,
        out_shape=[f32o, f32o, f32o, f32o, bft, bft],
        compiler_params=_cp("parallel"))(y5, ot_lat, ot_ctx, p, p, p, xa, modv, gpost, wc, wa, wo)


def _loss_grad(xa, target, n_lat, tm, name):
    na, d = xa.shape
    nlt = n_lat // tm

    def body(x_ref, t_ref, dx_ref, loss_ref):
        i = pl.program_id(0)

        @pl.when(i == 0)
        def _():
            loss_ref[...] = jnp.zeros_like(loss_ref)

        @pl.when(i < nlt)
        def _():
            err = x_ref[...] - t_ref[...]
            dx_ref[...] = err * (1.0 / d)
            loss_ref[...] += 0.5 * jnp.sum(_mean(err * err))

        @pl.when(i >= nlt)
        def _():
            dx_ref[...] = jnp.zeros_like(dx_ref)

    return _pcall(
        body, name=name, grid=(na // tm,),
        in_specs=[_rows(tm, d), pl.BlockSpec((tm, d), lambda i: (jnp.minimum(i, nlt - 1), 0))],
        out_specs=[_rows(tm, d), _full((8, 128))],
        out_shape=[jax.ShapeDtypeStruct((na, d), F32), jax.ShapeDtypeStruct((8, 128), F32)],
        compiler_params=_cp("arbitrary"))(xa, target)


def _merge_bwd(dxn, out, yc, ya, ot_lat, ot_ctx, p, y2, modv, gpost, ln_g, ln_b, wo_t, wc_t, wa_t, esel, n_lat, tm,
               name):
    na, d = dxn.shape

    def body(dx_ref, out_ref, yc_ref, ya_ref, ol_ref, oc_ref, gb_ref, ma_ref, mb_ref, gta_ref, y2_ref,
             mod_ref, gp_ref, lg_ref, lb_ref, wot_ref, wct_ref, wat_ref, es_ref,
             dgb_ref, dma_ref, dmb_ref, dgta_ref, dot_ref, dout_ref, dyc_ref, dya_ref, dy2_ref, dlt_ref, acc_ref):
        i = pl.program_id(0)
        ctx_tile = i >= n_lat // tm

        @pl.when(i == 0)
        def _():
            acc_ref[...] = jnp.zeros_like(acc_ref)

        gt = _seg_rows(mod_ref, 2 * d, 3 * d, ctx_tile)
        gp = gp_ref[...]
        dx = dx_ref[...]
        out = out_ref[...]
        r2 = lax.rsqrt(_mean(out * out) + EPS)
        n2 = out * r2
        dgt = _colsum(dx * (n2 * gp))

        @pl.when(ctx_tile)
        def _():
            acc_ref[1:2, :] += dgt

        @pl.when(jnp.logical_not(ctx_tile))
        def _():
            acc_ref[0:1, :] += dgt

        acc_ref[2:3, :] += _colsum(dx * gt * n2)
        dn2 = dx * gt * gp
        dout = (r2 * (dn2 - n2 * _mean(dn2 * n2))).astype(BF16)
        dout_ref[...] = dout
        dz = _dot(dout, wot_ref[...])
        sa = _sig(ma_ref[...])
        sb = _sig(mb_ref[...])
        dyc = (dz * sa).astype(BF16)
        dya = (dz * sb).astype(BF16)
        dyc_ref[...] = dyc
        dya_ref[...] = dya
        dma_ref[...] = (dz * yc_ref[...] * sa * (1.0 - sa)).astype(BF16)
        dmb_ref[...] = (dz * ya_ref[...] * sb * (1.0 - sb)).astype(BF16)
        dy5 = _dot(dyc, wct_ref[...])
        dog = _dot(dya, wat_ref[...])

        gate_b = gb_ref[...]
        sgb = _sig(gate_b)
        o = jnp.where(ctx_tile, oc_ref[...], ol_ref[...]).T
        do = dog * (gate_b * sgb)
        dot_ref[...] = do.T.astype(BF16)
        dgb_ref[...] = (dog * o * (sgb * (1.0 + gate_b * (1.0 - sgb)))).astype(BF16)
        dlt_ref[...] = _split_dot(do * o, es_ref[...]).T

        y2 = y2_ref[...]
        xc = y2 - _mean(y2)
        rstd = lax.rsqrt(_mean(xc * xc) + EPS)
        xhat = xc * rstd
        lg = lg_ref[...]
        y3 = xhat * lg + lb_ref[...]
        s3 = _sig(y3)
        gate_a = gta_ref[...]
        sga = _sig(gate_a)
        dgta_ref[...] = (dy5 * (y3 * s3) * (sga * (1.0 + gate_a * (1.0 - sga)))).astype(BF16)
        dy3 = dy5 * (gate_a * sga) * (s3 * (1.0 + y3 * (1.0 - s3)))
        acc_ref[3:4, :] += _colsum(dy3 * xhat)
        acc_ref[4:5, :] += _colsum(dy3)
        dxh = dy3 * lg
        dy2 = rstd * (dxh - _mean(dxh) - xhat * _mean(dxh * xhat))
        dy2_ref[...] = dy2
        acc_ref[5:6, :] += _colsum(dy2)

    f32o = jax.ShapeDtypeStruct((na, d), F32)
    bfo = jax.ShapeDtypeStruct((na, d), BF16)
    r = _rows(tm, d)
    ol, oc = _split_cols(d, tm, n_lat)
    return _pcall(
        body, name=name, grid=(na // tm,),
        in_specs=[r, r, r, r, ol, oc, _rows(tm, d, 4), _rows(tm, d, 5), _rows(tm, d, 6), _rows(tm, d, 2), r,
                  _full((8, 3 * d)), _full((1, d)), _full((1, d)), _full((1, d)),
                  _full((d, d)), _full((d, d)), _full((d, d)), _full((d, 128))],
        out_specs=[r] * 4 + [_cols(d, tm)] + [r] * 4 + [_cols(128, tm), _full((8, d))],
        out_shape=[bfo] * 4 + [jax.ShapeDtypeStruct((d, na), BF16)] + [bfo] * 3
        + [f32o, jax.ShapeDtypeStruct((128, na), F32), jax.ShapeDtypeStruct((8, d), F32)],
        compiler_params=_cp("arbitrary"),
    )(dxn, out, yc, ya, ot_lat, ot_ctx, p, p, p, p, y2, modv, gpost, ln_g, ln_b, wo_t, wc_t, wa_t, esel)


def _attn_bwd(qt_hm, k_hm, kt_hm, v_hm, dot_hm, lse_r, dl_r, na, tq, tk, cw, name):
    h, hd, _ = qt_hm.shape
    kv, nkeys, _ = k_hm.shape
    nq, nk = na // tq, nkeys // tk
    nsub = tq // cw
    chains = [(a, u) for a in range(GROUP) for u in range(nsub)]

    def body(qt_ref, k_ref, kt_ref, v_ref, dot_ref, lse_ref, dl_ref, dq_hbm, dk_ref, dv_ref,
             dq_acc, dk_acc, dv_acc, sem):
        g = pl.program_id(0)
        j = pl.program_id(1)
        i = pl.program_id(2)

        @pl.when(jnp.logical_and(j == 0, i == 0))
        def _():
            dq_acc[...] = jnp.zeros_like(dq_acc)

        @pl.when(i == 0)
        def _():
            dk_acc[...] = jnp.zeros_like(dk_acc)
            dv_acc[...] = jnp.zeros_like(dv_acc)

        k = k_ref[...]
        v = v_ref[...]
        kt = kt_ref[...]

        def products(n):
            a, u = chains[n]
            cols = slice(u * cw, (u + 1) * cw)
            return _dot(k, qt_ref[a, :, cols]), _dot(v, dot_ref[a, :, cols])

        def accumulate(done):
            a, u, dv_t, dk_t, dq_t = done
            dv_acc[...] += dv_t
            dk_acc[...] += dk_t
            at = pl.multiple_of(i * tq + u * cw, cw)
            dq_acc[a, :, pl.ds(at, cw)] += dq_t

        tiles = {n: products(n) for n in range(min(ATTN_BWD_AHEAD, len(chains)))}
        pending = None
        for n, (a, u) in enumerate(chains):
            cols = slice(u * cw, (u + 1) * cw)
            s_t, dp_t = tiles.pop(n)
            p_t = jnp.exp2(s_t - lse_ref[a, :, cols])
            ds_b = (p_t * (dp_t - dl_ref[a, :, cols])).astype(BF16)
            p_b = p_t.astype(BF16)
            dv_t = _dot_nt(dot_ref[a, :, cols], p_b)
            dk_t = _dot_nt(qt_ref[a, :, cols], ds_b)
            dq_t = _dot(kt, ds_b)
            if pending is not None:
                accumulate(pending)
            pending = (a, u, dv_t, dk_t, dq_t)
            if n + ATTN_BWD_AHEAD < len(chains):
                tiles[n + ATTN_BWD_AHEAD] = products(n + ATTN_BWD_AHEAD)
        accumulate(pending)

        @pl.when(i == nq - 1)
        def _():
            dk_ref[...] = dk_acc[...]
            dv_ref[...] = dv_acc[...]

        @pl.when(jnp.logical_and(j == nk - 1, i == nq - 1))
        def _():
            cp = pltpu.make_async_copy(dq_acc, dq_hbm.at[pl.ds(g * GROUP, GROUP)], sem)
            cp.start()
            cp.wait()

    qtspec = pl.BlockSpec((GROUP, hd, tq), lambda g, j, i: (g, 0, i))
    kspec = pl.BlockSpec((None, tk, hd), lambda g, j, i: (g, j, 0))
    ktspec = pl.BlockSpec((None, hd, tk), lambda g, j, i: (g, 0, j))
    rspec = pl.BlockSpec((GROUP, 1, tq), lambda g, j, i: (g, 0, i))
    return _pcall(
        body, name=name, grid=(kv, nk, nq),
        in_specs=[qtspec, kspec, ktspec, kspec, qtspec, rspec, rspec],
        out_specs=[ANY, ktspec, ktspec],
        out_shape=[jax.ShapeDtypeStruct((h, hd, na), F32), jax.ShapeDtypeStruct((kv, hd, nkeys), F32),
                   jax.ShapeDtypeStruct((kv, hd, nkeys), F32)],
        scratch_shapes=[pltpu.VMEM((GROUP, hd, na), F32), pltpu.VMEM((hd, tk), F32), pltpu.VMEM((hd, tk), F32),
                        pltpu.SemaphoreType.DMA],
        compiler_params=_cp("arbitrary", "arbitrary", "arbitrary"),
    )(qt_hm, k_hm, kt_hm, v_hm, dot_hm, lse_r, dl_r)


def _qknorm_bwd(dqt_lat, dqt_ctx, dkt, dkt_ctx, dvt, dvt_ctx, p, cos_t, sin_t, gq, gk, bd, n_lat, tm, name):
    na = p.shape[0]
    d = gq.shape[1]
    kw = d // GROUP

    def body(dql_ref, dqc_ref, dkl_ref, dkc_ref, dvl_ref, dvc_ref, q_ref, kv_ref, cos_ref, sin_ref, gq_ref, gk_ref,
             bd_ref, dqo_ref, dkvo_ref, acc_ref):
        is_ctx = pl.program_id(0) >= n_lat // tm

        @pl.when(pl.program_id(0) == 0)
        def _():
            acc_ref[...] = jnp.zeros_like(acc_ref)

        dq_in = jnp.where(is_ctx, dqc_ref[...], dql_ref[...]).T
        dk_in = (dkl_ref[...] + jnp.where(is_ctx, dkc_ref[...], 0.0)).T
        dv_in = (dvl_ref[...] + jnp.where(is_ctx, dvc_ref[...], 0.0)).T

        cos = cos_ref[...]
        sin = sin_ref[...]

        def back(dy, xh, g, w):
            bdw = bd_ref[0:w, 0:w]
            dn = dy * _lane_tile(cos, w) - _partner(dy) * _lane_tile(sin, w)
            rs = lax.rsqrt(_split_dot(xh * xh, bdw) * (1.0 / HEAD_DIM) + EPS)
            y = xh * rs
            dg = _colsum(dn * y)
            dyn = dn * g
            dx = rs * (dyn - y * (_split_dot(dyn * y, bdw) * (1.0 / HEAD_DIM)))
            return dx, dg

        dq, dgq = back(dq_in * ATTN_SCALE, q_ref[...], gq_ref[...], d)
        dqo_ref[...] = dq.astype(BF16)
        acc_ref[0:1, :] += dgq
        kv = kv_ref[...]
        dk, dgk = back(dk_in * (1.0 / LOG2E), kv[:, 0:kw], gk_ref[...], kw)
        acc_ref[1:2, 0:kw] += dgk
        dkvo_ref[:, 0:kw] = dk.astype(BF16)
        dkvo_ref[:, kw:2 * kw] = dv_in.astype(BF16)

    dql, dqc = _split_cols(d, tm, n_lat)
    _, kvc = _split_cols(kw, tm, n_lat)
    return _pcall(
        body, name=name, grid=(na // tm,),
        in_specs=[dql, dqc, _cols(kw, tm), kvc, _cols(kw, tm), kvc, _rows(tm, d, 3), _rows(tm, d // 2, 14),
                  _rows(tm, 128), _rows(tm, 128), _full((1, d)), _full((1, kw)), _full((d, d))],
        out_specs=[_rows(tm, d), _rows(tm, d // 2), _full((8, d))],
        out_shape=[jax.ShapeDtypeStruct((na, d), BF16), jax.ShapeDtypeStruct((na, d // 2), BF16),
                   jax.ShapeDtypeStruct((8, d), F32)],
        compiler_params=_cp("arbitrary"))(dqt_lat, dqt_ctx, dkt, dkt_ctx, dvt, dvt_ctx, p, p, cos_t, sin_t, gq, gk, bd)


def _conv_bwd(dy2, p, conv_w, n_lat, tm, name):
    na = p.shape[0]
    ktaps, d = conv_w.shape
    pad = ktaps // 2

    def body(dy_ref, dyp_ref, dyn_ref, a_ref, ap_ref, an_ref, g_ref, gp_ref, gn_ref, w_ref,
             da_ref, dg_ref, dw_ref, dwin, ywin, dy1_s, part):
        i = pl.program_id(0)

        @pl.when(i == 0)
        def _():
            part[...] = jnp.zeros_like(part)

        first, last = _seq_ends(i, n_lat, na, tm)
        a = a_ref[...]
        sg = _sig(g_ref[...])
        _window(dwin, dyp_ref[...], dy_ref[...], dyn_ref[...], first, last, tm)
        _window(ywin, ap_ref[...] * _sig(gp_ref[...]), a * sg, an_ref[...] * _sig(gn_ref[...]), first, last, tm)

        def chunk(c, carry):
            r = pl.multiple_of(c * CONV_ROWS, CONV_ROWS)
            for lanes in _lane_blocks(d):
                dy = dy_ref[pl.ds(r, CONV_ROWS), lanes]
                acc = jnp.zeros((CONV_ROWS, 128), F32)
                for off, sh in _sublane_shifts(dwin[pl.ds(r, CONV_ROWS + 2 * HALO), lanes]):
                    k = HALO + pad - off
                    if 0 <= k < ktaps:
                        acc = acc + w_ref[k:k + 1, lanes] * sh
                dy1_s[pl.ds(r, CONV_ROWS), lanes] = acc
                for off, sh in _sublane_shifts(ywin[pl.ds(r, CONV_ROWS + 2 * HALO), lanes]):
                    k = off - (HALO - pad)
                    if 0 <= k < ktaps:
                        part[k, :, lanes] += jnp.sum((dy * sh).reshape(CONV_ROWS // 8, 8, 128), axis=0)
            return carry

        lax.fori_loop(0, tm // CONV_ROWS, chunk, 0)
        dy1 = dy1_s[...]
        da_ref[...] = (dy1 * sg).astype(BF16)
        dg_ref[...] = (dy1 * a * sg * (1.0 - sg)).astype(BF16)

        @pl.when(i == na // tm - 1)
        def _():
            dw_ref[...] = jnp.sum(part[...], axis=1)

    dyp, dyn = _halo_specs(tm, d, na, 0)
    ap, an = _halo_specs(tm, d, na, 0)
    gp, gn = _halo_specs(tm, d, na, 1)
    bfo = jax.ShapeDtypeStruct((na, d), BF16)
    return _pcall(
        body, name=name, grid=(na // tm,),
        in_specs=[_rows(tm, d), dyp, dyn, _rows(tm, d, 0), ap, an, _rows(tm, d, 1), gp, gn, _full((ktaps, d))],
        out_specs=[_rows(tm, d), _rows(tm, d), _full((ktaps, d))],
        out_shape=[bfo, bfo, jax.ShapeDtypeStruct((ktaps, d), F32)],
        scratch_shapes=[pltpu.VMEM((tm + 2 * HALO, d), F32), pltpu.VMEM((tm + 2 * HALO, d), F32),
                        pltpu.VMEM((tm, d), F32), pltpu.VMEM((ktaps, 8, d), F32)],
        compiler_params=_cp("arbitrary"))(dy2, dy2, dy2, p, p, p, p, p, p, conv_w)


def _inproj_bwd(segs, dkv, xa, dxn, modv, gpre, wp_t, n_lat, out_rows, tm, name):
    na, d = xa.shape
    nseg = len(segs)
    wrows = wp_t.shape[0]

    def body(*refs):
        seg_refs = refs[:nseg]
        dkv_ref, x_ref, dxn_ref, mod_ref, g_ref, wt_hbm, dx_ref, acc_ref, wt, sem = refs[nseg:]
        i = pl.program_id(0)
        ctx_tile = i >= n_lat // tm

        @pl.when(i == 0)
        def _():
            cp = pltpu.make_async_copy(wt_hbm, wt, sem)
            cp.start()
            cp.wait()
            acc_ref[...] = jnp.zeros_like(acc_ref)

        dh = _dot(dkv_ref[...], wt[nseg * d:wrows, :])
        for s in range(nseg):
            dh = dh + _dot(seg_refs[s][...], wt[s * d:(s + 1) * d, :])
        x = x_ref[...]
        r = lax.rsqrt(_mean(x * x) + EPS)
        xn = x * r
        g = g_ref[...]
        sc1 = 1.0 + _seg_rows(mod_ref, d, 2 * d, ctx_tile)
        dsh = _colsum(dh)
        dsc = _colsum(dh * (xn * g))

        @pl.when(ctx_tile)
        def _():
            acc_ref[1:2, :] += dsh
            acc_ref[3:4, :] += dsc

        @pl.when(jnp.logical_not(ctx_tile))
        def _():
            acc_ref[0:1, :] += dsh
            acc_ref[2:3, :] += dsc

        acc_ref[4:5, :] += _colsum(dh * xn * sc1)
        dxh = dh * g * sc1

        @pl.when(i < out_tiles)
        def _():
            dx_ref[...] = dxn_ref[...] + r * (dxh - xn * _mean(dxh * xn))

    r_ = _rows(tm, d)
    out_tiles = out_rows // tm
    return _pcall(
        body, name=name, grid=(na // tm,),
        in_specs=[r_] * nseg + [_rows(tm, d // 2), r_, r_, _full((8, 3 * d)), _full((1, d)), ANY],
        out_specs=[pl.BlockSpec((tm, d), lambda i: (jnp.minimum(i, out_tiles - 1), 0)), _full((8, d))],
        out_shape=[jax.ShapeDtypeStruct((out_rows, d), F32), jax.ShapeDtypeStruct((8, d), F32)],
        scratch_shapes=[pltpu.VMEM(wp_t.shape, BF16), pltpu.SemaphoreType.DMA],
        compiler_params=_cp("arbitrary"))(*segs, dkv, xa, dxn, modv, gpre, wp_t)


def _grad_matmul(a_t, b, tk, name):
    ka, na = a_t.shape
    nb = b.shape[1]
    tn = min(nb, 1024)

    def body(a_ref, b_ref, o_ref):
        @pl.when(pl.program_id(1) == 0)
        def _():
            o_ref[...] = jnp.zeros_like(o_ref)

        o_ref[...] += _dot(a_ref[...], b_ref[...])

    return _pcall(
        body, name=name, grid=(nb // tn, na // tk),
        in_specs=[pl.BlockSpec((ka, tk), lambda n, k: (0, k)), pl.BlockSpec((tk, tn), lambda n, k: (k, n))],
        out_specs=pl.BlockSpec((ka, tn), lambda n, k: (0, n)),
        out_shape=jax.ShapeDtypeStruct((ka, nb), F32),
        compiler_params=_cp("parallel", "arbitrary"))(a_t, b)


def _pack(parts, cols, row_mult):
    flat = jnp.concatenate([q.astype(F32).reshape(-1) for q in parts])
    rows = -(-flat.shape[0] // (cols * row_mult)) * row_mult
    return jnp.pad(flat, (0, rows * cols - flat.shape[0])).reshape(1, rows, cols)


def _unpack(flat, shapes):
    out, off = [], 0
    for s in shapes:
        n = math.prod(s)
        out.append(flat[off:off + n].reshape(tuple(s)))
        off += n
    return out


def _cols_by_dest(g):
    l, a, w8 = g.shape
    return g.reshape(l, a, N_DEV, w8 // N_DEV).transpose(2, 0, 1, 3)


def _rows_by_dest(g):
    l, r8, b = g.shape
    return g.reshape(l, N_DEV, r8 // N_DEV, b).transpose(1, 0, 2, 3)


def _cols_from_src(s):
    n, l, a, w = s.shape
    return s.transpose(1, 2, 0, 3).reshape(l, a, n * w)


def _rows_from_src(s):
    n, l, r, b = s.shape
    return s.transpose(1, 0, 2, 3).reshape(l, n * r, b)


def _rope_tables(n_lat, n_ctx):
    half = HEAD_DIM // 2
    rows = n_lat // GRID_W
    row = jnp.repeat(jnp.arange(rows, dtype=F32), GRID_W)
    col = jnp.tile(jnp.arange(GRID_W, dtype=F32), rows)
    inv_freq = ROPE_THETA ** (-jnp.arange(0, half, 2, dtype=F32) / half)
    ang = jnp.concatenate([row[:, None] * inv_freq, col[:, None] * inv_freq], axis=-1)
    cos, sin = jnp.cos(ang), jnp.sin(ang)
    cos_t = jnp.concatenate([jnp.tile(cos, (1, 4)), jnp.ones((n_ctx, 128), F32)], axis=0)
    sin_t = jnp.concatenate([jnp.tile(jnp.concatenate([-sin, sin], axis=-1), (1, 2)),
                             jnp.zeros((n_ctx, 128), F32)], axis=0)
    return cos_t, sin_t


def _to_heads(t):
    na, w = t.shape
    return t.reshape(na, w // HEAD_DIM, HEAD_DIM).transpose(1, 0, 2)


def kernel(x, c, ctx, c_ctx, w_mod, b_mod, g_pre, g_post, w_in, conv_w, conv_b, ln_g, ln_b, w_conv_out, q_norm_g, k_norm_g, w_attn_out, w_out, loss_target, m_c_ctx, m_w_mod, m_b_mod, m_g_pre, m_g_post, m_w_in, m_conv_w, m_conv_b, m_ln_g, m_ln_b, m_w_conv_out, m_q_norm_g, m_k_norm_g, m_w_attn_out, m_w_out, v_c_ctx, v_w_mod, v_b_mod, v_g_pre, v_g_post, v_w_in, v_conv_w, v_conv_b, v_ln_g, v_ln_b, v_w_conv_out, v_q_norm_g, v_k_norm_g, v_w_attn_out, v_w_out):
    depth, d, _ = w_mod.shape
    n_lat, n_ctx = x.shape[1], ctx.shape[1]
    na = n_lat + n_ctx
    heads = d // HEAD_DIM
    kvh = heads // GROUP
    kw = d // GROUP
    ktaps = conv_w.shape[1]
    tm = n_ctx
    tm_half = tm // 2
    tbig = 3 * tm if na % (3 * tm) == 0 else tm
    tq_lat = 4 * tm if n_lat % (4 * tm) == 0 else tm
    tq_bwd = 4 * tm if n_lat % (4 * tm) == 0 else tm

    s_mod, s_in, s_co, s_ao, s_oo, s_cw = _all_gather(
        [w_mod.astype(BF16), w_in.astype(BF16), w_conv_out.astype(BF16), w_attn_out.astype(BF16),
         w_out.astype(BF16), conv_w], "gather_weights")
    wmod_f = _cols_from_src(s_mod)
    win_f = _cols_from_src(s_in)
    wc_f, wa_f, wo_f = _rows_from_src(s_co), _rows_from_src(s_ao), _rows_from_src(s_oo)
    wp_f = jnp.concatenate([win_f[:, :, :4 * d], win_f[:, :, 4 * d + 2 * kw:], win_f[:, :, 4 * d:4 * d + 2 * kw]], axis=2)
    convw_f = _cols_from_src(s_cw)

    cos_t, sin_t = _rope_tables(n_lat, n_ctx)
    lane = jnp.arange(d)
    bd = (lane[:, None] // HEAD_DIM == lane[None, :] // HEAD_DIM).astype(BF16)
    esel = (lane[:, None] // HEAD_DIM == jnp.arange(128)[None, :]).astype(BF16)
    cvec = jnp.zeros((8, d), F32).at[0].set(c[0]).at[1].set(c_ctx)
    cvec_t = jnp.zeros((d, 128), F32).at[:, 0].set(c[0]).at[:, 1].set(c_ctx)

    xa = jnp.concatenate([x[0], ctx[0]], axis=0)

    saved = []
    for l in range(depth):
        tag = f"_l{l}"
        gq = jnp.tile(q_norm_g[l], heads)[None, :]
        gk = jnp.tile(k_norm_g[l], heads // GROUP)[None, :]
        modv = _mod_fwd(cvec, wmod_f[l], b_mod[l][None, :], "mod_fwd" + tag)
        p, h_t = _inproj(xa, modv, g_pre[l][None, :], wp_f[l], n_lat, tbig, "inproj" + tag)
        q_t, kr, vb, k_t, v_t = _qknorm_fwd(p, cos_t, sin_t, gq, gk, bd, tm, "qknorm_fwd" + tag)
        qt_hm = q_t.reshape(heads, HEAD_DIM, na)
        kt_hm = k_t.reshape(kvh, HEAD_DIM, na)
        k_hm, v_hm = _to_heads(kr), _to_heads(vb)
        vt_ones = jnp.concatenate([v_t.reshape(kvh, HEAD_DIM, na), jnp.ones((kvh, 16, na), BF16)], axis=1)
        ot_lat, lse_lat = _attn_fwd(qt_hm, k_hm, vt_ones, n_lat, tq_lat, tbig, tm, "attn_fwd" + tag)
        ot_ctx, lse_ctx = _attn_fwd(qt_hm[:, :, n_lat:], k_hm[:, n_lat:], vt_ones[:, :, n_lat:], n_ctx, tm, tm, tm,
                                    "attn_ctx_fwd" + tag)
        ot_lat, ot_ctx = ot_lat.reshape(d, n_lat), ot_ctx.reshape(d, n_ctx)
        y5, y5_t, y2 = _conv_fwd(p, convw_f[l], conv_b[l][None, :], ln_g[l][None, :], ln_b[l][None, :], n_lat, tm,
                                 "conv_fwd" + tag)
        xa_new, yc, ya, out, z_t, og_t = _merge_fwd(y5, ot_lat, ot_ctx, p, xa, modv, g_post[l][None, :], wc_f[l],
                                                    wa_f[l], wo_f[l], n_lat, tm, "merge_fwd" + tag)
        saved.append(dict(xa=xa, modv=modv, p=p, h_t=h_t, qt_hm=qt_hm, k_hm=k_hm, kt_hm=kt_hm, v_hm=v_hm,
                          ot_lat=ot_lat, ot_ctx=ot_ctx, lse_lat=lse_lat, lse_ctx=lse_ctx,
                          y5_t=y5_t, y2=y2, yc=yc, ya=ya, out=out, z_t=z_t, og_t=og_t, gq=gq, gk=gk))
        xa = xa_new

    dxa, loss_blk = _loss_grad(xa, loss_target[0], n_lat, tm, "loss_grad")

    g_wmod, g_win, g_convw, g_wc, g_wa, g_wo = [], [], [], [], [], []
    g_bmod, g_gpre, g_gpost, g_convb, g_lng, g_lnb, g_qg, g_kg = [], [], [], [], [], [], [], []
    g_cctx = jnp.zeros((d,), F32)
    for l in reversed(range(depth)):
        tag = f"_l{l}"
        s = saved[l]
        p = s["p"]
        (dgb, dma, dmb, dgta, do_t, dout, dyc, dya, dy2, dl_t, acc_m) = _merge_bwd(
            dxa, s["out"], s["yc"], s["ya"], s["ot_lat"], s["ot_ctx"], p, s["y2"], s["modv"], g_post[l][None, :],
            ln_g[l][None, :], ln_b[l][None, :], wo_f[l].T, wc_f[l].T, wa_f[l].T, esel, n_lat, tm_half,
            "merge_bwd" + tag)
        dl_r = dl_t[:heads].reshape(heads, 1, na)
        dot_hm = do_t.reshape(heads, HEAD_DIM, na)
        qt_hm, k_hm, kt_hm, v_hm = s["qt_hm"], s["k_hm"], s["kt_hm"], s["v_hm"]
        dqt_lat, dkt_hm, dvt_hm = _attn_bwd(qt_hm, k_hm, kt_hm, v_hm, dot_hm, s["lse_lat"], dl_r,
                                            n_lat, tq_bwd, tm, tm, "attn_bwd" + tag)
        dqt_ctx, dkt_ctx, dvt_ctx = _attn_bwd(
            qt_hm[:, :, n_lat:], k_hm[:, n_lat:], kt_hm[:, :, n_lat:], v_hm[:, n_lat:], dot_hm[:, :, n_lat:],
            s["lse_ctx"], dl_r[:, :, n_lat:], n_ctx, tm, tm, tm, "attn_ctx_bwd" + tag)
        dq, dkv, acc_q = _qknorm_bwd(
            dqt_lat.reshape(d, n_lat), dqt_ctx.reshape(d, n_ctx), dkt_hm.reshape(kw, na), dkt_ctx.reshape(kw, n_ctx),
            dvt_hm.reshape(kw, na), dvt_ctx.reshape(kw, n_ctx), p, cos_t, sin_t, s["gq"], s["gk"], bd, n_lat, tm,
            "qknorm_bwd" + tag)
        da, dg, dconvw = _conv_bwd(dy2, p, convw_f[l], n_lat, tm, "conv_bwd" + tag)
        segs = [da, dg, dgta, dq, dgb, dma, dmb]
        dxa, acc_h = _inproj_bwd(segs, dkv, s["xa"], dxa, s["modv"], g_pre[l][None, :], wp_f[l].T, n_lat,
                                 na if l else n_lat, tm, "inproj_bwd" + tag)

        dwp = [_grad_matmul(s["h_t"], sg, tbig, f"grad_w_in{k}" + tag) for k, sg in enumerate(segs + [dkv])]
        g_win.append(jnp.concatenate(dwp[:4] + [dwp[7]] + dwp[4:7], axis=1))
        g_wc.append(_grad_matmul(s["y5_t"], dyc, tbig, "grad_w_conv_out" + tag))
        g_wa.append(_grad_matmul(s["og_t"], dya, tbig, "grad_w_attn_out" + tag))
        g_wo.append(_grad_matmul(s["z_t"], dout, tbig, "grad_w_out" + tag))
        g_convw.append(dconvw)

        dmod = jnp.zeros((8, 3 * d), F32)
        dmod = dmod.at[0].set(jnp.concatenate([acc_h[0], acc_h[2], acc_m[0]]))
        dmod = dmod.at[1].set(jnp.concatenate([acc_h[1], acc_h[3], acc_m[1]]))
        dwm, dbm, dcv = _mod_bwd(dmod, cvec, cvec_t, wmod_f[l], "mod_bwd" + tag)
        g_wmod.append(dwm)
        g_bmod.append(dbm[0])
        g_cctx = g_cctx + dcv[1]
        g_gpre.append(acc_h[4])
        g_gpost.append(acc_m[2])
        g_lng.append(acc_m[3])
        g_lnb.append(acc_m[4])
        g_convb.append(acc_m[5])
        g_qg.append(acc_q[0].reshape(heads, HEAD_DIM).sum(0))
        g_kg.append(acc_q[1, :kw].reshape(heads // GROUP, HEAD_DIM).sum(0))

    grad_x = dxa[None]

    def stack(lst):
        return jnp.stack(lst[::-1])

    big_names = ["w_mod", "w_in", "w_conv_out", "w_attn_out", "w_out", "conv_w"]
    big_w = dict(w_mod=w_mod, w_in=w_in, w_conv_out=w_conv_out, w_attn_out=w_attn_out, w_out=w_out, conv_w=conv_w)
    big_m = dict(w_mod=m_w_mod, w_in=m_w_in, w_conv_out=m_w_conv_out, w_attn_out=m_w_attn_out, w_out=m_w_out,
                 conv_w=m_conv_w)
    big_v = dict(w_mod=v_w_mod, w_in=v_w_in, w_conv_out=v_w_conv_out, w_attn_out=v_w_attn_out, w_out=v_w_out,
                 conv_w=v_conv_w)
    by_dest = [_cols_by_dest(stack(g_wmod)), _cols_by_dest(stack(g_win)), _rows_by_dest(stack(g_wc)),
               _rows_by_dest(stack(g_wa)), _rows_by_dest(stack(g_wo)), _cols_by_dest(stack(g_convw))]
    send = [q.reshape((4, 2) + q.shape[1:]) for q in by_dest]
    from_sibling = _swap_with_sibling(send, "reduce_sibling")
    chip_part = [_add_sibling_part(sb, rc, BF16, "reduce_sibling_add_" + n)
                 for n, sb, rc in zip(big_names, send, from_sibling)]
    from_chips = _exchange_chips(chip_part, "reduce_chips")
    big_g, big_d, big_nm, big_nv = {}, {}, {}, {}
    for n, st in zip(big_names, from_chips):
        big_g[n], big_d[n], big_nm[n], big_nv[n] = _sum_adamw(st, big_w[n], big_m[n], big_v[n], "adamw_" + n)

    small_names = ["c_ctx", "b_mod", "g_pre", "g_post", "conv_b", "ln_g", "ln_b", "q_norm_g", "k_norm_g", "loss"]
    zero1 = jnp.zeros((1,), F32)
    small_w = dict(c_ctx=c_ctx, b_mod=b_mod, g_pre=g_pre, g_post=g_post, conv_b=conv_b, ln_g=ln_g, ln_b=ln_b,
                   q_norm_g=q_norm_g, k_norm_g=k_norm_g, loss=zero1)
    small_m = dict(c_ctx=m_c_ctx, b_mod=m_b_mod, g_pre=m_g_pre, g_post=m_g_post, conv_b=m_conv_b, ln_g=m_ln_g,
                   ln_b=m_ln_b, q_norm_g=m_q_norm_g, k_norm_g=m_k_norm_g, loss=zero1)
    small_v = dict(c_ctx=v_c_ctx, b_mod=v_b_mod, g_pre=v_g_pre, g_post=v_g_post, conv_b=v_conv_b, ln_g=v_ln_g,
                   ln_b=v_ln_b, q_norm_g=v_q_norm_g, k_norm_g=v_k_norm_g, loss=zero1)
    small_g = dict(c_ctx=g_cctx, b_mod=stack(g_bmod), g_pre=stack(g_gpre), g_post=stack(g_gpost),
                   conv_b=stack(g_convb), ln_g=stack(g_lng), ln_b=stack(g_lnb), q_norm_g=stack(g_qg),
                   k_norm_g=stack(g_kg), loss=loss_blk[0, 0:1])
    small_shapes = [small_w[n].shape for n in small_names]

    def pack_small(tree):
        return _pack([tree[n] for n in small_names], d, 8)

    small_parts, = _all_gather([pack_small(small_g)], "gather_small_grads")
    small_out = _sum_adamw(small_parts, pack_small(small_w), pack_small(small_m), pack_small(small_v),
                           "adamw_replicated")
    sm_g, sm_d, sm_nm, sm_nv = [dict(zip(small_names, _unpack(o.reshape(-1), small_shapes))) for o in small_out]
    loss = sm_g["loss"].reshape(())

    order = ["c_ctx", "w_mod", "b_mod", "g_pre", "g_post", "w_in", "conv_w", "conv_b", "ln_g", "ln_b",
             "w_conv_out", "q_norm_g", "k_norm_g", "w_attn_out", "w_out"]

    def pick(big, small):
        return [big[n] if n in big else small[n] for n in order]

    return (loss, grad_x, *pick(big_g, sm_g), *pick(big_d, sm_d), *pick(big_nm, sm_nm), *pick(big_nv, sm_nv))
```

```python
import math

import jax
import jax.numpy as jnp
from jax import lax
from jax.experimental import pallas as pl
from jax.experimental.pallas import tpu as pltpu

F32 = jnp.float32
BF16 = jnp.bfloat16

HEAD_DIM = 64
GROUP = 4
GRID_W = 64
ROPE_THETA = 10000.0
EPS = 1e-6
ATTN_SCALE = HEAD_DIM ** -0.5
LOG2E = 1.4426950408889634
Q_PRESCALE = ATTN_SCALE * LOG2E
HALO = 16
CONV_ROWS = 64
ATTN_FWD_AHEAD = 5
ATTN_BWD_AHEAD = 3

ADAM_LR = 0.001
ADAM_B1 = 0.9
ADAM_B2 = 0.999
ADAM_EPS = 1e-08
ADAM_WD = 0.01
ADAM_STEP = 10

N_DEV = 8
MESH_AXES = ("x", "y", "c")
V7X_VMEM_LIMIT = 56 * 1024 * 1024
NEG_BIG = -1e30

MESH = pl.DeviceIdType.MESH
ANY = pl.BlockSpec(memory_space=pl.ANY)


def _pcall(body, **kw):
    return pl.pallas_call(body, **kw)


def _cp(*sem):
    return pltpu.CompilerParams(dimension_semantics=sem, vmem_limit_bytes=V7X_VMEM_LIMIT)


def _sig(x):
    return 0.5 * jnp.tanh(0.5 * x) + 0.5


def _mean(x):
    return jnp.mean(x, axis=-1, keepdims=True)


def _colsum(x):
    return jnp.sum(x, axis=0, keepdims=True)


def _bf_round(x):
    return x.astype(BF16).astype(F32)


def _dot(a, b):
    return jnp.dot(a, b, preferred_element_type=F32)


def _dot_nt(a, b):
    return lax.dot_general(a, b, (((1,), (1,)), ((), ())), preferred_element_type=F32)


def _split_dot(x, m):
    hi = x.astype(BF16)
    lo = (x - hi.astype(F32)).astype(BF16)
    return _dot(hi, m) + _dot(lo, m)


def _full(shape):
    nd = len(shape)
    return pl.BlockSpec(shape, lambda *_: (0,) * nd)


def _rows(tm, width, colblk=0):
    return pl.BlockSpec((tm, width), lambda i: (i, colblk))


def _cols(height, tm):
    return pl.BlockSpec((height, tm), lambda i: (0, i))


def _split_cols(height, tm, n_lat):
    nl = n_lat // tm
    return (pl.BlockSpec((height, tm), lambda i: (0, jnp.minimum(i, nl - 1))),
            pl.BlockSpec((height, tm), lambda i: (0, jnp.maximum(i - nl, 0))))


def _my_place():
    return lax.axis_index("x"), lax.axis_index("y"), lax.axis_index("c")


def _sem_arrays(n):
    return [pltpu.SemaphoreType.DMA((n,)), pltpu.SemaphoreType.DMA((n,))]


def _gather_steps(x_refs, out_refs, send_sems, recv_sems, local_sems):
    n = len(x_refs)
    x, y, c = _my_place()
    me, sibling = (x, y, c), (x, y, 1 - c)
    chips = [(1 - x, y), (x, 1 - y), (1 - x, 1 - y)]

    def slab(a, px, py, pc):
        return out_refs[a].at[4 * px + 2 * py + pc]

    def copies(k, block, to, from_input=False):
        return [pltpu.make_async_remote_copy(
            src_ref=x_refs[a] if from_input else slab(a, *block), dst_ref=slab(a, *block),
            send_sem=send_sems.at[k * n + a], recv_sem=recv_sems.at[k * n + a],
            device_id=to, device_id_type=MESH) for a in range(n)]

    def mine():
        return [pltpu.make_async_copy(x_refs[a], slab(a, *me), local_sems.at[a]) for a in range(n)]

    def first():
        out = copies(0, me, sibling, True)
        for j, chip in enumerate(chips):
            out += copies(1 + j, me, (*chip, c), True)
        return out

    def onward(j):
        return copies(4 + j, (*chips[j], c), sibling)

    def start():
        for cp in mine() + first():
            cp.start()

    def relay():
        for j, chip in enumerate(chips):
            for cp in copies(1 + j, (*chip, c), me):
                cp.wait_recv()
            for cp in onward(j):
                cp.start()

    def finish():
        for cp in copies(0, sibling, me):
            cp.wait_recv()
        for j, chip in enumerate(chips):
            for cp in copies(4 + j, (*chip, 1 - c), me):
                cp.wait_recv()
        for cp in first() + onward(0) + onward(1) + onward(2):
            cp.wait_send()
        for cp in mine():
            cp.wait()

    return start, relay, finish


def _gather_scratch(n):
    return _sem_arrays(7 * n) + [pltpu.SemaphoreType.DMA((n,))]


def _all_gather(shards, name):
    n = len(shards)

    def body(*refs):
        start, relay, finish = _gather_steps(refs[:n], refs[n:2 * n], *refs[2 * n:])
        start()
        relay()
        finish()

    return _pcall(
        body, name=name,
        out_shape=[jax.ShapeDtypeStruct((N_DEV,) + q.shape, q.dtype) for q in shards],
        in_specs=[ANY] * n, out_specs=[ANY] * n,
        scratch_shapes=_gather_scratch(n),
    )(*shards)


def _swap_with_sibling(bufs, name):
    n = len(bufs)

    def body(*refs):
        buf_refs, recv_refs = refs[:n], refs[n:2 * n]
        send_sems, recv_sems = refs[2 * n:]
        x, y, c = _my_place()
        copies = [
            pltpu.make_async_remote_copy(
                src_ref=buf_refs[a].at[k, 1 - c], dst_ref=recv_refs[a].at[k],
                send_sem=send_sems.at[k * n + a], recv_sem=recv_sems.at[k * n + a],
                device_id=(x, y, 1 - c), device_id_type=MESH)
            for k in range(4) for a in range(n)]
        for cp in copies:
            cp.start()
        for cp in copies:
            cp.wait()

    return _pcall(
        body, name=name,
        out_shape=[jax.ShapeDtypeStruct((4,) + q.shape[2:], q.dtype) for q in bufs],
        in_specs=[ANY] * n, out_specs=[ANY] * n,
        scratch_shapes=_sem_arrays(4 * n),
    )(*bufs)


def _exchange_chips(parts, name):
    n = len(parts)

    def body(*refs):
        s_refs, recv_refs = refs[:n], refs[n:2 * n]
        send_sems, recv_sems, local_sems = refs[2 * n:]
        x, y, c = _my_place()
        mychip = 2 * x + y
        chips = [(1 - x, y), (x, 1 - y), (1 - x, 1 - y)]
        mine = [pltpu.make_async_copy(s_refs[a].at[mychip], recv_refs[a].at[mychip], local_sems.at[a])
                for a in range(n)]
        for cp in mine:
            cp.start()
        copies = [
            pltpu.make_async_remote_copy(
                src_ref=s_refs[a].at[2 * px + py], dst_ref=recv_refs[a].at[mychip],
                send_sem=send_sems.at[j * n + a], recv_sem=recv_sems.at[j * n + a],
                device_id=(px, py, c), device_id_type=MESH)
            for j, (px, py) in enumerate(chips) for a in range(n)]
        for cp in copies:
            cp.start()
        for cp in copies:
            cp.wait()
        for cp in mine:
            cp.wait()

    return _pcall(
        body, name=name,
        out_shape=[jax.ShapeDtypeStruct(q.shape, q.dtype) for q in parts],
        in_specs=[ANY] * n, out_specs=[ANY] * n,
        scratch_shapes=_sem_arrays(3 * n) + [pltpu.SemaphoreType.DMA((n,))],
    )(*parts)


def _row_tile(a):
    for t in range(256, 7, -8):
        if a % t == 0:
            return t
    return a


def _add_sibling_part(buf, recv, out_dtype, name):
    _, _, nl, a, b = buf.shape
    ta = _row_tile(a)
    core = lax.axis_index("c").astype(jnp.int32).reshape(1)

    def body(core_ref, a_ref, b_ref, o_ref):
        o_ref[...] = (a_ref[...] + b_ref[...]).astype(o_ref.dtype)

    grid_spec = pltpu.PrefetchScalarGridSpec(
        num_scalar_prefetch=1, grid=(4, nl, a // ta),
        in_specs=[pl.BlockSpec((None, None, None, ta, b), lambda k, l, r, cr: (k, cr[0], l, r, 0)),
                  pl.BlockSpec((None, None, ta, b), lambda k, l, r, cr: (k, l, r, 0))],
        out_specs=pl.BlockSpec((None, None, ta, b), lambda k, l, r, cr: (k, l, r, 0)))
    return _pcall(body, name=name, grid_spec=grid_spec,
                  out_shape=jax.ShapeDtypeStruct((4, nl, a, b), out_dtype),
                  compiler_params=_cp("parallel", "parallel", "parallel"))(core, buf, recv)


def _sum_adamw(stack, w, m, v, name):
    ns, nl, a, b = stack.shape
    ta = _row_tile(a)
    c1 = 1.0 - ADAM_B1 ** ADAM_STEP
    c2 = 1.0 - ADAM_B2 ** ADAM_STEP

    def body(s_ref, w_ref, m_ref, v_ref, g_out, d_out, m_out, v_out):
        g = s_ref[0].astype(F32)
        for k in range(1, ns):
            g = g + s_ref[k].astype(F32)
        m_new = ADAM_B1 * m_ref[...] + (1.0 - ADAM_B1) * g
        v_new = ADAM_B2 * v_ref[...] + (1.0 - ADAM_B2) * (g * g)
        m_hat = m_new / c1
        v_hat = v_new / c2
        g_out[...] = g
        d_out[...] = -ADAM_LR * (m_hat / (jnp.sqrt(v_hat) + ADAM_EPS) + ADAM_WD * w_ref[...])
        m_out[...] = m_new
        v_out[...] = v_new

    blk = pl.BlockSpec((None, ta, b), lambda l, i: (l, i, 0))
    return _pcall(
        body, name=name, grid=(nl, a // ta),
        in_specs=[pl.BlockSpec((ns, None, ta, b), lambda l, i: (0, l, i, 0)), blk, blk, blk],
        out_specs=[blk] * 4,
        out_shape=[jax.ShapeDtypeStruct((nl, a, b), F32)] * 4,
        compiler_params=_cp("parallel", "parallel"))(stack, w, m, v)


def _mod_fwd(cvec, wmod, bmod, name):
    _, d = cvec.shape

    def body(c_ref, w_ref, b_ref, o_ref):
        cv = c_ref[...]
        cs = cv * _sig(cv)
        o_ref[...] = _dot(cs.astype(BF16), w_ref[...]) + b_ref[...]

    return _pcall(
        body, name=name, grid=(3,),
        in_specs=[_full((8, d)), pl.BlockSpec((d, d), lambda n: (0, n)), pl.BlockSpec((1, d), lambda n: (0, n))],
        out_specs=pl.BlockSpec((8, d), lambda n: (0, n)),
        out_shape=jax.ShapeDtypeStruct((8, 3 * d), F32),
        compiler_params=_cp("parallel"))(cvec, wmod, bmod)


def _mod_bwd(dmod, cvec, cvec_t, wmod, name):
    _, d = cvec.shape

    def body(dm_ref, c_ref, ct_ref, w_ref, dw_ref, db_ref, dc_ref):
        n = pl.program_id(0)
        dm = dm_ref[...]
        ct = ct_ref[...]
        cs_t = _bf_round(ct * _sig(ct))
        d0 = _bf_round(dm[0:1, :])
        d1 = _bf_round(dm[1:2, :])
        dw_ref[...] = cs_t[:, 0:1] * d0 + cs_t[:, 1:2] * d1
        db_ref[...] = dm[0:1, :] + dm[1:2, :]

        @pl.when(n == 0)
        def _():
            dc_ref[...] = jnp.zeros_like(dc_ref)

        dc_ref[...] += _dot_nt(dm.astype(BF16), w_ref[...])

        @pl.when(n == 2)
        def _():
            cv = c_ref[...]
            s = _sig(cv)
            dc_ref[...] = dc_ref[...] * (s * (1.0 + cv * (1.0 - s)))

    return _pcall(
        body, name=name, grid=(3,),
        in_specs=[pl.BlockSpec((8, d), lambda n: (0, n)), _full((8, d)), _full((d, 128)),
                  pl.BlockSpec((d, d), lambda n: (0, n))],
        out_specs=[pl.BlockSpec((d, d), lambda n: (0, n)), pl.BlockSpec((1, d), lambda n: (0, n)),
                   _full((8, d))],
        out_shape=[jax.ShapeDtypeStruct((d, 3 * d), F32), jax.ShapeDtypeStruct((1, 3 * d), F32),
                   jax.ShapeDtypeStruct((8, d), F32)],
        compiler_params=_cp("arbitrary"))(dmod, cvec, cvec_t, wmod)


def _seg_rows(mod_ref, lo, hi, is_ctx):
    return jnp.where(is_ctx, mod_ref[1:2, lo:hi], mod_ref[0:1, lo:hi])


def _inproj(xa, modv, gpre, wp, n_lat, tm, name, shards=()):
    na, d = xa.shape
    wcols = wp.shape[1]
    tn = 5 * d // 4 if (5 * d // 4) % 128 == 0 and wcols % (5 * d // 4) == 0 else d // 2
    ns = len(shards)
    ni, nj = na // tm, wcols // tn

    def body(x_ref, mod_ref, g_ref, w_ref, *rest):
        p_ref, ht_ref = rest[ns:ns + 2]
        h_s = rest[2 * ns + 2]
        i = pl.program_id(0)
        j = pl.program_id(1)

        if ns:
            start, relay, finish = _gather_steps(rest[:ns], rest[ns + 2:2 * ns + 2], *rest[2 * ns + 3:])
            pl.when(jnp.logical_and(i == 0, j == 0))(start)
            pl.when(jnp.logical_and(i == ni // 2, j == 0))(relay)

        @pl.when(j == 0)
        def _():
            x = x_ref[...]
            r = lax.rsqrt(_mean(x * x) + EPS)
            row = i * tm + lax.broadcasted_iota(jnp.int32, (tm, 1), 0)
            is_ctx = row >= n_lat
            sh = _seg_rows(mod_ref, 0, d, is_ctx)
            sc = _seg_rows(mod_ref, d, 2 * d, is_ctx)
            h = (x * r * g_ref[...]) * (1.0 + sc) + sh
            h_s[...] = h.astype(BF16)
            ht_ref[...] = h.T.astype(BF16)

        p_ref[...] = _dot(h_s[...], w_ref[...])

        if ns:
            pl.when(jnp.logical_and(i == ni - 1, j == nj - 1))(finish)

    return _pcall(
        body, name=name, grid=(ni, nj),
        in_specs=[pl.BlockSpec((tm, d), lambda i, j: (i, 0)), _full((8, 3 * d)), _full((1, d)),
                  pl.BlockSpec((d, tn), lambda i, j: (0, j))] + [ANY] * ns,
        out_specs=[pl.BlockSpec((tm, tn), lambda i, j: (i, j)), pl.BlockSpec((d, tm), lambda i, j: (0, i))]
        + [ANY] * ns,
        out_shape=[jax.ShapeDtypeStruct((na, wcols), F32), jax.ShapeDtypeStruct((d, na), BF16)]
        + [jax.ShapeDtypeStruct((N_DEV,) + q.shape, q.dtype) for q in shards],
        scratch_shapes=[pltpu.VMEM((tm, d), BF16)] + (_gather_scratch(ns) if ns else []),
        compiler_params=_cp("arbitrary" if ns else "parallel", "arbitrary"))(xa, modv, gpre, wp, *shards)


def _lane_tile(t, width):
    if width >= 128:
        return jnp.tile(t, (1, width // 128))
    return t[:, :width]


def _partner(x):
    w = x.shape[-1]
    lane = lax.broadcasted_iota(jnp.int32, x.shape, 1)
    low = (lane % HEAD_DIM) < (HEAD_DIM // 2)
    return jnp.where(low, pltpu.roll(x, w - HEAD_DIM // 2, 1), pltpu.roll(x, HEAD_DIM // 2, 1))


def _qknorm_fwd(p, cos_t, sin_t, gq, gk, bd, tm, name):
    na = p.shape[0]
    d = gq.shape[1]
    kw = d // GROUP

    def body(q_ref, kv_ref, cos_ref, sin_ref, gq_ref, gk_ref, bd_ref, qt_ref, ko_ref, vo_ref, kt_ref, vt_ref):
        cos = cos_ref[...]
        sin = sin_ref[...]

        def norm_rope(xh, g, w):
            ms = _split_dot(xh * xh, bd_ref[0:w, 0:w]) * (1.0 / HEAD_DIM)
            xn = xh * lax.rsqrt(ms + EPS) * g
            return xn * _lane_tile(cos, w) + _partner(xn) * _lane_tile(sin, w)

        qt_ref[...] = (norm_rope(q_ref[...], gq_ref[...], d) * Q_PRESCALE).T.astype(BF16)
        kv = kv_ref[...]
        k = norm_rope(kv[:, 0:kw], gk_ref[...], kw)
        v = kv[:, kw:2 * kw]
        ko_ref[...] = k.astype(BF16)
        vo_ref[...] = v.astype(BF16)
        kt_ref[...] = k.T.astype(BF16)
        vt_ref[...] = v.T.astype(BF16)

    cols = lambda w: pl.BlockSpec((w, tm), lambda i: (0, i))
    return _pcall(
        body, name=name, grid=(na // tm,),
        in_specs=[_rows(tm, d, 3), _rows(tm, d // 2, 14), _rows(tm, 128), _rows(tm, 128),
                  _full((1, d)), _full((1, kw)), _full((d, d))],
        out_specs=[cols(d), _rows(tm, kw), _rows(tm, kw), cols(kw), cols(kw)],
        out_shape=[jax.ShapeDtypeStruct((d, na), BF16), jax.ShapeDtypeStruct((na, kw), BF16),
                   jax.ShapeDtypeStruct((na, kw), BF16), jax.ShapeDtypeStruct((kw, na), BF16),
                   jax.ShapeDtypeStruct((kw, na), BF16)],
        compiler_params=_cp("parallel"))(p, p, cos_t, sin_t, gq, gk, bd)


def _attn_fwd(qt_hm, k_hm, vt_hm, na, tq, tk, cw, name):
    h, hd, _ = qt_hm.shape
    kv, nkeys, _ = k_hm.shape
    vrows = vt_hm.shape[1]
    nq, nk = na // tq, nkeys // tk
    nsub = tq // cw
    chains = [(a, u) for a in range(GROUP) for u in range(nsub)]

    def body(q_ref, k_ref, vt_ref, o_ref, lse_ref, m_s, acc_s):
        j = pl.program_id(2)

        @pl.when(j == 0)
        def _():
            m_s[...] = jnp.full_like(m_s, NEG_BIG)
            acc_s[...] = jnp.zeros_like(acc_s)

        k = k_ref[...]
        vt = vt_ref[...]

        def scores(n):
            a, u = chains[n]
            return _dot(k, q_ref[a, :, u * cw:(u + 1) * cw])

        s_tiles = {n: scores(n) for n in range(min(ATTN_FWD_AHEAD, len(chains)))}
        pending = None
        for n, (a, u) in enumerate(chains):
            cols = slice(u * cw, (u + 1) * cw)
            s_t = s_tiles.pop(n)
            m_prev = m_s[a, :, cols]
            m_new = jnp.maximum(m_prev, jnp.max(s_t, axis=0, keepdims=True))
            m_s[a, :, cols] = m_new
            pv = _dot(vt, jnp.exp2(s_t - m_new).astype(BF16))
            if pending is not None:
                pa, pcols, palpha, ppv = pending
                acc_s[pa, :, pcols] = palpha * acc_s[pa, :, pcols] + ppv
            pending = (a, cols, jnp.exp2(m_prev - m_new), pv)
            if n + ATTN_FWD_AHEAD < len(chains):
                s_tiles[n + ATTN_FWD_AHEAD] = scores(n + ATTN_FWD_AHEAD)
        pa, pcols, palpha, ppv = pending
        acc_s[pa, :, pcols] = palpha * acc_s[pa, :, pcols] + ppv

        @pl.when(j == nk - 1)
        def _():
            for a in range(GROUP):
                acc = acc_s[a]
                l = acc[hd:hd + 1, :]
                o_ref[a] = acc[0:hd, :] / l
                lse_ref[a] = m_s[a] + jnp.log2(l)

    return _pcall(
        body, name=name, grid=(kv, nq, nk),
        in_specs=[pl.BlockSpec((GROUP, hd, tq), lambda g, i, j: (g, 0, i)),
                  pl.BlockSpec((None, tk, hd), lambda g, i, j: (g, j, 0)),
                  pl.BlockSpec((None, vrows, tk), lambda g, i, j: (g, 0, j))],
        out_specs=[pl.BlockSpec((GROUP, hd, tq), lambda g, i, j: (g, 0, i)),
                   pl.BlockSpec((GROUP, 1, tq), lambda g, i, j: (g, 0, i))],
        out_shape=[jax.ShapeDtypeStruct((h, hd, na), F32), jax.ShapeDtypeStruct((h, 1, na), F32)],
        scratch_shapes=[pltpu.VMEM((GROUP, 1, tq), F32), pltpu.VMEM((GROUP, vrows, tq), F32)],
        compiler_params=_cp("parallel", "parallel", "arbitrary"))(qt_hm, k_hm, vt_hm)


def _window(win_ref, prev, cur, nxt, first, last, tm):
    win_ref[0:HALO, :] = jnp.where(first, 0.0, prev)
    win_ref[HALO:HALO + tm, :] = cur
    win_ref[HALO + tm:HALO + tm + HALO, :] = jnp.where(last, 0.0, nxt)


def _lane_blocks(d):
    return [slice(b, b + 128) for b in range(0, d, 128)]


def _sublane_shifts(slab):
    n = slab.shape[0]
    for b in range(8):
        sh = slab if b == 0 else pltpu.roll(slab, n - b, 0)
        for a8 in range(0, 2 * HALO, 8):
            yield a8 + b, sh[a8:a8 + CONV_ROWS, :]


def _halo_specs(tm, d, na, colblk):
    per = tm // HALO
    last_blk = na // HALO - 1
    prev = pl.BlockSpec((HALO, d), lambda i: (jnp.maximum(i * per - 1, 0), colblk))
    nxt = pl.BlockSpec((HALO, d), lambda i: (jnp.minimum((i + 1) * per, last_blk), colblk))
    return prev, nxt


def _seq_ends(i, n_lat, na, tm):
    first = jnp.logical_or(i == 0, i == n_lat // tm)
    last = jnp.logical_or(i == n_lat // tm - 1, i == na // tm - 1)
    return first, last


def _conv_fwd(p, conv_w, conv_b, ln_g, ln_b, n_lat, tm, name):
    na = p.shape[0]
    ktaps, d = conv_w.shape
    pad = ktaps // 2

    def body(a_ref, ap_ref, an_ref, g_ref, gp_ref, gn_ref, ga_ref, w_ref, cb_ref, lg_ref, lb_ref,
             y5_ref, y5t_ref, y2_ref, win):
        i = pl.program_id(0)
        first, last = _seq_ends(i, n_lat, na, tm)
        _window(win, ap_ref[...] * _sig(gp_ref[...]), a_ref[...] * _sig(g_ref[...]),
                an_ref[...] * _sig(gn_ref[...]), first, last, tm)

        def chunk(c, carry):
            r = pl.multiple_of(c * CONV_ROWS, CONV_ROWS)
            for lanes in _lane_blocks(d):
                acc = jnp.zeros((CONV_ROWS, 128), F32)
                for off, sh in _sublane_shifts(win[pl.ds(r, CONV_ROWS + 2 * HALO), lanes]):
                    k = off - (HALO - pad)
                    if 0 <= k < ktaps:
                        acc = acc + w_ref[k:k + 1, lanes] * sh
                y2_ref[pl.ds(r, CONV_ROWS), lanes] = acc + cb_ref[:, lanes]
            return carry

        lax.fori_loop(0, tm // CONV_ROWS, chunk, 0)
        y2 = y2_ref[...]
        xc = y2 - _mean(y2)
        y3 = xc * lax.rsqrt(_mean(xc * xc) + EPS) * lg_ref[...] + lb_ref[...]
        gate = ga_ref[...]
        y5 = (y3 * _sig(y3)) * (gate * _sig(gate))
        y5_ref[...] = y5.astype(BF16)
        y5t_ref[...] = y5.T.astype(BF16)

    ap, an = _halo_specs(tm, d, na, 0)
    gp, gn = _halo_specs(tm, d, na, 1)
    return _pcall(
        body, name=name, grid=(na // tm,),
        in_specs=[_rows(tm, d, 0), ap, an, _rows(tm, d, 1), gp, gn, _rows(tm, d, 2),
                  _full((ktaps, d)), _full((1, d)), _full((1, d)), _full((1, d))],
        out_specs=[_rows(tm, d), _cols(d, tm), _rows(tm, d)],
        out_shape=[jax.ShapeDtypeStruct((na, d), BF16), jax.ShapeDtypeStruct((d, na), BF16),
                   jax.ShapeDtypeStruct((na, d), F32)],
        scratch_shapes=[pltpu.VMEM((tm + 2 * HALO, d), F32)],
        compiler_params=_cp("parallel"))(p, p, p, p, p, p, p, conv_w, conv_b, ln_g, ln_b)


def _merge_fwd(y5, ot_lat, ot_ctx, p, xa, modv, gpost, wc, wa, wo, n_lat, tm, name):
    na, d = xa.shape

    def body(y5_ref, ol_ref, oc_ref, gb_ref, ma_ref, mb_ref, x_ref, mod_ref, gp_ref, wc_ref, wa_ref, wo_ref,
             xn_ref, yc_ref, ya_ref, out_ref, zt_ref, ogt_ref):
        is_ctx = pl.program_id(0) >= n_lat // tm
        gate_b = gb_ref[...]
        o = jnp.where(is_ctx, oc_ref[...], ol_ref[...]).T
        og = o * (gate_b * _sig(gate_b))
        ogt_ref[...] = og.T.astype(BF16)
        yc = _dot(y5_ref[...], wc_ref[...])
        ya = _dot(og.astype(BF16), wa_ref[...])
        yc_ref[...] = yc
        ya_ref[...] = ya
        z = _sig(ma_ref[...]) * yc + _sig(mb_ref[...]) * ya
        zt_ref[...] = z.T.astype(BF16)
        out = _dot(z.astype(BF16), wo_ref[...])
        out_ref[...] = out
        gt = _seg_rows(mod_ref, 2 * d, 3 * d, is_ctx)
        xn_ref[...] = x_ref[...] + gt * (out * lax.rsqrt(_mean(out * out) + EPS) * gp_ref[...])

    f32o = jax.ShapeDtypeStruct((na, d), F32)
    bft = jax.ShapeDtypeStruct((d, na), BF16)
    ol, oc = _split_cols(d, tm, n_lat)
    return _pcall(
        body, name=name, grid=(na // tm,),
        in_specs=[_rows(tm, d), ol, oc, _rows(tm, d, 4), _rows(tm, d, 5), _rows(tm, d, 6), _rows(tm, d),
                  _full((8, 3 * d)), _full((1, d)), _full((d, d)), _full((d, d)), _full((d, d))],
        out_specs=[_rows(tm, d)] * 4 + [_cols(d, tm)] * 2,
        out_shape=[f32o, f32o, f32o, f32o, bft, bft],
        compiler_params=_cp("parallel"))(y5, ot_lat, ot_ctx, p, p, p, xa, modv, gpost, wc, wa, wo)


def _loss_grad(xa, target, n_lat, tm, name):
    na, d = xa.shape
    nlt = n_lat // tm

    def body(x_ref, t_ref, dx_ref, loss_ref):
        i = pl.program_id(0)

        @pl.when(i == 0)
        def _():
            loss_ref[...] = jnp.zeros_like(loss_ref)

        @pl.when(i < nlt)
        def _():
            err = x_ref[...] - t_ref[...]
            dx_ref[...] = err * (1.0 / d)
            loss_ref[...] += 0.5 * jnp.sum(_mean(err * err))

        @pl.when(i >= nlt)
        def _():
            dx_ref[...] = jnp.zeros_like(dx_ref)

    return _pcall(
        body, name=name, grid=(na // tm,),
        in_specs=[_rows(tm, d), pl.BlockSpec((tm, d), lambda i: (jnp.minimum(i, nlt - 1), 0))],
        out_specs=[_rows(tm, d), _full((8, 128))],
        out_shape=[jax.ShapeDtypeStruct((na, d), F32), jax.ShapeDtypeStruct((8, 128), F32)],
        compiler_params=_cp("arbitrary"))(xa, target)


def _merge_bwd(dxn, out, yc, ya, ot_lat, ot_ctx, p, y2, modv, gpost, ln_g, ln_b, wo_t, wc_t, wa_t, esel, n_lat, tm,
               name):
    na, d = dxn.shape

    def body(dx_ref, out_ref, yc_ref, ya_ref, ol_ref, oc_ref, gb_ref, ma_ref, mb_ref, gta_ref, y2_ref,
             mod_ref, gp_ref, lg_ref, lb_ref, wot_ref, wct_ref, wat_ref, es_ref,
             dgb_ref, dma_ref, dmb_ref, dgta_ref, dot_ref, dout_ref, dyc_ref, dya_ref, dy2_ref, dlt_ref, acc_ref):
        i = pl.program_id(0)
        ctx_tile = i >= n_lat // tm

        @pl.when(i == 0)
        def _():
            acc_ref[...] = jnp.zeros_like(acc_ref)

        gt = _seg_rows(mod_ref, 2 * d, 3 * d, ctx_tile)
        gp = gp_ref[...]
        dx = dx_ref[...]
        out = out_ref[...]
        r2 = lax.rsqrt(_mean(out * out) + EPS)
        n2 = out * r2
        dgt = _colsum(dx * (n2 * gp))

        @pl.when(ctx_tile)
        def _():
            acc_ref[1:2, :] += dgt

        @pl.when(jnp.logical_not(ctx_tile))
        def _():
            acc_ref[0:1, :] += dgt

        acc_ref[2:3, :] += _colsum(dx * gt * n2)
        dn2 = dx * gt * gp
        dout = (r2 * (dn2 - n2 * _mean(dn2 * n2))).astype(BF16)
        dout_ref[...] = dout
        dz = _dot(dout, wot_ref[...])
        sa = _sig(ma_ref[...])
        sb = _sig(mb_ref[...])
        dyc = (dz * sa).astype(BF16)
        dya = (dz * sb).astype(BF16)
        dyc_ref[...] = dyc
        dya_ref[...] = dya
        dma_ref[...] = (dz * yc_ref[...] * sa * (1.0 - sa)).astype(BF16)
        dmb_ref[...] = (dz * ya_ref[...] * sb * (1.0 - sb)).astype(BF16)
        dy5 = _dot(dyc, wct_ref[...])
        dog = _dot(dya, wat_ref[...])

        gate_b = gb_ref[...]
        sgb = _sig(gate_b)
        o = jnp.where(ctx_tile, oc_ref[...], ol_ref[...]).T
        do = dog * (gate_b * sgb)
        dot_ref[...] = do.T.astype(BF16)
        dgb_ref[...] = (dog * o * (sgb * (1.0 + gate_b * (1.0 - sgb)))).astype(BF16)
        dlt_ref[...] = _split_dot(do * o, es_ref[...]).T

        y2 = y2_ref[...]
        xc = y2 - _mean(y2)
        rstd = lax.rsqrt(_mean(xc * xc) + EPS)
        xhat = xc * rstd
        lg = lg_ref[...]
        y3 = xhat * lg + lb_ref[...]
        s3 = _sig(y3)
        gate_a = gta_ref[...]
        sga = _sig(gate_a)
        dgta_ref[...] = (dy5 * (y3 * s3) * (sga * (1.0 + gate_a * (1.0 - sga)))).astype(BF16)
        dy3 = dy5 * (gate_a * sga) * (s3 * (1.0 + y3 * (1.0 - s3)))
        acc_ref[3:4, :] += _colsum(dy3 * xhat)
        acc_ref[4:5, :] += _colsum(dy3)
        dxh = dy3 * lg
        dy2 = rstd * (dxh - _mean(dxh) - xhat * _mean(dxh * xhat))
        dy2_ref[...] = dy2
        acc_ref[5:6, :] += _colsum(dy2)

    f32o = jax.ShapeDtypeStruct((na, d), F32)
    bfo = jax.ShapeDtypeStruct((na, d), BF16)
    r = _rows(tm, d)
    ol, oc = _split_cols(d, tm, n_lat)
    return _pcall(
        body, name=name, grid=(na // tm,),
        in_specs=[r, r, r, r, ol, oc, _rows(tm, d, 4), _rows(tm, d, 5), _rows(tm, d, 6), _rows(tm, d, 2), r,
                  _full((8, 3 * d)), _full((1, d)), _full((1, d)), _full((1, d)),
                  _full((d, d)), _full((d, d)), _full((d, d)), _full((d, 128))],
        out_specs=[r] * 4 + [_cols(d, tm)] + [r] * 4 + [_cols(128, tm), _full((8, d))],
        out_shape=[bfo] * 4 + [jax.ShapeDtypeStruct((d, na), BF16)] + [bfo] * 3
        + [f32o, jax.ShapeDtypeStruct((128, na), F32), jax.ShapeDtypeStruct((8, d), F32)],
        compiler_params=_cp("arbitrary"),
    )(dxn, out, yc, ya, ot_lat, ot_ctx, p, p, p, p, y2, modv, gpost, ln_g, ln_b, wo_t, wc_t, wa_t, esel)


def _attn_bwd(qt_hm, k_hm, kt_hm, v_hm, dot_hm, lse_r, dl_r, na, tq, tk, cw, name):
    h, hd, _ = qt_hm.shape
    kv, nkeys, _ = k_hm.shape
    nq, nk = na // tq, nkeys // tk
    nsub = tq // cw
    chains = [(a, u) for a in range(GROUP) for u in range(nsub)]

    def body(qt_ref, k_ref, kt_ref, v_ref, dot_ref, lse_ref, dl_ref, dq_hbm, dk_ref, dv_ref,
             dq_acc, dk_acc, dv_acc, sem):
        g = pl.program_id(0)
        j = pl.program_id(1)
        i = pl.program_id(2)

        @pl.when(jnp.logical_and(j == 0, i == 0))
        def _():
            dq_acc[...] = jnp.zeros_like(dq_acc)

        @pl.when(i == 0)
        def _():
            dk_acc[...] = jnp.zeros_like(dk_acc)
            dv_acc[...] = jnp.zeros_like(dv_acc)

        k = k_ref[...]
        v = v_ref[...]
        kt = kt_ref[...]

        def products(n):
            a, u = chains[n]
            cols = slice(u * cw, (u + 1) * cw)
            return _dot(k, qt_ref[a, :, cols]), _dot(v, dot_ref[a, :, cols])

        def accumulate(done):
            a, u, dv_t, dk_t, dq_t = done
            dv_acc[...] += dv_t
            dk_acc[...] += dk_t
            at = pl.multiple_of(i * tq + u * cw, cw)
            dq_acc[a, :, pl.ds(at, cw)] += dq_t

        tiles = {n: products(n) for n in range(min(ATTN_BWD_AHEAD, len(chains)))}
        pending = None
        for n, (a, u) in enumerate(chains):
            cols = slice(u * cw, (u + 1) * cw)
            s_t, dp_t = tiles.pop(n)
            p_t = jnp.exp2(s_t - lse_ref[a, :, cols])
            ds_b = (p_t * (dp_t - dl_ref[a, :, cols])).astype(BF16)
            p_b = p_t.astype(BF16)
            dv_t = _dot_nt(dot_ref[a, :, cols], p_b)
            dk_t = _dot_nt(qt_ref[a, :, cols], ds_b)
            dq_t = _dot(kt, ds_b)
            if pending is not None:
                accumulate(pending)
            pending = (a, u, dv_t, dk_t, dq_t)
            if n + ATTN_BWD_AHEAD < len(chains):
                tiles[n + ATTN_BWD_AHEAD] = products(n + ATTN_BWD_AHEAD)
        accumulate(pending)

        @pl.when(i == nq - 1)
        def _():
            dk_ref[...] = dk_acc[...]
            dv_ref[...] = dv_acc[...]

        @pl.when(jnp.logical_and(j == nk - 1, i == nq - 1))
        def _():
            cp = pltpu.make_async_copy(dq_acc, dq_hbm.at[pl.ds(g * GROUP, GROUP)], sem)
            cp.start()
            cp.wait()

    qtspec = pl.BlockSpec((GROUP, hd, tq), lambda g, j, i: (g, 0, i))
    kspec = pl.BlockSpec((None, tk, hd), lambda g, j, i: (g, j, 0))
    ktspec = pl.BlockSpec((None, hd, tk), lambda g, j, i: (g, 0, j))
    rspec = pl.BlockSpec((GROUP, 1, tq), lambda g, j, i: (g, 0, i))
    return _pcall(
        body, name=name, grid=(kv, nk, nq),
        in_specs=[qtspec, kspec, ktspec, kspec, qtspec, rspec, rspec],
        out_specs=[ANY, ktspec, ktspec],
        out_shape=[jax.ShapeDtypeStruct((h, hd, na), F32), jax.ShapeDtypeStruct((kv, hd, nkeys), F32),
                   jax.ShapeDtypeStruct((kv, hd, nkeys), F32)],
        scratch_shapes=[pltpu.VMEM((GROUP, hd, na), F32), pltpu.VMEM((hd, tk), F32), pltpu.VMEM((hd, tk), F32),
                        pltpu.SemaphoreType.DMA],
        compiler_params=_cp("arbitrary", "arbitrary", "arbitrary"),
    )(qt_hm, k_hm, kt_hm, v_hm, dot_hm, lse_r, dl_r)


def _qknorm_bwd(dqt_lat, dqt_ctx, dkt, dkt_ctx, dvt, dvt_ctx, p, cos_t, sin_t, gq, gk, bd, n_lat, tm, name):
    na = p.shape[0]
    d = gq.shape[1]
    kw = d // GROUP

    def body(dql_ref, dqc_ref, dkl_ref, dkc_ref, dvl_ref, dvc_ref, q_ref, kv_ref, cos_ref, sin_ref, gq_ref, gk_ref,
             bd_ref, dqo_ref, dkvo_ref, acc_ref):
        is_ctx = pl.program_id(0) >= n_lat // tm

        @pl.when(pl.program_id(0) == 0)
        def _():
            acc_ref[...] = jnp.zeros_like(acc_ref)

        dq_in = jnp.where(is_ctx, dqc_ref[...], dql_ref[...]).T
        dk_in = (dkl_ref[...] + jnp.where(is_ctx, dkc_ref[...], 0.0)).T
        dv_in = (dvl_ref[...] + jnp.where(is_ctx, dvc_ref[...], 0.0)).T

        cos = cos_ref[...]
        sin = sin_ref[...]

        def back(dy, xh, g, w):
            bdw = bd_ref[0:w, 0:w]
            dn = dy * _lane_tile(cos, w) - _partner(dy) * _lane_tile(sin, w)
            rs = lax.rsqrt(_split_dot(xh * xh, bdw) * (1.0 / HEAD_DIM) + EPS)
            y = xh * rs
            dg = _colsum(dn * y)
            dyn = dn * g
            dx = rs * (dyn - y * (_split_dot(dyn * y, bdw) * (1.0 / HEAD_DIM)))
            return dx, dg

        dq, dgq = back(dq_in * ATTN_SCALE, q_ref[...], gq_ref[...], d)
        dqo_ref[...] = dq.astype(BF16)
        acc_ref[0:1, :] += dgq
        kv = kv_ref[...]
        dk, dgk = back(dk_in * (1.0 / LOG2E), kv[:, 0:kw], gk_ref[...], kw)
        acc_ref[1:2, 0:kw] += dgk
        dkvo_ref[:, 0:kw] = dk.astype(BF16)
        dkvo_ref[:, kw:2 * kw] = dv_in.astype(BF16)

    dql, dqc = _split_cols(d, tm, n_lat)
    _, kvc = _split_cols(kw, tm, n_lat)
    return _pcall(
        body, name=name, grid=(na // tm,),
        in_specs=[dql, dqc, _cols(kw, tm), kvc, _cols(kw, tm), kvc, _rows(tm, d, 3), _rows(tm, d // 2, 14),
                  _rows(tm, 128), _rows(tm, 128), _full((1, d)), _full((1, kw)), _full((d, d))],
        out_specs=[_rows(tm, d), _rows(tm, d // 2), _full((8, d))],
        out_shape=[jax.ShapeDtypeStruct((na, d), BF16), jax.ShapeDtypeStruct((na, d // 2), BF16),
                   jax.ShapeDtypeStruct((8, d), F32)],
        compiler_params=_cp("arbitrary"))(dqt_lat, dqt_ctx, dkt, dkt_ctx, dvt, dvt_ctx, p, p, cos_t, sin_t, gq, gk, bd)


def _conv_bwd(dy2, p, conv_w, n_lat, tm, name):
    na = p.shape[0]
    ktaps, d = conv_w.shape
    pad = ktaps // 2

    def body(dy_ref, dyp_ref, dyn_ref, a_ref, ap_ref, an_ref, g_ref, gp_ref, gn_ref, w_ref,
             da_ref, dg_ref, dw_ref, dwin, ywin, dy1_s, part):
        i = pl.program_id(0)

        @pl.when(i == 0)
        def _():
            part[...] = jnp.zeros_like(part)

        first, last = _seq_ends(i, n_lat, na, tm)
        a = a_ref[...]
        sg = _sig(g_ref[...])
        _window(dwin, dyp_ref[...], dy_ref[...], dyn_ref[...], first, last, tm)
        _window(ywin, ap_ref[...] * _sig(gp_ref[...]), a * sg, an_ref[...] * _sig(gn_ref[...]), first, last, tm)

        def chunk(c, carry):
            r = pl.multiple_of(c * CONV_ROWS, CONV_ROWS)
            for lanes in _lane_blocks(d):
                dy = dy_ref[pl.ds(r, CONV_ROWS), lanes]
                acc = jnp.zeros((CONV_ROWS, 128), F32)
                for off, sh in _sublane_shifts(dwin[pl.ds(r, CONV_ROWS + 2 * HALO), lanes]):
                    k = HALO + pad - off
                    if 0 <= k < ktaps:
                        acc = acc + w_ref[k:k + 1, lanes] * sh
                dy1_s[pl.ds(r, CONV_ROWS), lanes] = acc
                for off, sh in _sublane_shifts(ywin[pl.ds(r, CONV_ROWS + 2 * HALO), lanes]):
                    k = off - (HALO - pad)
                    if 0 <= k < ktaps:
                        part[k, :, lanes] += jnp.sum((dy * sh).reshape(CONV_ROWS // 8, 8, 128), axis=0)
            return carry

        lax.fori_loop(0, tm // CONV_ROWS, chunk, 0)
        dy1 = dy1_s[...]
        da_ref[...] = (dy1 * sg).astype(BF16)
        dg_ref[...] = (dy1 * a * sg * (1.0 - sg)).astype(BF16)

        @pl.when(i == na // tm - 1)
        def _():
            dw_ref[...] = jnp.sum(part[...], axis=1)

    dyp, dyn = _halo_specs(tm, d, na, 0)
    ap, an = _halo_specs(tm, d, na, 0)
    gp, gn = _halo_specs(tm, d, na, 1)
    bfo = jax.ShapeDtypeStruct((na, d), BF16)
    return _pcall(
        body, name=name, grid=(na // tm,),
        in_specs=[_rows(tm, d), dyp, dyn, _rows(tm, d, 0), ap, an, _rows(tm, d, 1), gp, gn, _full((ktaps, d))],
        out_specs=[_rows(tm, d), _rows(tm, d), _full((ktaps, d))],
        out_shape=[bfo, bfo, jax.ShapeDtypeStruct((ktaps, d), F32)],
        scratch_shapes=[pltpu.VMEM((tm + 2 * HALO, d), F32), pltpu.VMEM((tm + 2 * HALO, d), F32),
                        pltpu.VMEM((tm, d), F32), pltpu.VMEM((ktaps, 8, d), F32)],
        compiler_params=_cp("arbitrary"))(dy2, dy2, dy2, p, p, p, p, p, p, conv_w)


def _inproj_bwd(segs, dkv, xa, dxn, modv, gpre, wp_t, n_lat, out_rows, tm, name):
    na, d = xa.shape
    nseg = len(segs)
    wrows = wp_t.shape[0]

    def body(*refs):
        seg_refs = refs[:nseg]
        dkv_ref, x_ref, dxn_ref, mod_ref, g_ref, wt_hbm, dx_ref, acc_ref, wt, sem = refs[nseg:]
        i = pl.program_id(0)
        ctx_tile = i >= n_lat // tm

        @pl.when(i == 0)
        def _():
            cp = pltpu.make_async_copy(wt_hbm, wt, sem)
            cp.start()
            cp.wait()
            acc_ref[...] = jnp.zeros_like(acc_ref)

        dh = _dot(dkv_ref[...], wt[nseg * d:wrows, :])
        for s in range(nseg):
            dh = dh + _dot(seg_refs[s][...], wt[s * d:(s + 1) * d, :])
        x = x_ref[...]
        r = lax.rsqrt(_mean(x * x) + EPS)
        xn = x * r
        g = g_ref[...]
        sc1 = 1.0 + _seg_rows(mod_ref, d, 2 * d, ctx_tile)
        dsh = _colsum(dh)
        dsc = _colsum(dh * (xn * g))

        @pl.when(ctx_tile)
        def _():
            acc_ref[1:2, :] += dsh
            acc_ref[3:4, :] += dsc

        @pl.when(jnp.logical_not(ctx_tile))
        def _():
            acc_ref[0:1, :] += dsh
            acc_ref[2:3, :] += dsc

        acc_ref[4:5, :] += _colsum(dh * xn * sc1)
        dxh = dh * g * sc1

        @pl.when(i < out_tiles)
        def _():
            dx_ref[...] = dxn_ref[...] + r * (dxh - xn * _mean(dxh * xn))

    r_ = _rows(tm, d)
    out_tiles = out_rows // tm
    return _pcall(
        body, name=name, grid=(na // tm,),
        in_specs=[r_] * nseg + [_rows(tm, d // 2), r_, r_, _full((8, 3 * d)), _full((1, d)), ANY],
        out_specs=[pl.BlockSpec((tm, d), lambda i: (jnp.minimum(i, out_tiles - 1), 0)), _full((8, d))],
        out_shape=[jax.ShapeDtypeStruct((out_rows, d), F32), jax.ShapeDtypeStruct((8, d), F32)],
        scratch_shapes=[pltpu.VMEM(wp_t.shape, BF16), pltpu.SemaphoreType.DMA],
        compiler_params=_cp("arbitrary"))(*segs, dkv, xa, dxn, modv, gpre, wp_t)


def _grad_matmul(a_t, b, tk, name):
    ka, na = a_t.shape
    nb = b.shape[1]
    tn = min(nb, 1024)

    def body(a_ref, b_ref, o_ref):
        @pl.when(pl.program_id(1) == 0)
        def _():
            o_ref[...] = jnp.zeros_like(o_ref)

        o_ref[...] += _dot(a_ref[...], b_ref[...])

    return _pcall(
        body, name=name, grid=(nb // tn, na // tk),
        in_specs=[pl.BlockSpec((ka, tk), lambda n, k: (0, k)), pl.BlockSpec((tk, tn), lambda n, k: (k, n))],
        out_specs=pl.BlockSpec((ka, tn), lambda n, k: (0, n)),
        out_shape=jax.ShapeDtypeStruct((ka, nb), F32),
        compiler_params=_cp("parallel", "arbitrary"))(a_t, b)


def _pack(parts, cols, row_mult):
    flat = jnp.concatenate([q.astype(F32).reshape(-1) for q in parts])
    rows = -(-flat.shape[0] // (cols * row_mult)) * row_mult
    return jnp.pad(flat, (0, rows * cols - flat.shape[0])).reshape(1, rows, cols)


def _unpack(flat, shapes):
    out, off = [], 0
    for s in shapes:
        n = math.prod(s)
        out.append(flat[off:off + n].reshape(tuple(s)))
        off += n
    return out


def _cols_by_dest(g):
    l, a, w8 = g.shape
    return g.reshape(l, a, N_DEV, w8 // N_DEV).transpose(2, 0, 1, 3)


def _rows_by_dest(g):
    l, r8, b = g.shape
    return g.reshape(l, N_DEV, r8 // N_DEV, b).transpose(1, 0, 2, 3)


def _cols_from_src(s):
    n, l, a, w = s.shape
    return s.transpose(1, 2, 0, 3).reshape(l, a, n * w)


def _rows_from_src(s):
    n, l, r, b = s.shape
    return s.transpose(1, 0, 2, 3).reshape(l, n * r, b)


def _rope_tables(n_lat, n_ctx):
    half = HEAD_DIM // 2
    rows = n_lat // GRID_W
    row = jnp.repeat(jnp.arange(rows, dtype=F32), GRID_W)
    col = jnp.tile(jnp.arange(GRID_W, dtype=F32), rows)
    inv_freq = ROPE_THETA ** (-jnp.arange(0, half, 2, dtype=F32) / half)
    ang = jnp.concatenate([row[:, None] * inv_freq, col[:, None] * inv_freq], axis=-1)
    cos, sin = jnp.cos(ang), jnp.sin(ang)
    cos_t = jnp.concatenate([jnp.tile(cos, (1, 4)), jnp.ones((n_ctx, 128), F32)], axis=0)
    sin_t = jnp.concatenate([jnp.tile(jnp.concatenate([-sin, sin], axis=-1), (1, 2)),
                             jnp.zeros((n_ctx, 128), F32)], axis=0)
    return cos_t, sin_t


def _to_heads(t):
    na, w = t.shape
    return t.reshape(na, w // HEAD_DIM, HEAD_DIM).transpose(1, 0, 2)


def kernel(x, c, ctx, c_ctx, w_mod, b_mod, g_pre, g_post, w_in, conv_w, conv_b, ln_g, ln_b, w_conv_out, q_norm_g, k_norm_g, w_attn_out, w_out, loss_target, m_c_ctx, m_w_mod, m_b_mod, m_g_pre, m_g_post, m_w_in, m_conv_w, m_conv_b, m_ln_g, m_ln_b, m_w_conv_out, m_q_norm_g, m_k_norm_g, m_w_attn_out, m_w_out, v_c_ctx, v_w_mod, v_b_mod, v_g_pre, v_g_post, v_w_in, v_conv_w, v_conv_b, v_ln_g, v_ln_b, v_w_conv_out, v_q_norm_g, v_k_norm_g, v_w_attn_out, v_w_out):
    depth, d, _ = w_mod.shape
    n_lat, n_ctx = x.shape[1], ctx.shape[1]
    na = n_lat + n_ctx
    heads = d // HEAD_DIM
    kvh = heads // GROUP
    kw = d // GROUP
    ktaps = conv_w.shape[1]
    tm = n_ctx
    tm_half = tm // 2
    tbig = 3 * tm if na % (3 * tm) == 0 else tm
    tq_lat = 4 * tm if n_lat % (4 * tm) == 0 else tm
    tq_bwd = 4 * tm if n_lat % (4 * tm) == 0 else tm

    def layer_shards(l):
        return [w_mod[l].astype(BF16), w_in[l].astype(BF16), w_conv_out[l].astype(BF16), w_attn_out[l].astype(BF16),
                w_out[l].astype(BF16), conv_w[l]]

    def full_weights(gathered):
        s_mod, s_in, s_co, s_ao, s_oo, s_cw = [q[:, None] for q in gathered]
        win = _cols_from_src(s_in)[0]
        wp = jnp.concatenate([win[:, :4 * d], win[:, 4 * d + 2 * kw:], win[:, 4 * d:4 * d + 2 * kw]], axis=1)
        return dict(wmod=_cols_from_src(s_mod)[0], wp=wp, wc=_rows_from_src(s_co)[0], wa=_rows_from_src(s_ao)[0],
                    wo=_rows_from_src(s_oo)[0], convw=_cols_from_src(s_cw)[0])

    weights = [full_weights(_all_gather(layer_shards(0), "gather_weights_l0"))]

    cos_t, sin_t = _rope_tables(n_lat, n_ctx)
    lane = jnp.arange(d)
    bd = (lane[:, None] // HEAD_DIM == lane[None, :] // HEAD_DIM).astype(BF16)
    esel = (lane[:, None] // HEAD_DIM == jnp.arange(128)[None, :]).astype(BF16)
    cvec = jnp.zeros((8, d), F32).at[0].set(c[0]).at[1].set(c_ctx)
    cvec_t = jnp.zeros((d, 128), F32).at[:, 0].set(c[0]).at[:, 1].set(c_ctx)

    xa = jnp.concatenate([x[0], ctx[0]], axis=0)

    saved = []
    for l in range(depth):
        tag = f"_l{l}"
        gq = jnp.tile(q_norm_g[l], heads)[None, :]
        gk = jnp.tile(k_norm_g[l], heads // GROUP)[None, :]
        wl = weights[l]
        modv = _mod_fwd(cvec, wl["wmod"], b_mod[l][None, :], "mod_fwd" + tag)
        p, h_t, *gathered = _inproj(xa, modv, g_pre[l][None, :], wl["wp"], n_lat, tbig, "inproj" + tag,
                                    shards=layer_shards(l + 1) if l + 1 < depth else ())
        if gathered:
            weights.append(full_weights(gathered))
        q_t, kr, vb, k_t, v_t = _qknorm_fwd(p, cos_t, sin_t, gq, gk, bd, tm, "qknorm_fwd" + tag)
        qt_hm = q_t.reshape(heads, HEAD_DIM, na)
        kt_hm = k_t.reshape(kvh, HEAD_DIM, na)
        k_hm, v_hm = _to_heads(kr), _to_heads(vb)
        vt_ones = jnp.concatenate([v_t.reshape(kvh, HEAD_DIM, na), jnp.ones((kvh, 16, na), BF16)], axis=1)
        ot_lat, lse_lat = _attn_fwd(qt_hm, k_hm, vt_ones, n_lat, tq_lat, tbig, tm, "attn_fwd" + tag)
        ot_ctx, lse_ctx = _attn_fwd(qt_hm[:, :, n_lat:], k_hm[:, n_lat:], vt_ones[:, :, n_lat:], n_ctx, tm, tm, tm,
                                    "attn_ctx_fwd" + tag)
        ot_lat, ot_ctx = ot_lat.reshape(d, n_lat), ot_ctx.reshape(d, n_ctx)
        y5, y5_t, y2 = _conv_fwd(p, wl["convw"], conv_b[l][None, :], ln_g[l][None, :], ln_b[l][None, :], n_lat, tm,
                                 "conv_fwd" + tag)
        xa_new, yc, ya, out, z_t, og_t = _merge_fwd(y5, ot_lat, ot_ctx, p, xa, modv, g_post[l][None, :], wl["wc"],
                                                    wl["wa"], wl["wo"], n_lat, tm, "merge_fwd" + tag)
        saved.append(dict(xa=xa, modv=modv, p=p, h_t=h_t, qt_hm=qt_hm, k_hm=k_hm, kt_hm=kt_hm, v_hm=v_hm,
                          ot_lat=ot_lat, ot_ctx=ot_ctx, lse_lat=lse_lat, lse_ctx=lse_ctx,
                          y5_t=y5_t, y2=y2, yc=yc, ya=ya, out=out, z_t=z_t, og_t=og_t, gq=gq, gk=gk))
        xa = xa_new

    dxa, loss_blk = _loss_grad(xa, loss_target[0], n_lat, tm, "loss_grad")

    g_wmod, g_win, g_convw, g_wc, g_wa, g_wo = [], [], [], [], [], []
    g_bmod, g_gpre, g_gpost, g_convb, g_lng, g_lnb, g_qg, g_kg = [], [], [], [], [], [], [], []
    g_cctx = jnp.zeros((d,), F32)
    for l in reversed(range(depth)):
        tag = f"_l{l}"
        s = saved[l]
        wl = weights[l]
        p = s["p"]
        (dgb, dma, dmb, dgta, do_t, dout, dyc, dya, dy2, dl_t, acc_m) = _merge_bwd(
            dxa, s["out"], s["yc"], s["ya"], s["ot_lat"], s["ot_ctx"], p, s["y2"], s["modv"], g_post[l][None, :],
            ln_g[l][None, :], ln_b[l][None, :], wl["wo"].T, wl["wc"].T, wl["wa"].T, esel, n_lat, tm_half,
            "merge_bwd" + tag)
        dl_r = dl_t[:heads].reshape(heads, 1, na)
        dot_hm = do_t.reshape(heads, HEAD_DIM, na)
        qt_hm, k_hm, kt_hm, v_hm = s["qt_hm"], s["k_hm"], s["kt_hm"], s["v_hm"]
        dqt_lat, dkt_hm, dvt_hm = _attn_bwd(qt_hm, k_hm, kt_hm, v_hm, dot_hm, s["lse_lat"], dl_r,
                                            n_lat, tq_bwd, tm, tm, "attn_bwd" + tag)
        dqt_ctx, dkt_ctx, dvt_ctx = _attn_bwd(
            qt_hm[:, :, n_lat:], k_hm[:, n_lat:], kt_hm[:, :, n_lat:], v_hm[:, n_lat:], dot_hm[:, :, n_lat:],
            s["lse_ctx"], dl_r[:, :, n_lat:], n_ctx, tm, tm, tm, "attn_ctx_bwd" + tag)
        dq, dkv, acc_q = _qknorm_bwd(
            dqt_lat.reshape(d, n_lat), dqt_ctx.reshape(d, n_ctx), dkt_hm.reshape(kw, na), dkt_ctx.reshape(kw, n_ctx),
            dvt_hm.reshape(kw, na), dvt_ctx.reshape(kw, n_ctx), p, cos_t, sin_t, s["gq"], s["gk"], bd, n_lat, tm,
            "qknorm_bwd" + tag)
        da, dg, dconvw = _conv_bwd(dy2, p, wl["convw"], n_lat, tm, "conv_bwd" + tag)
        segs = [da, dg, dgta, dq, dgb, dma, dmb]
        dxa, acc_h = _inproj_bwd(segs, dkv, s["xa"], dxa, s["modv"], g_pre[l][None, :], wl["wp"].T, n_lat,
                                 na if l else n_lat, tm, "inproj_bwd" + tag)

        dwp = [_grad_matmul(s["h_t"], sg, tbig, f"grad_w_in{k}" + tag) for k, sg in enumerate(segs + [dkv])]
        g_win.append(jnp.concatenate(dwp[:4] + [dwp[7]] + dwp[4:7], axis=1))
        g_wc.append(_grad_matmul(s["y5_t"], dyc, tbig, "grad_w_conv_out" + tag))
        g_wa.append(_grad_matmul(s["og_t"], dya, tbig, "grad_w_attn_out" + tag))
        g_wo.append(_grad_matmul(s["z_t"], dout, tbig, "grad_w_out" + tag))
        g_convw.append(dconvw)

        dmod = jnp.zeros((8, 3 * d), F32)
        dmod = dmod.at[0].set(jnp.concatenate([acc_h[0], acc_h[2], acc_m[0]]))
        dmod = dmod.at[1].set(jnp.concatenate([acc_h[1], acc_h[3], acc_m[1]]))
        dwm, dbm, dcv = _mod_bwd(dmod, cvec, cvec_t, wl["wmod"], "mod_bwd" + tag)
        g_wmod.append(dwm)
        g_bmod.append(dbm[0])
        g_cctx = g_cctx + dcv[1]
        g_gpre.append(acc_h[4])
        g_gpost.append(acc_m[2])
        g_lng.append(acc_m[3])
        g_lnb.append(acc_m[4])
        g_convb.append(acc_m[5])
        g_qg.append(acc_q[0].reshape(heads, HEAD_DIM).sum(0))
        g_kg.append(acc_q[1, :kw].reshape(heads // GROUP, HEAD_DIM).sum(0))

    grad_x = dxa[None]

    def stack(lst):
        return jnp.stack(lst[::-1])

    big_names = ["w_mod", "w_in", "w_conv_out", "w_attn_out", "w_out", "conv_w"]
    big_w = dict(w_mod=w_mod, w_in=w_in, w_conv_out=w_conv_out, w_attn_out=w_attn_out, w_out=w_out, conv_w=conv_w)
    big_m = dict(w_mod=m_w_mod, w_in=m_w_in, w_conv_out=m_w_conv_out, w_attn_out=m_w_attn_out, w_out=m_w_out,
                 conv_w=m_conv_w)
    big_v = dict(w_mod=v_w_mod, w_in=v_w_in, w_conv_out=v_w_conv_out, w_attn_out=v_w_attn_out, w_out=v_w_out,
                 conv_w=v_conv_w)
    by_dest = [_cols_by_dest(stack(g_wmod)), _cols_by_dest(stack(g_win)), _rows_by_dest(stack(g_wc)),
               _rows_by_dest(stack(g_wa)), _rows_by_dest(stack(g_wo)), _cols_by_dest(stack(g_convw))]
    send = [q.reshape((4, 2) + q.shape[1:]) for q in by_dest]
    from_sibling = _swap_with_sibling(send, "reduce_sibling")
    chip_part = [_add_sibling_part(sb, rc, BF16, "reduce_sibling_add_" + n)
                 for n, sb, rc in zip(big_names, send, from_sibling)]
    from_chips = _exchange_chips(chip_part, "reduce_chips")
    big_g, big_d, big_nm, big_nv = {}, {}, {}, {}
    for n, st in zip(big_names, from_chips):
        big_g[n], big_d[n], big_nm[n], big_nv[n] = _sum_adamw(st, big_w[n], big_m[n], big_v[n], "adamw_" + n)

    small_names = ["c_ctx", "b_mod", "g_pre", "g_post", "conv_b", "ln_g", "ln_b", "q_norm_g", "k_norm_g", "loss"]
    zero1 = jnp.zeros((1,), F32)
    small_w = dict(c_ctx=c_ctx, b_mod=b_mod, g_pre=g_pre, g_post=g_post, conv_b=conv_b, ln_g=ln_g, ln_b=ln_b,
                   q_norm_g=q_norm_g, k_norm_g=k_norm_g, loss=zero1)
    small_m = dict(c_ctx=m_c_ctx, b_mod=m_b_mod, g_pre=m_g_pre, g_post=m_g_post, conv_b=m_conv_b, ln_g=m_ln_g,
                   ln_b=m_ln_b, q_norm_g=m_q_norm_g, k_norm_g=m_k_norm_g, loss=zero1)
    small_v = dict(c_ctx=v_c_ctx, b_mod=v_b_mod, g_pre=v_g_pre, g_post=v_g_post, conv_b=v_conv_b, ln_g=v_ln_g,
                   ln_b=v_ln_b, q_norm_g=v_q_norm_g, k_norm_g=v_k_norm_g, loss=zero1)
    small_g = dict(c_ctx=g_cctx, b_mod=stack(g_bmod), g_pre=stack(g_gpre), g_post=stack(g_gpost),
                   conv_b=stack(g_convb), ln_g=stack(g_lng), ln_b=stack(g_lnb), q_norm_g=stack(g_qg),
                   k_norm_g=stack(g_kg), loss=loss_blk[0, 0:1])
    small_shapes = [small_w[n].shape for n in small_names]

    def pack_small(tree):
        return _pack([tree[n] for n in small_names], d, 8)

    small_parts, = _all_gather([pack_small(small_g)], "gather_small_grads")
    small_out = _sum_adamw(small_parts, pack_small(small_w), pack_small(small_m), pack_small(small_v),
                           "adamw_replicated")
    sm_g, sm_d, sm_nm, sm_nv = [dict(zip(small_names, _unpack(o.reshape(-1), small_shapes))) for o in small_out]
    loss = sm_g["loss"].reshape(())

    order = ["c_ctx", "w_mod", "b_mod", "g_pre", "g_post", "w_in", "conv_w", "conv_b", "ln_g", "ln_b",
             "w_conv_out", "q_norm_g", "k_norm_g", "w_attn_out", "w_out"]

    def pick(big, small):
        return [big[n] if n in big else small[n] for n in order]

    return (loss, grad_x, *pick(big_g, sm_g), *pick(big_d, sm_d), *pick(big_nm, sm_nm), *pick(big_nv, sm_nv))
```

```python
import math

import jax
import jax.numpy as jnp
from jax import lax
from jax.experimental import pallas as pl
from jax.experimental.pallas import tpu as pltpu

F32 = jnp.float32
BF16 = jnp.bfloat16

HEAD_DIM = 64
GROUP = 4
GRID_W = 64
ROPE_THETA = 10000.0
EPS = 1e-6
ATTN_SCALE = HEAD_DIM ** -0.5
LOG2E = 1.4426950408889634
Q_PRESCALE = ATTN_SCALE * LOG2E
HALO = 16
CONV_ROWS = 64
ATTN_FWD_AHEAD = 5
ATTN_BWD_AHEAD = 3

ADAM_LR = 0.001
ADAM_B1 = 0.9
ADAM_B2 = 0.999
ADAM_EPS = 1e-08
ADAM_WD = 0.01
ADAM_STEP = 10

N_DEV = 8
MESH_AXES = ("x", "y", "c")
V7X_VMEM_LIMIT = 56 * 1024 * 1024
NEG_BIG = -1e30

MESH = pl.DeviceIdType.MESH
ANY = pl.BlockSpec(memory_space=pl.ANY)


def _pcall(body, **kw):
    return pl.pallas_call(body, **kw)


def _cp(*sem):
    return pltpu.CompilerParams(dimension_semantics=sem, vmem_limit_bytes=V7X_VMEM_LIMIT)


def _sig(x):
    return 0.5 * jnp.tanh(0.5 * x) + 0.5


def _mean(x):
    return jnp.mean(x, axis=-1, keepdims=True)


def _colsum(x):
    return jnp.sum(x, axis=0, keepdims=True)


def _bf_round(x):
    return x.astype(BF16).astype(F32)


def _dot(a, b):
    return jnp.dot(a, b, preferred_element_type=F32)


def _dot_nt(a, b):
    return lax.dot_general(a, b, (((1,), (1,)), ((), ())), preferred_element_type=F32)


def _split_dot(x, m):
    hi = x.astype(BF16)
    lo = (x - hi.astype(F32)).astype(BF16)
    return _dot(hi, m) + _dot(lo, m)


def _full(shape):
    nd = len(shape)
    return pl.BlockSpec(shape, lambda *_: (0,) * nd)


def _rows(tm, width, colblk=0):
    return pl.BlockSpec((tm, width), lambda i: (i, colblk))


def _cols(height, tm):
    return pl.BlockSpec((height, tm), lambda i: (0, i))


def _split_cols(height, tm, n_lat):
    nl = n_lat // tm
    return (pl.BlockSpec((height, tm), lambda i: (0, jnp.minimum(i, nl - 1))),
            pl.BlockSpec((height, tm), lambda i: (0, jnp.maximum(i - nl, 0))))


def _my_place():
    return lax.axis_index("x"), lax.axis_index("y"), lax.axis_index("c")


def _sem_arrays(n):
    return [pltpu.SemaphoreType.DMA((n,)), pltpu.SemaphoreType.DMA((n,))]


def _gather_steps(x_refs, out_refs, send_sems, recv_sems, local_sems):
    n = len(x_refs)
    x, y, c = _my_place()
    me, sibling = (x, y, c), (x, y, 1 - c)
    chips = [(1 - x, y), (x, 1 - y), (1 - x, 1 - y)]

    def slab(a, px, py, pc):
        return out_refs[a].at[4 * px + 2 * py + pc]

    def copies(k, block, to, from_input=False):
        return [pltpu.make_async_remote_copy(
            src_ref=x_refs[a] if from_input else slab(a, *block), dst_ref=slab(a, *block),
            send_sem=send_sems.at[k * n + a], recv_sem=recv_sems.at[k * n + a],
            device_id=to, device_id_type=MESH) for a in range(n)]

    def mine():
        return [pltpu.make_async_copy(x_refs[a], slab(a, *me), local_sems.at[a]) for a in range(n)]

    def first():
        out = copies(0, me, sibling, True)
        for j, chip in enumerate(chips):
            out += copies(1 + j, me, (*chip, c), True)
        return out

    def onward(j):
        return copies(4 + j, (*chips[j], c), sibling)

    def start():
        for cp in mine() + first():
            cp.start()

    def relay():
        for j, chip in enumerate(chips):
            for cp in copies(1 + j, (*chip, c), me):
                cp.wait_recv()
            for cp in onward(j):
                cp.start()

    def finish():
        for cp in copies(0, sibling, me):
            cp.wait_recv()
        for j, chip in enumerate(chips):
            for cp in copies(4 + j, (*chip, 1 - c), me):
                cp.wait_recv()
        for cp in first() + onward(0) + onward(1) + onward(2):
            cp.wait_send()
        for cp in mine():
            cp.wait()

    return start, relay, finish


def _gather_scratch(n):
    return _sem_arrays(7 * n) + [pltpu.SemaphoreType.DMA((n,))]


def _all_gather(shards, name):
    n = len(shards)

    def body(*refs):
        start, relay, finish = _gather_steps(refs[:n], refs[n:2 * n], *refs[2 * n:])
        start()
        relay()
        finish()

    return _pcall(
        body, name=name,
        out_shape=[jax.ShapeDtypeStruct((N_DEV,) + q.shape, q.dtype) for q in shards],
        in_specs=[ANY] * n, out_specs=[ANY] * n,
        scratch_shapes=_gather_scratch(n),
    )(*shards)


def _swap_steps(buf_refs, recv_refs, send_sems, recv_sems):
    n = len(buf_refs)
    x, y, c = _my_place()

    def copies():
        return [pltpu.make_async_remote_copy(
            src_ref=buf_refs[a].at[k, 1 - c], dst_ref=recv_refs[a].at[k],
            send_sem=send_sems.at[k * n + a], recv_sem=recv_sems.at[k * n + a],
            device_id=(x, y, 1 - c), device_id_type=MESH) for k in range(4) for a in range(n)]

    def start():
        for cp in copies():
            cp.start()

    def finish():
        for cp in copies():
            cp.wait()

    return start, finish


def _exchange_steps(s_refs, recv_refs, send_sems, recv_sems, local_sems):
    n = len(s_refs)
    x, y, c = _my_place()
    mychip = 2 * x + y
    chips = [(1 - x, y), (x, 1 - y), (1 - x, 1 - y)]

    def copies():
        mine = [pltpu.make_async_copy(s_refs[a].at[mychip], recv_refs[a].at[mychip], local_sems.at[a])
                for a in range(n)]
        return mine + [pltpu.make_async_remote_copy(
            src_ref=s_refs[a].at[2 * px + py], dst_ref=recv_refs[a].at[mychip],
            send_sem=send_sems.at[j * n + a], recv_sem=recv_sems.at[j * n + a],
            device_id=(px, py, c), device_id_type=MESH) for j, (px, py) in enumerate(chips) for a in range(n)]

    def start():
        for cp in copies():
            cp.start()

    def finish():
        for cp in copies():
            cp.wait()

    return start, finish


def _swap_rider(bufs):
    n = len(bufs)
    return dict(steps=_swap_steps, ins=list(bufs), scratch=_sem_arrays(4 * n),
                outs=[jax.ShapeDtypeStruct((4,) + q.shape[2:], q.dtype) for q in bufs])


def _exchange_rider(parts):
    n = len(parts)
    return dict(steps=_exchange_steps, ins=list(parts), scratch=_sem_arrays(3 * n) + [pltpu.SemaphoreType.DMA((n,))],
                outs=[jax.ShapeDtypeStruct(q.shape, q.dtype) for q in parts])


def _run_rider(rider, name):
    n = len(rider["ins"])

    def body(*refs):
        start, finish = rider["steps"](refs[:n], refs[n:2 * n], *refs[2 * n:])
        start()
        finish()

    return _pcall(body, name=name, out_shape=rider["outs"], in_specs=[ANY] * n, out_specs=[ANY] * n,
                  scratch_shapes=rider["scratch"])(*rider["ins"])


def _ride(body, rider, n_in, n_out, nsteps):
    if rider is None:
        return body
    n = len(rider["ins"])
    ns = len(rider["scratch"])

    def wrapped(*refs):
        ins, r_in = refs[:n_in], refs[n_in:n_in + n]
        outs, r_out = refs[n_in + n:n_in + n + n_out], refs[n_in + n + n_out:n_in + 2 * n + n_out]
        rest = refs[n_in + 2 * n + n_out:]
        scratch, r_scratch = rest[:len(rest) - ns], rest[len(rest) - ns:]
        start, finish = rider["steps"](r_in, r_out, *r_scratch)
        pl.when(pl.program_id(0) == 0)(start)
        body(*ins, *outs, *scratch)
        pl.when(pl.program_id(0) == nsteps - 1)(finish)

    return wrapped


def _row_tile(a):
    for t in range(256, 7, -8):
        if a % t == 0:
            return t
    return a


def _add_sibling_part(buf, recv, out_dtype, name):
    _, _, nl, a, b = buf.shape
    ta = _row_tile(a)
    core = lax.axis_index("c").astype(jnp.int32).reshape(1)

    def body(core_ref, a_ref, b_ref, o_ref):
        o_ref[...] = (a_ref[...] + b_ref[...]).astype(o_ref.dtype)

    grid_spec = pltpu.PrefetchScalarGridSpec(
        num_scalar_prefetch=1, grid=(4, nl, a // ta),
        in_specs=[pl.BlockSpec((None, None, None, ta, b), lambda k, l, r, cr: (k, cr[0], l, r, 0)),
                  pl.BlockSpec((None, None, ta, b), lambda k, l, r, cr: (k, l, r, 0))],
        out_specs=pl.BlockSpec((None, None, ta, b), lambda k, l, r, cr: (k, l, r, 0)))
    return _pcall(body, name=name, grid_spec=grid_spec,
                  out_shape=jax.ShapeDtypeStruct((4, nl, a, b), out_dtype),
                  compiler_params=_cp("parallel", "parallel", "parallel"))(core, buf, recv)


def _sum_adamw(stack, w, m, v, name):
    ns, nl, a, b = stack.shape
    ta = _row_tile(a)
    c1 = 1.0 - ADAM_B1 ** ADAM_STEP
    c2 = 1.0 - ADAM_B2 ** ADAM_STEP

    def body(s_ref, w_ref, m_ref, v_ref, g_out, d_out, m_out, v_out):
        g = s_ref[0].astype(F32)
        for k in range(1, ns):
            g = g + s_ref[k].astype(F32)
        m_new = ADAM_B1 * m_ref[...] + (1.0 - ADAM_B1) * g
        v_new = ADAM_B2 * v_ref[...] + (1.0 - ADAM_B2) * (g * g)
        m_hat = m_new / c1
        v_hat = v_new / c2
        g_out[...] = g
        d_out[...] = -ADAM_LR * (m_hat / (jnp.sqrt(v_hat) + ADAM_EPS) + ADAM_WD * w_ref[...])
        m_out[...] = m_new
        v_out[...] = v_new

    blk = pl.BlockSpec((None, ta, b), lambda l, i: (l, i, 0))
    return _pcall(
        body, name=name, grid=(nl, a // ta),
        in_specs=[pl.BlockSpec((ns, None, ta, b), lambda l, i: (0, l, i, 0)), blk, blk, blk],
        out_specs=[blk] * 4,
        out_shape=[jax.ShapeDtypeStruct((nl, a, b), F32)] * 4,
        compiler_params=_cp("parallel", "parallel"))(stack, w, m, v)


def _mod_fwd(cvec, wmod, bmod, name):
    _, d = cvec.shape

    def body(c_ref, w_ref, b_ref, o_ref):
        cv = c_ref[...]
        cs = cv * _sig(cv)
        o_ref[...] = _dot(cs.astype(BF16), w_ref[...]) + b_ref[...]

    return _pcall(
        body, name=name, grid=(3,),
        in_specs=[_full((8, d)), pl.BlockSpec((d, d), lambda n: (0, n)), pl.BlockSpec((1, d), lambda n: (0, n))],
        out_specs=pl.BlockSpec((8, d), lambda n: (0, n)),
        out_shape=jax.ShapeDtypeStruct((8, 3 * d), F32),
        compiler_params=_cp("parallel"))(cvec, wmod, bmod)


def _mod_bwd(dmod, cvec, cvec_t, wmod, name):
    _, d = cvec.shape

    def body(dm_ref, c_ref, ct_ref, w_ref, dw_ref, db_ref, dc_ref):
        n = pl.program_id(0)
        dm = dm_ref[...]
        ct = ct_ref[...]
        cs_t = _bf_round(ct * _sig(ct))
        d0 = _bf_round(dm[0:1, :])
        d1 = _bf_round(dm[1:2, :])
        dw_ref[...] = cs_t[:, 0:1] * d0 + cs_t[:, 1:2] * d1
        db_ref[...] = dm[0:1, :] + dm[1:2, :]

        @pl.when(n == 0)
        def _():
            dc_ref[...] = jnp.zeros_like(dc_ref)

        dc_ref[...] += _dot_nt(dm.astype(BF16), w_ref[...])

        @pl.when(n == 2)
        def _():
            cv = c_ref[...]
            s = _sig(cv)
            dc_ref[...] = dc_ref[...] * (s * (1.0 + cv * (1.0 - s)))

    return _pcall(
        body, name=name, grid=(3,),
        in_specs=[pl.BlockSpec((8, d), lambda n: (0, n)), _full((8, d)), _full((d, 128)),
                  pl.BlockSpec((d, d), lambda n: (0, n))],
        out_specs=[pl.BlockSpec((d, d), lambda n: (0, n)), pl.BlockSpec((1, d), lambda n: (0, n)),
                   _full((8, d))],
        out_shape=[jax.ShapeDtypeStruct((d, 3 * d), F32), jax.ShapeDtypeStruct((1, 3 * d), F32),
                   jax.ShapeDtypeStruct((8, d), F32)],
        compiler_params=_cp("arbitrary"))(dmod, cvec, cvec_t, wmod)


def _seg_rows(mod_ref, lo, hi, is_ctx):
    return jnp.where(is_ctx, mod_ref[1:2, lo:hi], mod_ref[0:1, lo:hi])


def _inproj(xa, modv, gpre, wp, n_lat, tm, name, shards=()):
    na, d = xa.shape
    wcols = wp.shape[1]
    tn = 5 * d // 4 if (5 * d // 4) % 128 == 0 and wcols % (5 * d // 4) == 0 else d // 2
    ns = len(shards)
    ni, nj = na // tm, wcols // tn

    def body(x_ref, mod_ref, g_ref, w_ref, *rest):
        p_ref, ht_ref = rest[ns:ns + 2]
        h_s = rest[2 * ns + 2]
        i = pl.program_id(0)
        j = pl.program_id(1)

        if ns:
            start, relay, finish = _gather_steps(rest[:ns], rest[ns + 2:2 * ns + 2], *rest[2 * ns + 3:])
            pl.when(jnp.logical_and(i == 0, j == 0))(start)
            pl.when(jnp.logical_and(i == ni // 2, j == 0))(relay)

        @pl.when(j == 0)
        def _():
            x = x_ref[...]
            r = lax.rsqrt(_mean(x * x) + EPS)
            row = i * tm + lax.broadcasted_iota(jnp.int32, (tm, 1), 0)
            is_ctx = row >= n_lat
            sh = _seg_rows(mod_ref, 0, d, is_ctx)
            sc = _seg_rows(mod_ref, d, 2 * d, is_ctx)
            h = (x * r * g_ref[...]) * (1.0 + sc) + sh
            h_s[...] = h.astype(BF16)
            ht_ref[...] = h.T.astype(BF16)

        p_ref[...] = _dot(h_s[...], w_ref[...])

        if ns:
            pl.when(jnp.logical_and(i == ni - 1, j == nj - 1))(finish)

    return _pcall(
        body, name=name, grid=(ni, nj),
        in_specs=[pl.BlockSpec((tm, d), lambda i, j: (i, 0)), _full((8, 3 * d)), _full((1, d)),
                  pl.BlockSpec((d, tn), lambda i, j: (0, j))] + [ANY] * ns,
        out_specs=[pl.BlockSpec((tm, tn), lambda i, j: (i, j)), pl.BlockSpec((d, tm), lambda i, j: (0, i))]
        + [ANY] * ns,
        out_shape=[jax.ShapeDtypeStruct((na, wcols), F32), jax.ShapeDtypeStruct((d, na), BF16)]
        + [jax.ShapeDtypeStruct((N_DEV,) + q.shape, q.dtype) for q in shards],
        scratch_shapes=[pltpu.VMEM((tm, d), BF16)] + (_gather_scratch(ns) if ns else []),
        compiler_params=_cp("arbitrary" if ns else "parallel", "arbitrary"))(xa, modv, gpre, wp, *shards)


def _lane_tile(t, width):
    if width >= 128:
        return jnp.tile(t, (1, width // 128))
    return t[:, :width]


def _partner(x):
    w = x.shape[-1]
    lane = lax.broadcasted_iota(jnp.int32, x.shape, 1)
    low = (lane % HEAD_DIM) < (HEAD_DIM // 2)
    return jnp.where(low, pltpu.roll(x, w - HEAD_DIM // 2, 1), pltpu.roll(x, HEAD_DIM // 2, 1))


def _qknorm_fwd(p, cos_t, sin_t, gq, gk, bd, tm, name):
    na = p.shape[0]
    d = gq.shape[1]
    kw = d // GROUP

    def body(q_ref, kv_ref, cos_ref, sin_ref, gq_ref, gk_ref, bd_ref, qt_ref, ko_ref, vo_ref, kt_ref, vt_ref):
        cos = cos_ref[...]
        sin = sin_ref[...]

        def norm_rope(xh, g, w):
            ms = _split_dot(xh * xh, bd_ref[0:w, 0:w]) * (1.0 / HEAD_DIM)
            xn = xh * lax.rsqrt(ms + EPS) * g
            return xn * _lane_tile(cos, w) + _partner(xn) * _lane_tile(sin, w)

        qt_ref[...] = (norm_rope(q_ref[...], gq_ref[...], d) * Q_PRESCALE).T.astype(BF16)
        kv = kv_ref[...]
        k = norm_rope(kv[:, 0:kw], gk_ref[...], kw)
        v = kv[:, kw:2 * kw]
        ko_ref[...] = k.astype(BF16)
        vo_ref[...] = v.astype(BF16)
        kt_ref[...] = k.T.astype(BF16)
        vt_ref[...] = v.T.astype(BF16)

    cols = lambda w: pl.BlockSpec((w, tm), lambda i: (0, i))
    return _pcall(
        body, name=name, grid=(na // tm,),
        in_specs=[_rows(tm, d, 3), _rows(tm, d // 2, 14), _rows(tm, 128), _rows(tm, 128),
                  _full((1, d)), _full((1, kw)), _full((d, d))],
        out_specs=[cols(d), _rows(tm, kw), _rows(tm, kw), cols(kw), cols(kw)],
        out_shape=[jax.ShapeDtypeStruct((d, na), BF16), jax.ShapeDtypeStruct((na, kw), BF16),
                   jax.ShapeDtypeStruct((na, kw), BF16), jax.ShapeDtypeStruct((kw, na), BF16),
                   jax.ShapeDtypeStruct((kw, na), BF16)],
        compiler_params=_cp("parallel"))(p, p, cos_t, sin_t, gq, gk, bd)


def _attn_fwd(qt_hm, k_hm, vt_hm, na, tq, tk, cw, name):
    h, hd, _ = qt_hm.shape
    kv, nkeys, _ = k_hm.shape
    vrows = vt_hm.shape[1]
    nq, nk = na // tq, nkeys // tk
    nsub = tq // cw
    chains = [(a, u) for a in range(GROUP) for u in range(nsub)]

    def body(q_ref, k_ref, vt_ref, o_ref, lse_ref, m_s, acc_s):
        j = pl.program_id(2)

        @pl.when(j == 0)
        def _():
            m_s[...] = jnp.full_like(m_s, NEG_BIG)
            acc_s[...] = jnp.zeros_like(acc_s)

        k = k_ref[...]
        vt = vt_ref[...]

        def scores(n):
            a, u = chains[n]
            return _dot(k, q_ref[a, :, u * cw:(u + 1) * cw])

        s_tiles = {n: scores(n) for n in range(min(ATTN_FWD_AHEAD, len(chains)))}
        pending = None
        for n, (a, u) in enumerate(chains):
            cols = slice(u * cw, (u + 1) * cw)
            s_t = s_tiles.pop(n)
            m_prev = m_s[a, :, cols]
            m_new = jnp.maximum(m_prev, jnp.max(s_t, axis=0, keepdims=True))
            m_s[a, :, cols] = m_new
            pv = _dot(vt, jnp.exp2(s_t - m_new).astype(BF16))
            if pending is not None:
                pa, pcols, palpha, ppv = pending
                acc_s[pa, :, pcols] = palpha * acc_s[pa, :, pcols] + ppv
            pending = (a, cols, jnp.exp2(m_prev - m_new), pv)
            if n + ATTN_FWD_AHEAD < len(chains):
                s_tiles[n + ATTN_FWD_AHEAD] = scores(n + ATTN_FWD_AHEAD)
        pa, pcols, palpha, ppv = pending
        acc_s[pa, :, pcols] = palpha * acc_s[pa, :, pcols] + ppv

        @pl.when(j == nk - 1)
        def _():
            for a in range(GROUP):
                acc = acc_s[a]
                l = acc[hd:hd + 1, :]
                o_ref[a] = acc[0:hd, :] / l
                lse_ref[a] = m_s[a] + jnp.log2(l)

    return _pcall(
        body, name=name, grid=(kv, nq, nk),
        in_specs=[pl.BlockSpec((GROUP, hd, tq), lambda g, i, j: (g, 0, i)),
                  pl.BlockSpec((None, tk, hd), lambda g, i, j: (g, j, 0)),
                  pl.BlockSpec((None, vrows, tk), lambda g, i, j: (g, 0, j))],
        out_specs=[pl.BlockSpec((GROUP, hd, tq), lambda g, i, j: (g, 0, i)),
                   pl.BlockSpec((GROUP, 1, tq), lambda g, i, j: (g, 0, i))],
        out_shape=[jax.ShapeDtypeStruct((h, hd, na), F32), jax.ShapeDtypeStruct((h, 1, na), F32)],
        scratch_shapes=[pltpu.VMEM((GROUP, 1, tq), F32), pltpu.VMEM((GROUP, vrows, tq), F32)],
        compiler_params=_cp("parallel", "parallel", "arbitrary"))(qt_hm, k_hm, vt_hm)


def _window(win_ref, prev, cur, nxt, first, last, tm):
    win_ref[0:HALO, :] = jnp.where(first, 0.0, prev)
    win_ref[HALO:HALO + tm, :] = cur
    win_ref[HALO + tm:HALO + tm + HALO, :] = jnp.where(last, 0.0, nxt)


def _lane_blocks(d):
    return [slice(b, b + 128) for b in range(0, d, 128)]


def _sublane_shifts(slab):
    n = slab.shape[0]
    for b in range(8):
        sh = slab if b == 0 else pltpu.roll(slab, n - b, 0)
        for a8 in range(0, 2 * HALO, 8):
            yield a8 + b, sh[a8:a8 + CONV_ROWS, :]


def _halo_specs(tm, d, na, colblk):
    per = tm // HALO
    last_blk = na // HALO - 1
    prev = pl.BlockSpec((HALO, d), lambda i: (jnp.maximum(i * per - 1, 0), colblk))
    nxt = pl.BlockSpec((HALO, d), lambda i: (jnp.minimum((i + 1) * per, last_blk), colblk))
    return prev, nxt


def _seq_ends(i, n_lat, na, tm):
    first = jnp.logical_or(i == 0, i == n_lat // tm)
    last = jnp.logical_or(i == n_lat // tm - 1, i == na // tm - 1)
    return first, last


def _conv_fwd(p, conv_w, conv_b, ln_g, ln_b, n_lat, tm, name):
    na = p.shape[0]
    ktaps, d = conv_w.shape
    pad = ktaps // 2

    def body(a_ref, ap_ref, an_ref, g_ref, gp_ref, gn_ref, ga_ref, w_ref, cb_ref, lg_ref, lb_ref,
             y5_ref, y5t_ref, y2_ref, win):
        i = pl.program_id(0)
        first, last = _seq_ends(i, n_lat, na, tm)
        _window(win, ap_ref[...] * _sig(gp_ref[...]), a_ref[...] * _sig(g_ref[...]),
                an_ref[...] * _sig(gn_ref[...]), first, last, tm)

        def chunk(c, carry):
            r = pl.multiple_of(c * CONV_ROWS, CONV_ROWS)
            for lanes in _lane_blocks(d):
                acc = jnp.zeros((CONV_ROWS, 128), F32)
                for off, sh in _sublane_shifts(win[pl.ds(r, CONV_ROWS + 2 * HALO), lanes]):
                    k = off - (HALO - pad)
                    if 0 <= k < ktaps:
                        acc = acc + w_ref[k:k + 1, lanes] * sh
                y2_ref[pl.ds(r, CONV_ROWS), lanes] = acc + cb_ref[:, lanes]
            return carry

        lax.fori_loop(0, tm // CONV_ROWS, chunk, 0)
        y2 = y2_ref[...]
        xc = y2 - _mean(y2)
        y3 = xc * lax.rsqrt(_mean(xc * xc) + EPS) * lg_ref[...] + lb_ref[...]
        gate = ga_ref[...]
        y5 = (y3 * _sig(y3)) * (gate * _sig(gate))
        y5_ref[...] = y5.astype(BF16)
        y5t_ref[...] = y5.T.astype(BF16)

    ap, an = _halo_specs(tm, d, na, 0)
    gp, gn = _halo_specs(tm, d, na, 1)
    return _pcall(
        body, name=name, grid=(na // tm,),
        in_specs=[_rows(tm, d, 0), ap, an, _rows(tm, d, 1), gp, gn, _rows(tm, d, 2),
                  _full((ktaps, d)), _full((1, d)), _full((1, d)), _full((1, d))],
        out_specs=[_rows(tm, d), _cols(d, tm), _rows(tm, d)],
        out_shape=[jax.ShapeDtypeStruct((na, d), BF16), jax.ShapeDtypeStruct((d, na), BF16),
                   jax.ShapeDtypeStruct((na, d), F32)],
        scratch_shapes=[pltpu.VMEM((tm + 2 * HALO, d), F32)],
        compiler_params=_cp("parallel"))(p, p, p, p, p, p, p, conv_w, conv_b, ln_g, ln_b)


def _merge_fwd(y5, ot_lat, ot_ctx, p, xa, modv, gpost, wc, wa, wo, n_lat, tm, name):
    na, d = xa.shape

    def body(y5_ref, ol_ref, oc_ref, gb_ref, ma_ref, mb_ref, x_ref, mod_ref, gp_ref, wc_ref, wa_ref, wo_ref,
             xn_ref, yc_ref, ya_ref, out_ref, zt_ref, ogt_ref):
        is_ctx = pl.program_id(0) >= n_lat // tm
        gate_b = gb_ref[...]
        o = jnp.where(is_ctx, oc_ref[...], ol_ref[...]).T
        og = o * (gate_b * _sig(gate_b))
        ogt_ref[...] = og.T.astype(BF16)
        yc = _dot(y5_ref[...], wc_ref[...])
        ya = _dot(og.astype(BF16), wa_ref[...])
        yc_ref[...] = yc
        ya_ref[...] = ya
        z = _sig(ma_ref[...]) * yc + _sig(mb_ref[...]) * ya
        zt_ref[...] = z.T.astype(BF16)
        out = _dot(z.astype(BF16), wo_ref[...])
        out_ref[...] = out
        gt = _seg_rows(mod_ref, 2 * d, 3 * d, is_ctx)
        xn_ref[...] = x_ref[...] + gt * (out * lax.rsqrt(_mean(out * out) + EPS) * gp_ref[...])

    f32o = jax.ShapeDtypeStruct((na, d), F32)
    bft = jax.ShapeDtypeStruct((d, na), BF16)
    ol, oc = _split_cols(d, tm, n_lat)
    return _pcall(
        body, name=name, grid=(na // tm,),
        in_specs=[_rows(tm, d), ol, oc, _rows(tm, d, 4), _rows(tm, d, 5), _rows(tm, d, 6), _rows(tm, d),
                  _full((8, 3 * d)), _full((1, d)), _full((d, d)), _full((d, d)), _full((d, d))],
        out_specs=[_rows(tm, d)] * 4 + [_cols(d, tm)] * 2,
        out_shape=[f32o, f32o, f32o, f32o, bft, bft],
        compiler_params=_cp("parallel"))(y5, ot_lat, ot_ctx, p, p, p, xa, modv, gpost, wc, wa, wo)


def _loss_grad(xa, target, n_lat, tm, name):
    na, d = xa.shape
    nlt = n_lat // tm

    def body(x_ref, t_ref, dx_ref, loss_ref):
        i = pl.program_id(0)

        @pl.when(i == 0)
        def _():
            loss_ref[...] = jnp.zeros_like(loss_ref)

        @pl.when(i < nlt)
        def _():
            err = x_ref[...] - t_ref[...]
            dx_ref[...] = err * (1.0 / d)
            loss_ref[...] += 0.5 * jnp.sum(_mean(err * err))

        @pl.when(i >= nlt)
        def _():
            dx_ref[...] = jnp.zeros_like(dx_ref)

    return _pcall(
        body, name=name, grid=(na // tm,),
        in_specs=[_rows(tm, d), pl.BlockSpec((tm, d), lambda i: (jnp.minimum(i, nlt - 1), 0))],
        out_specs=[_rows(tm, d), _full((8, 128))],
        out_shape=[jax.ShapeDtypeStruct((na, d), F32), jax.ShapeDtypeStruct((8, 128), F32)],
        compiler_params=_cp("arbitrary"))(xa, target)


def _merge_bwd(dxn, out, yc, ya, ot_lat, ot_ctx, p, y2, modv, gpost, ln_g, ln_b, wo_t, wc_t, wa_t, esel, n_lat, tm,
               name, rider=None):
    na, d = dxn.shape

    def body(dx_ref, out_ref, yc_ref, ya_ref, ol_ref, oc_ref, gb_ref, ma_ref, mb_ref, gta_ref, y2_ref,
             mod_ref, gp_ref, lg_ref, lb_ref, wot_ref, wct_ref, wat_ref, es_ref,
             dgb_ref, dma_ref, dmb_ref, dgta_ref, dot_ref, dout_ref, dyc_ref, dya_ref, dy2_ref, dlt_ref, acc_ref):
        i = pl.program_id(0)
        ctx_tile = i >= n_lat // tm

        @pl.when(i == 0)
        def _():
            acc_ref[...] = jnp.zeros_like(acc_ref)

        gt = _seg_rows(mod_ref, 2 * d, 3 * d, ctx_tile)
        gp = gp_ref[...]
        dx = dx_ref[...]
        out = out_ref[...]
        r2 = lax.rsqrt(_mean(out * out) + EPS)
        n2 = out * r2
        dgt = _colsum(dx * (n2 * gp))

        @pl.when(ctx_tile)
        def _():
            acc_ref[1:2, :] += dgt

        @pl.when(jnp.logical_not(ctx_tile))
        def _():
            acc_ref[0:1, :] += dgt

        acc_ref[2:3, :] += _colsum(dx * gt * n2)
        dn2 = dx * gt * gp
        dout = (r2 * (dn2 - n2 * _mean(dn2 * n2))).astype(BF16)
        dout_ref[...] = dout
        dz = _dot(dout, wot_ref[...])
        sa = _sig(ma_ref[...])
        sb = _sig(mb_ref[...])
        dyc = (dz * sa).astype(BF16)
        dya = (dz * sb).astype(BF16)
        dyc_ref[...] = dyc
        dya_ref[...] = dya
        dma_ref[...] = (dz * yc_ref[...] * sa * (1.0 - sa)).astype(BF16)
        dmb_ref[...] = (dz * ya_ref[...] * sb * (1.0 - sb)).astype(BF16)
        dy5 = _dot(dyc, wct_ref[...])
        dog = _dot(dya, wat_ref[...])

        gate_b = gb_ref[...]
        sgb = _sig(gate_b)
        o = jnp.where(ctx_tile, oc_ref[...], ol_ref[...]).T
        do = dog * (gate_b * sgb)
        dot_ref[...] = do.T.astype(BF16)
        dgb_ref[...] = (dog * o * (sgb * (1.0 + gate_b * (1.0 - sgb)))).astype(BF16)
        dlt_ref[...] = _split_dot(do * o, es_ref[...]).T

        y2 = y2_ref[...]
        xc = y2 - _mean(y2)
        rstd = lax.rsqrt(_mean(xc * xc) + EPS)
        xhat = xc * rstd
        lg = lg_ref[...]
        y3 = xhat * lg + lb_ref[...]
        s3 = _sig(y3)
        gate_a = gta_ref[...]
        sga = _sig(gate_a)
        dgta_ref[...] = (dy5 * (y3 * s3) * (sga * (1.0 + gate_a * (1.0 - sga)))).astype(BF16)
        dy3 = dy5 * (gate_a * sga) * (s3 * (1.0 + y3 * (1.0 - s3)))
        acc_ref[3:4, :] += _colsum(dy3 * xhat)
        acc_ref[4:5, :] += _colsum(dy3)
        dxh = dy3 * lg
        dy2 = rstd * (dxh - _mean(dxh) - xhat * _mean(dxh * xhat))
        dy2_ref[...] = dy2
        acc_ref[5:6, :] += _colsum(dy2)

    f32o = jax.ShapeDtypeStruct((na, d), F32)
    bfo = jax.ShapeDtypeStruct((na, d), BF16)
    r = _rows(tm, d)
    ol, oc = _split_cols(d, tm, n_lat)
    ins = (dxn, out, yc, ya, ot_lat, ot_ctx, p, p, p, p, y2, modv, gpost, ln_g, ln_b, wo_t, wc_t, wa_t, esel)
    extra = rider or dict(ins=[], outs=[], scratch=[])
    return _pcall(
        _ride(body, rider, len(ins), 11, na // tm), name=name, grid=(na // tm,),
        in_specs=[r, r, r, r, ol, oc, _rows(tm, d, 4), _rows(tm, d, 5), _rows(tm, d, 6), _rows(tm, d, 2), r,
                  _full((8, 3 * d)), _full((1, d)), _full((1, d)), _full((1, d)),
                  _full((d, d)), _full((d, d)), _full((d, d)), _full((d, 128))] + [ANY] * len(extra["ins"]),
        out_specs=[r] * 4 + [_cols(d, tm)] + [r] * 4 + [_cols(128, tm), _full((8, d))] + [ANY] * len(extra["outs"]),
        out_shape=[bfo] * 4 + [jax.ShapeDtypeStruct((d, na), BF16)] + [bfo] * 3
        + [f32o, jax.ShapeDtypeStruct((128, na), F32), jax.ShapeDtypeStruct((8, d), F32)] + extra["outs"],
        scratch_shapes=extra["scratch"],
        compiler_params=_cp("arbitrary"),
    )(*ins, *extra["ins"])


def _attn_bwd(qt_hm, k_hm, kt_hm, v_hm, dot_hm, lse_r, dl_r, na, tq, tk, cw, name):
    h, hd, _ = qt_hm.shape
    kv, nkeys, _ = k_hm.shape
    nq, nk = na // tq, nkeys // tk
    nsub = tq // cw
    chains = [(a, u) for a in range(GROUP) for u in range(nsub)]

    def body(qt_ref, k_ref, kt_ref, v_ref, dot_ref, lse_ref, dl_ref, dq_hbm, dk_ref, dv_ref,
             dq_acc, dk_acc, dv_acc, sem):
        g = pl.program_id(0)
        j = pl.program_id(1)
        i = pl.program_id(2)

        @pl.when(jnp.logical_and(j == 0, i == 0))
        def _():
            dq_acc[...] = jnp.zeros_like(dq_acc)

        @pl.when(i == 0)
        def _():
            dk_acc[...] = jnp.zeros_like(dk_acc)
            dv_acc[...] = jnp.zeros_like(dv_acc)

        k = k_ref[...]
        v = v_ref[...]
        kt = kt_ref[...]

        def products(n):
            a, u = chains[n]
            cols = slice(u * cw, (u + 1) * cw)
            return _dot(k, qt_ref[a, :, cols]), _dot(v, dot_ref[a, :, cols])

        def accumulate(done):
            a, u, dv_t, dk_t, dq_t = done
            dv_acc[...] += dv_t
            dk_acc[...] += dk_t
            at = pl.multiple_of(i * tq + u * cw, cw)
            dq_acc[a, :, pl.ds(at, cw)] += dq_t

        tiles = {n: products(n) for n in range(min(ATTN_BWD_AHEAD, len(chains)))}
        pending = None
        for n, (a, u) in enumerate(chains):
            cols = slice(u * cw, (u + 1) * cw)
            s_t, dp_t = tiles.pop(n)
            p_t = jnp.exp2(s_t - lse_ref[a, :, cols])
            ds_b = (p_t * (dp_t - dl_ref[a, :, cols])).astype(BF16)
            p_b = p_t.astype(BF16)
            dv_t = _dot_nt(dot_ref[a, :, cols], p_b)
            dk_t = _dot_nt(qt_ref[a, :, cols], ds_b)
            dq_t = _dot(kt, ds_b)
            if pending is not None:
                accumulate(pending)
            pending = (a, u, dv_t, dk_t, dq_t)
            if n + ATTN_BWD_AHEAD < len(chains):
                tiles[n + ATTN_BWD_AHEAD] = products(n + ATTN_BWD_AHEAD)
        accumulate(pending)

        @pl.when(i == nq - 1)
        def _():
            dk_ref[...] = dk_acc[...]
            dv_ref[...] = dv_acc[...]

        @pl.when(jnp.logical_and(j == nk - 1, i == nq - 1))
        def _():
            cp = pltpu.make_async_copy(dq_acc, dq_hbm.at[pl.ds(g * GROUP, GROUP)], sem)
            cp.start()
            cp.wait()

    qtspec = pl.BlockSpec((GROUP, hd, tq), lambda g, j, i: (g, 0, i))
    kspec = pl.BlockSpec((None, tk, hd), lambda g, j, i: (g, j, 0))
    ktspec = pl.BlockSpec((None, hd, tk), lambda g, j, i: (g, 0, j))
    rspec = pl.BlockSpec((GROUP, 1, tq), lambda g, j, i: (g, 0, i))
    return _pcall(
        body, name=name, grid=(kv, nk, nq),
        in_specs=[qtspec, kspec, ktspec, kspec, qtspec, rspec, rspec],
        out_specs=[ANY, ktspec, ktspec],
        out_shape=[jax.ShapeDtypeStruct((h, hd, na), F32), jax.ShapeDtypeStruct((kv, hd, nkeys), F32),
                   jax.ShapeDtypeStruct((kv, hd, nkeys), F32)],
        scratch_shapes=[pltpu.VMEM((GROUP, hd, na), F32), pltpu.VMEM((hd, tk), F32), pltpu.VMEM((hd, tk), F32),
                        pltpu.SemaphoreType.DMA],
        compiler_params=_cp("arbitrary", "arbitrary", "arbitrary"),
    )(qt_hm, k_hm, kt_hm, v_hm, dot_hm, lse_r, dl_r)


def _qknorm_bwd(dqt_lat, dqt_ctx, dkt, dkt_ctx, dvt, dvt_ctx, p, cos_t, sin_t, gq, gk, bd, n_lat, tm, name):
    na = p.shape[0]
    d = gq.shape[1]
    kw = d // GROUP

    def body(dql_ref, dqc_ref, dkl_ref, dkc_ref, dvl_ref, dvc_ref, q_ref, kv_ref, cos_ref, sin_ref, gq_ref, gk_ref,
             bd_ref, dqo_ref, dkvo_ref, acc_ref):
        is_ctx = pl.program_id(0) >= n_lat // tm

        @pl.when(pl.program_id(0) == 0)
        def _():
            acc_ref[...] = jnp.zeros_like(acc_ref)

        dq_in = jnp.where(is_ctx, dqc_ref[...], dql_ref[...]).T
        dk_in = (dkl_ref[...] + jnp.where(is_ctx, dkc_ref[...], 0.0)).T
        dv_in = (dvl_ref[...] + jnp.where(is_ctx, dvc_ref[...], 0.0)).T

        cos = cos_ref[...]
        sin = sin_ref[...]

        def back(dy, xh, g, w):
            bdw = bd_ref[0:w, 0:w]
            dn = dy * _lane_tile(cos, w) - _partner(dy) * _lane_tile(sin, w)
            rs = lax.rsqrt(_split_dot(xh * xh, bdw) * (1.0 / HEAD_DIM) + EPS)
            y = xh * rs
            dg = _colsum(dn * y)
            dyn = dn * g
            dx = rs * (dyn - y * (_split_dot(dyn * y, bdw) * (1.0 / HEAD_DIM)))
            return dx, dg

        dq, dgq = back(dq_in * ATTN_SCALE, q_ref[...], gq_ref[...], d)
        dqo_ref[...] = dq.astype(BF16)
        acc_ref[0:1, :] += dgq
        kv = kv_ref[...]
        dk, dgk = back(dk_in * (1.0 / LOG2E), kv[:, 0:kw], gk_ref[...], kw)
        acc_ref[1:2, 0:kw] += dgk
        dkvo_ref[:, 0:kw] = dk.astype(BF16)
        dkvo_ref[:, kw:2 * kw] = dv_in.astype(BF16)

    dql, dqc = _split_cols(d, tm, n_lat)
    _, kvc = _split_cols(kw, tm, n_lat)
    return _pcall(
        body, name=name, grid=(na // tm,),
        in_specs=[dql, dqc, _cols(kw, tm), kvc, _cols(kw, tm), kvc, _rows(tm, d, 3), _rows(tm, d // 2, 14),
                  _rows(tm, 128), _rows(tm, 128), _full((1, d)), _full((1, kw)), _full((d, d))],
        out_specs=[_rows(tm, d), _rows(tm, d // 2), _full((8, d))],
        out_shape=[jax.ShapeDtypeStruct((na, d), BF16), jax.ShapeDtypeStruct((na, d // 2), BF16),
                   jax.ShapeDtypeStruct((8, d), F32)],
        compiler_params=_cp("arbitrary"))(dqt_lat, dqt_ctx, dkt, dkt_ctx, dvt, dvt_ctx, p, p, cos_t, sin_t, gq, gk, bd)


def _conv_bwd(dy2, p, conv_w, n_lat, tm, name, rider=None):
    na = p.shape[0]
    ktaps, d = conv_w.shape
    pad = ktaps // 2

    def body(dy_ref, dyp_ref, dyn_ref, a_ref, ap_ref, an_ref, g_ref, gp_ref, gn_ref, w_ref,
             da_ref, dg_ref, dw_ref, dwin, ywin, dy1_s, part):
        i = pl.program_id(0)

        @pl.when(i == 0)
        def _():
            part[...] = jnp.zeros_like(part)

        first, last = _seq_ends(i, n_lat, na, tm)
        a = a_ref[...]
        sg = _sig(g_ref[...])
        _window(dwin, dyp_ref[...], dy_ref[...], dyn_ref[...], first, last, tm)
        _window(ywin, ap_ref[...] * _sig(gp_ref[...]), a * sg, an_ref[...] * _sig(gn_ref[...]), first, last, tm)

        def chunk(c, carry):
            r = pl.multiple_of(c * CONV_ROWS, CONV_ROWS)
            for lanes in _lane_blocks(d):
                dy = dy_ref[pl.ds(r, CONV_ROWS), lanes]
                acc = jnp.zeros((CONV_ROWS, 128), F32)
                for off, sh in _sublane_shifts(dwin[pl.ds(r, CONV_ROWS + 2 * HALO), lanes]):
                    k = HALO + pad - off
                    if 0 <= k < ktaps:
                        acc = acc + w_ref[k:k + 1, lanes] * sh
                dy1_s[pl.ds(r, CONV_ROWS), lanes] = acc
                for off, sh in _sublane_shifts(ywin[pl.ds(r, CONV_ROWS + 2 * HALO), lanes]):
                    k = off - (HALO - pad)
                    if 0 <= k < ktaps:
                        part[k, :, lanes] += jnp.sum((dy * sh).reshape(CONV_ROWS // 8, 8, 128), axis=0)
            return carry

        lax.fori_loop(0, tm // CONV_ROWS, chunk, 0)
        dy1 = dy1_s[...]
        da_ref[...] = (dy1 * sg).astype(BF16)
        dg_ref[...] = (dy1 * a * sg * (1.0 - sg)).astype(BF16)

        @pl.when(i == na // tm - 1)
        def _():
            dw_ref[...] = jnp.sum(part[...], axis=1)

    dyp, dyn = _halo_specs(tm, d, na, 0)
    ap, an = _halo_specs(tm, d, na, 0)
    gp, gn = _halo_specs(tm, d, na, 1)
    bfo = jax.ShapeDtypeStruct((na, d), BF16)
    ins = (dy2, dy2, dy2, p, p, p, p, p, p, conv_w)
    extra = rider or dict(ins=[], outs=[], scratch=[])
    return _pcall(
        _ride(body, rider, len(ins), 3, na // tm), name=name, grid=(na // tm,),
        in_specs=[_rows(tm, d), dyp, dyn, _rows(tm, d, 0), ap, an, _rows(tm, d, 1), gp, gn, _full((ktaps, d))]
        + [ANY] * len(extra["ins"]),
        out_specs=[_rows(tm, d), _rows(tm, d), _full((ktaps, d))] + [ANY] * len(extra["outs"]),
        out_shape=[bfo, bfo, jax.ShapeDtypeStruct((ktaps, d), F32)] + extra["outs"],
        scratch_shapes=[pltpu.VMEM((tm + 2 * HALO, d), F32), pltpu.VMEM((tm + 2 * HALO, d), F32),
                        pltpu.VMEM((tm, d), F32), pltpu.VMEM((ktaps, 8, d), F32)] + extra["scratch"],
        compiler_params=_cp("arbitrary"))(*ins, *extra["ins"])


def _inproj_bwd(segs, dkv, xa, dxn, modv, gpre, wp_t, n_lat, out_rows, tm, name):
    na, d = xa.shape
    nseg = len(segs)
    wrows = wp_t.shape[0]

    def body(*refs):
        seg_refs = refs[:nseg]
        dkv_ref, x_ref, dxn_ref, mod_ref, g_ref, wt_hbm, dx_ref, acc_ref, wt, sem = refs[nseg:]
        i = pl.program_id(0)
        ctx_tile = i >= n_lat // tm

        @pl.when(i == 0)
        def _():
            cp = pltpu.make_async_copy(wt_hbm, wt, sem)
            cp.start()
            cp.wait()
            acc_ref[...] = jnp.zeros_like(acc_ref)

        dh = _dot(dkv_ref[...], wt[nseg * d:wrows, :])
        for s in range(nseg):
            dh = dh + _dot(seg_refs[s][...], wt[s * d:(s + 1) * d, :])
        x = x_ref[...]
        r = lax.rsqrt(_mean(x * x) + EPS)
        xn = x * r
        g = g_ref[...]
        sc1 = 1.0 + _seg_rows(mod_ref, d, 2 * d, ctx_tile)
        dsh = _colsum(dh)
        dsc = _colsum(dh * (xn * g))

        @pl.when(ctx_tile)
        def _():
            acc_ref[1:2, :] += dsh
            acc_ref[3:4, :] += dsc

        @pl.when(jnp.logical_not(ctx_tile))
        def _():
            acc_ref[0:1, :] += dsh
            acc_ref[2:3, :] += dsc

        acc_ref[4:5, :] += _colsum(dh * xn * sc1)
        dxh = dh * g * sc1

        @pl.when(i < out_tiles)
        def _():
            dx_ref[...] = dxn_ref[...] + r * (dxh - xn * _mean(dxh * xn))

    r_ = _rows(tm, d)
    out_tiles = out_rows // tm
    return _pcall(
        body, name=name, grid=(na // tm,),
        in_specs=[r_] * nseg + [_rows(tm, d // 2), r_, r_, _full((8, 3 * d)), _full((1, d)), ANY],
        out_specs=[pl.BlockSpec((tm, d), lambda i: (jnp.minimum(i, out_tiles - 1), 0)), _full((8, d))],
        out_shape=[jax.ShapeDtypeStruct((out_rows, d), F32), jax.ShapeDtypeStruct((8, d), F32)],
        scratch_shapes=[pltpu.VMEM(wp_t.shape, BF16), pltpu.SemaphoreType.DMA],
        compiler_params=_cp("arbitrary"))(*segs, dkv, xa, dxn, modv, gpre, wp_t)


def _grad_matmul(a_t, b, tk, name):
    ka, na = a_t.shape
    nb = b.shape[1]
    tn = min(nb, 1024)

    def body(a_ref, b_ref, o_ref):
        @pl.when(pl.program_id(1) == 0)
        def _():
            o_ref[...] = jnp.zeros_like(o_ref)

        o_ref[...] += _dot(a_ref[...], b_ref[...])

    return _pcall(
        body, name=name, grid=(nb // tn, na // tk),
        in_specs=[pl.BlockSpec((ka, tk), lambda n, k: (0, k)), pl.BlockSpec((tk, tn), lambda n, k: (k, n))],
        out_specs=pl.BlockSpec((ka, tn), lambda n, k: (0, n)),
        out_shape=jax.ShapeDtypeStruct((ka, nb), F32),
        compiler_params=_cp("parallel", "arbitrary"))(a_t, b)


def _pack(parts, cols, row_mult):
    flat = jnp.concatenate([q.astype(F32).reshape(-1) for q in parts])
    rows = -(-flat.shape[0] // (cols * row_mult)) * row_mult
    return jnp.pad(flat, (0, rows * cols - flat.shape[0])).reshape(1, rows, cols)


def _unpack(flat, shapes):
    out, off = [], 0
    for s in shapes:
        n = math.prod(s)
        out.append(flat[off:off + n].reshape(tuple(s)))
        off += n
    return out


def _cols_by_dest(g):
    l, a, w8 = g.shape
    return g.reshape(l, a, N_DEV, w8 // N_DEV).transpose(2, 0, 1, 3)


def _rows_by_dest(g):
    l, r8, b = g.shape
    return g.reshape(l, N_DEV, r8 // N_DEV, b).transpose(1, 0, 2, 3)


def _cols_from_src(s):
    n, l, a, w = s.shape
    return s.transpose(1, 2, 0, 3).reshape(l, a, n * w)


def _rows_from_src(s):
    n, l, r, b = s.shape
    return s.transpose(1, 0, 2, 3).reshape(l, n * r, b)


def _rope_tables(n_lat, n_ctx):
    half = HEAD_DIM // 2
    rows = n_lat // GRID_W
    row = jnp.repeat(jnp.arange(rows, dtype=F32), GRID_W)
    col = jnp.tile(jnp.arange(GRID_W, dtype=F32), rows)
    inv_freq = ROPE_THETA ** (-jnp.arange(0, half, 2, dtype=F32) / half)
    ang = jnp.concatenate([row[:, None] * inv_freq, col[:, None] * inv_freq], axis=-1)
    cos, sin = jnp.cos(ang), jnp.sin(ang)
    cos_t = jnp.concatenate([jnp.tile(cos, (1, 4)), jnp.ones((n_ctx, 128), F32)], axis=0)
    sin_t = jnp.concatenate([jnp.tile(jnp.concatenate([-sin, sin], axis=-1), (1, 2)),
                             jnp.zeros((n_ctx, 128), F32)], axis=0)
    return cos_t, sin_t


def _to_heads(t):
    na, w = t.shape
    return t.reshape(na, w // HEAD_DIM, HEAD_DIM).transpose(1, 0, 2)


def kernel(x, c, ctx, c_ctx, w_mod, b_mod, g_pre, g_post, w_in, conv_w, conv_b, ln_g, ln_b, w_conv_out, q_norm_g, k_norm_g, w_attn_out, w_out, loss_target, m_c_ctx, m_w_mod, m_b_mod, m_g_pre, m_g_post, m_w_in, m_conv_w, m_conv_b, m_ln_g, m_ln_b, m_w_conv_out, m_q_norm_g, m_k_norm_g, m_w_attn_out, m_w_out, v_c_ctx, v_w_mod, v_b_mod, v_g_pre, v_g_post, v_w_in, v_conv_w, v_conv_b, v_ln_g, v_ln_b, v_w_conv_out, v_q_norm_g, v_k_norm_g, v_w_attn_out, v_w_out):
    depth, d, _ = w_mod.shape
    n_lat, n_ctx = x.shape[1], ctx.shape[1]
    na = n_lat + n_ctx
    heads = d // HEAD_DIM
    kvh = heads // GROUP
    kw = d // GROUP
    ktaps = conv_w.shape[1]
    tm = n_ctx
    tm_half = tm // 2
    tbig = 3 * tm if na % (3 * tm) == 0 else tm
    tq_lat = 4 * tm if n_lat % (4 * tm) == 0 else tm
    tq_bwd = 4 * tm if n_lat % (4 * tm) == 0 else tm

    def layer_shards(l):
        return [w_mod[l].astype(BF16), w_in[l].astype(BF16), w_conv_out[l].astype(BF16), w_attn_out[l].astype(BF16),
                w_out[l].astype(BF16), conv_w[l]]

    def full_weights(gathered):
        s_mod, s_in, s_co, s_ao, s_oo, s_cw = [q[:, None] for q in gathered]
        win = _cols_from_src(s_in)[0]
        wp = jnp.concatenate([win[:, :4 * d], win[:, 4 * d + 2 * kw:], win[:, 4 * d:4 * d + 2 * kw]], axis=1)
        return dict(wmod=_cols_from_src(s_mod)[0], wp=wp, wc=_rows_from_src(s_co)[0], wa=_rows_from_src(s_ao)[0],
                    wo=_rows_from_src(s_oo)[0], convw=_cols_from_src(s_cw)[0])

    weights = [full_weights(_all_gather(layer_shards(0), "gather_weights_l0"))]

    cos_t, sin_t = _rope_tables(n_lat, n_ctx)
    lane = jnp.arange(d)
    bd = (lane[:, None] // HEAD_DIM == lane[None, :] // HEAD_DIM).astype(BF16)
    esel = (lane[:, None] // HEAD_DIM == jnp.arange(128)[None, :]).astype(BF16)
    cvec = jnp.zeros((8, d), F32).at[0].set(c[0]).at[1].set(c_ctx)
    cvec_t = jnp.zeros((d, 128), F32).at[:, 0].set(c[0]).at[:, 1].set(c_ctx)

    xa = jnp.concatenate([x[0], ctx[0]], axis=0)

    saved = []
    for l in range(depth):
        tag = f"_l{l}"
        gq = jnp.tile(q_norm_g[l], heads)[None, :]
        gk = jnp.tile(k_norm_g[l], heads // GROUP)[None, :]
        wl = weights[l]
        modv = _mod_fwd(cvec, wl["wmod"], b_mod[l][None, :], "mod_fwd" + tag)
        p, h_t, *gathered = _inproj(xa, modv, g_pre[l][None, :], wl["wp"], n_lat, tbig, "inproj" + tag,
                                    shards=layer_shards(l + 1) if l + 1 < depth else ())
        if gathered:
            weights.append(full_weights(gathered))
        q_t, kr, vb, k_t, v_t = _qknorm_fwd(p, cos_t, sin_t, gq, gk, bd, tm, "qknorm_fwd" + tag)
        qt_hm = q_t.reshape(heads, HEAD_DIM, na)
        kt_hm = k_t.reshape(kvh, HEAD_DIM, na)
        k_hm, v_hm = _to_heads(kr), _to_heads(vb)
        vt_ones = jnp.concatenate([v_t.reshape(kvh, HEAD_DIM, na), jnp.ones((kvh, 16, na), BF16)], axis=1)
        ot_lat, lse_lat = _attn_fwd(qt_hm, k_hm, vt_ones, n_lat, tq_lat, tbig, tm, "attn_fwd" + tag)
        ot_ctx, lse_ctx = _attn_fwd(qt_hm[:, :, n_lat:], k_hm[:, n_lat:], vt_ones[:, :, n_lat:], n_ctx, tm, tm, tm,
                                    "attn_ctx_fwd" + tag)
        ot_lat, ot_ctx = ot_lat.reshape(d, n_lat), ot_ctx.reshape(d, n_ctx)
        y5, y5_t, y2 = _conv_fwd(p, wl["convw"], conv_b[l][None, :], ln_g[l][None, :], ln_b[l][None, :], n_lat, tm,
                                 "conv_fwd" + tag)
        xa_new, yc, ya, out, z_t, og_t = _merge_fwd(y5, ot_lat, ot_ctx, p, xa, modv, g_post[l][None, :], wl["wc"],
                                                    wl["wa"], wl["wo"], n_lat, tm, "merge_fwd" + tag)
        saved.append(dict(xa=xa, modv=modv, p=p, h_t=h_t, qt_hm=qt_hm, k_hm=k_hm, kt_hm=kt_hm, v_hm=v_hm,
                          ot_lat=ot_lat, ot_ctx=ot_ctx, lse_lat=lse_lat, lse_ctx=lse_ctx,
                          y5_t=y5_t, y2=y2, yc=yc, ya=ya, out=out, z_t=z_t, og_t=og_t, gq=gq, gk=gk))
        xa = xa_new

    dxa, loss_blk = _loss_grad(xa, loss_target[0], n_lat, tm, "loss_grad")

    big_names = ["w_mod", "w_in", "w_conv_out", "w_attn_out", "w_out", "conv_w"]
    g_bmod, g_gpre, g_gpost, g_convb, g_lng, g_lnb, g_qg, g_kg = [], [], [], [], [], [], [], []
    g_cctx = jnp.zeros((d,), F32)
    pending_send, pending_layer, from_chips = None, None, {}

    def add_sibling_parts(send, swapped, layer):
        return [_add_sibling_part(sb, rc, BF16, f"reduce_sibling_add_{n}_l{layer}")
                for n, sb, rc in zip(big_names, send, swapped)]

    for l in reversed(range(depth)):
        tag = f"_l{l}"
        s = saved[l]
        wl = weights[l]
        p = s["p"]
        (dgb, dma, dmb, dgta, do_t, dout, dyc, dya, dy2, dl_t, acc_m, *swapped) = _merge_bwd(
            dxa, s["out"], s["yc"], s["ya"], s["ot_lat"], s["ot_ctx"], p, s["y2"], s["modv"], g_post[l][None, :],
            ln_g[l][None, :], ln_b[l][None, :], wl["wo"].T, wl["wc"].T, wl["wa"].T, esel, n_lat, tm_half,
            "merge_bwd" + tag, rider=_swap_rider(pending_send) if pending_send else None)
        exchange = _exchange_rider(add_sibling_parts(pending_send, swapped, pending_layer)) if swapped else None
        dl_r = dl_t[:heads].reshape(heads, 1, na)
        dot_hm = do_t.reshape(heads, HEAD_DIM, na)
        qt_hm, k_hm, kt_hm, v_hm = s["qt_hm"], s["k_hm"], s["kt_hm"], s["v_hm"]
        dqt_lat, dkt_hm, dvt_hm = _attn_bwd(qt_hm, k_hm, kt_hm, v_hm, dot_hm, s["lse_lat"], dl_r,
                                            n_lat, tq_bwd, tm, tm, "attn_bwd" + tag)
        dqt_ctx, dkt_ctx, dvt_ctx = _attn_bwd(
            qt_hm[:, :, n_lat:], k_hm[:, n_lat:], kt_hm[:, :, n_lat:], v_hm[:, n_lat:], dot_hm[:, :, n_lat:],
            s["lse_ctx"], dl_r[:, :, n_lat:], n_ctx, tm, tm, tm, "attn_ctx_bwd" + tag)
        dq, dkv, acc_q = _qknorm_bwd(
            dqt_lat.reshape(d, n_lat), dqt_ctx.reshape(d, n_ctx), dkt_hm.reshape(kw, na), dkt_ctx.reshape(kw, n_ctx),
            dvt_hm.reshape(kw, na), dvt_ctx.reshape(kw, n_ctx), p, cos_t, sin_t, s["gq"], s["gk"], bd, n_lat, tm,
            "qknorm_bwd" + tag)
        da, dg, dconvw, *exchanged = _conv_bwd(dy2, p, wl["convw"], n_lat, tm, "conv_bwd" + tag, rider=exchange)
        if exchanged:
            from_chips[pending_layer] = exchanged
        segs = [da, dg, dgta, dq, dgb, dma, dmb]
        dxa, acc_h = _inproj_bwd(segs, dkv, s["xa"], dxa, s["modv"], g_pre[l][None, :], wl["wp"].T, n_lat,
                                 na if l else n_lat, tm, "inproj_bwd" + tag)

        dwp = [_grad_matmul(s["h_t"], sg, tbig, f"grad_w_in{k}" + tag) for k, sg in enumerate(segs + [dkv])]
        g_win = jnp.concatenate(dwp[:4] + [dwp[7]] + dwp[4:7], axis=1)
        g_wc = _grad_matmul(s["y5_t"], dyc, tbig, "grad_w_conv_out" + tag)
        g_wa = _grad_matmul(s["og_t"], dya, tbig, "grad_w_attn_out" + tag)
        g_wo = _grad_matmul(s["z_t"], dout, tbig, "grad_w_out" + tag)

        dmod = jnp.zeros((8, 3 * d), F32)
        dmod = dmod.at[0].set(jnp.concatenate([acc_h[0], acc_h[2], acc_m[0]]))
        dmod = dmod.at[1].set(jnp.concatenate([acc_h[1], acc_h[3], acc_m[1]]))
        dwm, dbm, dcv = _mod_bwd(dmod, cvec, cvec_t, wl["wmod"], "mod_bwd" + tag)
        by_dest = [_cols_by_dest(dwm[None]), _cols_by_dest(g_win[None]), _rows_by_dest(g_wc[None]),
                   _rows_by_dest(g_wa[None]), _rows_by_dest(g_wo[None]), _cols_by_dest(dconvw[None])]
        pending_send, pending_layer = [q.reshape((4, 2) + q.shape[1:]) for q in by_dest], l
        g_bmod.append(dbm[0])
        g_cctx = g_cctx + dcv[1]
        g_gpre.append(acc_h[4])
        g_gpost.append(acc_m[2])
        g_lng.append(acc_m[3])
        g_lnb.append(acc_m[4])
        g_convb.append(acc_m[5])
        g_qg.append(acc_q[0].reshape(heads, HEAD_DIM).sum(0))
        g_kg.append(acc_q[1, :kw].reshape(heads // GROUP, HEAD_DIM).sum(0))

    grad_x = dxa[None]

    def stack(lst):
        return jnp.stack(lst[::-1])

    big_w = dict(w_mod=w_mod, w_in=w_in, w_conv_out=w_conv_out, w_attn_out=w_attn_out, w_out=w_out, conv_w=conv_w)
    big_m = dict(w_mod=m_w_mod, w_in=m_w_in, w_conv_out=m_w_conv_out, w_attn_out=m_w_attn_out, w_out=m_w_out,
                 conv_w=m_conv_w)
    big_v = dict(w_mod=v_w_mod, w_in=v_w_in, w_conv_out=v_w_conv_out, w_attn_out=v_w_attn_out, w_out=v_w_out,
                 conv_w=v_conv_w)
    swapped = _run_rider(_swap_rider(pending_send), "reduce_sibling")
    from_chips[pending_layer] = _run_rider(
        _exchange_rider(add_sibling_parts(pending_send, swapped, pending_layer)), "reduce_chips")
    big_g, big_d, big_nm, big_nv = {}, {}, {}, {}
    for k, n in enumerate(big_names):
        st = jnp.concatenate([from_chips[l][k] for l in range(depth)], axis=1)
        big_g[n], big_d[n], big_nm[n], big_nv[n] = _sum_adamw(st, big_w[n], big_m[n], big_v[n], "adamw_" + n)

    small_names = ["c_ctx", "b_mod", "g_pre", "g_post", "conv_b", "ln_g", "ln_b", "q_norm_g", "k_norm_g", "loss"]
    zero1 = jnp.zeros((1,), F32)
    small_w = dict(c_ctx=c_ctx, b_mod=b_mod, g_pre=g_pre, g_post=g_post, conv_b=conv_b, ln_g=ln_g, ln_b=ln_b,
                   q_norm_g=q_norm_g, k_norm_g=k_norm_g, loss=zero1)
    small_m = dict(c_ctx=m_c_ctx, b_mod=m_b_mod, g_pre=m_g_pre, g_post=m_g_post, conv_b=m_conv_b, ln_g=m_ln_g,
                   ln_b=m_ln_b, q_norm_g=m_q_norm_g, k_norm_g=m_k_norm_g, loss=zero1)
    small_v = dict(c_ctx=v_c_ctx, b_mod=v_b_mod, g_pre=v_g_pre, g_post=v_g_post, conv_b=v_conv_b, ln_g=v_ln_g,
                   ln_b=v_ln_b, q_norm_g=v_q_norm_g, k_norm_g=v_k_norm_g, loss=zero1)
    small_g = dict(c_ctx=g_cctx, b_mod=stack(g_bmod), g_pre=stack(g_gpre), g_post=stack(g_gpost),
                   conv_b=stack(g_convb), ln_g=stack(g_lng), ln_b=stack(g_lnb), q_norm_g=stack(g_qg),
                   k_norm_g=stack(g_kg), loss=loss_blk[0, 0:1])
    small_shapes = [small_w[n].shape for n in small_names]

    def pack_small(tree):
        return _pack([tree[n] for n in small_names], d, 8)

    small_parts, = _all_gather([pack_small(small_g)], "gather_small_grads")
    small_out = _sum_adamw(small_parts, pack_small(small_w), pack_small(small_m), pack_small(small_v),
                           "adamw_replicated")
    sm_g, sm_d, sm_nm, sm_nv = [dict(zip(small_names, _unpack(o.reshape(-1), small_shapes))) for o in small_out]
    loss = sm_g["loss"].reshape(())

    order = ["c_ctx", "w_mod", "b_mod", "g_pre", "g_post", "w_in", "conv_w", "conv_b", "ln_g", "ln_b",
             "w_conv_out", "q_norm_g", "k_norm_g", "w_attn_out", "w_out"]

    def pick(big, small):
        return [big[n] if n in big else small[n] for n in order]

    return (loss, grad_x, *pick(big_g, sm_g), *pick(big_d, sm_d), *pick(big_nm, sm_nm), *pick(big_nv, sm_nv))
```

```python
import math

import jax
import jax.numpy as jnp
from jax import lax
from jax.experimental import pallas as pl
from jax.experimental.pallas import tpu as pltpu

F32 = jnp.float32
BF16 = jnp.bfloat16

HEAD_DIM = 64
GROUP = 4
GRID_W = 64
ROPE_THETA = 10000.0
EPS = 1e-6
ATTN_SCALE = HEAD_DIM ** -0.5
LOG2E = 1.4426950408889634
Q_PRESCALE = ATTN_SCALE * LOG2E
HALO = 16
CONV_ROWS = 64
ATTN_FWD_AHEAD = 5
ATTN_BWD_AHEAD = 3

ADAM_LR = 0.001
ADAM_B1 = 0.9
ADAM_B2 = 0.999
ADAM_EPS = 1e-08
ADAM_WD = 0.01
ADAM_STEP = 10

N_DEV = 8
MESH_AXES = ("x", "y", "c")
V7X_VMEM_LIMIT = 56 * 1024 * 1024
NEG_BIG = -1e30

MESH = pl.DeviceIdType.MESH
ANY = pl.BlockSpec(memory_space=pl.ANY)


def _pcall(body, **kw):
    return pl.pallas_call(body, **kw)


def _cp(*sem):
    return pltpu.CompilerParams(dimension_semantics=sem, vmem_limit_bytes=V7X_VMEM_LIMIT)


def _sig(x):
    return 0.5 * jnp.tanh(0.5 * x) + 0.5


def _mean(x):
    return jnp.mean(x, axis=-1, keepdims=True)


def _colsum(x):
    return jnp.sum(x, axis=0, keepdims=True)


def _bf_round(x):
    return x.astype(BF16).astype(F32)


def _dot(a, b):
    return jnp.dot(a, b, preferred_element_type=F32)


def _dot_nt(a, b):
    return lax.dot_general(a, b, (((1,), (1,)), ((), ())), preferred_element_type=F32)


def _split_dot(x, m):
    hi = x.astype(BF16)
    lo = (x - hi.astype(F32)).astype(BF16)
    return _dot(hi, m) + _dot(lo, m)


def _head_sum(x, es, es_t):
    return _split_dot(_split_dot(x, es), es_t)


def _full(shape):
    nd = len(shape)
    return pl.BlockSpec(shape, lambda *_: (0,) * nd)


def _rows(tm, width, colblk=0):
    return pl.BlockSpec((tm, width), lambda i: (i, colblk))


def _cols(height, tm):
    return pl.BlockSpec((height, tm), lambda i: (0, i))


def _split_cols(height, tm, n_lat):
    nl = n_lat // tm
    return (pl.BlockSpec((height, tm), lambda i: (0, jnp.minimum(i, nl - 1))),
            pl.BlockSpec((height, tm), lambda i: (0, jnp.maximum(i - nl, 0))))


def _my_place():
    return lax.axis_index("x"), lax.axis_index("y"), lax.axis_index("c")


def _sem_arrays(n):
    return [pltpu.SemaphoreType.DMA((n,)), pltpu.SemaphoreType.DMA((n,))]


def _gather_steps(x_refs, out_refs, send_sems, recv_sems, local_sems):
    n = len(x_refs)
    x, y, c = _my_place()
    me, sibling = (x, y, c), (x, y, 1 - c)
    chips = [(1 - x, y), (x, 1 - y), (1 - x, 1 - y)]

    def slab(a, px, py, pc):
        return out_refs[a].at[4 * px + 2 * py + pc]

    def copies(k, block, to, from_input=False):
        return [pltpu.make_async_remote_copy(
            src_ref=x_refs[a] if from_input else slab(a, *block), dst_ref=slab(a, *block),
            send_sem=send_sems.at[k * n + a], recv_sem=recv_sems.at[k * n + a],
            device_id=to, device_id_type=MESH) for a in range(n)]

    def mine():
        return [pltpu.make_async_copy(x_refs[a], slab(a, *me), local_sems.at[a]) for a in range(n)]

    def first():
        out = copies(0, me, sibling, True)
        for j, chip in enumerate(chips):
            out += copies(1 + j, me, (*chip, c), True)
        return out

    def onward(j):
        return copies(4 + j, (*chips[j], c), sibling)

    def start():
        for cp in mine() + first():
            cp.start()

    def relay():
        for j, chip in enumerate(chips):
            for cp in copies(1 + j, (*chip, c), me):
                cp.wait_recv()
            for cp in onward(j):
                cp.start()

    def finish():
        for cp in copies(0, sibling, me):
            cp.wait_recv()
        for j, chip in enumerate(chips):
            for cp in copies(4 + j, (*chip, 1 - c), me):
                cp.wait_recv()
        for cp in first() + onward(0) + onward(1) + onward(2):
            cp.wait_send()
        for cp in mine():
            cp.wait()

    return start, relay, finish


def _gather_scratch(n):
    return _sem_arrays(7 * n) + [pltpu.SemaphoreType.DMA((n,))]


def _all_gather(shards, name):
    n = len(shards)

    def body(*refs):
        start, relay, finish = _gather_steps(refs[:n], refs[n:2 * n], *refs[2 * n:])
        start()
        relay()
        finish()

    return _pcall(
        body, name=name,
        out_shape=[jax.ShapeDtypeStruct((N_DEV,) + q.shape, q.dtype) for q in shards],
        in_specs=[ANY] * n, out_specs=[ANY] * n,
        scratch_shapes=_gather_scratch(n),
    )(*shards)


def _swap_steps(buf_refs, recv_refs, send_sems, recv_sems):
    n = len(buf_refs)
    x, y, c = _my_place()

    def copies():
        return [pltpu.make_async_remote_copy(
            src_ref=buf_refs[a].at[k, 1 - c], dst_ref=recv_refs[a].at[k],
            send_sem=send_sems.at[k * n + a], recv_sem=recv_sems.at[k * n + a],
            device_id=(x, y, 1 - c), device_id_type=MESH) for k in range(4) for a in range(n)]

    def start():
        for cp in copies():
            cp.start()

    def finish():
        for cp in copies():
            cp.wait()

    return start, finish


def _exchange_steps(s_refs, recv_refs, send_sems, recv_sems, local_sems):
    n = len(s_refs)
    x, y, c = _my_place()
    mychip = 2 * x + y
    chips = [(1 - x, y), (x, 1 - y), (1 - x, 1 - y)]

    def copies():
        mine = [pltpu.make_async_copy(s_refs[a].at[mychip], recv_refs[a].at[mychip], local_sems.at[a])
                for a in range(n)]
        return mine + [pltpu.make_async_remote_copy(
            src_ref=s_refs[a].at[2 * px + py], dst_ref=recv_refs[a].at[mychip],
            send_sem=send_sems.at[j * n + a], recv_sem=recv_sems.at[j * n + a],
            device_id=(px, py, c), device_id_type=MESH) for j, (px, py) in enumerate(chips) for a in range(n)]

    def start():
        for cp in copies():
            cp.start()

    def finish():
        for cp in copies():
            cp.wait()

    return start, finish


def _swap_rider(bufs):
    n = len(bufs)
    return dict(steps=_swap_steps, ins=list(bufs), scratch=_sem_arrays(4 * n),
                outs=[jax.ShapeDtypeStruct((4,) + q.shape[2:], q.dtype) for q in bufs])


def _exchange_rider(parts):
    n = len(parts)
    return dict(steps=_exchange_steps, ins=list(parts), scratch=_sem_arrays(3 * n) + [pltpu.SemaphoreType.DMA((n,))],
                outs=[jax.ShapeDtypeStruct(q.shape, q.dtype) for q in parts])


def _run_rider(rider, name):
    n = len(rider["ins"])

    def body(*refs):
        start, finish = rider["steps"](refs[:n], refs[n:2 * n], *refs[2 * n:])
        start()
        finish()

    return _pcall(body, name=name, out_shape=rider["outs"], in_specs=[ANY] * n, out_specs=[ANY] * n,
                  scratch_shapes=rider["scratch"])(*rider["ins"])


def _ride(body, rider, n_in, n_out, nsteps):
    if rider is None:
        return body
    n = len(rider["ins"])
    ns = len(rider["scratch"])

    def wrapped(*refs):
        ins, r_in = refs[:n_in], refs[n_in:n_in + n]
        outs, r_out = refs[n_in + n:n_in + n + n_out], refs[n_in + n + n_out:n_in + 2 * n + n_out]
        rest = refs[n_in + 2 * n + n_out:]
        scratch, r_scratch = rest[:len(rest) - ns], rest[len(rest) - ns:]
        start, finish = rider["steps"](r_in, r_out, *r_scratch)
        pl.when(pl.program_id(0) == 0)(start)
        body(*ins, *outs, *scratch)
        pl.when(pl.program_id(0) == nsteps - 1)(finish)

    return wrapped


def _row_tile(a):
    for t in range(256, 7, -8):
        if a % t == 0:
            return t
    return a


def _add_sibling_part(buf, recv, out_dtype, name):
    _, _, nl, a, b = buf.shape
    ta = _row_tile(a)
    core = lax.axis_index("c").astype(jnp.int32).reshape(1)

    def body(core_ref, a_ref, b_ref, o_ref):
        o_ref[...] = (a_ref[...] + b_ref[...]).astype(o_ref.dtype)

    grid_spec = pltpu.PrefetchScalarGridSpec(
        num_scalar_prefetch=1, grid=(4, nl, a // ta),
        in_specs=[pl.BlockSpec((None, None, None, ta, b), lambda k, l, r, cr: (k, cr[0], l, r, 0)),
                  pl.BlockSpec((None, None, ta, b), lambda k, l, r, cr: (k, l, r, 0))],
        out_specs=pl.BlockSpec((None, None, ta, b), lambda k, l, r, cr: (k, l, r, 0)))
    return _pcall(body, name=name, grid_spec=grid_spec,
                  out_shape=jax.ShapeDtypeStruct((4, nl, a, b), out_dtype),
                  compiler_params=_cp("parallel", "parallel", "parallel"))(core, buf, recv)


def _sum_adamw(stack, w, m, v, name):
    ns, nl, a, b = stack.shape
    ta = _row_tile(a)
    c1 = 1.0 - ADAM_B1 ** ADAM_STEP
    c2 = 1.0 - ADAM_B2 ** ADAM_STEP

    def body(s_ref, w_ref, m_ref, v_ref, g_out, d_out, m_out, v_out):
        g = s_ref[0].astype(F32)
        for k in range(1, ns):
            g = g + s_ref[k].astype(F32)
        m_new = ADAM_B1 * m_ref[...] + (1.0 - ADAM_B1) * g
        v_new = ADAM_B2 * v_ref[...] + (1.0 - ADAM_B2) * (g * g)
        m_hat = m_new / c1
        v_hat = v_new / c2
        g_out[...] = g
        d_out[...] = -ADAM_LR * (m_hat / (jnp.sqrt(v_hat) + ADAM_EPS) + ADAM_WD * w_ref[...])
        m_out[...] = m_new
        v_out[...] = v_new

    blk = pl.BlockSpec((None, ta, b), lambda l, i: (l, i, 0))
    return _pcall(
        body, name=name, grid=(nl, a // ta),
        in_specs=[pl.BlockSpec((ns, None, ta, b), lambda l, i: (0, l, i, 0)), blk, blk, blk],
        out_specs=[blk] * 4,
        out_shape=[jax.ShapeDtypeStruct((nl, a, b), F32)] * 4,
        compiler_params=_cp("parallel", "parallel"))(stack, w, m, v)


def _mod_fwd(cvec, wmod, bmod, name):
    _, d = cvec.shape

    def body(c_ref, w_ref, b_ref, o_ref):
        cv = c_ref[...]
        cs = cv * _sig(cv)
        o_ref[...] = _dot(cs.astype(BF16), w_ref[...]) + b_ref[...]

    return _pcall(
        body, name=name, grid=(3,),
        in_specs=[_full((8, d)), pl.BlockSpec((d, d), lambda n: (0, n)), pl.BlockSpec((1, d), lambda n: (0, n))],
        out_specs=pl.BlockSpec((8, d), lambda n: (0, n)),
        out_shape=jax.ShapeDtypeStruct((8, 3 * d), F32),
        compiler_params=_cp("parallel"))(cvec, wmod, bmod)


def _mod_bwd(dmod, cvec, cvec_t, wmod, name):
    _, d = cvec.shape

    def body(dm_ref, c_ref, ct_ref, w_ref, dw_ref, db_ref, dc_ref):
        n = pl.program_id(0)
        dm = dm_ref[...]
        ct = ct_ref[...]
        cs_t = _bf_round(ct * _sig(ct))
        d0 = _bf_round(dm[0:1, :])
        d1 = _bf_round(dm[1:2, :])
        dw_ref[...] = cs_t[:, 0:1] * d0 + cs_t[:, 1:2] * d1
        db_ref[...] = dm[0:1, :] + dm[1:2, :]

        @pl.when(n == 0)
        def _():
            dc_ref[...] = jnp.zeros_like(dc_ref)

        dc_ref[...] += _dot_nt(dm.astype(BF16), w_ref[...])

        @pl.when(n == 2)
        def _():
            cv = c_ref[...]
            s = _sig(cv)
            dc_ref[...] = dc_ref[...] * (s * (1.0 + cv * (1.0 - s)))

    return _pcall(
        body, name=name, grid=(3,),
        in_specs=[pl.BlockSpec((8, d), lambda n: (0, n)), _full((8, d)), _full((d, 128)),
                  pl.BlockSpec((d, d), lambda n: (0, n))],
        out_specs=[pl.BlockSpec((d, d), lambda n: (0, n)), pl.BlockSpec((1, d), lambda n: (0, n)),
                   _full((8, d))],
        out_shape=[jax.ShapeDtypeStruct((d, 3 * d), F32), jax.ShapeDtypeStruct((1, 3 * d), F32),
                   jax.ShapeDtypeStruct((8, d), F32)],
        compiler_params=_cp("arbitrary"))(dmod, cvec, cvec_t, wmod)


def _seg_rows(mod_ref, lo, hi, is_ctx):
    return jnp.where(is_ctx, mod_ref[1:2, lo:hi], mod_ref[0:1, lo:hi])


def _inproj(xa, modv, gpre, wp, n_lat, tm, name, shards=()):
    na, d = xa.shape
    wcols = wp.shape[1]
    tn = 5 * d // 4 if (5 * d // 4) % 128 == 0 and wcols % (5 * d // 4) == 0 else d // 2
    ns = len(shards)
    ni, nj = na // tm, wcols // tn

    def body(x_ref, mod_ref, g_ref, w_ref, *rest):
        p_ref, ht_ref = rest[ns:ns + 2]
        h_s = rest[2 * ns + 2]
        i = pl.program_id(0)
        j = pl.program_id(1)

        if ns:
            start, relay, finish = _gather_steps(rest[:ns], rest[ns + 2:2 * ns + 2], *rest[2 * ns + 3:])
            pl.when(jnp.logical_and(i == 0, j == 0))(start)
            pl.when(jnp.logical_and(i == max(ni - 2, 0), j == 0))(relay)

        @pl.when(j == 0)
        def _():
            x = x_ref[...]
            r = lax.rsqrt(_mean(x * x) + EPS)
            row = i * tm + lax.broadcasted_iota(jnp.int32, (tm, 1), 0)
            is_ctx = row >= n_lat
            sh = _seg_rows(mod_ref, 0, d, is_ctx)
            sc = _seg_rows(mod_ref, d, 2 * d, is_ctx)
            h = (x * r * g_ref[...]) * (1.0 + sc) + sh
            h_s[...] = h.astype(BF16)
            ht_ref[...] = h.T.astype(BF16)

        p_ref[...] = _dot(h_s[...], w_ref[...])

        if ns:
            pl.when(jnp.logical_and(i == ni - 1, j == nj - 1))(finish)

    return _pcall(
        body, name=name, grid=(ni, nj),
        in_specs=[pl.BlockSpec((tm, d), lambda i, j: (i, 0)), _full((8, 3 * d)), _full((1, d)),
                  pl.BlockSpec((d, tn), lambda i, j: (0, j))] + [ANY] * ns,
        out_specs=[pl.BlockSpec((tm, tn), lambda i, j: (i, j)), pl.BlockSpec((d, tm), lambda i, j: (0, i))]
        + [ANY] * ns,
        out_shape=[jax.ShapeDtypeStruct((na, wcols), F32), jax.ShapeDtypeStruct((d, na), BF16)]
        + [jax.ShapeDtypeStruct((N_DEV,) + q.shape, q.dtype) for q in shards],
        scratch_shapes=[pltpu.VMEM((tm, d), BF16)] + (_gather_scratch(ns) if ns else []),
        compiler_params=_cp("arbitrary" if ns else "parallel", "arbitrary"))(xa, modv, gpre, wp, *shards)


def _lane_tile(t, width):
    if width >= 128:
        return jnp.tile(t, (1, width // 128))
    return t[:, :width]


def _partner(x):
    w = x.shape[-1]
    lane = lax.broadcasted_iota(jnp.int32, x.shape, 1)
    low = (lane % HEAD_DIM) < (HEAD_DIM // 2)
    return jnp.where(low, pltpu.roll(x, w - HEAD_DIM // 2, 1), pltpu.roll(x, HEAD_DIM // 2, 1))


def _qknorm_fwd(p, cos_t, sin_t, gq, gk, esel, esel_t, tm, name):
    na = p.shape[0]
    d = gq.shape[1]
    kw = d // GROUP

    def body(q_ref, kv_ref, cos_ref, sin_ref, gq_ref, gk_ref, es_ref, est_ref, qt_ref, ko_ref, vo_ref, kt_ref,
             vt_ref):
        cos = cos_ref[...]
        sin = sin_ref[...]

        def norm_rope(xh, g, w):
            ms = _head_sum(xh * xh, es_ref[0:w, :], est_ref[:, 0:w]) * (1.0 / HEAD_DIM)
            xn = xh * lax.rsqrt(ms + EPS) * g
            return xn * _lane_tile(cos, w) + _partner(xn) * _lane_tile(sin, w)

        qt_ref[...] = (norm_rope(q_ref[...], gq_ref[...], d) * Q_PRESCALE).T.astype(BF16)
        kv = kv_ref[...]
        k = norm_rope(kv[:, 0:kw], gk_ref[...], kw)
        v = kv[:, kw:2 * kw]
        ko_ref[...] = k.astype(BF16)
        vo_ref[...] = v.astype(BF16)
        kt_ref[...] = k.T.astype(BF16)
        vt_ref[...] = v.T.astype(BF16)

    cols = lambda w: pl.BlockSpec((w, tm), lambda i: (0, i))
    return _pcall(
        body, name=name, grid=(na // tm,),
        in_specs=[_rows(tm, d, 3), _rows(tm, d // 2, 14), _rows(tm, 128), _rows(tm, 128),
                  _full((1, d)), _full((1, kw)), _full((d, 128)), _full((128, d))],
        out_specs=[cols(d), _rows(tm, kw), _rows(tm, kw), cols(kw), cols(kw)],
        out_shape=[jax.ShapeDtypeStruct((d, na), BF16), jax.ShapeDtypeStruct((na, kw), BF16),
                   jax.ShapeDtypeStruct((na, kw), BF16), jax.ShapeDtypeStruct((kw, na), BF16),
                   jax.ShapeDtypeStruct((kw, na), BF16)],
        compiler_params=_cp("parallel"))(p, p, cos_t, sin_t, gq, gk, esel, esel_t)


def _attn_fwd(qt_hm, k_hm, vt_hm, na, tq, tk, cw, name):
    h, hd, _ = qt_hm.shape
    kv, nkeys, _ = k_hm.shape
    vrows = vt_hm.shape[1]
    nq, nk = na // tq, nkeys // tk
    nsub = tq // cw
    chains = [(a, u) for a in range(GROUP) for u in range(nsub)]

    def body(q_ref, k_ref, vt_ref, o_ref, lse_ref, m_s, acc_s):
        j = pl.program_id(2)

        @pl.when(j == 0)
        def _():
            m_s[...] = jnp.full_like(m_s, NEG_BIG)
            acc_s[...] = jnp.zeros_like(acc_s)

        k = k_ref[...]
        vt = vt_ref[...]

        def scores(n):
            a, u = chains[n]
            return _dot(k, q_ref[a, :, u * cw:(u + 1) * cw])

        s_tiles = {n: scores(n) for n in range(min(ATTN_FWD_AHEAD, len(chains)))}
        pending = None
        for n, (a, u) in enumerate(chains):
            cols = slice(u * cw, (u + 1) * cw)
            s_t = s_tiles.pop(n)
            m_prev = m_s[a, :, cols]
            m_new = jnp.maximum(m_prev, jnp.max(s_t, axis=0, keepdims=True))
            m_s[a, :, cols] = m_new
            pv = _dot(vt, jnp.exp2(s_t - m_new).astype(BF16))
            if pending is not None:
                pa, pcols, palpha, ppv = pending
                acc_s[pa, :, pcols] = palpha * acc_s[pa, :, pcols] + ppv
            pending = (a, cols, jnp.exp2(m_prev - m_new), pv)
            if n + ATTN_FWD_AHEAD < len(chains):
                s_tiles[n + ATTN_FWD_AHEAD] = scores(n + ATTN_FWD_AHEAD)
        pa, pcols, palpha, ppv = pending
        acc_s[pa, :, pcols] = palpha * acc_s[pa, :, pcols] + ppv

        @pl.when(j == nk - 1)
        def _():
            for a in range(GROUP):
                acc = acc_s[a]
                l = acc[hd:hd + 1, :]
                o_ref[a] = acc[0:hd, :] / l
                lse_ref[a] = m_s[a] + jnp.log2(l)

    return _pcall(
        body, name=name, grid=(kv, nq, nk),
        in_specs=[pl.BlockSpec((GROUP, hd, tq), lambda g, i, j: (g, 0, i)),
                  pl.BlockSpec((None, tk, hd), lambda g, i, j: (g, j, 0)),
                  pl.BlockSpec((None, vrows, tk), lambda g, i, j: (g, 0, j))],
        out_specs=[pl.BlockSpec((GROUP, hd, tq), lambda g, i, j: (g, 0, i)),
                   pl.BlockSpec((GROUP, 1, tq), lambda g, i, j: (g, 0, i))],
        out_shape=[jax.ShapeDtypeStruct((h, hd, na), F32), jax.ShapeDtypeStruct((h, 1, na), F32)],
        scratch_shapes=[pltpu.VMEM((GROUP, 1, tq), F32), pltpu.VMEM((GROUP, vrows, tq), F32)],
        compiler_params=_cp("parallel", "parallel", "arbitrary"))(qt_hm, k_hm, vt_hm)


def _window(win_ref, prev, cur, nxt, first, last, tm):
    win_ref[0:HALO, :] = jnp.where(first, 0.0, prev)
    win_ref[HALO:HALO + tm, :] = cur
    win_ref[HALO + tm:HALO + tm + HALO, :] = jnp.where(last, 0.0, nxt)


def _lane_blocks(d):
    return [slice(b, b + 128) for b in range(0, d, 128)]


def _sublane_shifts(slab):
    n = slab.shape[0]
    for b in range(8):
        sh = slab if b == 0 else pltpu.roll(slab, n - b, 0)
        for a8 in range(0, 2 * HALO, 8):
            yield a8 + b, sh[a8:a8 + CONV_ROWS, :]


def _halo_specs(tm, d, na, colblk):
    per = tm // HALO
    last_blk = na // HALO - 1
    prev = pl.BlockSpec((HALO, d), lambda i: (jnp.maximum(i * per - 1, 0), colblk))
    nxt = pl.BlockSpec((HALO, d), lambda i: (jnp.minimum((i + 1) * per, last_blk), colblk))
    return prev, nxt


def _seq_ends(i, n_lat, na, tm):
    first = jnp.logical_or(i == 0, i == n_lat // tm)
    last = jnp.logical_or(i == n_lat // tm - 1, i == na // tm - 1)
    return first, last


def _conv_fwd(p, conv_w, conv_b, ln_g, ln_b, n_lat, tm, name):
    na = p.shape[0]
    ktaps, d = conv_w.shape
    pad = ktaps // 2

    def body(a_ref, ap_ref, an_ref, g_ref, gp_ref, gn_ref, ga_ref, w_ref, cb_ref, lg_ref, lb_ref,
             y5_ref, y5t_ref, y2_ref, win):
        i = pl.program_id(0)
        first, last = _seq_ends(i, n_lat, na, tm)
        _window(win, ap_ref[...] * _sig(gp_ref[...]), a_ref[...] * _sig(g_ref[...]),
                an_ref[...] * _sig(gn_ref[...]), first, last, tm)

        def chunk(c, carry):
            r = pl.multiple_of(c * CONV_ROWS, CONV_ROWS)
            for lanes in _lane_blocks(d):
                acc = jnp.zeros((CONV_ROWS, 128), F32)
                for off, sh in _sublane_shifts(win[pl.ds(r, CONV_ROWS + 2 * HALO), lanes]):
                    k = off - (HALO - pad)
                    if 0 <= k < ktaps:
                        acc = acc + w_ref[k:k + 1, lanes] * sh
                y2_ref[pl.ds(r, CONV_ROWS), lanes] = acc + cb_ref[:, lanes]
            return carry

        lax.fori_loop(0, tm // CONV_ROWS, chunk, 0)
        y2 = y2_ref[...]
        xc = y2 - _mean(y2)
        y3 = xc * lax.rsqrt(_mean(xc * xc) + EPS) * lg_ref[...] + lb_ref[...]
        gate = ga_ref[...]
        y5 = (y3 * _sig(y3)) * (gate * _sig(gate))
        y5_ref[...] = y5.astype(BF16)
        y5t_ref[...] = y5.T.astype(BF16)

    ap, an = _halo_specs(tm, d, na, 0)
    gp, gn = _halo_specs(tm, d, na, 1)
    return _pcall(
        body, name=name, grid=(na // tm,),
        in_specs=[_rows(tm, d, 0), ap, an, _rows(tm, d, 1), gp, gn, _rows(tm, d, 2),
                  _full((ktaps, d)), _full((1, d)), _full((1, d)), _full((1, d))],
        out_specs=[_rows(tm, d), _cols(d, tm), _rows(tm, d)],
        out_shape=[jax.ShapeDtypeStruct((na, d), BF16), jax.ShapeDtypeStruct((d, na), BF16),
                   jax.ShapeDtypeStruct((na, d), F32)],
        scratch_shapes=[pltpu.VMEM((tm + 2 * HALO, d), F32)],
        compiler_params=_cp("parallel"))(p, p, p, p, p, p, p, conv_w, conv_b, ln_g, ln_b)


def _merge_fwd(y5, ot_lat, ot_ctx, p, xa, modv, gpost, wc, wa, wo, n_lat, tm, name):
    na, d = xa.shape

    def body(y5_ref, ol_ref, oc_ref, gb_ref, ma_ref, mb_ref, x_ref, mod_ref, gp_ref, wc_ref, wa_ref, wo_ref,
             xn_ref, yc_ref, ya_ref, out_ref, zt_ref, ogt_ref):
        is_ctx = pl.program_id(0) >= n_lat // tm
        gate_b = gb_ref[...]
        o = jnp.where(is_ctx, oc_ref[...], ol_ref[...]).T
        og = o * (gate_b * _sig(gate_b))
        ogt_ref[...] = og.T.astype(BF16)
        yc = _dot(y5_ref[...], wc_ref[...])
        ya = _dot(og.astype(BF16), wa_ref[...])
        yc_ref[...] = yc
        ya_ref[...] = ya
        z = _sig(ma_ref[...]) * yc + _sig(mb_ref[...]) * ya
        zt_ref[...] = z.T.astype(BF16)
        out = _dot(z.astype(BF16), wo_ref[...])
        out_ref[...] = out
        gt = _seg_rows(mod_ref, 2 * d, 3 * d, is_ctx)
        xn_ref[...] = x_ref[...] + gt * (out * lax.rsqrt(_mean(out * out) + EPS) * gp_ref[...])

    f32o = jax.ShapeDtypeStruct((na, d), F32)
    bft = jax.ShapeDtypeStruct((d, na), BF16)
    ol, oc = _split_cols(d, tm, n_lat)
    return _pcall(
        body, name=name, grid=(na // tm,),
        in_specs=[_rows(tm, d), ol, oc, _rows(tm, d, 4), _rows(tm, d, 5), _rows(tm, d, 6), _rows(tm, d),
                  _full((8, 3 * d)), _full((1, d)), _full((d, d)), _full((d, d)), _full((d, d))],
        out_specs=[_rows(tm, d)] * 4 + [_cols(d, tm)] * 2,
        out_shape=[f32o, f32o, f32o, f32o, bft, bft],
        compiler_params=_cp("parallel"))(y5, ot_lat, ot_ctx, p, p, p, xa, modv, gpost, wc, wa, wo)


def _loss_grad(xa, target, n_lat, tm, name):
    na, d = xa.shape
    nlt = n_lat // tm

    def body(x_ref, t_ref, dx_ref, loss_ref):
        i = pl.program_id(0)

        @pl.when(i == 0)
        def _():
            loss_ref[...] = jnp.zeros_like(loss_ref)

        @pl.when(i < nlt)
        def _():
            err = x_ref[...] - t_ref[...]
            dx_ref[...] = err * (1.0 / d)
            loss_ref[...] += 0.5 * jnp.sum(_mean(err * err))

        @pl.when(i >= nlt)
        def _():
            dx_ref[...] = jnp.zeros_like(dx_ref)

    return _pcall(
        body, name=name, grid=(na // tm,),
        in_specs=[_rows(tm, d), pl.BlockSpec((tm, d), lambda i: (jnp.minimum(i, nlt - 1), 0))],
        out_specs=[_rows(tm, d), _full((8, 128))],
        out_shape=[jax.ShapeDtypeStruct((na, d), F32), jax.ShapeDtypeStruct((8, 128), F32)],
        compiler_params=_cp("arbitrary"))(xa, target)


def _merge_bwd(dxn, out, yc, ya, ot_lat, ot_ctx, p, y2, modv, gpost, ln_g, ln_b, wo_t, wc_t, wa_t, esel, n_lat, tm,
               name, rider=None):
    na, d = dxn.shape

    def body(dx_ref, out_ref, yc_ref, ya_ref, ol_ref, oc_ref, gb_ref, ma_ref, mb_ref, gta_ref, y2_ref,
             mod_ref, gp_ref, lg_ref, lb_ref, wot_ref, wct_ref, wat_ref, es_ref,
             dgb_ref, dma_ref, dmb_ref, dgta_ref, dot_ref, dout_ref, dyc_ref, dya_ref, dy2_ref, dlt_ref, acc_ref):
        i = pl.program_id(0)
        ctx_tile = i >= n_lat // tm

        @pl.when(i == 0)
        def _():
            acc_ref[...] = jnp.zeros_like(acc_ref)

        gt = _seg_rows(mod_ref, 2 * d, 3 * d, ctx_tile)
        gp = gp_ref[...]
        dx = dx_ref[...]
        out = out_ref[...]
        r2 = lax.rsqrt(_mean(out * out) + EPS)
        n2 = out * r2
        dgt = _colsum(dx * (n2 * gp))

        @pl.when(ctx_tile)
        def _():
            acc_ref[1:2, :] += dgt

        @pl.when(jnp.logical_not(ctx_tile))
        def _():
            acc_ref[0:1, :] += dgt

        acc_ref[2:3, :] += _colsum(dx * gt * n2)
        dn2 = dx * gt * gp
        dout = (r2 * (dn2 - n2 * _mean(dn2 * n2))).astype(BF16)
        dout_ref[...] = dout
        dz = _dot(dout, wot_ref[...])
        sa = _sig(ma_ref[...])
        sb = _sig(mb_ref[...])
        dyc = (dz * sa).astype(BF16)
        dya = (dz * sb).astype(BF16)
        dyc_ref[...] = dyc
        dya_ref[...] = dya
        dma_ref[...] = (dz * yc_ref[...] * sa * (1.0 - sa)).astype(BF16)
        dmb_ref[...] = (dz * ya_ref[...] * sb * (1.0 - sb)).astype(BF16)
        dy5 = _dot(dyc, wct_ref[...])
        dog = _dot(dya, wat_ref[...])

        gate_b = gb_ref[...]
        sgb = _sig(gate_b)
        o = jnp.where(ctx_tile, oc_ref[...], ol_ref[...]).T
        do = dog * (gate_b * sgb)
        dot_ref[...] = do.T.astype(BF16)
        dgb_ref[...] = (dog * o * (sgb * (1.0 + gate_b * (1.0 - sgb)))).astype(BF16)
        dlt_ref[...] = _split_dot(do * o, es_ref[...]).T

        y2 = y2_ref[...]
        xc = y2 - _mean(y2)
        rstd = lax.rsqrt(_mean(xc * xc) + EPS)
        xhat = xc * rstd
        lg = lg_ref[...]
        y3 = xhat * lg + lb_ref[...]
        s3 = _sig(y3)
        gate_a = gta_ref[...]
        sga = _sig(gate_a)
        dgta_ref[...] = (dy5 * (y3 * s3) * (sga * (1.0 + gate_a * (1.0 - sga)))).astype(BF16)
        dy3 = dy5 * (gate_a * sga) * (s3 * (1.0 + y3 * (1.0 - s3)))
        acc_ref[3:4, :] += _colsum(dy3 * xhat)
        acc_ref[4:5, :] += _colsum(dy3)
        dxh = dy3 * lg
        dy2 = rstd * (dxh - _mean(dxh) - xhat * _mean(dxh * xhat))
        dy2_ref[...] = dy2
        acc_ref[5:6, :] += _colsum(dy2)

    f32o = jax.ShapeDtypeStruct((na, d), F32)
    bfo = jax.ShapeDtypeStruct((na, d), BF16)
    r = _rows(tm, d)
    ol, oc = _split_cols(d, tm, n_lat)
    ins = (dxn, out, yc, ya, ot_lat, ot_ctx, p, p, p, p, y2, modv, gpost, ln_g, ln_b, wo_t, wc_t, wa_t, esel)
    extra = rider or dict(ins=[], outs=[], scratch=[])
    return _pcall(
        _ride(body, rider, len(ins), 11, na // tm), name=name, grid=(na // tm,),
        in_specs=[r, r, r, r, ol, oc, _rows(tm, d, 4), _rows(tm, d, 5), _rows(tm, d, 6), _rows(tm, d, 2), r,
                  _full((8, 3 * d)), _full((1, d)), _full((1, d)), _full((1, d)),
                  _full((d, d)), _full((d, d)), _full((d, d)), _full((d, 128))] + [ANY] * len(extra["ins"]),
        out_specs=[r] * 4 + [_cols(d, tm)] + [r] * 4 + [_cols(128, tm), _full((8, d))] + [ANY] * len(extra["outs"]),
        out_shape=[bfo] * 4 + [jax.ShapeDtypeStruct((d, na), BF16)] + [bfo] * 3
        + [f32o, jax.ShapeDtypeStruct((128, na), F32), jax.ShapeDtypeStruct((8, d), F32)] + extra["outs"],
        scratch_shapes=extra["scratch"],
        compiler_params=_cp("arbitrary"),
    )(*ins, *extra["ins"])


def _attn_bwd(qt_hm, k_hm, kt_hm, v_hm, dot_hm, lse_r, dl_r, na, tq, tk, cw, name):
    h, hd, _ = qt_hm.shape
    kv, nkeys, _ = k_hm.shape
    nq, nk = na // tq, nkeys // tk
    nsub = tq // cw
    chains = [(a, u) for a in range(GROUP) for u in range(nsub)]

    def body(qt_ref, k_ref, kt_ref, v_ref, dot_ref, lse_ref, dl_ref, dq_hbm, dk_ref, dv_ref,
             dq_acc, dk_acc, dv_acc, sem):
        g = pl.program_id(0)
        j = pl.program_id(1)
        i = pl.program_id(2)

        @pl.when(jnp.logical_and(j == 0, i == 0))
        def _():
            dq_acc[...] = jnp.zeros_like(dq_acc)

        @pl.when(i == 0)
        def _():
            dk_acc[...] = jnp.zeros_like(dk_acc)
            dv_acc[...] = jnp.zeros_like(dv_acc)

        k = k_ref[...]
        v = v_ref[...]
        kt = kt_ref[...]

        def products(n):
            a, u = chains[n]
            cols = slice(u * cw, (u + 1) * cw)
            return _dot(k, qt_ref[a, :, cols]), _dot(v, dot_ref[a, :, cols])

        def accumulate(done):
            a, u, dv_t, dk_t, dq_t = done
            dv_acc[...] += dv_t
            dk_acc[...] += dk_t
            at = pl.multiple_of(i * tq + u * cw, cw)
            dq_acc[a, :, pl.ds(at, cw)] += dq_t

        tiles = {n: products(n) for n in range(min(ATTN_BWD_AHEAD, len(chains)))}
        pending = None
        for n, (a, u) in enumerate(chains):
            cols = slice(u * cw, (u + 1) * cw)
            s_t, dp_t = tiles.pop(n)
            p_t = jnp.exp2(s_t - lse_ref[a, :, cols])
            ds_b = (p_t * (dp_t - dl_ref[a, :, cols])).astype(BF16)
            p_b = p_t.astype(BF16)
            dv_t = _dot_nt(dot_ref[a, :, cols], p_b)
            dk_t = _dot_nt(qt_ref[a, :, cols], ds_b)
            dq_t = _dot(kt, ds_b)
            if pending is not None:
                accumulate(pending)
            pending = (a, u, dv_t, dk_t, dq_t)
            if n + ATTN_BWD_AHEAD < len(chains):
                tiles[n + ATTN_BWD_AHEAD] = products(n + ATTN_BWD_AHEAD)
        accumulate(pending)

        @pl.when(i == nq - 1)
        def _():
            dk_ref[...] = dk_acc[...]
            dv_ref[...] = dv_acc[...]

        @pl.when(jnp.logical_and(j == nk - 1, i == nq - 1))
        def _():
            cp = pltpu.make_async_copy(dq_acc, dq_hbm.at[pl.ds(g * GROUP, GROUP)], sem)
            cp.start()
            cp.wait()

    qtspec = pl.BlockSpec((GROUP, hd, tq), lambda g, j, i: (g, 0, i))
    kspec = pl.BlockSpec((None, tk, hd), lambda g, j, i: (g, j, 0))
    ktspec = pl.BlockSpec((None, hd, tk), lambda g, j, i: (g, 0, j))
    rspec = pl.BlockSpec((GROUP, 1, tq), lambda g, j, i: (g, 0, i))
    return _pcall(
        body, name=name, grid=(kv, nk, nq),
        in_specs=[qtspec, kspec, ktspec, kspec, qtspec, rspec, rspec],
        out_specs=[ANY, ktspec, ktspec],
        out_shape=[jax.ShapeDtypeStruct((h, hd, na), F32), jax.ShapeDtypeStruct((kv, hd, nkeys), F32),
                   jax.ShapeDtypeStruct((kv, hd, nkeys), F32)],
        scratch_shapes=[pltpu.VMEM((GROUP, hd, na), F32), pltpu.VMEM((hd, tk), F32), pltpu.VMEM((hd, tk), F32),
                        pltpu.SemaphoreType.DMA],
        compiler_params=_cp("arbitrary", "arbitrary", "arbitrary"),
    )(qt_hm, k_hm, kt_hm, v_hm, dot_hm, lse_r, dl_r)


def _qknorm_bwd(dqt_lat, dqt_ctx, dkt, dkt_ctx, dvt, dvt_ctx, p, cos_t, sin_t, gq, gk, esel, esel_t, n_lat, tm, name):
    na = p.shape[0]
    d = gq.shape[1]
    kw = d // GROUP

    def body(dql_ref, dqc_ref, dkl_ref, dkc_ref, dvl_ref, dvc_ref, q_ref, kv_ref, cos_ref, sin_ref, gq_ref, gk_ref,
             es_ref, est_ref, dqo_ref, dkvo_ref, acc_ref):
        is_ctx = pl.program_id(0) >= n_lat // tm

        @pl.when(pl.program_id(0) == 0)
        def _():
            acc_ref[...] = jnp.zeros_like(acc_ref)

        dq_in = jnp.where(is_ctx, dqc_ref[...], dql_ref[...]).T
        dk_in = (dkl_ref[...] + jnp.where(is_ctx, dkc_ref[...], 0.0)).T
        dv_in = (dvl_ref[...] + jnp.where(is_ctx, dvc_ref[...], 0.0)).T

        cos = cos_ref[...]
        sin = sin_ref[...]

        def back(dy, xh, g, w):
            es, es_t = es_ref[0:w, :], est_ref[:, 0:w]
            dn = dy * _lane_tile(cos, w) - _partner(dy) * _lane_tile(sin, w)
            rs = lax.rsqrt(_head_sum(xh * xh, es, es_t) * (1.0 / HEAD_DIM) + EPS)
            y = xh * rs
            dg = _colsum(dn * y)
            dyn = dn * g
            dx = rs * (dyn - y * (_head_sum(dyn * y, es, es_t) * (1.0 / HEAD_DIM)))
            return dx, dg

        dq, dgq = back(dq_in * ATTN_SCALE, q_ref[...], gq_ref[...], d)
        dqo_ref[...] = dq.astype(BF16)
        acc_ref[0:1, :] += dgq
        kv = kv_ref[...]
        dk, dgk = back(dk_in * (1.0 / LOG2E), kv[:, 0:kw], gk_ref[...], kw)
        acc_ref[1:2, 0:kw] += dgk
        dkvo_ref[:, 0:kw] = dk.astype(BF16)
        dkvo_ref[:, kw:2 * kw] = dv_in.astype(BF16)

    dql, dqc = _split_cols(d, tm, n_lat)
    _, kvc = _split_cols(kw, tm, n_lat)
    return _pcall(
        body, name=name, grid=(na // tm,),
        in_specs=[dql, dqc, _cols(kw, tm), kvc, _cols(kw, tm), kvc, _rows(tm, d, 3), _rows(tm, d // 2, 14),
                  _rows(tm, 128), _rows(tm, 128), _full((1, d)), _full((1, kw)), _full((d, 128)), _full((128, d))],
        out_specs=[_rows(tm, d), _rows(tm, d // 2), _full((8, d))],
        out_shape=[jax.ShapeDtypeStruct((na, d), BF16), jax.ShapeDtypeStruct((na, d // 2), BF16),
                   jax.ShapeDtypeStruct((8, d), F32)],
        compiler_params=_cp("arbitrary"))(dqt_lat, dqt_ctx, dkt, dkt_ctx, dvt, dvt_ctx, p, p, cos_t, sin_t, gq, gk,
                                          esel, esel_t)


def _conv_bwd(dy2, p, conv_w, n_lat, tm, name, rider=None):
    na = p.shape[0]
    ktaps, d = conv_w.shape
    pad = ktaps // 2

    def body(dy_ref, dyp_ref, dyn_ref, a_ref, ap_ref, an_ref, g_ref, gp_ref, gn_ref, w_ref,
             da_ref, dg_ref, dw_ref, dwin, ywin, dy1_s, part):
        i = pl.program_id(0)

        @pl.when(i == 0)
        def _():
            part[...] = jnp.zeros_like(part)

        first, last = _seq_ends(i, n_lat, na, tm)
        a = a_ref[...]
        sg = _sig(g_ref[...])
        _window(dwin, dyp_ref[...], dy_ref[...], dyn_ref[...], first, last, tm)
        _window(ywin, ap_ref[...] * _sig(gp_ref[...]), a * sg, an_ref[...] * _sig(gn_ref[...]), first, last, tm)

        def chunk(c, carry):
            r = pl.multiple_of(c * CONV_ROWS, CONV_ROWS)
            for lanes in _lane_blocks(d):
                dy = dy_ref[pl.ds(r, CONV_ROWS), lanes]
                acc = jnp.zeros((CONV_ROWS, 128), F32)
                for off, sh in _sublane_shifts(dwin[pl.ds(r, CONV_ROWS + 2 * HALO), lanes]):
                    k = HALO + pad - off
                    if 0 <= k < ktaps:
                        acc = acc + w_ref[k:k + 1, lanes] * sh
                dy1_s[pl.ds(r, CONV_ROWS), lanes] = acc
                for off, sh in _sublane_shifts(ywin[pl.ds(r, CONV_ROWS + 2 * HALO), lanes]):
                    k = off - (HALO - pad)
                    if 0 <= k < ktaps:
                        part[k, :, lanes] += jnp.sum((dy * sh).reshape(CONV_ROWS // 8, 8, 128), axis=0)
            return carry

        lax.fori_loop(0, tm // CONV_ROWS, chunk, 0)
        dy1 = dy1_s[...]
        da_ref[...] = (dy1 * sg).astype(BF16)
        dg_ref[...] = (dy1 * a * sg * (1.0 - sg)).astype(BF16)

        @pl.when(i == na // tm - 1)
        def _():
            dw_ref[...] = jnp.sum(part[...], axis=1)

    dyp, dyn = _halo_specs(tm, d, na, 0)
    ap, an = _halo_specs(tm, d, na, 0)
    gp, gn = _halo_specs(tm, d, na, 1)
    bfo = jax.ShapeDtypeStruct((na, d), BF16)
    ins = (dy2, dy2, dy2, p, p, p, p, p, p, conv_w)
    extra = rider or dict(ins=[], outs=[], scratch=[])
    return _pcall(
        _ride(body, rider, len(ins), 3, na // tm), name=name, grid=(na // tm,),
        in_specs=[_rows(tm, d), dyp, dyn, _rows(tm, d, 0), ap, an, _rows(tm, d, 1), gp, gn, _full((ktaps, d))]
        + [ANY] * len(extra["ins"]),
        out_specs=[_rows(tm, d), _rows(tm, d), _full((ktaps, d))] + [ANY] * len(extra["outs"]),
        out_shape=[bfo, bfo, jax.ShapeDtypeStruct((ktaps, d), F32)] + extra["outs"],
        scratch_shapes=[pltpu.VMEM((tm + 2 * HALO, d), F32), pltpu.VMEM((tm + 2 * HALO, d), F32),
                        pltpu.VMEM((tm, d), F32), pltpu.VMEM((ktaps, 8, d), F32)] + extra["scratch"],
        compiler_params=_cp("arbitrary"))(*ins, *extra["ins"])


def _inproj_bwd(segs, dkv, xa, dxn, modv, gpre, wp_t, n_lat, out_rows, tm, name):
    na, d = xa.shape
    nseg = len(segs)
    wrows = wp_t.shape[0]

    def body(*refs):
        seg_refs = refs[:nseg]
        dkv_ref, x_ref, dxn_ref, mod_ref, g_ref, wt_hbm, dx_ref, acc_ref, wt, sem = refs[nseg:]
        i = pl.program_id(0)
        ctx_tile = i >= n_lat // tm

        @pl.when(i == 0)
        def _():
            cp = pltpu.make_async_copy(wt_hbm, wt, sem)
            cp.start()
            cp.wait()
            acc_ref[...] = jnp.zeros_like(acc_ref)

        dh = _dot(dkv_ref[...], wt[nseg * d:wrows, :])
        for s in range(nseg):
            dh = dh + _dot(seg_refs[s][...], wt[s * d:(s + 1) * d, :])
        x = x_ref[...]
        r = lax.rsqrt(_mean(x * x) + EPS)
        xn = x * r
        g = g_ref[...]
        sc1 = 1.0 + _seg_rows(mod_ref, d, 2 * d, ctx_tile)
        dsh = _colsum(dh)
        dsc = _colsum(dh * (xn * g))

        @pl.when(ctx_tile)
        def _():
            acc_ref[1:2, :] += dsh
            acc_ref[3:4, :] += dsc

        @pl.when(jnp.logical_not(ctx_tile))
        def _():
            acc_ref[0:1, :] += dsh
            acc_ref[2:3, :] += dsc

        acc_ref[4:5, :] += _colsum(dh * xn * sc1)
        dxh = dh * g * sc1

        @pl.when(i < out_tiles)
        def _():
            dx_ref[...] = dxn_ref[...] + r * (dxh - xn * _mean(dxh * xn))

    r_ = _rows(tm, d)
    out_tiles = out_rows // tm
    return _pcall(
        body, name=name, grid=(na // tm,),
        in_specs=[r_] * nseg + [_rows(tm, d // 2), r_, r_, _full((8, 3 * d)), _full((1, d)), ANY],
        out_specs=[pl.BlockSpec((tm, d), lambda i: (jnp.minimum(i, out_tiles - 1), 0)), _full((8, d))],
        out_shape=[jax.ShapeDtypeStruct((out_rows, d), F32), jax.ShapeDtypeStruct((8, d), F32)],
        scratch_shapes=[pltpu.VMEM(wp_t.shape, BF16), pltpu.SemaphoreType.DMA],
        compiler_params=_cp("arbitrary"))(*segs, dkv, xa, dxn, modv, gpre, wp_t)


def _grad_matmul(a_t, b, tk, name):
    ka, na = a_t.shape
    nb = b.shape[1]
    tn = min(nb, 1024)

    def body(a_ref, b_ref, o_ref):
        @pl.when(pl.program_id(1) == 0)
        def _():
            o_ref[...] = jnp.zeros_like(o_ref)

        o_ref[...] += _dot(a_ref[...], b_ref[...])

    return _pcall(
        body, name=name, grid=(nb // tn, na // tk),
        in_specs=[pl.BlockSpec((ka, tk), lambda n, k: (0, k)), pl.BlockSpec((tk, tn), lambda n, k: (k, n))],
        out_specs=pl.BlockSpec((ka, tn), lambda n, k: (0, n)),
        out_shape=jax.ShapeDtypeStruct((ka, nb), F32),
        compiler_params=_cp("parallel", "arbitrary"))(a_t, b)


def _pack(parts, cols, row_mult):
    flat = jnp.concatenate([q.astype(F32).reshape(-1) for q in parts])
    rows = -(-flat.shape[0] // (cols * row_mult)) * row_mult
    return jnp.pad(flat, (0, rows * cols - flat.shape[0])).reshape(1, rows, cols)


def _unpack(flat, shapes):
    out, off = [], 0
    for s in shapes:
        n = math.prod(s)
        out.append(flat[off:off + n].reshape(tuple(s)))
        off += n
    return out


def _cols_by_dest(g):
    l, a, w8 = g.shape
    return g.reshape(l, a, N_DEV, w8 // N_DEV).transpose(2, 0, 1, 3)


def _rows_by_dest(g):
    l, r8, b = g.shape
    return g.reshape(l, N_DEV, r8 // N_DEV, b).transpose(1, 0, 2, 3)


def _cols_from_src(s):
    n, l, a, w = s.shape
    return s.transpose(1, 2, 0, 3).reshape(l, a, n * w)


def _rows_from_src(s):
    n, l, r, b = s.shape
    return s.transpose(1, 0, 2, 3).reshape(l, n * r, b)


def _rope_tables(n_lat, n_ctx):
    half = HEAD_DIM // 2
    rows = n_lat // GRID_W
    row = jnp.repeat(jnp.arange(rows, dtype=F32), GRID_W)
    col = jnp.tile(jnp.arange(GRID_W, dtype=F32), rows)
    inv_freq = ROPE_THETA ** (-jnp.arange(0, half, 2, dtype=F32) / half)
    ang = jnp.concatenate([row[:, None] * inv_freq, col[:, None] * inv_freq], axis=-1)
    cos, sin = jnp.cos(ang), jnp.sin(ang)
    cos_t = jnp.concatenate([jnp.tile(cos, (1, 4)), jnp.ones((n_ctx, 128), F32)], axis=0)
    sin_t = jnp.concatenate([jnp.tile(jnp.concatenate([-sin, sin], axis=-1), (1, 2)),
                             jnp.zeros((n_ctx, 128), F32)], axis=0)
    return cos_t, sin_t


def _to_heads(t):
    na, w = t.shape
    return t.reshape(na, w // HEAD_DIM, HEAD_DIM).transpose(1, 0, 2)


def kernel(x, c, ctx, c_ctx, w_mod, b_mod, g_pre, g_post, w_in, conv_w, conv_b, ln_g, ln_b, w_conv_out, q_norm_g, k_norm_g, w_attn_out, w_out, loss_target, m_c_ctx, m_w_mod, m_b_mod, m_g_pre, m_g_post, m_w_in, m_conv_w, m_conv_b, m_ln_g, m_ln_b, m_w_conv_out, m_q_norm_g, m_k_norm_g, m_w_attn_out, m_w_out, v_c_ctx, v_w_mod, v_b_mod, v_g_pre, v_g_post, v_w_in, v_conv_w, v_conv_b, v_ln_g, v_ln_b, v_w_conv_out, v_q_norm_g, v_k_norm_g, v_w_attn_out, v_w_out):
    depth, d, _ = w_mod.shape
    n_lat, n_ctx = x.shape[1], ctx.shape[1]
    na = n_lat + n_ctx
    heads = d // HEAD_DIM
    kvh = heads // GROUP
    kw = d // GROUP
    ktaps = conv_w.shape[1]
    tm = n_ctx
    tm_half = tm // 2
    tbig = 3 * tm if na % (3 * tm) == 0 else tm
    tq_lat = 4 * tm if n_lat % (4 * tm) == 0 else tm
    tq_bwd = 4 * tm if n_lat % (4 * tm) == 0 else tm

    def layer_shards(l):
        return [w_mod[l].astype(BF16), w_in[l].astype(BF16), w_conv_out[l].astype(BF16), w_attn_out[l].astype(BF16),
                w_out[l].astype(BF16), conv_w[l]]

    def full_weights(gathered):
        s_mod, s_in, s_co, s_ao, s_oo, s_cw = [q[:, None] for q in gathered]
        win = _cols_from_src(s_in)[0]
        wp = jnp.concatenate([win[:, :4 * d], win[:, 4 * d + 2 * kw:], win[:, 4 * d:4 * d + 2 * kw]], axis=1)
        return dict(wmod=_cols_from_src(s_mod)[0], wp=wp, wc=_rows_from_src(s_co)[0], wa=_rows_from_src(s_ao)[0],
                    wo=_rows_from_src(s_oo)[0], convw=_cols_from_src(s_cw)[0])

    weights = [full_weights(_all_gather(layer_shards(0), "gather_weights_l0"))]

    cos_t, sin_t = _rope_tables(n_lat, n_ctx)
    lane = jnp.arange(d)
    esel = (lane[:, None] // HEAD_DIM == jnp.arange(128)[None, :]).astype(BF16)
    esel_t = esel.T
    cvec = jnp.zeros((8, d), F32).at[0].set(c[0]).at[1].set(c_ctx)
    cvec_t = jnp.zeros((d, 128), F32).at[:, 0].set(c[0]).at[:, 1].set(c_ctx)

    xa = jnp.concatenate([x[0], ctx[0]], axis=0)

    saved = []
    for l in range(depth):
        tag = f"_l{l}"
        gq = jnp.tile(q_norm_g[l], heads)[None, :]
        gk = jnp.tile(k_norm_g[l], heads // GROUP)[None, :]
        wl = weights[l]
        modv = _mod_fwd(cvec, wl["wmod"], b_mod[l][None, :], "mod_fwd" + tag)
        p, h_t, *gathered = _inproj(xa, modv, g_pre[l][None, :], wl["wp"], n_lat, tbig, "inproj" + tag,
                                    shards=layer_shards(l + 1) if l + 1 < depth else ())
        if gathered:
            weights.append(full_weights(gathered))
        q_t, kr, vb, k_t, v_t = _qknorm_fwd(p, cos_t, sin_t, gq, gk, esel, esel_t, tm, "qknorm_fwd" + tag)
        qt_hm = q_t.reshape(heads, HEAD_DIM, na)
        kt_hm = k_t.reshape(kvh, HEAD_DIM, na)
        k_hm, v_hm = _to_heads(kr), _to_heads(vb)
        vt_ones = jnp.concatenate([v_t.reshape(kvh, HEAD_DIM, na), jnp.ones((kvh, 16, na), BF16)], axis=1)
        ot_lat, lse_lat = _attn_fwd(qt_hm, k_hm, vt_ones, n_lat, tq_lat, tbig, tm, "attn_fwd" + tag)
        ot_ctx, lse_ctx = _attn_fwd(qt_hm[:, :, n_lat:], k_hm[:, n_lat:], vt_ones[:, :, n_lat:], n_ctx, tm, tm, tm,
                                    "attn_ctx_fwd" + tag)
        ot_lat, ot_ctx = ot_lat.reshape(d, n_lat), ot_ctx.reshape(d, n_ctx)
        y5, y5_t, y2 = _conv_fwd(p, wl["convw"], conv_b[l][None, :], ln_g[l][None, :], ln_b[l][None, :], n_lat, tm,
                                 "conv_fwd" + tag)
        xa_new, yc, ya, out, z_t, og_t = _merge_fwd(y5, ot_lat, ot_ctx, p, xa, modv, g_post[l][None, :], wl["wc"],
                                                    wl["wa"], wl["wo"], n_lat, tm, "merge_fwd" + tag)
        saved.append(dict(xa=xa, modv=modv, p=p, h_t=h_t, qt_hm=qt_hm, k_hm=k_hm, kt_hm=kt_hm, v_hm=v_hm,
                          ot_lat=ot_lat, ot_ctx=ot_ctx, lse_lat=lse_lat, lse_ctx=lse_ctx,
                          y5_t=y5_t, y2=y2, yc=yc, ya=ya, out=out, z_t=z_t, og_t=og_t, gq=gq, gk=gk))
        xa = xa_new

    dxa, loss_blk = _loss_grad(xa, loss_target[0], n_lat, tm, "loss_grad")

    big_names = ["w_mod", "w_in", "w_conv_out", "w_attn_out", "w_out", "conv_w"]
    g_bmod, g_gpre, g_gpost, g_convb, g_lng, g_lnb, g_qg, g_kg = [], [], [], [], [], [], [], []
    g_cctx = jnp.zeros((d,), F32)
    pending_send, pending_layer, from_chips = None, None, {}

    def add_sibling_parts(send, swapped, layer):
        return [_add_sibling_part(sb, rc, BF16, f"reduce_sibling_add_{n}_l{layer}")
                for n, sb, rc in zip(big_names, send, swapped)]

    for l in reversed(range(depth)):
        tag = f"_l{l}"
        s = saved[l]
        wl = weights[l]
        p = s["p"]
        (dgb, dma, dmb, dgta, do_t, dout, dyc, dya, dy2, dl_t, acc_m, *swapped) = _merge_bwd(
            dxa, s["out"], s["yc"], s["ya"], s["ot_lat"], s["ot_ctx"], p, s["y2"], s["modv"], g_post[l][None, :],
            ln_g[l][None, :], ln_b[l][None, :], wl["wo"].T, wl["wc"].T, wl["wa"].T, esel, n_lat, tm_half,
            "merge_bwd" + tag, rider=_swap_rider(pending_send) if pending_send else None)
        exchange = _exchange_rider(add_sibling_parts(pending_send, swapped, pending_layer)) if swapped else None
        dl_r = dl_t[:heads].reshape(heads, 1, na)
        dot_hm = do_t.reshape(heads, HEAD_DIM, na)
        qt_hm, k_hm, kt_hm, v_hm = s["qt_hm"], s["k_hm"], s["kt_hm"], s["v_hm"]
        dqt_lat, dkt_hm, dvt_hm = _attn_bwd(qt_hm, k_hm, kt_hm, v_hm, dot_hm, s["lse_lat"], dl_r,
                                            n_lat, tq_bwd, tm, tm, "attn_bwd" + tag)
        dqt_ctx, dkt_ctx, dvt_ctx = _attn_bwd(
            qt_hm[:, :, n_lat:], k_hm[:, n_lat:], kt_hm[:, :, n_lat:], v_hm[:, n_lat:], dot_hm[:, :, n_lat:],
            s["lse_ctx"], dl_r[:, :, n_lat:], n_ctx, tm, tm, tm, "attn_ctx_bwd" + tag)
        dq, dkv, acc_q = _qknorm_bwd(
            dqt_lat.reshape(d, n_lat), dqt_ctx.reshape(d, n_ctx), dkt_hm.reshape(kw, na), dkt_ctx.reshape(kw, n_ctx),
            dvt_hm.reshape(kw, na), dvt_ctx.reshape(kw, n_ctx), p, cos_t, sin_t, s["gq"], s["gk"], esel, esel_t, n_lat, tm,
            "qknorm_bwd" + tag)
        da, dg, dconvw, *exchanged = _conv_bwd(dy2, p, wl["convw"], n_lat, tm, "conv_bwd" + tag, rider=exchange)
        if exchanged:
            from_chips[pending_layer] = exchanged
        segs = [da, dg, dgta, dq, dgb, dma, dmb]
        dxa, acc_h = _inproj_bwd(segs, dkv, s["xa"], dxa, s["modv"], g_pre[l][None, :], wl["wp"].T, n_lat,
                                 na if l else n_lat, tm, "inproj_bwd" + tag)

        dwp = [_grad_matmul(s["h_t"], sg, tbig, f"grad_w_in{k}" + tag) for k, sg in enumerate(segs + [dkv])]
        g_win = jnp.concatenate(dwp[:4] + [dwp[7]] + dwp[4:7], axis=1)
        g_wc = _grad_matmul(s["y5_t"], dyc, tbig, "grad_w_conv_out" + tag)
        g_wa = _grad_matmul(s["og_t"], dya, tbig, "grad_w_attn_out" + tag)
        g_wo = _grad_matmul(s["z_t"], dout, tbig, "grad_w_out" + tag)

        dmod = jnp.zeros((8, 3 * d), F32)
        dmod = dmod.at[0].set(jnp.concatenate([acc_h[0], acc_h[2], acc_m[0]]))
        dmod = dmod.at[1].set(jnp.concatenate([acc_h[1], acc_h[3], acc_m[1]]))
        dwm, dbm, dcv = _mod_bwd(dmod, cvec, cvec_t, wl["wmod"], "mod_bwd" + tag)
        by_dest = [_cols_by_dest(dwm[None]), _cols_by_dest(g_win[None]), _rows_by_dest(g_wc[None]),
                   _rows_by_dest(g_wa[None]), _rows_by_dest(g_wo[None]), _cols_by_dest(dconvw[None])]
        pending_send, pending_layer = [q.reshape((4, 2) + q.shape[1:]) for q in by_dest], l
        g_bmod.append(dbm[0])
        g_cctx = g_cctx + dcv[1]
        g_gpre.append(acc_h[4])
        g_gpost.append(acc_m[2])
        g_lng.append(acc_m[3])
        g_lnb.append(acc_m[4])
        g_convb.append(acc_m[5])
        g_qg.append(acc_q[0].reshape(heads, HEAD_DIM).sum(0))
        g_kg.append(acc_q[1, :kw].reshape(heads // GROUP, HEAD_DIM).sum(0))

    grad_x = dxa[None]

    def stack(lst):
        return jnp.stack(lst[::-1])

    big_w = dict(w_mod=w_mod, w_in=w_in, w_conv_out=w_conv_out, w_attn_out=w_attn_out, w_out=w_out, conv_w=conv_w)
    big_m = dict(w_mod=m_w_mod, w_in=m_w_in, w_conv_out=m_w_conv_out, w_attn_out=m_w_attn_out, w_out=m_w_out,
                 conv_w=m_conv_w)
    big_v = dict(w_mod=v_w_mod, w_in=v_w_in, w_conv_out=v_w_conv_out, w_attn_out=v_w_attn_out, w_out=v_w_out,
                 conv_w=v_conv_w)
    swapped = _run_rider(_swap_rider(pending_send), "reduce_sibling")
    from_chips[pending_layer] = _run_rider(
        _exchange_rider(add_sibling_parts(pending_send, swapped, pending_layer)), "reduce_chips")
    big_g, big_d, big_nm, big_nv = {}, {}, {}, {}
    for k, n in enumerate(big_names):
        st = jnp.concatenate([from_chips[l][k] for l in range(depth)], axis=1)
        big_g[n], big_d[n], big_nm[n], big_nv[n] = _sum_adamw(st, big_w[n], big_m[n], big_v[n], "adamw_" + n)

    small_names = ["c_ctx", "b_mod", "g_pre", "g_post", "conv_b", "ln_g", "ln_b", "q_norm_g", "k_norm_g", "loss"]
    zero1 = jnp.zeros((1,), F32)
    small_w = dict(c_ctx=c_ctx, b_mod=b_mod, g_pre=g_pre, g_post=g_post, conv_b=conv_b, ln_g=ln_g, ln_b=ln_b,
                   q_norm_g=q_norm_g, k_norm_g=k_norm_g, loss=zero1)
    small_m = dict(c_ctx=m_c_ctx, b_mod=m_b_mod, g_pre=m_g_pre, g_post=m_g_post, conv_b=m_conv_b, ln_g=m_ln_g,
                   ln_b=m_ln_b, q_norm_g=m_q_norm_g, k_norm_g=m_k_norm_g, loss=zero1)
    small_v = dict(c_ctx=v_c_ctx, b_mod=v_b_mod, g_pre=v_g_pre, g_post=v_g_post, conv_b=v_conv_b, ln_g=v_ln_g,
                   ln_b=v_ln_b, q_norm_g=v_q_norm_g, k_norm_g=v_k_norm_g, loss=zero1)
    small_g = dict(c_ctx=g_cctx, b_mod=stack(g_bmod), g_pre=stack(g_gpre), g_post=stack(g_gpost),
                   conv_b=stack(g_convb), ln_g=stack(g_lng), ln_b=stack(g_lnb), q_norm_g=stack(g_qg),
                   k_norm_g=stack(g_kg), loss=loss_blk[0, 0:1])
    small_shapes = [small_w[n].shape for n in small_names]

    def pack_small(tree):
        return _pack([tree[n] for n in small_names], d, 8)

    small_parts, = _all_gather([pack_small(small_g)], "gather_small_grads")
    small_out = _sum_adamw(small_parts, pack_small(small_w), pack_small(small_m), pack_small(small_v),
                           "adamw_replicated")
    sm_g, sm_d, sm_nm, sm_nv = [dict(zip(small_names, _unpack(o.reshape(-1), small_shapes))) for o in small_out]
    loss = sm_g["loss"].reshape(())

    order = ["c_ctx", "w_mod", "b_mod", "g_pre", "g_post", "w_in", "conv_w", "conv_b", "ln_g", "ln_b",
             "w_conv_out", "q_norm_g", "k_norm_g", "w_attn_out", "w_out"]

    def pick(big, small):
        return [big[n] if n in big else small[n] for n in order]

    return (loss, grad_x, *pick(big_g, sm_g), *pick(big_d, sm_d), *pick(big_nm, sm_nm), *pick(big_nv, sm_nv))
```

```python
import math

import jax
import jax.numpy as jnp
from jax import lax
from jax.experimental import pallas as pl
from jax.experimental.pallas import tpu as pltpu

F32 = jnp.float32
BF16 = jnp.bfloat16

HEAD_DIM = 64
GROUP = 4
GRID_W = 64
ROPE_THETA = 10000.0
EPS = 1e-6
ATTN_SCALE = HEAD_DIM ** -0.5
LOG2E = 1.4426950408889634
Q_PRESCALE = ATTN_SCALE * LOG2E
HALO = 16
CONV_ROWS = 64
ATTN_FWD_AHEAD = 5
ATTN_BWD_AHEAD = 3

ADAM_LR = 0.001
ADAM_B1 = 0.9
ADAM_B2 = 0.999
ADAM_EPS = 1e-08
ADAM_WD = 0.01
ADAM_STEP = 10

N_DEV = 8
MESH_AXES = ("x", "y", "c")
V7X_VMEM_LIMIT = 56 * 1024 * 1024
NEG_BIG = -1e30

MESH = pl.DeviceIdType.MESH
ANY = pl.BlockSpec(memory_space=pl.ANY)


def _pcall(body, **kw):
    return pl.pallas_call(body, **kw)


def _cp(*sem):
    return pltpu.CompilerParams(dimension_semantics=sem, vmem_limit_bytes=V7X_VMEM_LIMIT)


def _sig(x):
    return 0.5 * jnp.tanh(0.5 * x) + 0.5


def _mean(x):
    return jnp.mean(x, axis=-1, keepdims=True)


def _colsum(x):
    return jnp.sum(x, axis=0, keepdims=True)


def _bf_round(x):
    return x.astype(BF16).astype(F32)


def _dot(a, b):
    return jnp.dot(a, b, preferred_element_type=F32)


def _dot_nt(a, b):
    return lax.dot_general(a, b, (((1,), (1,)), ((), ())), preferred_element_type=F32)


def _split_dot(x, m):
    hi = x.astype(BF16)
    lo = (x - hi.astype(F32)).astype(BF16)
    return _dot(hi, m) + _dot(lo, m)


def _head_sum(x, es, es_t):
    return _split_dot(_split_dot(x, es), es_t)


def _full(shape):
    nd = len(shape)
    return pl.BlockSpec(shape, lambda *_: (0,) * nd)


def _rows(tm, width, colblk=0):
    return pl.BlockSpec((tm, width), lambda i: (i, colblk))


def _cols(height, tm):
    return pl.BlockSpec((height, tm), lambda i: (0, i))


def _split_cols(height, tm, n_lat):
    nl = n_lat // tm
    return (pl.BlockSpec((height, tm), lambda i: (0, jnp.minimum(i, nl - 1))),
            pl.BlockSpec((height, tm), lambda i: (0, jnp.maximum(i - nl, 0))))


def _my_place():
    return lax.axis_index("x"), lax.axis_index("y"), lax.axis_index("c")


def _sem_arrays(n):
    return [pltpu.SemaphoreType.DMA((n,)), pltpu.SemaphoreType.DMA((n,))]


def _gather_steps(x_refs, out_refs, send_sems, recv_sems, local_sems):
    n = len(x_refs)
    x, y, c = _my_place()
    me, sibling = (x, y, c), (x, y, 1 - c)
    chips = [(1 - x, y), (x, 1 - y), (1 - x, 1 - y)]

    def slab(a, px, py, pc):
        return out_refs[a].at[4 * px + 2 * py + pc]

    def copies(k, block, to, from_input=False):
        return [pltpu.make_async_remote_copy(
            src_ref=x_refs[a] if from_input else slab(a, *block), dst_ref=slab(a, *block),
            send_sem=send_sems.at[k * n + a], recv_sem=recv_sems.at[k * n + a],
            device_id=to, device_id_type=MESH) for a in range(n)]

    def mine():
        return [pltpu.make_async_copy(x_refs[a], slab(a, *me), local_sems.at[a]) for a in range(n)]

    def first():
        out = copies(0, me, sibling, True)
        for j, chip in enumerate(chips):
            out += copies(1 + j, me, (*chip, c), True)
        return out

    def onward(j):
        return copies(4 + j, (*chips[j], c), sibling)

    def start():
        for cp in mine() + first():
            cp.start()

    def relay():
        for j, chip in enumerate(chips):
            for cp in copies(1 + j, (*chip, c), me):
                cp.wait_recv()
            for cp in onward(j):
                cp.start()

    def finish():
        for cp in copies(0, sibling, me):
            cp.wait_recv()
        for j, chip in enumerate(chips):
            for cp in copies(4 + j, (*chip, 1 - c), me):
                cp.wait_recv()
        for cp in first() + onward(0) + onward(1) + onward(2):
            cp.wait_send()
        for cp in mine():
            cp.wait()

    return start, relay, finish


def _gather_scratch(n):
    return _sem_arrays(7 * n) + [pltpu.SemaphoreType.DMA((n,))]


def _all_gather(shards, name):
    n = len(shards)

    def body(*refs):
        start, relay, finish = _gather_steps(refs[:n], refs[n:2 * n], *refs[2 * n:])
        start()
        relay()
        finish()

    return _pcall(
        body, name=name,
        out_shape=[jax.ShapeDtypeStruct((N_DEV,) + q.shape, q.dtype) for q in shards],
        in_specs=[ANY] * n, out_specs=[ANY] * n,
        scratch_shapes=_gather_scratch(n),
    )(*shards)


def _swap_steps(buf_refs, recv_refs, send_sems, recv_sems):
    n = len(buf_refs)
    x, y, c = _my_place()

    def copies():
        return [pltpu.make_async_remote_copy(
            src_ref=buf_refs[a].at[k, 1 - c], dst_ref=recv_refs[a].at[k],
            send_sem=send_sems.at[k * n + a], recv_sem=recv_sems.at[k * n + a],
            device_id=(x, y, 1 - c), device_id_type=MESH) for k in range(4) for a in range(n)]

    def start():
        for cp in copies():
            cp.start()

    def finish():
        for cp in copies():
            cp.wait()

    return start, finish


def _exchange_steps(s_refs, recv_refs, send_sems, recv_sems, local_sems):
    n = len(s_refs)
    x, y, c = _my_place()
    mychip = 2 * x + y
    chips = [(1 - x, y), (x, 1 - y), (1 - x, 1 - y)]

    def copies():
        mine = [pltpu.make_async_copy(s_refs[a].at[mychip], recv_refs[a].at[mychip], local_sems.at[a])
                for a in range(n)]
        return mine + [pltpu.make_async_remote_copy(
            src_ref=s_refs[a].at[2 * px + py], dst_ref=recv_refs[a].at[mychip],
            send_sem=send_sems.at[j * n + a], recv_sem=recv_sems.at[j * n + a],
            device_id=(px, py, c), device_id_type=MESH) for j, (px, py) in enumerate(chips) for a in range(n)]

    def start():
        for cp in copies():
            cp.start()

    def finish():
        for cp in copies():
            cp.wait()

    return start, finish


def _swap_rider(bufs):
    n = len(bufs)
    return dict(steps=_swap_steps, ins=list(bufs), scratch=_sem_arrays(4 * n),
                outs=[jax.ShapeDtypeStruct((4,) + q.shape[2:], q.dtype) for q in bufs])


def _exchange_rider(parts):
    n = len(parts)
    return dict(steps=_exchange_steps, ins=list(parts), scratch=_sem_arrays(3 * n) + [pltpu.SemaphoreType.DMA((n,))],
                outs=[jax.ShapeDtypeStruct(q.shape, q.dtype) for q in parts])


def _run_rider(rider, name):
    n = len(rider["ins"])

    def body(*refs):
        start, finish = rider["steps"](refs[:n], refs[n:2 * n], *refs[2 * n:])
        start()
        finish()

    return _pcall(body, name=name, out_shape=rider["outs"], in_specs=[ANY] * n, out_specs=[ANY] * n,
                  scratch_shapes=rider["scratch"])(*rider["ins"])


def _ride(body, rider, n_in, n_out, nsteps):
    if rider is None:
        return body
    n = len(rider["ins"])
    ns = len(rider["scratch"])

    def wrapped(*refs):
        ins, r_in = refs[:n_in], refs[n_in:n_in + n]
        outs, r_out = refs[n_in + n:n_in + n + n_out], refs[n_in + n + n_out:n_in + 2 * n + n_out]
        rest = refs[n_in + 2 * n + n_out:]
        scratch, r_scratch = rest[:len(rest) - ns], rest[len(rest) - ns:]
        start, finish = rider["steps"](r_in, r_out, *r_scratch)
        pl.when(pl.program_id(0) == 0)(start)
        body(*ins, *outs, *scratch)
        pl.when(pl.program_id(0) == nsteps - 1)(finish)

    return wrapped


def _row_tile(a):
    for t in range(256, 7, -8):
        if a % t == 0:
            return t
    return a


def _add_sibling_part(buf, recv, out_dtype, name):
    _, _, nl, a, b = buf.shape
    ta = _row_tile(a)
    core = lax.axis_index("c").astype(jnp.int32).reshape(1)

    def body(core_ref, a_ref, b_ref, o_ref):
        o_ref[...] = (a_ref[...] + b_ref[...]).astype(o_ref.dtype)

    grid_spec = pltpu.PrefetchScalarGridSpec(
        num_scalar_prefetch=1, grid=(4, nl, a // ta),
        in_specs=[pl.BlockSpec((None, None, None, ta, b), lambda k, l, r, cr: (k, cr[0], l, r, 0)),
                  pl.BlockSpec((None, None, ta, b), lambda k, l, r, cr: (k, l, r, 0))],
        out_specs=pl.BlockSpec((None, None, ta, b), lambda k, l, r, cr: (k, l, r, 0)))
    return _pcall(body, name=name, grid_spec=grid_spec,
                  out_shape=jax.ShapeDtypeStruct((4, nl, a, b), out_dtype),
                  compiler_params=_cp("parallel", "parallel", "parallel"))(core, buf, recv)


def _sum_adamw(stack, w, m, v, name):
    ns, nl, a, b = stack.shape
    ta = _row_tile(a)
    c1 = 1.0 - ADAM_B1 ** ADAM_STEP
    c2 = 1.0 - ADAM_B2 ** ADAM_STEP

    def body(s_ref, w_ref, m_ref, v_ref, g_out, d_out, m_out, v_out):
        g = s_ref[0].astype(F32)
        for k in range(1, ns):
            g = g + s_ref[k].astype(F32)
        m_new = ADAM_B1 * m_ref[...] + (1.0 - ADAM_B1) * g
        v_new = ADAM_B2 * v_ref[...] + (1.0 - ADAM_B2) * (g * g)
        m_hat = m_new / c1
        v_hat = v_new / c2
        g_out[...] = g
        d_out[...] = -ADAM_LR * (m_hat / (jnp.sqrt(v_hat) + ADAM_EPS) + ADAM_WD * w_ref[...])
        m_out[...] = m_new
        v_out[...] = v_new

    blk = pl.BlockSpec((None, ta, b), lambda l, i: (l, i, 0))
    return _pcall(
        body, name=name, grid=(nl, a // ta),
        in_specs=[pl.BlockSpec((ns, None, ta, b), lambda l, i: (0, l, i, 0)), blk, blk, blk],
        out_specs=[blk] * 4,
        out_shape=[jax.ShapeDtypeStruct((nl, a, b), F32)] * 4,
        compiler_params=_cp("parallel", "parallel"))(stack, w, m, v)


def _mod_fwd(cvec, wmod, bmod, name):
    _, d = cvec.shape

    def body(c_ref, w_ref, b_ref, o_ref):
        cv = c_ref[...]
        cs = cv * _sig(cv)
        o_ref[...] = _dot(cs.astype(BF16), w_ref[...]) + b_ref[...]

    return _pcall(
        body, name=name, grid=(3,),
        in_specs=[_full((8, d)), pl.BlockSpec((d, d), lambda n: (0, n)), pl.BlockSpec((1, d), lambda n: (0, n))],
        out_specs=pl.BlockSpec((8, d), lambda n: (0, n)),
        out_shape=jax.ShapeDtypeStruct((8, 3 * d), F32),
        compiler_params=_cp("parallel"))(cvec, wmod, bmod)


def _mod_bwd(dmod, cvec, cvec_t, wmod, name):
    _, d = cvec.shape

    def body(dm_ref, c_ref, ct_ref, w_ref, dw_ref, db_ref, dc_ref):
        n = pl.program_id(0)
        dm = dm_ref[...]
        ct = ct_ref[...]
        cs_t = _bf_round(ct * _sig(ct))
        d0 = _bf_round(dm[0:1, :])
        d1 = _bf_round(dm[1:2, :])
        dw_ref[...] = cs_t[:, 0:1] * d0 + cs_t[:, 1:2] * d1
        db_ref[...] = dm[0:1, :] + dm[1:2, :]

        @pl.when(n == 0)
        def _():
            dc_ref[...] = jnp.zeros_like(dc_ref)

        dc_ref[...] += _dot_nt(dm.astype(BF16), w_ref[...])

        @pl.when(n == 2)
        def _():
            cv = c_ref[...]
            s = _sig(cv)
            dc_ref[...] = dc_ref[...] * (s * (1.0 + cv * (1.0 - s)))

    return _pcall(
        body, name=name, grid=(3,),
        in_specs=[pl.BlockSpec((8, d), lambda n: (0, n)), _full((8, d)), _full((d, 128)),
                  pl.BlockSpec((d, d), lambda n: (0, n))],
        out_specs=[pl.BlockSpec((d, d), lambda n: (0, n)), pl.BlockSpec((1, d), lambda n: (0, n)),
                   _full((8, d))],
        out_shape=[jax.ShapeDtypeStruct((d, 3 * d), F32), jax.ShapeDtypeStruct((1, 3 * d), F32),
                   jax.ShapeDtypeStruct((8, d), F32)],
        compiler_params=_cp("arbitrary"))(dmod, cvec, cvec_t, wmod)


def _seg_rows(mod_ref, lo, hi, is_ctx):
    return jnp.where(is_ctx, mod_ref[1:2, lo:hi], mod_ref[0:1, lo:hi])


def _inproj(xa, modv, gpre, wp, n_lat, tm, name, shards=()):
    na, d = xa.shape
    wcols = wp.shape[1]
    tn = 5 * d // 4 if (5 * d // 4) % 128 == 0 and wcols % (5 * d // 4) == 0 else d // 2
    ns = len(shards)
    ni, nj = na // tm, wcols // tn

    def body(x_ref, mod_ref, g_ref, w_ref, *rest):
        p_ref, ht_ref = rest[ns:ns + 2]
        h_s = rest[2 * ns + 2]
        i = pl.program_id(0)
        j = pl.program_id(1)

        if ns:
            start, relay, finish = _gather_steps(rest[:ns], rest[ns + 2:2 * ns + 2], *rest[2 * ns + 3:])
            pl.when(jnp.logical_and(i == 0, j == 0))(start)
            pl.when(jnp.logical_and(i == max(ni - 2, 0), j == 0))(relay)

        @pl.when(j == 0)
        def _():
            x = x_ref[...]
            r = lax.rsqrt(_mean(x * x) + EPS)
            row = i * tm + lax.broadcasted_iota(jnp.int32, (tm, 1), 0)
            is_ctx = row >= n_lat
            sh = _seg_rows(mod_ref, 0, d, is_ctx)
            sc = _seg_rows(mod_ref, d, 2 * d, is_ctx)
            h = (x * r * g_ref[...]) * (1.0 + sc) + sh
            h_s[...] = h.astype(BF16)
            ht_ref[...] = h.T.astype(BF16)

        p_ref[...] = _dot(h_s[...], w_ref[...])

        if ns:
            pl.when(jnp.logical_and(i == ni - 1, j == nj - 1))(finish)

    return _pcall(
        body, name=name, grid=(ni, nj),
        in_specs=[pl.BlockSpec((tm, d), lambda i, j: (i, 0)), _full((8, 3 * d)), _full((1, d)),
                  pl.BlockSpec((d, tn), lambda i, j: (0, j))] + [ANY] * ns,
        out_specs=[pl.BlockSpec((tm, tn), lambda i, j: (i, j)), pl.BlockSpec((d, tm), lambda i, j: (0, i))]
        + [ANY] * ns,
        out_shape=[jax.ShapeDtypeStruct((na, wcols), F32), jax.ShapeDtypeStruct((d, na), BF16)]
        + [jax.ShapeDtypeStruct((N_DEV,) + q.shape, q.dtype) for q in shards],
        scratch_shapes=[pltpu.VMEM((tm, d), BF16)] + (_gather_scratch(ns) if ns else []),
        compiler_params=_cp("arbitrary" if ns else "parallel", "arbitrary"))(xa, modv, gpre, wp, *shards)


def _lane_tile(t, width):
    if width >= 128:
        return jnp.tile(t, (1, width // 128))
    return t[:, :width]


def _partner(x):
    w = x.shape[-1]
    lane = lax.broadcasted_iota(jnp.int32, x.shape, 1)
    low = (lane % HEAD_DIM) < (HEAD_DIM // 2)
    return jnp.where(low, pltpu.roll(x, w - HEAD_DIM // 2, 1), pltpu.roll(x, HEAD_DIM // 2, 1))


def _qknorm_fwd(p, cos_t, sin_t, gq, gk, esel, esel_t, tm, name):
    na = p.shape[0]
    d = gq.shape[1]
    kw = d // GROUP

    def body(q_ref, kv_ref, cos_ref, sin_ref, gq_ref, gk_ref, es_ref, est_ref, qt_ref, ko_ref, vo_ref, kt_ref,
             vt_ref):
        cos = cos_ref[...]
        sin = sin_ref[...]

        def norm_rope(xh, g, w):
            ms = _head_sum(xh * xh, es_ref[0:w, :], est_ref[:, 0:w]) * (1.0 / HEAD_DIM)
            xn = xh * lax.rsqrt(ms + EPS) * g
            return xn * _lane_tile(cos, w) + _partner(xn) * _lane_tile(sin, w)

        qt_ref[...] = (norm_rope(q_ref[...], gq_ref[...], d) * Q_PRESCALE).T.astype(BF16)
        kv = kv_ref[...]
        k = norm_rope(kv[:, 0:kw], gk_ref[...], kw)
        v = kv[:, kw:2 * kw]
        ko_ref[...] = k.astype(BF16)
        vo_ref[...] = v.astype(BF16)
        kt_ref[...] = k.T.astype(BF16)
        vt_ref[...] = v.T.astype(BF16)

    cols = lambda w: pl.BlockSpec((w, tm), lambda i: (0, i))
    return _pcall(
        body, name=name, grid=(na // tm,),
        in_specs=[_rows(tm, d, 3), _rows(tm, d // 2, 14), _rows(tm, 128), _rows(tm, 128),
                  _full((1, d)), _full((1, kw)), _full((d, 128)), _full((128, d))],
        out_specs=[cols(d), _rows(tm, kw), _rows(tm, kw), cols(kw), cols(kw)],
        out_shape=[jax.ShapeDtypeStruct((d, na), BF16), jax.ShapeDtypeStruct((na, kw), BF16),
                   jax.ShapeDtypeStruct((na, kw), BF16), jax.ShapeDtypeStruct((kw, na), BF16),
                   jax.ShapeDtypeStruct((kw, na), BF16)],
        compiler_params=_cp("parallel"))(p, p, cos_t, sin_t, gq, gk, esel, esel_t)


def _attn_fwd(qt_hm, k_hm, vt_hm, na, tq, tk, cw, name):
    h, hd, _ = qt_hm.shape
    kv, nkeys, _ = k_hm.shape
    vrows = vt_hm.shape[1]
    nq, nk = na // tq, nkeys // tk
    nsub = tq // cw
    chains = [(a, u) for a in range(GROUP) for u in range(nsub)]

    def body(q_ref, k_ref, vt_ref, o_ref, lse_ref, m_s, acc_s):
        j = pl.program_id(2)

        @pl.when(j == 0)
        def _():
            m_s[...] = jnp.full_like(m_s, NEG_BIG)
            acc_s[...] = jnp.zeros_like(acc_s)

        k = k_ref[...]
        vt = vt_ref[...]

        def scores(n):
            a, u = chains[n]
            return _dot(k, q_ref[a, :, u * cw:(u + 1) * cw])

        s_tiles = {n: scores(n) for n in range(min(ATTN_FWD_AHEAD, len(chains)))}
        pending = None
        for n, (a, u) in enumerate(chains):
            cols = slice(u * cw, (u + 1) * cw)
            s_t = s_tiles.pop(n)
            m_prev = m_s[a, :, cols]
            m_new = jnp.maximum(m_prev, jnp.max(s_t, axis=0, keepdims=True))
            m_s[a, :, cols] = m_new
            pv = _dot(vt, jnp.exp2(s_t - m_new).astype(BF16))
            if pending is not None:
                pa, pcols, palpha, ppv = pending
                acc_s[pa, :, pcols] = palpha * acc_s[pa, :, pcols] + ppv
            pending = (a, cols, jnp.exp2(m_prev - m_new), pv)
            if n + ATTN_FWD_AHEAD < len(chains):
                s_tiles[n + ATTN_FWD_AHEAD] = scores(n + ATTN_FWD_AHEAD)
        pa, pcols, palpha, ppv = pending
        acc_s[pa, :, pcols] = palpha * acc_s[pa, :, pcols] + ppv

        @pl.when(j == nk - 1)
        def _():
            for a in range(GROUP):
                acc = acc_s[a]
                l = acc[hd:hd + 1, :]
                o_ref[a] = acc[0:hd, :] / l
                lse_ref[a] = m_s[a] + jnp.log2(l)

    return _pcall(
        body, name=name, grid=(kv, nq, nk),
        in_specs=[pl.BlockSpec((GROUP, hd, tq), lambda g, i, j: (g, 0, i)),
                  pl.BlockSpec((None, tk, hd), lambda g, i, j: (g, j, 0)),
                  pl.BlockSpec((None, vrows, tk), lambda g, i, j: (g, 0, j))],
        out_specs=[pl.BlockSpec((GROUP, hd, tq), lambda g, i, j: (g, 0, i)),
                   pl.BlockSpec((GROUP, 1, tq), lambda g, i, j: (g, 0, i))],
        out_shape=[jax.ShapeDtypeStruct((h, hd, na), F32), jax.ShapeDtypeStruct((h, 1, na), F32)],
        scratch_shapes=[pltpu.VMEM((GROUP, 1, tq), F32), pltpu.VMEM((GROUP, vrows, tq), F32)],
        compiler_params=_cp("parallel", "parallel", "arbitrary"))(qt_hm, k_hm, vt_hm)


def _window(win_ref, prev, cur, nxt, first, last, tm):
    win_ref[0:HALO, :] = jnp.where(first, 0.0, prev)
    win_ref[HALO:HALO + tm, :] = cur
    win_ref[HALO + tm:HALO + tm + HALO, :] = jnp.where(last, 0.0, nxt)


def _lane_blocks(d):
    return [slice(b, b + 128) for b in range(0, d, 128)]


def _sublane_shifts(slab):
    n = slab.shape[0]
    for b in range(8):
        sh = slab if b == 0 else pltpu.roll(slab, n - b, 0)
        for a8 in range(0, 2 * HALO, 8):
            yield a8 + b, sh[a8:a8 + CONV_ROWS, :]


def _halo_specs(tm, d, na, colblk):
    per = tm // HALO
    last_blk = na // HALO - 1
    prev = pl.BlockSpec((HALO, d), lambda i: (jnp.maximum(i * per - 1, 0), colblk))
    nxt = pl.BlockSpec((HALO, d), lambda i: (jnp.minimum((i + 1) * per, last_blk), colblk))
    return prev, nxt


def _seq_ends(i, n_lat, na, tm):
    first = jnp.logical_or(i == 0, i == n_lat // tm)
    last = jnp.logical_or(i == n_lat // tm - 1, i == na // tm - 1)
    return first, last


def _conv_fwd(p, conv_w, conv_b, ln_g, ln_b, n_lat, tm, name):
    na = p.shape[0]
    ktaps, d = conv_w.shape
    pad = ktaps // 2

    def body(a_ref, ap_ref, an_ref, g_ref, gp_ref, gn_ref, ga_ref, w_ref, cb_ref, lg_ref, lb_ref,
             y5_ref, y5t_ref, y2_ref, win):
        i = pl.program_id(0)
        first, last = _seq_ends(i, n_lat, na, tm)
        _window(win, ap_ref[...] * _sig(gp_ref[...]), a_ref[...] * _sig(g_ref[...]),
                an_ref[...] * _sig(gn_ref[...]), first, last, tm)

        def chunk(c, carry):
            r = pl.multiple_of(c * CONV_ROWS, CONV_ROWS)
            for lanes in _lane_blocks(d):
                acc = jnp.zeros((CONV_ROWS, 128), F32)
                for off, sh in _sublane_shifts(win[pl.ds(r, CONV_ROWS + 2 * HALO), lanes]):
                    k = off - (HALO - pad)
                    if 0 <= k < ktaps:
                        acc = acc + w_ref[k:k + 1, lanes] * sh
                y2_ref[pl.ds(r, CONV_ROWS), lanes] = acc + cb_ref[:, lanes]
            return carry

        lax.fori_loop(0, tm // CONV_ROWS, chunk, 0)
        y2 = y2_ref[...]
        xc = y2 - _mean(y2)
        y3 = xc * lax.rsqrt(_mean(xc * xc) + EPS) * lg_ref[...] + lb_ref[...]
        gate = ga_ref[...]
        y5 = (y3 * _sig(y3)) * (gate * _sig(gate))
        y5_ref[...] = y5.astype(BF16)
        y5t_ref[...] = y5.T.astype(BF16)

    ap, an = _halo_specs(tm, d, na, 0)
    gp, gn = _halo_specs(tm, d, na, 1)
    return _pcall(
        body, name=name, grid=(na // tm,),
        in_specs=[_rows(tm, d, 0), ap, an, _rows(tm, d, 1), gp, gn, _rows(tm, d, 2),
                  _full((ktaps, d)), _full((1, d)), _full((1, d)), _full((1, d))],
        out_specs=[_rows(tm, d), _cols(d, tm), _rows(tm, d)],
        out_shape=[jax.ShapeDtypeStruct((na, d), BF16), jax.ShapeDtypeStruct((d, na), BF16),
                   jax.ShapeDtypeStruct((na, d), F32)],
        scratch_shapes=[pltpu.VMEM((tm + 2 * HALO, d), F32)],
        compiler_params=_cp("parallel"))(p, p, p, p, p, p, p, conv_w, conv_b, ln_g, ln_b)


def _merge_fwd(y5, ot_lat, ot_ctx, p, xa, modv, gpost, wc, wa, wo, n_lat, tm, name):
    na, d = xa.shape

    def body(y5_ref, ol_ref, oc_ref, gb_ref, ma_ref, mb_ref, x_ref, mod_ref, gp_ref, wc_ref, wa_ref, wo_ref,
             xn_ref, yc_ref, ya_ref, out_ref, zt_ref, ogt_ref):
        is_ctx = pl.program_id(0) >= n_lat // tm
        gate_b = gb_ref[...]
        o = jnp.where(is_ctx, oc_ref[...], ol_ref[...]).T
        og = o * (gate_b * _sig(gate_b))
        ogt_ref[...] = og.T.astype(BF16)
        yc = _dot(y5_ref[...], wc_ref[...])
        ya = _dot(og.astype(BF16), wa_ref[...])
        yc_ref[...] = yc
        ya_ref[...] = ya
        z = _sig(ma_ref[...]) * yc + _sig(mb_ref[...]) * ya
        zt_ref[...] = z.T.astype(BF16)
        out = _dot(z.astype(BF16), wo_ref[...])
        out_ref[...] = out
        gt = _seg_rows(mod_ref, 2 * d, 3 * d, is_ctx)
        xn_ref[...] = x_ref[...] + gt * (out * lax.rsqrt(_mean(out * out) + EPS) * gp_ref[...])

    f32o = jax.ShapeDtypeStruct((na, d), F32)
    bft = jax.ShapeDtypeStruct((d, na), BF16)
    ol, oc = _split_cols(d, tm, n_lat)
    return _pcall(
        body, name=name, grid=(na // tm,),
        in_specs=[_rows(tm, d), ol, oc, _rows(tm, d, 4), _rows(tm, d, 5), _rows(tm, d, 6), _rows(tm, d),
                  _full((8, 3 * d)), _full((1, d)), _full((d, d)), _full((d, d)), _full((d, d))],
        out_specs=[_rows(tm, d)] * 4 + [_cols(d, tm)] * 2,
        out_shape=[f32o, f32o, f32o, f32o, bft, bft],
        compiler_params=_cp("parallel"))(y5, ot_lat, ot_ctx, p, p, p, xa, modv, gpost, wc, wa, wo)


def _loss_grad(xa, target, n_lat, tm, name):
    na, d = xa.shape
    nlt = n_lat // tm

    def body(x_ref, t_ref, dx_ref, loss_ref):
        i = pl.program_id(0)

        @pl.when(i == 0)
        def _():
            loss_ref[...] = jnp.zeros_like(loss_ref)

        @pl.when(i < nlt)
        def _():
            err = x_ref[...] - t_ref[...]
            dx_ref[...] = err * (1.0 / d)
            loss_ref[...] += 0.5 * jnp.sum(_mean(err * err))

        @pl.when(i >= nlt)
        def _():
            dx_ref[...] = jnp.zeros_like(dx_ref)

    return _pcall(
        body, name=name, grid=(na // tm,),
        in_specs=[_rows(tm, d), pl.BlockSpec((tm, d), lambda i: (jnp.minimum(i, nlt - 1), 0))],
        out_specs=[_rows(tm, d), _full((8, 128))],
        out_shape=[jax.ShapeDtypeStruct((na, d), F32), jax.ShapeDtypeStruct((8, 128), F32)],
        compiler_params=_cp("arbitrary"))(xa, target)


def _merge_bwd(dxn, out, yc, ya, ot_lat, ot_ctx, p, y2, modv, gpost, ln_g, ln_b, wo_t, wc_t, wa_t, esel, n_lat, tm,
               name, rider=None):
    na, d = dxn.shape

    def body(dx_ref, out_ref, yc_ref, ya_ref, ol_ref, oc_ref, gb_ref, ma_ref, mb_ref, gta_ref, y2_ref,
             mod_ref, gp_ref, lg_ref, lb_ref, wot_ref, wct_ref, wat_ref, es_ref,
             dgb_ref, dma_ref, dmb_ref, dgta_ref, dot_ref, dout_ref, dyc_ref, dya_ref, dy2_ref, dlt_ref, acc_ref):
        i = pl.program_id(0)
        ctx_tile = i >= n_lat // tm

        @pl.when(i == 0)
        def _():
            acc_ref[...] = jnp.zeros_like(acc_ref)

        gt = _seg_rows(mod_ref, 2 * d, 3 * d, ctx_tile)
        gp = gp_ref[...]
        dx = dx_ref[...]
        out = out_ref[...]
        r2 = lax.rsqrt(_mean(out * out) + EPS)
        n2 = out * r2
        dgt = _colsum(dx * (n2 * gp))

        @pl.when(ctx_tile)
        def _():
            acc_ref[1:2, :] += dgt

        @pl.when(jnp.logical_not(ctx_tile))
        def _():
            acc_ref[0:1, :] += dgt

        acc_ref[2:3, :] += _colsum(dx * gt * n2)
        dn2 = dx * gt * gp
        dout = (r2 * (dn2 - n2 * _mean(dn2 * n2))).astype(BF16)
        dout_ref[...] = dout
        dz = _dot(dout, wot_ref[...])
        sa = _sig(ma_ref[...])
        sb = _sig(mb_ref[...])
        dyc = (dz * sa).astype(BF16)
        dya = (dz * sb).astype(BF16)
        dyc_ref[...] = dyc
        dya_ref[...] = dya
        dma_ref[...] = (dz * yc_ref[...] * sa * (1.0 - sa)).astype(BF16)
        dmb_ref[...] = (dz * ya_ref[...] * sb * (1.0 - sb)).astype(BF16)
        dy5 = _dot(dyc, wct_ref[...])
        dog = _dot(dya, wat_ref[...])

        gate_b = gb_ref[...]
        sgb = _sig(gate_b)
        o = jnp.where(ctx_tile, oc_ref[...], ol_ref[...]).T
        do = dog * (gate_b * sgb)
        dot_ref[...] = do.T.astype(BF16)
        dgb_ref[...] = (dog * o * (sgb * (1.0 + gate_b * (1.0 - sgb)))).astype(BF16)
        dlt_ref[...] = _split_dot(do * o, es_ref[...]).T

        y2 = y2_ref[...]
        xc = y2 - _mean(y2)
        rstd = lax.rsqrt(_mean(xc * xc) + EPS)
        xhat = xc * rstd
        lg = lg_ref[...]
        y3 = xhat * lg + lb_ref[...]
        s3 = _sig(y3)
        gate_a = gta_ref[...]
        sga = _sig(gate_a)
        dgta_ref[...] = (dy5 * (y3 * s3) * (sga * (1.0 + gate_a * (1.0 - sga)))).astype(BF16)
        dy3 = dy5 * (gate_a * sga) * (s3 * (1.0 + y3 * (1.0 - s3)))
        acc_ref[3:4, :] += _colsum(dy3 * xhat)
        acc_ref[4:5, :] += _colsum(dy3)
        dxh = dy3 * lg
        dy2 = rstd * (dxh - _mean(dxh) - xhat * _mean(dxh * xhat))
        dy2_ref[...] = dy2
        acc_ref[5:6, :] += _colsum(dy2)

    f32o = jax.ShapeDtypeStruct((na, d), F32)
    bfo = jax.ShapeDtypeStruct((na, d), BF16)
    r = _rows(tm, d)
    ol, oc = _split_cols(d, tm, n_lat)
    ins = (dxn, out, yc, ya, ot_lat, ot_ctx, p, p, p, p, y2, modv, gpost, ln_g, ln_b, wo_t, wc_t, wa_t, esel)
    extra = rider or dict(ins=[], outs=[], scratch=[])
    return _pcall(
        _ride(body, rider, len(ins), 11, na // tm), name=name, grid=(na // tm,),
        in_specs=[r, r, r, r, ol, oc, _rows(tm, d, 4), _rows(tm, d, 5), _rows(tm, d, 6), _rows(tm, d, 2), r,
                  _full((8, 3 * d)), _full((1, d)), _full((1, d)), _full((1, d)),
                  _full((d, d)), _full((d, d)), _full((d, d)), _full((d, 128))] + [ANY] * len(extra["ins"]),
        out_specs=[r] * 4 + [_cols(d, tm)] + [r] * 4 + [_cols(128, tm), _full((8, d))] + [ANY] * len(extra["outs"]),
        out_shape=[bfo] * 4 + [jax.ShapeDtypeStruct((d, na), BF16)] + [bfo] * 3
        + [f32o, jax.ShapeDtypeStruct((128, na), F32), jax.ShapeDtypeStruct((8, d), F32)] + extra["outs"],
        scratch_shapes=extra["scratch"],
        compiler_params=_cp("arbitrary"),
    )(*ins, *extra["ins"])


def _attn_bwd(qt_hm, k_hm, kt_hm, v_hm, dot_hm, lse_r, dl_r, na, tq, tk, cw, name):
    h, hd, _ = qt_hm.shape
    kv, nkeys, _ = k_hm.shape
    nq, nk = na // tq, nkeys // tk
    nsub = tq // cw
    chains = [(a, u) for a in range(GROUP) for u in range(nsub)]

    def body(qt_ref, k_ref, kt_ref, v_ref, dot_ref, lse_ref, dl_ref, dq_hbm, dk_ref, dv_ref,
             dq_acc, dk_acc, dv_acc, sem):
        g = pl.program_id(0)
        j = pl.program_id(1)
        i = pl.program_id(2)

        @pl.when(jnp.logical_and(j == 0, i == 0))
        def _():
            dq_acc[...] = jnp.zeros_like(dq_acc)

        @pl.when(i == 0)
        def _():
            dk_acc[...] = jnp.zeros_like(dk_acc)
            dv_acc[...] = jnp.zeros_like(dv_acc)

        k = k_ref[...]
        v = v_ref[...]
        kt = kt_ref[...]

        def products(n):
            a, u = chains[n]
            cols = slice(u * cw, (u + 1) * cw)
            return _dot(k, qt_ref[a, :, cols]), _dot(v, dot_ref[a, :, cols])

        def accumulate(done):
            a, u, dv_t, dk_t, dq_t = done
            dv_acc[...] += dv_t
            dk_acc[...] += dk_t
            at = pl.multiple_of(i * tq + u * cw, cw)
            dq_acc[a, :, pl.ds(at, cw)] += dq_t

        tiles = {n: products(n) for n in range(min(ATTN_BWD_AHEAD, len(chains)))}
        pending = None
        for n, (a, u) in enumerate(chains):
            cols = slice(u * cw, (u + 1) * cw)
            s_t, dp_t = tiles.pop(n)
            p_t = jnp.exp2(s_t - lse_ref[a, :, cols])
            ds_b = (p_t * (dp_t - dl_ref[a, :, cols])).astype(BF16)
            p_b = p_t.astype(BF16)
            dv_t = _dot_nt(dot_ref[a, :, cols], p_b)
            dk_t = _dot_nt(qt_ref[a, :, cols], ds_b)
            dq_t = _dot(kt, ds_b)
            if pending is not None:
                accumulate(pending)
            pending = (a, u, dv_t, dk_t, dq_t)
            if n + ATTN_BWD_AHEAD < len(chains):
                tiles[n + ATTN_BWD_AHEAD] = products(n + ATTN_BWD_AHEAD)
        accumulate(pending)

        @pl.when(i == nq - 1)
        def _():
            dk_ref[...] = dk_acc[...]
            dv_ref[...] = dv_acc[...]

        @pl.when(jnp.logical_and(j == nk - 1, i == nq - 1))
        def _():
            cp = pltpu.make_async_copy(dq_acc, dq_hbm.at[pl.ds(g * GROUP, GROUP)], sem)
            cp.start()
            cp.wait()

    qtspec = pl.BlockSpec((GROUP, hd, tq), lambda g, j, i: (g, 0, i))
    kspec = pl.BlockSpec((None, tk, hd), lambda g, j, i: (g, j, 0))
    ktspec = pl.BlockSpec((None, hd, tk), lambda g, j, i: (g, 0, j))
    rspec = pl.BlockSpec((GROUP, 1, tq), lambda g, j, i: (g, 0, i))
    return _pcall(
        body, name=name, grid=(kv, nk, nq),
        in_specs=[qtspec, kspec, ktspec, kspec, qtspec, rspec, rspec],
        out_specs=[ANY, ktspec, ktspec],
        out_shape=[jax.ShapeDtypeStruct((h, hd, na), F32), jax.ShapeDtypeStruct((kv, hd, nkeys), F32),
                   jax.ShapeDtypeStruct((kv, hd, nkeys), F32)],
        scratch_shapes=[pltpu.VMEM((GROUP, hd, na), F32), pltpu.VMEM((hd, tk), F32), pltpu.VMEM((hd, tk), F32),
                        pltpu.SemaphoreType.DMA],
        compiler_params=_cp("arbitrary", "arbitrary", "arbitrary"),
    )(qt_hm, k_hm, kt_hm, v_hm, dot_hm, lse_r, dl_r)


def _qknorm_bwd(dqt_lat, dqt_ctx, dkt, dkt_ctx, dvt, dvt_ctx, p, cos_t, sin_t, gq, gk, esel, esel_t, n_lat, tm, name):
    na = p.shape[0]
    d = gq.shape[1]
    kw = d // GROUP

    def body(dql_ref, dqc_ref, dkl_ref, dkc_ref, dvl_ref, dvc_ref, q_ref, kv_ref, cos_ref, sin_ref, gq_ref, gk_ref,
             es_ref, est_ref, dqo_ref, dkvo_ref, acc_ref):
        is_ctx = pl.program_id(0) >= n_lat // tm

        @pl.when(pl.program_id(0) == 0)
        def _():
            acc_ref[...] = jnp.zeros_like(acc_ref)

        dq_in = jnp.where(is_ctx, dqc_ref[...], dql_ref[...]).T
        dk_in = (dkl_ref[...] + jnp.where(is_ctx, dkc_ref[...], 0.0)).T
        dv_in = (dvl_ref[...] + jnp.where(is_ctx, dvc_ref[...], 0.0)).T

        cos = cos_ref[...]
        sin = sin_ref[...]

        def back(dy, xh, g, w):
            es, es_t = es_ref[0:w, :], est_ref[:, 0:w]
            dn = dy * _lane_tile(cos, w) - _partner(dy) * _lane_tile(sin, w)
            rs = lax.rsqrt(_head_sum(xh * xh, es, es_t) * (1.0 / HEAD_DIM) + EPS)
            y = xh * rs
            dg = _colsum(dn * y)
            dyn = dn * g
            dx = rs * (dyn - y * (_head_sum(dyn * y, es, es_t) * (1.0 / HEAD_DIM)))
            return dx, dg

        dq, dgq = back(dq_in * ATTN_SCALE, q_ref[...], gq_ref[...], d)
        dqo_ref[...] = dq.astype(BF16)
        acc_ref[0:1, :] += dgq
        kv = kv_ref[...]
        dk, dgk = back(dk_in * (1.0 / LOG2E), kv[:, 0:kw], gk_ref[...], kw)
        acc_ref[1:2, 0:kw] += dgk
        dkvo_ref[:, 0:kw] = dk.astype(BF16)
        dkvo_ref[:, kw:2 * kw] = dv_in.astype(BF16)

    dql, dqc = _split_cols(d, tm, n_lat)
    _, kvc = _split_cols(kw, tm, n_lat)
    return _pcall(
        body, name=name, grid=(na // tm,),
        in_specs=[dql, dqc, _cols(kw, tm), kvc, _cols(kw, tm), kvc, _rows(tm, d, 3), _rows(tm, d // 2, 14),
                  _rows(tm, 128), _rows(tm, 128), _full((1, d)), _full((1, kw)), _full((d, 128)), _full((128, d))],
        out_specs=[_rows(tm, d), _rows(tm, d // 2), _full((8, d))],
        out_shape=[jax.ShapeDtypeStruct((na, d), BF16), jax.ShapeDtypeStruct((na, d // 2), BF16),
                   jax.ShapeDtypeStruct((8, d), F32)],
        compiler_params=_cp("arbitrary"))(dqt_lat, dqt_ctx, dkt, dkt_ctx, dvt, dvt_ctx, p, p, cos_t, sin_t, gq, gk,
                                          esel, esel_t)


def _conv_bwd(dy2, p, conv_w, n_lat, tm, name, rider=None):
    na = p.shape[0]
    ktaps, d = conv_w.shape
    pad = ktaps // 2

    def body(dy_ref, dyp_ref, dyn_ref, a_ref, ap_ref, an_ref, g_ref, gp_ref, gn_ref, w_ref,
             da_ref, dg_ref, dw_ref, dwin, ywin, dy1_s, part):
        i = pl.program_id(0)

        @pl.when(i == 0)
        def _():
            part[...] = jnp.zeros_like(part)

        first, last = _seq_ends(i, n_lat, na, tm)
        a = a_ref[...]
        sg = _sig(g_ref[...])
        _window(dwin, dyp_ref[...], dy_ref[...], dyn_ref[...], first, last, tm)
        _window(ywin, ap_ref[...] * _sig(gp_ref[...]), a * sg, an_ref[...] * _sig(gn_ref[...]), first, last, tm)

        def chunk(c, carry):
            r = pl.multiple_of(c * CONV_ROWS, CONV_ROWS)
            for lanes in _lane_blocks(d):
                dy = dy_ref[pl.ds(r, CONV_ROWS), lanes]
                acc = jnp.zeros((CONV_ROWS, 128), F32)
                for off, sh in _sublane_shifts(dwin[pl.ds(r, CONV_ROWS + 2 * HALO), lanes]):
                    k = HALO + pad - off
                    if 0 <= k < ktaps:
                        acc = acc + w_ref[k:k + 1, lanes] * sh
                dy1_s[pl.ds(r, CONV_ROWS), lanes] = acc
                for off, sh in _sublane_shifts(ywin[pl.ds(r, CONV_ROWS + 2 * HALO), lanes]):
                    k = off - (HALO - pad)
                    if 0 <= k < ktaps:
                        part[k, :, lanes] += jnp.sum((dy * sh).reshape(CONV_ROWS // 8, 8, 128), axis=0)
            return carry

        lax.fori_loop(0, tm // CONV_ROWS, chunk, 0)
        dy1 = dy1_s[...]
        da_ref[...] = (dy1 * sg).astype(BF16)
        dg_ref[...] = (dy1 * a * sg * (1.0 - sg)).astype(BF16)

        @pl.when(i == na // tm - 1)
        def _():
            dw_ref[...] = jnp.sum(part[...], axis=1)

    dyp, dyn = _halo_specs(tm, d, na, 0)
    ap, an = _halo_specs(tm, d, na, 0)
    gp, gn = _halo_specs(tm, d, na, 1)
    bfo = jax.ShapeDtypeStruct((na, d), BF16)
    ins = (dy2, dy2, dy2, p, p, p, p, p, p, conv_w)
    extra = rider or dict(ins=[], outs=[], scratch=[])
    return _pcall(
        _ride(body, rider, len(ins), 3, na // tm), name=name, grid=(na // tm,),
        in_specs=[_rows(tm, d), dyp, dyn, _rows(tm, d, 0), ap, an, _rows(tm, d, 1), gp, gn, _full((ktaps, d))]
        + [ANY] * len(extra["ins"]),
        out_specs=[_rows(tm, d), _rows(tm, d), _full((ktaps, d))] + [ANY] * len(extra["outs"]),
        out_shape=[bfo, bfo, jax.ShapeDtypeStruct((ktaps, d), F32)] + extra["outs"],
        scratch_shapes=[pltpu.VMEM((tm + 2 * HALO, d), F32), pltpu.VMEM((tm + 2 * HALO, d), F32),
                        pltpu.VMEM((tm, d), F32), pltpu.VMEM((ktaps, 8, d), F32)] + extra["scratch"],
        compiler_params=_cp("arbitrary"))(*ins, *extra["ins"])


def _inproj_bwd(segs, dkv, xa, dxn, modv, gpre, wp_t, n_lat, out_rows, tm, name):
    na, d = xa.shape
    nseg = len(segs)
    wrows = wp_t.shape[0]

    def body(*refs):
        seg_refs = refs[:nseg]
        dkv_ref, x_ref, dxn_ref, mod_ref, g_ref, wt_hbm, dx_ref, acc_ref, wt, sem = refs[nseg:]
        i = pl.program_id(0)
        ctx_tile = i >= n_lat // tm

        @pl.when(i == 0)
        def _():
            cp = pltpu.make_async_copy(wt_hbm, wt, sem)
            cp.start()
            cp.wait()
            acc_ref[...] = jnp.zeros_like(acc_ref)

        dh = _dot(dkv_ref[...], wt[nseg * d:wrows, :])
        for s in range(nseg):
            dh = dh + _dot(seg_refs[s][...], wt[s * d:(s + 1) * d, :])
        x = x_ref[...]
        r = lax.rsqrt(_mean(x * x) + EPS)
        xn = x * r
        g = g_ref[...]
        sc1 = 1.0 + _seg_rows(mod_ref, d, 2 * d, ctx_tile)
        dsh = _colsum(dh)
        dsc = _colsum(dh * (xn * g))

        @pl.when(ctx_tile)
        def _():
            acc_ref[1:2, :] += dsh
            acc_ref[3:4, :] += dsc

        @pl.when(jnp.logical_not(ctx_tile))
        def _():
            acc_ref[0:1, :] += dsh
            acc_ref[2:3, :] += dsc

        acc_ref[4:5, :] += _colsum(dh * xn * sc1)
        dxh = dh * g * sc1

        @pl.when(i < out_tiles)
        def _():
            dx_ref[...] = dxn_ref[...] + r * (dxh - xn * _mean(dxh * xn))

    r_ = _rows(tm, d)
    out_tiles = out_rows // tm
    return _pcall(
        body, name=name, grid=(na // tm,),
        in_specs=[r_] * nseg + [_rows(tm, d // 2), r_, r_, _full((8, 3 * d)), _full((1, d)), ANY],
        out_specs=[pl.BlockSpec((tm, d), lambda i: (jnp.minimum(i, out_tiles - 1), 0)), _full((8, d))],
        out_shape=[jax.ShapeDtypeStruct((out_rows, d), F32), jax.ShapeDtypeStruct((8, d), F32)],
        scratch_shapes=[pltpu.VMEM(wp_t.shape, BF16), pltpu.SemaphoreType.DMA],
        compiler_params=_cp("arbitrary"))(*segs, dkv, xa, dxn, modv, gpre, wp_t)


def _grad_matmul(a_t, b, tk, name):
    ka, na = a_t.shape
    nb = b.shape[1]
    tn = min(nb, 1024)

    def body(a_ref, b_ref, o_ref):
        @pl.when(pl.program_id(1) == 0)
        def _():
            o_ref[...] = jnp.zeros_like(o_ref)

        o_ref[...] += _dot(a_ref[...], b_ref[...])

    return _pcall(
        body, name=name, grid=(nb // tn, na // tk),
        in_specs=[pl.BlockSpec((ka, tk), lambda n, k: (0, k)), pl.BlockSpec((tk, tn), lambda n, k: (k, n))],
        out_specs=pl.BlockSpec((ka, tn), lambda n, k: (0, n)),
        out_shape=jax.ShapeDtypeStruct((ka, nb), F32),
        compiler_params=_cp("parallel", "arbitrary"))(a_t, b)


def _pack(parts, cols, row_mult):
    flat = jnp.concatenate([q.astype(F32).reshape(-1) for q in parts])
    rows = -(-flat.shape[0] // (cols * row_mult)) * row_mult
    return jnp.pad(flat, (0, rows * cols - flat.shape[0])).reshape(1, rows, cols)


def _unpack(flat, shapes):
    out, off = [], 0
    for s in shapes:
        n = math.prod(s)
        out.append(flat[off:off + n].reshape(tuple(s)))
        off += n
    return out


def _cols_by_dest(g):
    l, a, w8 = g.shape
    return g.reshape(l, a, N_DEV, w8 // N_DEV).transpose(2, 0, 1, 3)


def _rows_by_dest(g):
    l, r8, b = g.shape
    return g.reshape(l, N_DEV, r8 // N_DEV, b).transpose(1, 0, 2, 3)


def _cols_from_src(s):
    n, l, a, w = s.shape
    return s.transpose(1, 2, 0, 3).reshape(l, a, n * w)


def _rows_from_src(s):
    n, l, r, b = s.shape
    return s.transpose(1, 0, 2, 3).reshape(l, n * r, b)


def _rope_tables(n_lat, n_ctx):
    half = HEAD_DIM // 2
    rows = n_lat // GRID_W
    row = jnp.repeat(jnp.arange(rows, dtype=F32), GRID_W)
    col = jnp.tile(jnp.arange(GRID_W, dtype=F32), rows)
    inv_freq = ROPE_THETA ** (-jnp.arange(0, half, 2, dtype=F32) / half)
    ang = jnp.concatenate([row[:, None] * inv_freq, col[:, None] * inv_freq], axis=-1)
    cos, sin = jnp.cos(ang), jnp.sin(ang)
    cos_t = jnp.concatenate([jnp.tile(cos, (1, 4)), jnp.ones((n_ctx, 128), F32)], axis=0)
    sin_t = jnp.concatenate([jnp.tile(jnp.concatenate([-sin, sin], axis=-1), (1, 2)),
                             jnp.zeros((n_ctx, 128), F32)], axis=0)
    return cos_t, sin_t


def _to_heads(t):
    na, w = t.shape
    return t.reshape(na, w // HEAD_DIM, HEAD_DIM).transpose(1, 0, 2)


def kernel(x, c, ctx, c_ctx, w_mod, b_mod, g_pre, g_post, w_in, conv_w, conv_b, ln_g, ln_b, w_conv_out, q_norm_g, k_norm_g, w_attn_out, w_out, loss_target, m_c_ctx, m_w_mod, m_b_mod, m_g_pre, m_g_post, m_w_in, m_conv_w, m_conv_b, m_ln_g, m_ln_b, m_w_conv_out, m_q_norm_g, m_k_norm_g, m_w_attn_out, m_w_out, v_c_ctx, v_w_mod, v_b_mod, v_g_pre, v_g_post, v_w_in, v_conv_w, v_conv_b, v_ln_g, v_ln_b, v_w_conv_out, v_q_norm_g, v_k_norm_g, v_w_attn_out, v_w_out):
    depth, d, _ = w_mod.shape
    n_lat, n_ctx = x.shape[1], ctx.shape[1]
    na = n_lat + n_ctx
    heads = d // HEAD_DIM
    kvh = heads // GROUP
    kw = d // GROUP
    ktaps = conv_w.shape[1]
    tm = n_ctx
    tm_half = tm // 2
    tbig = 3 * tm if na % (3 * tm) == 0 else tm
    tq_lat = 8 * tm if n_lat % (8 * tm) == 0 else tm
    tq_bwd = 8 * tm if n_lat % (8 * tm) == 0 else tm

    def layer_shards(l):
        return [w_mod[l].astype(BF16), w_in[l].astype(BF16), w_conv_out[l].astype(BF16), w_attn_out[l].astype(BF16),
                w_out[l].astype(BF16), conv_w[l]]

    def full_weights(gathered):
        s_mod, s_in, s_co, s_ao, s_oo, s_cw = [q[:, None] for q in gathered]
        win = _cols_from_src(s_in)[0]
        wp = jnp.concatenate([win[:, :4 * d], win[:, 4 * d + 2 * kw:], win[:, 4 * d:4 * d + 2 * kw]], axis=1)
        return dict(wmod=_cols_from_src(s_mod)[0], wp=wp, wc=_rows_from_src(s_co)[0], wa=_rows_from_src(s_ao)[0],
                    wo=_rows_from_src(s_oo)[0], convw=_cols_from_src(s_cw)[0])

    weights = [full_weights(_all_gather(layer_shards(0), "gather_weights_l0"))]

    cos_t, sin_t = _rope_tables(n_lat, n_ctx)
    lane = jnp.arange(d)
    esel = (lane[:, None] // HEAD_DIM == jnp.arange(128)[None, :]).astype(BF16)
    esel_t = esel.T
    cvec = jnp.zeros((8, d), F32).at[0].set(c[0]).at[1].set(c_ctx)
    cvec_t = jnp.zeros((d, 128), F32).at[:, 0].set(c[0]).at[:, 1].set(c_ctx)

    xa = jnp.concatenate([x[0], ctx[0]], axis=0)

    saved = []
    for l in range(depth):
        tag = f"_l{l}"
        gq = jnp.tile(q_norm_g[l], heads)[None, :]
        gk = jnp.tile(k_norm_g[l], heads // GROUP)[None, :]
        wl = weights[l]
        modv = _mod_fwd(cvec, wl["wmod"], b_mod[l][None, :], "mod_fwd" + tag)
        p, h_t, *gathered = _inproj(xa, modv, g_pre[l][None, :], wl["wp"], n_lat, tbig, "inproj" + tag,
                                    shards=layer_shards(l + 1) if l + 1 < depth else ())
        if gathered:
            weights.append(full_weights(gathered))
        q_t, kr, vb, k_t, v_t = _qknorm_fwd(p, cos_t, sin_t, gq, gk, esel, esel_t, tm, "qknorm_fwd" + tag)
        qt_hm = q_t.reshape(heads, HEAD_DIM, na)
        kt_hm = k_t.reshape(kvh, HEAD_DIM, na)
        k_hm, v_hm = _to_heads(kr), _to_heads(vb)
        vt_ones = jnp.concatenate([v_t.reshape(kvh, HEAD_DIM, na), jnp.ones((kvh, 16, na), BF16)], axis=1)
        ot_lat, lse_lat = _attn_fwd(qt_hm, k_hm, vt_ones, n_lat, tq_lat, tbig, tm, "attn_fwd" + tag)
        ot_ctx, lse_ctx = _attn_fwd(qt_hm[:, :, n_lat:], k_hm[:, n_lat:], vt_ones[:, :, n_lat:], n_ctx, tm, tm, tm,
                                    "attn_ctx_fwd" + tag)
        ot_lat, ot_ctx = ot_lat.reshape(d, n_lat), ot_ctx.reshape(d, n_ctx)
        y5, y5_t, y2 = _conv_fwd(p, wl["convw"], conv_b[l][None, :], ln_g[l][None, :], ln_b[l][None, :], n_lat, tm,
                                 "conv_fwd" + tag)
        xa_new, yc, ya, out, z_t, og_t = _merge_fwd(y5, ot_lat, ot_ctx, p, xa, modv, g_post[l][None, :], wl["wc"],
                                                    wl["wa"], wl["wo"], n_lat, tm, "merge_fwd" + tag)
        saved.append(dict(xa=xa, modv=modv, p=p, h_t=h_t, qt_hm=qt_hm, k_hm=k_hm, kt_hm=kt_hm, v_hm=v_hm,
                          ot_lat=ot_lat, ot_ctx=ot_ctx, lse_lat=lse_lat, lse_ctx=lse_ctx,
                          y5_t=y5_t, y2=y2, yc=yc, ya=ya, out=out, z_t=z_t, og_t=og_t, gq=gq, gk=gk))
        xa = xa_new

    dxa, loss_blk = _loss_grad(xa, loss_target[0], n_lat, tm, "loss_grad")

    big_names = ["w_mod", "w_in", "w_conv_out", "w_attn_out", "w_out", "conv_w"]
    g_bmod, g_gpre, g_gpost, g_convb, g_lng, g_lnb, g_qg, g_kg = [], [], [], [], [], [], [], []
    g_cctx = jnp.zeros((d,), F32)
    pending_send, pending_layer, from_chips = None, None, {}

    def add_sibling_parts(send, swapped, layer):
        return [_add_sibling_part(sb, rc, BF16, f"reduce_sibling_add_{n}_l{layer}")
                for n, sb, rc in zip(big_names, send, swapped)]

    for l in reversed(range(depth)):
        tag = f"_l{l}"
        s = saved[l]
        wl = weights[l]
        p = s["p"]
        (dgb, dma, dmb, dgta, do_t, dout, dyc, dya, dy2, dl_t, acc_m, *swapped) = _merge_bwd(
            dxa, s["out"], s["yc"], s["ya"], s["ot_lat"], s["ot_ctx"], p, s["y2"], s["modv"], g_post[l][None, :],
            ln_g[l][None, :], ln_b[l][None, :], wl["wo"].T, wl["wc"].T, wl["wa"].T, esel, n_lat, tm_half,
            "merge_bwd" + tag, rider=_swap_rider(pending_send) if pending_send else None)
        exchange = _exchange_rider(add_sibling_parts(pending_send, swapped, pending_layer)) if swapped else None
        dl_r = dl_t[:heads].reshape(heads, 1, na)
        dot_hm = do_t.reshape(heads, HEAD_DIM, na)
        qt_hm, k_hm, kt_hm, v_hm = s["qt_hm"], s["k_hm"], s["kt_hm"], s["v_hm"]
        dqt_lat, dkt_hm, dvt_hm = _attn_bwd(qt_hm, k_hm, kt_hm, v_hm, dot_hm, s["lse_lat"], dl_r,
                                            n_lat, tq_bwd, tm, tm, "attn_bwd" + tag)
        dqt_ctx, dkt_ctx, dvt_ctx = _attn_bwd(
            qt_hm[:, :, n_lat:], k_hm[:, n_lat:], kt_hm[:, :, n_lat:], v_hm[:, n_lat:], dot_hm[:, :, n_lat:],
            s["lse_ctx"], dl_r[:, :, n_lat:], n_ctx, tm, tm, tm, "attn_ctx_bwd" + tag)
        dq, dkv, acc_q = _qknorm_bwd(
            dqt_lat.reshape(d, n_lat), dqt_ctx.reshape(d, n_ctx), dkt_hm.reshape(kw, na), dkt_ctx.reshape(kw, n_ctx),
            dvt_hm.reshape(kw, na), dvt_ctx.reshape(kw, n_ctx), p, cos_t, sin_t, s["gq"], s["gk"], esel, esel_t, n_lat, tm,
            "qknorm_bwd" + tag)
        da, dg, dconvw, *exchanged = _conv_bwd(dy2, p, wl["convw"], n_lat, tm, "conv_bwd" + tag, rider=exchange)
        if exchanged:
            from_chips[pending_layer] = exchanged
        segs = [da, dg, dgta, dq, dgb, dma, dmb]
        dxa, acc_h = _inproj_bwd(segs, dkv, s["xa"], dxa, s["modv"], g_pre[l][None, :], wl["wp"].T, n_lat,
                                 na if l else n_lat, tm, "inproj_bwd" + tag)

        dwp = [_grad_matmul(s["h_t"], sg, tbig, f"grad_w_in{k}" + tag) for k, sg in enumerate(segs + [dkv])]
        g_win = jnp.concatenate(dwp[:4] + [dwp[7]] + dwp[4:7], axis=1)
        g_wc = _grad_matmul(s["y5_t"], dyc, tbig, "grad_w_conv_out" + tag)
        g_wa = _grad_matmul(s["og_t"], dya, tbig, "grad_w_attn_out" + tag)
        g_wo = _grad_matmul(s["z_t"], dout, tbig, "grad_w_out" + tag)

        dmod = jnp.zeros((8, 3 * d), F32)
        dmod = dmod.at[0].set(jnp.concatenate([acc_h[0], acc_h[2], acc_m[0]]))
        dmod = dmod.at[1].set(jnp.concatenate([acc_h[1], acc_h[3], acc_m[1]]))
        dwm, dbm, dcv = _mod_bwd(dmod, cvec, cvec_t, wl["wmod"], "mod_bwd" + tag)
        by_dest = [_cols_by_dest(dwm[None]), _cols_by_dest(g_win[None]), _rows_by_dest(g_wc[None]),
                   _rows_by_dest(g_wa[None]), _rows_by_dest(g_wo[None]), _cols_by_dest(dconvw[None])]
        pending_send, pending_layer = [q.reshape((4, 2) + q.shape[1:]) for q in by_dest], l
        g_bmod.append(dbm[0])
        g_cctx = g_cctx + dcv[1]
        g_gpre.append(acc_h[4])
        g_gpost.append(acc_m[2])
        g_lng.append(acc_m[3])
        g_lnb.append(acc_m[4])
        g_convb.append(acc_m[5])
        g_qg.append(acc_q[0].reshape(heads, HEAD_DIM).sum(0))
        g_kg.append(acc_q[1, :kw].reshape(heads // GROUP, HEAD_DIM).sum(0))

    grad_x = dxa[None]

    def stack(lst):
        return jnp.stack(lst[::-1])

    big_w = dict(w_mod=w_mod, w_in=w_in, w_conv_out=w_conv_out, w_attn_out=w_attn_out, w_out=w_out, conv_w=conv_w)
    big_m = dict(w_mod=m_w_mod, w_in=m_w_in, w_conv_out=m_w_conv_out, w_attn_out=m_w_attn_out, w_out=m_w_out,
                 conv_w=m_conv_w)
    big_v = dict(w_mod=v_w_mod, w_in=v_w_in, w_conv_out=v_w_conv_out, w_attn_out=v_w_attn_out, w_out=v_w_out,
                 conv_w=v_conv_w)
    swapped = _run_rider(_swap_rider(pending_send), "reduce_sibling")
    from_chips[pending_layer] = _run_rider(
        _exchange_rider(add_sibling_parts(pending_send, swapped, pending_layer)), "reduce_chips")
    big_g, big_d, big_nm, big_nv = {}, {}, {}, {}
    for k, n in enumerate(big_names):
        st = jnp.concatenate([from_chips[l][k] for l in range(depth)], axis=1)
        big_g[n], big_d[n], big_nm[n], big_nv[n] = _sum_adamw(st, big_w[n], big_m[n], big_v[n], "adamw_" + n)

    small_names = ["c_ctx", "b_mod", "g_pre", "g_post", "conv_b", "ln_g", "ln_b", "q_norm_g", "k_norm_g", "loss"]
    zero1 = jnp.zeros((1,), F32)
    small_w = dict(c_ctx=c_ctx, b_mod=b_mod, g_pre=g_pre, g_post=g_post, conv_b=conv_b, ln_g=ln_g, ln_b=ln_b,
                   q_norm_g=q_norm_g, k_norm_g=k_norm_g, loss=zero1)
    small_m = dict(c_ctx=m_c_ctx, b_mod=m_b_mod, g_pre=m_g_pre, g_post=m_g_post, conv_b=m_conv_b, ln_g=m_ln_g,
                   ln_b=m_ln_b, q_norm_g=m_q_norm_g, k_norm_g=m_k_norm_g, loss=zero1)
    small_v = dict(c_ctx=v_c_ctx, b_mod=v_b_mod, g_pre=v_g_pre, g_post=v_g_post, conv_b=v_conv_b, ln_g=v_ln_g,
                   ln_b=v_ln_b, q_norm_g=v_q_norm_g, k_norm_g=v_k_norm_g, loss=zero1)
    small_g = dict(c_ctx=g_cctx, b_mod=stack(g_bmod), g_pre=stack(g_gpre), g_post=stack(g_gpost),
                   conv_b=stack(g_convb), ln_g=stack(g_lng), ln_b=stack(g_lnb), q_norm_g=stack(g_qg),
                   k_norm_g=stack(g_kg), loss=loss_blk[0, 0:1])
    small_shapes = [small_w[n].shape for n in small_names]

    def pack_small(tree):
        return _pack([tree[n] for n in small_names], d, 8)

    small_parts, = _all_gather([pack_small(small_g)], "gather_small_grads")
    small_out = _sum_adamw(small_parts, pack_small(small_w), pack_small(small_m), pack_small(small_v),
                           "adamw_replicated")
    sm_g, sm_d, sm_nm, sm_nv = [dict(zip(small_names, _unpack(o.reshape(-1), small_shapes))) for o in small_out]
    loss = sm_g["loss"].reshape(())

    order = ["c_ctx", "w_mod", "b_mod", "g_pre", "g_post", "w_in", "conv_w", "conv_b", "ln_g", "ln_b",
             "w_conv_out", "q_norm_g", "k_norm_g", "w_attn_out", "w_out"]

    def pick(big, small):
        return [big[n] if n in big else small[n] for n in order]

    return (loss, grad_x, *pick(big_g, sm_g), *pick(big_d, sm_d), *pick(big_nm, sm_nm), *pick(big_nv, sm_nv))
```

```python
import math

import jax
import jax.numpy as jnp
from jax import lax
from jax.experimental import pallas as pl
from jax.experimental.pallas import tpu as pltpu

F32 = jnp.float32
BF16 = jnp.bfloat16

HEAD_DIM = 64
GROUP = 4
GRID_W = 64
ROPE_THETA = 10000.0
EPS = 1e-6
ATTN_SCALE = HEAD_DIM ** -0.5
LOG2E = 1.4426950408889634
Q_PRESCALE = ATTN_SCALE * LOG2E
HALO = 16
CONV_ROWS = 64
ATTN_FWD_AHEAD = 5
ATTN_BWD_AHEAD = 3

ADAM_LR = 0.001
ADAM_B1 = 0.9
ADAM_B2 = 0.999
ADAM_EPS = 1e-08
ADAM_WD = 0.01
ADAM_STEP = 10

N_DEV = 8
MESH_AXES = ("x", "y", "c")
V7X_VMEM_LIMIT = 56 * 1024 * 1024
NEG_BIG = -1e30

MESH = pl.DeviceIdType.MESH
ANY = pl.BlockSpec(memory_space=pl.ANY)


def _pcall(body, **kw):
    return pl.pallas_call(body, **kw)


def _cp(*sem):
    return pltpu.CompilerParams(dimension_semantics=sem, vmem_limit_bytes=V7X_VMEM_LIMIT)


def _sig(x):
    return 0.5 * jnp.tanh(0.5 * x) + 0.5


def _mean(x):
    return jnp.mean(x, axis=-1, keepdims=True)


def _colsum(x):
    return jnp.sum(x, axis=0, keepdims=True)


def _bf_round(x):
    return x.astype(BF16).astype(F32)


def _dot(a, b):
    return jnp.dot(a, b, preferred_element_type=F32)


def _dot_nt(a, b):
    return lax.dot_general(a, b, (((1,), (1,)), ((), ())), preferred_element_type=F32)


def _split_dot(x, m):
    hi = x.astype(BF16)
    lo = (x - hi.astype(F32)).astype(BF16)
    return _dot(hi, m) + _dot(lo, m)


def _head_sum(x, es, es_t):
    return _split_dot(_split_dot(x, es), es_t)


def _full(shape):
    nd = len(shape)
    return pl.BlockSpec(shape, lambda *_: (0,) * nd)


def _rows(tm, width, colblk=0):
    return pl.BlockSpec((tm, width), lambda i: (i, colblk))


def _cols(height, tm):
    return pl.BlockSpec((height, tm), lambda i: (0, i))


def _split_cols(height, tm, n_lat):
    nl = n_lat // tm
    return (pl.BlockSpec((height, tm), lambda i: (0, jnp.minimum(i, nl - 1))),
            pl.BlockSpec((height, tm), lambda i: (0, jnp.maximum(i - nl, 0))))


def _my_place():
    return lax.axis_index("x"), lax.axis_index("y"), lax.axis_index("c")


def _sem_arrays(n):
    return [pltpu.SemaphoreType.DMA((n,)), pltpu.SemaphoreType.DMA((n,))]


def _gather_steps(x_refs, out_refs, send_sems, recv_sems, local_sems):
    n = len(x_refs)
    x, y, c = _my_place()
    me, sibling = (x, y, c), (x, y, 1 - c)
    chips = [(1 - x, y), (x, 1 - y), (1 - x, 1 - y)]

    def slab(a, px, py, pc):
        return out_refs[a].at[4 * px + 2 * py + pc]

    def copies(k, block, to, from_input=False):
        return [pltpu.make_async_remote_copy(
            src_ref=x_refs[a] if from_input else slab(a, *block), dst_ref=slab(a, *block),
            send_sem=send_sems.at[k * n + a], recv_sem=recv_sems.at[k * n + a],
            device_id=to, device_id_type=MESH) for a in range(n)]

    def mine():
        return [pltpu.make_async_copy(x_refs[a], slab(a, *me), local_sems.at[a]) for a in range(n)]

    def first():
        out = copies(0, me, sibling, True)
        for j, chip in enumerate(chips):
            out += copies(1 + j, me, (*chip, c), True)
        return out

    def onward(j):
        return copies(4 + j, (*chips[j], c), sibling)

    def start():
        for cp in mine() + first():
            cp.start()

    def relay():
        for j, chip in enumerate(chips):
            for cp in copies(1 + j, (*chip, c), me):
                cp.wait_recv()
            for cp in onward(j):
                cp.start()

    def finish():
        for cp in copies(0, sibling, me):
            cp.wait_recv()
        for j, chip in enumerate(chips):
            for cp in copies(4 + j, (*chip, 1 - c), me):
                cp.wait_recv()
        for cp in first() + onward(0) + onward(1) + onward(2):
            cp.wait_send()
        for cp in mine():
            cp.wait()

    return start, relay, finish


def _gather_scratch(n):
    return _sem_arrays(7 * n) + [pltpu.SemaphoreType.DMA((n,))]


def _all_gather(shards, name):
    n = len(shards)

    def body(*refs):
        start, relay, finish = _gather_steps(refs[:n], refs[n:2 * n], *refs[2 * n:])
        start()
        relay()
        finish()

    return _pcall(
        body, name=name,
        out_shape=[jax.ShapeDtypeStruct((N_DEV,) + q.shape, q.dtype) for q in shards],
        in_specs=[ANY] * n, out_specs=[ANY] * n,
        scratch_shapes=_gather_scratch(n),
    )(*shards)


def _swap_steps(buf_refs, recv_refs, send_sems, recv_sems):
    n = len(buf_refs)
    x, y, c = _my_place()

    def copies():
        return [pltpu.make_async_remote_copy(
            src_ref=buf_refs[a].at[k, 1 - c], dst_ref=recv_refs[a].at[k],
            send_sem=send_sems.at[k * n + a], recv_sem=recv_sems.at[k * n + a],
            device_id=(x, y, 1 - c), device_id_type=MESH) for k in range(4) for a in range(n)]

    def start():
        for cp in copies():
            cp.start()

    def finish():
        for cp in copies():
            cp.wait()

    return start, finish


def _exchange_steps(s_refs, recv_refs, send_sems, recv_sems, local_sems):
    n = len(s_refs)
    x, y, c = _my_place()
    mychip = 2 * x + y
    chips = [(1 - x, y), (x, 1 - y), (1 - x, 1 - y)]

    def copies():
        mine = [pltpu.make_async_copy(s_refs[a].at[mychip], recv_refs[a].at[mychip], local_sems.at[a])
                for a in range(n)]
        return mine + [pltpu.make_async_remote_copy(
            src_ref=s_refs[a].at[2 * px + py], dst_ref=recv_refs[a].at[mychip],
            send_sem=send_sems.at[j * n + a], recv_sem=recv_sems.at[j * n + a],
            device_id=(px, py, c), device_id_type=MESH) for j, (px, py) in enumerate(chips) for a in range(n)]

    def start():
        for cp in copies():
            cp.start()

    def finish():
        for cp in copies():
            cp.wait()

    return start, finish


def _swap_rider(bufs):
    n = len(bufs)
    return dict(steps=_swap_steps, ins=list(bufs), scratch=_sem_arrays(4 * n),
                outs=[jax.ShapeDtypeStruct((4,) + q.shape[2:], q.dtype) for q in bufs])


def _exchange_rider(parts):
    n = len(parts)
    return dict(steps=_exchange_steps, ins=list(parts), scratch=_sem_arrays(3 * n) + [pltpu.SemaphoreType.DMA((n,))],
                outs=[jax.ShapeDtypeStruct(q.shape, q.dtype) for q in parts])


def _run_rider(rider, name):
    n = len(rider["ins"])

    def body(*refs):
        start, finish = rider["steps"](refs[:n], refs[n:2 * n], *refs[2 * n:])
        start()
        finish()

    return _pcall(body, name=name, out_shape=rider["outs"], in_specs=[ANY] * n, out_specs=[ANY] * n,
                  scratch_shapes=rider["scratch"])(*rider["ins"])


def _ride(body, rider, n_in, n_out, nsteps):
    if rider is None:
        return body
    n = len(rider["ins"])
    ns = len(rider["scratch"])

    def wrapped(*refs):
        ins, r_in = refs[:n_in], refs[n_in:n_in + n]
        outs, r_out = refs[n_in + n:n_in + n + n_out], refs[n_in + n + n_out:n_in + 2 * n + n_out]
        rest = refs[n_in + 2 * n + n_out:]
        scratch, r_scratch = rest[:len(rest) - ns], rest[len(rest) - ns:]
        start, finish = rider["steps"](r_in, r_out, *r_scratch)
        pl.when(pl.program_id(0) == 0)(start)
        body(*ins, *outs, *scratch)
        pl.when(pl.program_id(0) == nsteps - 1)(finish)

    return wrapped


def _row_tile(a):
    for t in range(256, 7, -8):
        if a % t == 0:
            return t
    return a


def _add_sibling_part(buf, recv, out_dtype, name):
    _, _, nl, a, b = buf.shape
    ta = _row_tile(a)
    core = lax.axis_index("c").astype(jnp.int32).reshape(1)

    def body(core_ref, a_ref, b_ref, o_ref):
        o_ref[...] = (a_ref[...] + b_ref[...]).astype(o_ref.dtype)

    grid_spec = pltpu.PrefetchScalarGridSpec(
        num_scalar_prefetch=1, grid=(4, nl, a // ta),
        in_specs=[pl.BlockSpec((None, None, None, ta, b), lambda k, l, r, cr: (k, cr[0], l, r, 0)),
                  pl.BlockSpec((None, None, ta, b), lambda k, l, r, cr: (k, l, r, 0))],
        out_specs=pl.BlockSpec((None, None, ta, b), lambda k, l, r, cr: (k, l, r, 0)))
    return _pcall(body, name=name, grid_spec=grid_spec,
                  out_shape=jax.ShapeDtypeStruct((4, nl, a, b), out_dtype),
                  compiler_params=_cp("parallel", "parallel", "parallel"))(core, buf, recv)


def _sum_adamw(stack, w, m, v, name):
    ns, nl, a, b = stack.shape
    ta = _row_tile(a)
    c1 = 1.0 - ADAM_B1 ** ADAM_STEP
    c2 = 1.0 - ADAM_B2 ** ADAM_STEP

    def body(s_ref, w_ref, m_ref, v_ref, g_out, d_out, m_out, v_out):
        g = s_ref[0].astype(F32)
        for k in range(1, ns):
            g = g + s_ref[k].astype(F32)
        m_new = ADAM_B1 * m_ref[...] + (1.0 - ADAM_B1) * g
        v_new = ADAM_B2 * v_ref[...] + (1.0 - ADAM_B2) * (g * g)
        m_hat = m_new / c1
        v_hat = v_new / c2
        g_out[...] = g
        d_out[...] = -ADAM_LR * (m_hat / (jnp.sqrt(v_hat) + ADAM_EPS) + ADAM_WD * w_ref[...])
        m_out[...] = m_new
        v_out[...] = v_new

    blk = pl.BlockSpec((None, ta, b), lambda l, i: (l, i, 0))
    return _pcall(
        body, name=name, grid=(nl, a // ta),
        in_specs=[pl.BlockSpec((ns, None, ta, b), lambda l, i: (0, l, i, 0)), blk, blk, blk],
        out_specs=[blk] * 4,
        out_shape=[jax.ShapeDtypeStruct((nl, a, b), F32)] * 4,
        compiler_params=_cp("parallel", "parallel"))(stack, w, m, v)


def _mod_fwd(cvec, wmod, bmod, name):
    _, d = cvec.shape

    def body(c_ref, w_ref, b_ref, o_ref):
        cv = c_ref[...]
        cs = cv * _sig(cv)
        o_ref[...] = _dot(cs.astype(BF16), w_ref[...]) + b_ref[...]

    return _pcall(
        body, name=name, grid=(3,),
        in_specs=[_full((8, d)), pl.BlockSpec((d, d), lambda n: (0, n)), pl.BlockSpec((1, d), lambda n: (0, n))],
        out_specs=pl.BlockSpec((8, d), lambda n: (0, n)),
        out_shape=jax.ShapeDtypeStruct((8, 3 * d), F32),
        compiler_params=_cp("parallel"))(cvec, wmod, bmod)


def _mod_bwd(dmod, cvec, cvec_t, wmod, name):
    _, d = cvec.shape

    def body(dm_ref, c_ref, ct_ref, w_ref, dw_ref, db_ref, dc_ref):
        n = pl.program_id(0)
        dm = dm_ref[...]
        ct = ct_ref[...]
        cs_t = _bf_round(ct * _sig(ct))
        d0 = _bf_round(dm[0:1, :])
        d1 = _bf_round(dm[1:2, :])
        dw_ref[...] = cs_t[:, 0:1] * d0 + cs_t[:, 1:2] * d1
        db_ref[...] = dm[0:1, :] + dm[1:2, :]

        @pl.when(n == 0)
        def _():
            dc_ref[...] = jnp.zeros_like(dc_ref)

        dc_ref[...] += _dot_nt(dm.astype(BF16), w_ref[...])

        @pl.when(n == 2)
        def _():
            cv = c_ref[...]
            s = _sig(cv)
            dc_ref[...] = dc_ref[...] * (s * (1.0 + cv * (1.0 - s)))

    return _pcall(
        body, name=name, grid=(3,),
        in_specs=[pl.BlockSpec((8, d), lambda n: (0, n)), _full((8, d)), _full((d, 128)),
                  pl.BlockSpec((d, d), lambda n: (0, n))],
        out_specs=[pl.BlockSpec((d, d), lambda n: (0, n)), pl.BlockSpec((1, d), lambda n: (0, n)),
                   _full((8, d))],
        out_shape=[jax.ShapeDtypeStruct((d, 3 * d), F32), jax.ShapeDtypeStruct((1, 3 * d), F32),
                   jax.ShapeDtypeStruct((8, d), F32)],
        compiler_params=_cp("arbitrary"))(dmod, cvec, cvec_t, wmod)


def _seg_rows(mod_ref, lo, hi, is_ctx):
    return jnp.where(is_ctx, mod_ref[1:2, lo:hi], mod_ref[0:1, lo:hi])


def _inproj(xa, modv, gpre, wp, n_lat, tm, name, shards=()):
    na, d = xa.shape
    wcols = wp.shape[1]
    tn = 5 * d // 4 if (5 * d // 4) % 128 == 0 and wcols % (5 * d // 4) == 0 else d // 2
    ns = len(shards)
    ni, nj = na // tm, wcols // tn

    def body(x_ref, mod_ref, g_ref, w_ref, *rest):
        p_ref, ht_ref = rest[ns:ns + 2]
        h_s = rest[2 * ns + 2]
        i = pl.program_id(0)
        j = pl.program_id(1)

        if ns:
            start, relay, finish = _gather_steps(rest[:ns], rest[ns + 2:2 * ns + 2], *rest[2 * ns + 3:])
            pl.when(jnp.logical_and(i == 0, j == 0))(start)
            pl.when(jnp.logical_and(i == max(ni - 2, 0), j == 0))(relay)

        @pl.when(j == 0)
        def _():
            x = x_ref[...]
            r = lax.rsqrt(_mean(x * x) + EPS)
            row = i * tm + lax.broadcasted_iota(jnp.int32, (tm, 1), 0)
            is_ctx = row >= n_lat
            sh = _seg_rows(mod_ref, 0, d, is_ctx)
            sc = _seg_rows(mod_ref, d, 2 * d, is_ctx)
            h = (x * r * g_ref[...]) * (1.0 + sc) + sh
            h_s[...] = h.astype(BF16)
            ht_ref[...] = h.T.astype(BF16)

        p_ref[...] = _dot(h_s[...], w_ref[...])

        if ns:
            pl.when(jnp.logical_and(i == ni - 1, j == nj - 1))(finish)

    return _pcall(
        body, name=name, grid=(ni, nj),
        in_specs=[pl.BlockSpec((tm, d), lambda i, j: (i, 0)), _full((8, 3 * d)), _full((1, d)),
                  pl.BlockSpec((d, tn), lambda i, j: (0, j))] + [ANY] * ns,
        out_specs=[pl.BlockSpec((tm, tn), lambda i, j: (i, j)), pl.BlockSpec((d, tm), lambda i, j: (0, i))]
        + [ANY] * ns,
        out_shape=[jax.ShapeDtypeStruct((na, wcols), F32), jax.ShapeDtypeStruct((d, na), BF16)]
        + [jax.ShapeDtypeStruct((N_DEV,) + q.shape, q.dtype) for q in shards],
        scratch_shapes=[pltpu.VMEM((tm, d), BF16)] + (_gather_scratch(ns) if ns else []),
        compiler_params=_cp("arbitrary" if ns else "parallel", "arbitrary"))(xa, modv, gpre, wp, *shards)


def _lane_tile(t, width):
    if width >= 128:
        return jnp.tile(t, (1, width // 128))
    return t[:, :width]


def _partner(x):
    w = x.shape[-1]
    lane = lax.broadcasted_iota(jnp.int32, x.shape, 1)
    low = (lane % HEAD_DIM) < (HEAD_DIM // 2)
    return jnp.where(low, pltpu.roll(x, w - HEAD_DIM // 2, 1), pltpu.roll(x, HEAD_DIM // 2, 1))


def _qknorm_fwd(p, cos_t, sin_t, gq, gk, esel, esel_t, tm, name):
    na = p.shape[0]
    d = gq.shape[1]
    kw = d // GROUP

    def body(q_ref, kv_ref, cos_ref, sin_ref, gq_ref, gk_ref, es_ref, est_ref, qt_ref, ko_ref, vo_ref, kt_ref,
             vt_ref):
        cos = cos_ref[...]
        sin = sin_ref[...]

        def norm_rope(xh, g, w):
            ms = _head_sum(xh * xh, es_ref[0:w, :], est_ref[:, 0:w]) * (1.0 / HEAD_DIM)
            xn = xh * lax.rsqrt(ms + EPS) * g
            return xn * _lane_tile(cos, w) + _partner(xn) * _lane_tile(sin, w)

        qt_ref[...] = (norm_rope(q_ref[...], gq_ref[...], d) * Q_PRESCALE).T.astype(BF16)
        kv = kv_ref[...]
        k = norm_rope(kv[:, 0:kw], gk_ref[...], kw)
        v = kv[:, kw:2 * kw]
        ko_ref[...] = k.astype(BF16)
        vo_ref[...] = v.astype(BF16)
        kt_ref[...] = k.T.astype(BF16)
        vt_ref[...] = v.T.astype(BF16)

    cols = lambda w: pl.BlockSpec((w, tm), lambda i: (0, i))
    return _pcall(
        body, name=name, grid=(na // tm,),
        in_specs=[_rows(tm, d, 3), _rows(tm, d // 2, 14), _rows(tm, 128), _rows(tm, 128),
                  _full((1, d)), _full((1, kw)), _full((d, 128)), _full((128, d))],
        out_specs=[cols(d), _rows(tm, kw), _rows(tm, kw), cols(kw), cols(kw)],
        out_shape=[jax.ShapeDtypeStruct((d, na), BF16), jax.ShapeDtypeStruct((na, kw), BF16),
                   jax.ShapeDtypeStruct((na, kw), BF16), jax.ShapeDtypeStruct((kw, na), BF16),
                   jax.ShapeDtypeStruct((kw, na), BF16)],
        compiler_params=_cp("parallel"))(p, p, cos_t, sin_t, gq, gk, esel, esel_t)


def _attn_fwd(qt_hm, k_hm, vt_hm, na, tq, tk, cw, name):
    h, hd, _ = qt_hm.shape
    kv, nkeys, _ = k_hm.shape
    vrows = vt_hm.shape[1]
    nq, nk = na // tq, nkeys // tk
    nsub = tq // cw
    chains = [(a, u) for a in range(GROUP) for u in range(nsub)]

    def body(q_ref, k_ref, vt_ref, o_ref, lse_ref, m_s, acc_s):
        j = pl.program_id(2)

        @pl.when(j == 0)
        def _():
            m_s[...] = jnp.full_like(m_s, NEG_BIG)
            acc_s[...] = jnp.zeros_like(acc_s)

        k = k_ref[...]
        vt = vt_ref[...]

        def scores(n):
            a, u = chains[n]
            return _dot(k, q_ref[a, :, u * cw:(u + 1) * cw])

        s_tiles = {n: scores(n) for n in range(min(ATTN_FWD_AHEAD, len(chains)))}
        pending = None
        for n, (a, u) in enumerate(chains):
            cols = slice(u * cw, (u + 1) * cw)
            s_t = s_tiles.pop(n)
            m_prev = m_s[a, :, cols]
            m_new = jnp.maximum(m_prev, jnp.max(s_t, axis=0, keepdims=True))
            m_s[a, :, cols] = m_new
            pv = _dot(vt, jnp.exp2(s_t - m_new).astype(BF16))
            if pending is not None:
                pa, pcols, palpha, ppv = pending
                acc_s[pa, :, pcols] = palpha * acc_s[pa, :, pcols] + ppv
            pending = (a, cols, jnp.exp2(m_prev - m_new), pv)
            if n + ATTN_FWD_AHEAD < len(chains):
                s_tiles[n + ATTN_FWD_AHEAD] = scores(n + ATTN_FWD_AHEAD)
        pa, pcols, palpha, ppv = pending
        acc_s[pa, :, pcols] = palpha * acc_s[pa, :, pcols] + ppv

        @pl.when(j == nk - 1)
        def _():
            for a in range(GROUP):
                acc = acc_s[a]
                l = acc[hd:hd + 1, :]
                o_ref[a] = acc[0:hd, :] / l
                lse_ref[a] = m_s[a] + jnp.log2(l)

    return _pcall(
        body, name=name, grid=(kv, nq, nk),
        in_specs=[pl.BlockSpec((GROUP, hd, tq), lambda g, i, j: (g, 0, i)),
                  pl.BlockSpec((None, tk, hd), lambda g, i, j: (g, j, 0)),
                  pl.BlockSpec((None, vrows, tk), lambda g, i, j: (g, 0, j))],
        out_specs=[pl.BlockSpec((GROUP, hd, tq), lambda g, i, j: (g, 0, i)),
                   pl.BlockSpec((GROUP, 1, tq), lambda g, i, j: (g, 0, i))],
        out_shape=[jax.ShapeDtypeStruct((h, hd, na), F32), jax.ShapeDtypeStruct((h, 1, na), F32)],
        scratch_shapes=[pltpu.VMEM((GROUP, 1, tq), F32), pltpu.VMEM((GROUP, vrows, tq), F32)],
        compiler_params=_cp("parallel", "parallel", "arbitrary"))(qt_hm, k_hm, vt_hm)


def _window(win_ref, prev, cur, nxt, first, last, tm):
    win_ref[0:HALO, :] = jnp.where(first, 0.0, prev)
    win_ref[HALO:HALO + tm, :] = cur
    win_ref[HALO + tm:HALO + tm + HALO, :] = jnp.where(last, 0.0, nxt)


def _lane_blocks(d):
    return [slice(b, b + 128) for b in range(0, d, 128)]


def _sublane_shifts(slab):
    n = slab.shape[0]
    for b in range(8):
        sh = slab if b == 0 else pltpu.roll(slab, n - b, 0)
        for a8 in range(0, 2 * HALO, 8):
            yield a8 + b, sh[a8:a8 + CONV_ROWS, :]


def _halo_specs(tm, d, na, colblk):
    per = tm // HALO
    last_blk = na // HALO - 1
    prev = pl.BlockSpec((HALO, d), lambda i: (jnp.maximum(i * per - 1, 0), colblk))
    nxt = pl.BlockSpec((HALO, d), lambda i: (jnp.minimum((i + 1) * per, last_blk), colblk))
    return prev, nxt


def _seq_ends(i, n_lat, na, tm):
    first = jnp.logical_or(i == 0, i == n_lat // tm)
    last = jnp.logical_or(i == n_lat // tm - 1, i == na // tm - 1)
    return first, last


def _conv_fwd(p, conv_w, conv_b, ln_g, ln_b, n_lat, tm, name):
    na = p.shape[0]
    ktaps, d = conv_w.shape
    pad = ktaps // 2

    def body(a_ref, ap_ref, an_ref, g_ref, gp_ref, gn_ref, ga_ref, w_ref, cb_ref, lg_ref, lb_ref,
             y5_ref, y5t_ref, y2_ref, win):
        i = pl.program_id(0)
        first, last = _seq_ends(i, n_lat, na, tm)
        _window(win, ap_ref[...] * _sig(gp_ref[...]), a_ref[...] * _sig(g_ref[...]),
                an_ref[...] * _sig(gn_ref[...]), first, last, tm)

        def chunk(c, carry):
            r = pl.multiple_of(c * CONV_ROWS, CONV_ROWS)
            for lanes in _lane_blocks(d):
                acc = jnp.zeros((CONV_ROWS, 128), F32)
                for off, sh in _sublane_shifts(win[pl.ds(r, CONV_ROWS + 2 * HALO), lanes]):
                    k = off - (HALO - pad)
                    if 0 <= k < ktaps:
                        acc = acc + w_ref[k:k + 1, lanes] * sh
                y2_ref[pl.ds(r, CONV_ROWS), lanes] = acc + cb_ref[:, lanes]
            return carry

        lax.fori_loop(0, tm // CONV_ROWS, chunk, 0)
        y2 = y2_ref[...]
        xc = y2 - _mean(y2)
        y3 = xc * lax.rsqrt(_mean(xc * xc) + EPS) * lg_ref[...] + lb_ref[...]
        gate = ga_ref[...]
        y5 = (y3 * _sig(y3)) * (gate * _sig(gate))
        y5_ref[...] = y5.astype(BF16)
        y5t_ref[...] = y5.T.astype(BF16)

    ap, an = _halo_specs(tm, d, na, 0)
    gp, gn = _halo_specs(tm, d, na, 1)
    return _pcall(
        body, name=name, grid=(na // tm,),
        in_specs=[_rows(tm, d, 0), ap, an, _rows(tm, d, 1), gp, gn, _rows(tm, d, 2),
                  _full((ktaps, d)), _full((1, d)), _full((1, d)), _full((1, d))],
        out_specs=[_rows(tm, d), _cols(d, tm), _rows(tm, d)],
        out_shape=[jax.ShapeDtypeStruct((na, d), BF16), jax.ShapeDtypeStruct((d, na), BF16),
                   jax.ShapeDtypeStruct((na, d), F32)],
        scratch_shapes=[pltpu.VMEM((tm + 2 * HALO, d), F32)],
        compiler_params=_cp("parallel"))(p, p, p, p, p, p, p, conv_w, conv_b, ln_g, ln_b)


def _merge_fwd(y5, ot_lat, ot_ctx, p, xa, modv, gpost, wc, wa, wo, n_lat, tm, name):
    na, d = xa.shape

    def body(y5_ref, ol_ref, oc_ref, gb_ref, ma_ref, mb_ref, x_ref, mod_ref, gp_ref, wc_ref, wa_ref, wo_ref,
             xn_ref, yc_ref, ya_ref, out_ref, zt_ref, ogt_ref):
        is_ctx = pl.program_id(0) >= n_lat // tm
        gate_b = gb_ref[...]
        o = jnp.where(is_ctx, oc_ref[...], ol_ref[...]).T
        og = o * (gate_b * _sig(gate_b))
        ogt_ref[...] = og.T.astype(BF16)
        yc = _dot(y5_ref[...], wc_ref[...])
        ya = _dot(og.astype(BF16), wa_ref[...])
        yc_ref[...] = yc
        ya_ref[...] = ya
        z = _sig(ma_ref[...]) * yc + _sig(mb_ref[...]) * ya
        zt_ref[...] = z.T.astype(BF16)
        out = _dot(z.astype(BF16), wo_ref[...])
        out_ref[...] = out
        gt = _seg_rows(mod_ref, 2 * d, 3 * d, is_ctx)
        xn_ref[...] = x_ref[...] + gt * (out * lax.rsqrt(_mean(out * out) + EPS) * gp_ref[...])

    f32o = jax.ShapeDtypeStruct((na, d), F32)
    bft = jax.ShapeDtypeStruct((d, na), BF16)
    ol, oc = _split_cols(d, tm, n_lat)
    return _pcall(
        body, name=name, grid=(na // tm,),
        in_specs=[_rows(tm, d), ol, oc, _rows(tm, d, 4), _rows(tm, d, 5), _rows(tm, d, 6), _rows(tm, d),
                  _full((8, 3 * d)), _full((1, d)), _full((d, d)), _full((d, d)), _full((d, d))],
        out_specs=[_rows(tm, d)] * 4 + [_cols(d, tm)] * 2,
        out_shape=[f32o, f32o, f32o, f32o, bft, bft],
        compiler_params=_cp("parallel"))(y5, ot_lat, ot_ctx, p, p, p, xa, modv, gpost, wc, wa, wo)


def _loss_grad(xa, target, n_lat, tm, name):
    na, d = xa.shape
    nlt = n_lat // tm

    def body(x_ref, t_ref, dx_ref, loss_ref):
        i = pl.program_id(0)

        @pl.when(i == 0)
        def _():
            loss_ref[...] = jnp.zeros_like(loss_ref)

        @pl.when(i < nlt)
        def _():
            err = x_ref[...] - t_ref[...]
            dx_ref[...] = err * (1.0 / d)
            loss_ref[...] += 0.5 * jnp.sum(_mean(err * err))

        @pl.when(i >= nlt)
        def _():
            dx_ref[...] = jnp.zeros_like(dx_ref)

    return _pcall(
        body, name=name, grid=(na // tm,),
        in_specs=[_rows(tm, d), pl.BlockSpec((tm, d), lambda i: (jnp.minimum(i, nlt - 1), 0))],
        out_specs=[_rows(tm, d), _full((8, 128))],
        out_shape=[jax.ShapeDtypeStruct((na, d), F32), jax.ShapeDtypeStruct((8, 128), F32)],
        compiler_params=_cp("arbitrary"))(xa, target)


def _merge_bwd(dxn, out, yc, ya, ot_lat, ot_ctx, p, y2, modv, gpost, ln_g, ln_b, wo_t, wc_t, wa_t, esel, n_lat, tm,
               name, rider=None):
    na, d = dxn.shape

    def body(dx_ref, out_ref, yc_ref, ya_ref, ol_ref, oc_ref, gb_ref, ma_ref, mb_ref, gta_ref, y2_ref,
             mod_ref, gp_ref, lg_ref, lb_ref, wot_ref, wct_ref, wat_ref, es_ref,
             dgb_ref, dma_ref, dmb_ref, dgta_ref, dot_ref, dout_ref, dyc_ref, dya_ref, dy2_ref, dlt_ref, acc_ref):
        i = pl.program_id(0)
        ctx_tile = i >= n_lat // tm

        @pl.when(i == 0)
        def _():
            acc_ref[...] = jnp.zeros_like(acc_ref)

        gt = _seg_rows(mod_ref, 2 * d, 3 * d, ctx_tile)
        gp = gp_ref[...]
        dx = dx_ref[...]
        out = out_ref[...]
        r2 = lax.rsqrt(_mean(out * out) + EPS)
        n2 = out * r2
        dgt = _colsum(dx * (n2 * gp))

        @pl.when(ctx_tile)
        def _():
            acc_ref[1:2, :] += dgt

        @pl.when(jnp.logical_not(ctx_tile))
        def _():
            acc_ref[0:1, :] += dgt

        acc_ref[2:3, :] += _colsum(dx * gt * n2)
        dn2 = dx * gt * gp
        dout = (r2 * (dn2 - n2 * _mean(dn2 * n2))).astype(BF16)
        dout_ref[...] = dout
        dz = _dot(dout, wot_ref[...])
        sa = _sig(ma_ref[...])
        sb = _sig(mb_ref[...])
        dyc = (dz * sa).astype(BF16)
        dya = (dz * sb).astype(BF16)
        dyc_ref[...] = dyc
        dya_ref[...] = dya
        dma_ref[...] = (dz * yc_ref[...] * sa * (1.0 - sa)).astype(BF16)
        dmb_ref[...] = (dz * ya_ref[...] * sb * (1.0 - sb)).astype(BF16)
        dy5 = _dot(dyc, wct_ref[...])
        dog = _dot(dya, wat_ref[...])

        gate_b = gb_ref[...]
        sgb = _sig(gate_b)
        o = jnp.where(ctx_tile, oc_ref[...], ol_ref[...]).T
        do = dog * (gate_b * sgb)
        dot_ref[...] = do.T.astype(BF16)
        dgb_ref[...] = (dog * o * (sgb * (1.0 + gate_b * (1.0 - sgb)))).astype(BF16)
        dlt_ref[...] = _split_dot(do * o, es_ref[...]).T

        y2 = y2_ref[...]
        xc = y2 - _mean(y2)
        rstd = lax.rsqrt(_mean(xc * xc) + EPS)
        xhat = xc * rstd
        lg = lg_ref[...]
        y3 = xhat * lg + lb_ref[...]
        s3 = _sig(y3)
        gate_a = gta_ref[...]
        sga = _sig(gate_a)
        dgta_ref[...] = (dy5 * (y3 * s3) * (sga * (1.0 + gate_a * (1.0 - sga)))).astype(BF16)
        dy3 = dy5 * (gate_a * sga) * (s3 * (1.0 + y3 * (1.0 - s3)))
        acc_ref[3:4, :] += _colsum(dy3 * xhat)
        acc_ref[4:5, :] += _colsum(dy3)
        dxh = dy3 * lg
        dy2 = rstd * (dxh - _mean(dxh) - xhat * _mean(dxh * xhat))
        dy2_ref[...] = dy2
        acc_ref[5:6, :] += _colsum(dy2)

    f32o = jax.ShapeDtypeStruct((na, d), F32)
    bfo = jax.ShapeDtypeStruct((na, d), BF16)
    r = _rows(tm, d)
    ol, oc = _split_cols(d, tm, n_lat)
    ins = (dxn, out, yc, ya, ot_lat, ot_ctx, p, p, p, p, y2, modv, gpost, ln_g, ln_b, wo_t, wc_t, wa_t, esel)
    extra = rider or dict(ins=[], outs=[], scratch=[])
    return _pcall(
        _ride(body, rider, len(ins), 11, na // tm), name=name, grid=(na // tm,),
        in_specs=[r, r, r, r, ol, oc, _rows(tm, d, 4), _rows(tm, d, 5), _rows(tm, d, 6), _rows(tm, d, 2), r,
                  _full((8, 3 * d)), _full((1, d)), _full((1, d)), _full((1, d)),
                  _full((d, d)), _full((d, d)), _full((d, d)), _full((d, 128))] + [ANY] * len(extra["ins"]),
        out_specs=[r] * 4 + [_cols(d, tm)] + [r] * 4 + [_cols(128, tm), _full((8, d))] + [ANY] * len(extra["outs"]),
        out_shape=[bfo] * 4 + [jax.ShapeDtypeStruct((d, na), BF16)] + [bfo] * 3
        + [f32o, jax.ShapeDtypeStruct((128, na), F32), jax.ShapeDtypeStruct((8, d), F32)] + extra["outs"],
        scratch_shapes=extra["scratch"],
        compiler_params=_cp("arbitrary"),
    )(*ins, *extra["ins"])


def _attn_bwd(qt_hm, k_hm, kt_hm, v_hm, dot_hm, lse_r, dl_r, na, tq, tk, cw, name):
    h, hd, _ = qt_hm.shape
    kv, nkeys, _ = k_hm.shape
    nq, nk = na // tq, nkeys // tk
    nsub = tq // cw
    chains = [(a, u) for a in range(GROUP) for u in range(nsub)]

    def body(qt_ref, k_ref, kt_ref, v_ref, dot_ref, lse_ref, dl_ref, dq_hbm, dk_ref, dv_ref,
             dq_acc, dk_acc, dv_acc, sem):
        g = pl.program_id(0)
        j = pl.program_id(1)
        i = pl.program_id(2)

        @pl.when(jnp.logical_and(j == 0, i == 0))
        def _():
            dq_acc[...] = jnp.zeros_like(dq_acc)

        @pl.when(i == 0)
        def _():
            dk_acc[...] = jnp.zeros_like(dk_acc)
            dv_acc[...] = jnp.zeros_like(dv_acc)

        k = k_ref[...]
        v = v_ref[...]
        kt = kt_ref[...]

        def products(n):
            a, u = chains[n]
            cols = slice(u * cw, (u + 1) * cw)
            return _dot(k, qt_ref[a, :, cols]), _dot(v, dot_ref[a, :, cols])

        def accumulate(done):
            a, u, dv_t, dk_t, dq_t = done
            dv_acc[...] += dv_t
            dk_acc[...] += dk_t
            at = pl.multiple_of(i * tq + u * cw, cw)
            dq_acc[a, :, pl.ds(at, cw)] += dq_t

        tiles = {n: products(n) for n in range(min(ATTN_BWD_AHEAD, len(chains)))}
        pending = None
        for n, (a, u) in enumerate(chains):
            cols = slice(u * cw, (u + 1) * cw)
            s_t, dp_t = tiles.pop(n)
            p_t = jnp.exp2(s_t - lse_ref[a, :, cols])
            ds_b = (p_t * (dp_t - dl_ref[a, :, cols])).astype(BF16)
            p_b = p_t.astype(BF16)
            dv_t = _dot_nt(dot_ref[a, :, cols], p_b)
            dk_t = _dot_nt(qt_ref[a, :, cols], ds_b)
            dq_t = _dot(kt, ds_b)
            if pending is not None:
                accumulate(pending)
            pending = (a, u, dv_t, dk_t, dq_t)
            if n + ATTN_BWD_AHEAD < len(chains):
                tiles[n + ATTN_BWD_AHEAD] = products(n + ATTN_BWD_AHEAD)
        accumulate(pending)

        @pl.when(i == nq - 1)
        def _():
            dk_ref[...] = dk_acc[...]
            dv_ref[...] = dv_acc[...]

        @pl.when(jnp.logical_and(j == nk - 1, i == nq - 1))
        def _():
            cp = pltpu.make_async_copy(dq_acc, dq_hbm.at[pl.ds(g * GROUP, GROUP)], sem)
            cp.start()
            cp.wait()

    qtspec = pl.BlockSpec((GROUP, hd, tq), lambda g, j, i: (g, 0, i))
    kspec = pl.BlockSpec((None, tk, hd), lambda g, j, i: (g, j, 0))
    ktspec = pl.BlockSpec((None, hd, tk), lambda g, j, i: (g, 0, j))
    rspec = pl.BlockSpec((GROUP, 1, tq), lambda g, j, i: (g, 0, i))
    return _pcall(
        body, name=name, grid=(kv, nk, nq),
        in_specs=[qtspec, kspec, ktspec, kspec, qtspec, rspec, rspec],
        out_specs=[ANY, ktspec, ktspec],
        out_shape=[jax.ShapeDtypeStruct((h, hd, na), F32), jax.ShapeDtypeStruct((kv, hd, nkeys), F32),
                   jax.ShapeDtypeStruct((kv, hd, nkeys), F32)],
        scratch_shapes=[pltpu.VMEM((GROUP, hd, na), F32), pltpu.VMEM((hd, tk), F32), pltpu.VMEM((hd, tk), F32),
                        pltpu.SemaphoreType.DMA],
        compiler_params=_cp("arbitrary", "arbitrary", "arbitrary"),
    )(qt_hm, k_hm, kt_hm, v_hm, dot_hm, lse_r, dl_r)


def _qknorm_bwd(dqt_lat, dqt_ctx, dkt, dkt_ctx, dvt, dvt_ctx, p, cos_t, sin_t, gq, gk, esel, esel_t, n_lat, tm, name):
    na = p.shape[0]
    d = gq.shape[1]
    kw = d // GROUP

    def body(dql_ref, dqc_ref, dkl_ref, dkc_ref, dvl_ref, dvc_ref, q_ref, kv_ref, cos_ref, sin_ref, gq_ref, gk_ref,
             es_ref, est_ref, dqo_ref, dkvo_ref, acc_ref):
        is_ctx = pl.program_id(0) >= n_lat // tm

        @pl.when(pl.program_id(0) == 0)
        def _():
            acc_ref[...] = jnp.zeros_like(acc_ref)

        dq_in = jnp.where(is_ctx, dqc_ref[...], dql_ref[...]).T
        dk_in = (dkl_ref[...] + jnp.where(is_ctx, dkc_ref[...], 0.0)).T
        dv_in = (dvl_ref[...] + jnp.where(is_ctx, dvc_ref[...], 0.0)).T

        cos = cos_ref[...]
        sin = sin_ref[...]

        def back(dy, xh, g, w):
            es, es_t = es_ref[0:w, :], est_ref[:, 0:w]
            dn = dy * _lane_tile(cos, w) - _partner(dy) * _lane_tile(sin, w)
            rs = lax.rsqrt(_head_sum(xh * xh, es, es_t) * (1.0 / HEAD_DIM) + EPS)
            y = xh * rs
            dg = _colsum(dn * y)
            dyn = dn * g
            dx = rs * (dyn - y * (_head_sum(dyn * y, es, es_t) * (1.0 / HEAD_DIM)))
            return dx, dg

        dq, dgq = back(dq_in * ATTN_SCALE, q_ref[...], gq_ref[...], d)
        dqo_ref[...] = dq.astype(BF16)
        acc_ref[0:1, :] += dgq
        kv = kv_ref[...]
        dk, dgk = back(dk_in * (1.0 / LOG2E), kv[:, 0:kw], gk_ref[...], kw)
        acc_ref[1:2, 0:kw] += dgk
        dkvo_ref[:, 0:kw] = dk.astype(BF16)
        dkvo_ref[:, kw:2 * kw] = dv_in.astype(BF16)

    dql, dqc = _split_cols(d, tm, n_lat)
    _, kvc = _split_cols(kw, tm, n_lat)
    return _pcall(
        body, name=name, grid=(na // tm,),
        in_specs=[dql, dqc, _cols(kw, tm), kvc, _cols(kw, tm), kvc, _rows(tm, d, 3), _rows(tm, d // 2, 14),
                  _rows(tm, 128), _rows(tm, 128), _full((1, d)), _full((1, kw)), _full((d, 128)), _full((128, d))],
        out_specs=[_rows(tm, d), _rows(tm, d // 2), _full((8, d))],
        out_shape=[jax.ShapeDtypeStruct((na, d), BF16), jax.ShapeDtypeStruct((na, d // 2), BF16),
                   jax.ShapeDtypeStruct((8, d), F32)],
        compiler_params=_cp("arbitrary"))(dqt_lat, dqt_ctx, dkt, dkt_ctx, dvt, dvt_ctx, p, p, cos_t, sin_t, gq, gk,
                                          esel, esel_t)


def _conv_bwd(dy2, p, conv_w, n_lat, tm, name, rider=None):
    na = p.shape[0]
    ktaps, d = conv_w.shape
    pad = ktaps // 2

    def body(dy_ref, dyp_ref, dyn_ref, a_ref, ap_ref, an_ref, g_ref, gp_ref, gn_ref, w_ref,
             da_ref, dg_ref, dw_ref, dwin, ywin, dy1_s, part):
        i = pl.program_id(0)

        @pl.when(i == 0)
        def _():
            part[...] = jnp.zeros_like(part)

        first, last = _seq_ends(i, n_lat, na, tm)
        a = a_ref[...]
        sg = _sig(g_ref[...])
        _window(dwin, dyp_ref[...], dy_ref[...], dyn_ref[...], first, last, tm)
        _window(ywin, ap_ref[...] * _sig(gp_ref[...]), a * sg, an_ref[...] * _sig(gn_ref[...]), first, last, tm)

        def chunk(c, carry):
            r = pl.multiple_of(c * CONV_ROWS, CONV_ROWS)
            for lanes in _lane_blocks(d):
                dy = dy_ref[pl.ds(r, CONV_ROWS), lanes]
                acc = jnp.zeros((CONV_ROWS, 128), F32)
                for off, sh in _sublane_shifts(dwin[pl.ds(r, CONV_ROWS + 2 * HALO), lanes]):
                    k = HALO + pad - off
                    if 0 <= k < ktaps:
                        acc = acc + w_ref[k:k + 1, lanes] * sh
                dy1_s[pl.ds(r, CONV_ROWS), lanes] = acc
                for off, sh in _sublane_shifts(ywin[pl.ds(r, CONV_ROWS + 2 * HALO), lanes]):
                    k = off - (HALO - pad)
                    if 0 <= k < ktaps:
                        part[k, :, lanes] += jnp.sum((dy * sh).reshape(CONV_ROWS // 8, 8, 128), axis=0)
            return carry

        lax.fori_loop(0, tm // CONV_ROWS, chunk, 0)
        dy1 = dy1_s[...]
        da_ref[...] = (dy1 * sg).astype(BF16)
        dg_ref[...] = (dy1 * a * sg * (1.0 - sg)).astype(BF16)

        @pl.when(i == na // tm - 1)
        def _():
            dw_ref[...] = jnp.sum(part[...], axis=1)

    dyp, dyn = _halo_specs(tm, d, na, 0)
    ap, an = _halo_specs(tm, d, na, 0)
    gp, gn = _halo_specs(tm, d, na, 1)
    bfo = jax.ShapeDtypeStruct((na, d), BF16)
    ins = (dy2, dy2, dy2, p, p, p, p, p, p, conv_w)
    extra = rider or dict(ins=[], outs=[], scratch=[])
    return _pcall(
        _ride(body, rider, len(ins), 3, na // tm), name=name, grid=(na // tm,),
        in_specs=[_rows(tm, d), dyp, dyn, _rows(tm, d, 0), ap, an, _rows(tm, d, 1), gp, gn, _full((ktaps, d))]
        + [ANY] * len(extra["ins"]),
        out_specs=[_rows(tm, d), _rows(tm, d), _full((ktaps, d))] + [ANY] * len(extra["outs"]),
        out_shape=[bfo, bfo, jax.ShapeDtypeStruct((ktaps, d), F32)] + extra["outs"],
        scratch_shapes=[pltpu.VMEM((tm + 2 * HALO, d), F32), pltpu.VMEM((tm + 2 * HALO, d), F32),
                        pltpu.VMEM((tm, d), F32), pltpu.VMEM((ktaps, 8, d), F32)] + extra["scratch"],
        compiler_params=_cp("arbitrary"))(*ins, *extra["ins"])


def _inproj_bwd(segs, dkv, xa, dxn, modv, gpre, wp_t, n_lat, out_rows, tm, name):
    na, d = xa.shape
    nseg = len(segs)
    wrows = wp_t.shape[0]

    def body(*refs):
        seg_refs = refs[:nseg]
        dkv_ref, x_ref, dxn_ref, mod_ref, g_ref, wt_hbm, dx_ref, acc_ref, wt, sem = refs[nseg:]
        i = pl.program_id(0)
        ctx_tile = i >= n_lat // tm

        @pl.when(i == 0)
        def _():
            cp = pltpu.make_async_copy(wt_hbm, wt, sem)
            cp.start()
            cp.wait()
            acc_ref[...] = jnp.zeros_like(acc_ref)

        dh = _dot(dkv_ref[...], wt[nseg * d:wrows, :])
        for s in range(nseg):
            dh = dh + _dot(seg_refs[s][...], wt[s * d:(s + 1) * d, :])
        x = x_ref[...]
        r = lax.rsqrt(_mean(x * x) + EPS)
        xn = x * r
        g = g_ref[...]
        sc1 = 1.0 + _seg_rows(mod_ref, d, 2 * d, ctx_tile)
        dsh = _colsum(dh)
        dsc = _colsum(dh * (xn * g))

        @pl.when(ctx_tile)
        def _():
            acc_ref[1:2, :] += dsh
            acc_ref[3:4, :] += dsc

        @pl.when(jnp.logical_not(ctx_tile))
        def _():
            acc_ref[0:1, :] += dsh
            acc_ref[2:3, :] += dsc

        acc_ref[4:5, :] += _colsum(dh * xn * sc1)
        dxh = dh * g * sc1

        @pl.when(i < out_tiles)
        def _():
            dx_ref[...] = dxn_ref[...] + r * (dxh - xn * _mean(dxh * xn))

    r_ = _rows(tm, d)
    out_tiles = out_rows // tm
    return _pcall(
        body, name=name, grid=(na // tm,),
        in_specs=[r_] * nseg + [_rows(tm, d // 2), r_, r_, _full((8, 3 * d)), _full((1, d)), ANY],
        out_specs=[pl.BlockSpec((tm, d), lambda i: (jnp.minimum(i, out_tiles - 1), 0)), _full((8, d))],
        out_shape=[jax.ShapeDtypeStruct((out_rows, d), F32), jax.ShapeDtypeStruct((8, d), F32)],
        scratch_shapes=[pltpu.VMEM(wp_t.shape, BF16), pltpu.SemaphoreType.DMA],
        compiler_params=_cp("arbitrary"))(*segs, dkv, xa, dxn, modv, gpre, wp_t)


def _grad_matmul(a_t, b, tk, name):
    ka, na = a_t.shape
    nb = b.shape[1]
    tn = min(nb, 1024)

    def body(a_ref, b_ref, o_ref):
        @pl.when(pl.program_id(1) == 0)
        def _():
            o_ref[...] = jnp.zeros_like(o_ref)

        o_ref[...] += _dot(a_ref[...], b_ref[...])

    return _pcall(
        body, name=name, grid=(nb // tn, na // tk),
        in_specs=[pl.BlockSpec((ka, tk), lambda n, k: (0, k)), pl.BlockSpec((tk, tn), lambda n, k: (k, n))],
        out_specs=pl.BlockSpec((ka, tn), lambda n, k: (0, n)),
        out_shape=jax.ShapeDtypeStruct((ka, nb), F32),
        compiler_params=_cp("parallel", "arbitrary"))(a_t, b)


def _pack(parts, cols, row_mult):
    flat = jnp.concatenate([q.astype(F32).reshape(-1) for q in parts])
    rows = -(-flat.shape[0] // (cols * row_mult)) * row_mult
    return jnp.pad(flat, (0, rows * cols - flat.shape[0])).reshape(1, rows, cols)


def _unpack(flat, shapes):
    out, off = [], 0
    for s in shapes:
        n = math.prod(s)
        out.append(flat[off:off + n].reshape(tuple(s)))
        off += n
    return out


def _cols_by_dest(g):
    l, a, w8 = g.shape
    return g.reshape(l, a, N_DEV, w8 // N_DEV).transpose(2, 0, 1, 3)


def _rows_by_dest(g):
    l, r8, b = g.shape
    return g.reshape(l, N_DEV, r8 // N_DEV, b).transpose(1, 0, 2, 3)


def _cols_from_src(s):
    n, l, a, w = s.shape
    return s.transpose(1, 2, 0, 3).reshape(l, a, n * w)


def _rows_from_src(s):
    n, l, r, b = s.shape
    return s.transpose(1, 0, 2, 3).reshape(l, n * r, b)


def _rope_tables(n_lat, n_ctx):
    half = HEAD_DIM // 2
    rows = n_lat // GRID_W
    row = jnp.repeat(jnp.arange(rows, dtype=F32), GRID_W)
    col = jnp.tile(jnp.arange(GRID_W, dtype=F32), rows)
    inv_freq = ROPE_THETA ** (-jnp.arange(0, half, 2, dtype=F32) / half)
    ang = jnp.concatenate([row[:, None] * inv_freq, col[:, None] * inv_freq], axis=-1)
    cos, sin = jnp.cos(ang), jnp.sin(ang)
    cos_t = jnp.concatenate([jnp.tile(cos, (1, 4)), jnp.ones((n_ctx, 128), F32)], axis=0)
    sin_t = jnp.concatenate([jnp.tile(jnp.concatenate([-sin, sin], axis=-1), (1, 2)),
                             jnp.zeros((n_ctx, 128), F32)], axis=0)
    return cos_t, sin_t


def _to_heads(t):
    na, w = t.shape
    return t.reshape(na, w // HEAD_DIM, HEAD_DIM).transpose(1, 0, 2)


def kernel(x, c, ctx, c_ctx, w_mod, b_mod, g_pre, g_post, w_in, conv_w, conv_b, ln_g, ln_b, w_conv_out, q_norm_g, k_norm_g, w_attn_out, w_out, loss_target, m_c_ctx, m_w_mod, m_b_mod, m_g_pre, m_g_post, m_w_in, m_conv_w, m_conv_b, m_ln_g, m_ln_b, m_w_conv_out, m_q_norm_g, m_k_norm_g, m_w_attn_out, m_w_out, v_c_ctx, v_w_mod, v_b_mod, v_g_pre, v_g_post, v_w_in, v_conv_w, v_conv_b, v_ln_g, v_ln_b, v_w_conv_out, v_q_norm_g, v_k_norm_g, v_w_attn_out, v_w_out):
    depth, d, _ = w_mod.shape
    n_lat, n_ctx = x.shape[1], ctx.shape[1]
    na = n_lat + n_ctx
    heads = d // HEAD_DIM
    kvh = heads // GROUP
    kw = d // GROUP
    ktaps = conv_w.shape[1]
    tm = n_ctx
    tm_half = tm // 2
    tbig = 3 * tm if na % (3 * tm) == 0 else tm
    tq_lat = 16 * tm if n_lat % (16 * tm) == 0 else tm
    tq_bwd = 16 * tm if n_lat % (16 * tm) == 0 else tm

    def layer_shards(l):
        return [w_mod[l].astype(BF16), w_in[l].astype(BF16), w_conv_out[l].astype(BF16), w_attn_out[l].astype(BF16),
                w_out[l].astype(BF16), conv_w[l]]

    def full_weights(gathered):
        s_mod, s_in, s_co, s_ao, s_oo, s_cw = [q[:, None] for q in gathered]
        win = _cols_from_src(s_in)[0]
        wp = jnp.concatenate([win[:, :4 * d], win[:, 4 * d + 2 * kw:], win[:, 4 * d:4 * d + 2 * kw]], axis=1)
        return dict(wmod=_cols_from_src(s_mod)[0], wp=wp, wc=_rows_from_src(s_co)[0], wa=_rows_from_src(s_ao)[0],
                    wo=_rows_from_src(s_oo)[0], convw=_cols_from_src(s_cw)[0])

    weights = [full_weights(_all_gather(layer_shards(0), "gather_weights_l0"))]

    cos_t, sin_t = _rope_tables(n_lat, n_ctx)
    lane = jnp.arange(d)
    esel = (lane[:, None] // HEAD_DIM == jnp.arange(128)[None, :]).astype(BF16)
    esel_t = esel.T
    cvec = jnp.zeros((8, d), F32).at[0].set(c[0]).at[1].set(c_ctx)
    cvec_t = jnp.zeros((d, 128), F32).at[:, 0].set(c[0]).at[:, 1].set(c_ctx)

    xa = jnp.concatenate([x[0], ctx[0]], axis=0)

    saved = []
    for l in range(depth):
        tag = f"_l{l}"
        gq = jnp.tile(q_norm_g[l], heads)[None, :]
        gk = jnp.tile(k_norm_g[l], heads // GROUP)[None, :]
        wl = weights[l]
        modv = _mod_fwd(cvec, wl["wmod"], b_mod[l][None, :], "mod_fwd" + tag)
        p, h_t, *gathered = _inproj(xa, modv, g_pre[l][None, :], wl["wp"], n_lat, tbig, "inproj" + tag,
                                    shards=layer_shards(l + 1) if l + 1 < depth else ())
        if gathered:
            weights.append(full_weights(gathered))
        q_t, kr, vb, k_t, v_t = _qknorm_fwd(p, cos_t, sin_t, gq, gk, esel, esel_t, tm, "qknorm_fwd" + tag)
        qt_hm = q_t.reshape(heads, HEAD_DIM, na)
        kt_hm = k_t.reshape(kvh, HEAD_DIM, na)
        k_hm, v_hm = _to_heads(kr), _to_heads(vb)
        vt_ones = jnp.concatenate([v_t.reshape(kvh, HEAD_DIM, na), jnp.ones((kvh, 16, na), BF16)], axis=1)
        ot_lat, lse_lat = _attn_fwd(qt_hm, k_hm, vt_ones, n_lat, tq_lat, tbig, tm, "attn_fwd" + tag)
        ot_ctx, lse_ctx = _attn_fwd(qt_hm[:, :, n_lat:], k_hm[:, n_lat:], vt_ones[:, :, n_lat:], n_ctx, tm, tm, tm,
                                    "attn_ctx_fwd" + tag)
        ot_lat, ot_ctx = ot_lat.reshape(d, n_lat), ot_ctx.reshape(d, n_ctx)
        y5, y5_t, y2 = _conv_fwd(p, wl["convw"], conv_b[l][None, :], ln_g[l][None, :], ln_b[l][None, :], n_lat, tm,
                                 "conv_fwd" + tag)
        xa_new, yc, ya, out, z_t, og_t = _merge_fwd(y5, ot_lat, ot_ctx, p, xa, modv, g_post[l][None, :], wl["wc"],
                                                    wl["wa"], wl["wo"], n_lat, tm, "merge_fwd" + tag)
        saved.append(dict(xa=xa, modv=modv, p=p, h_t=h_t, qt_hm=qt_hm, k_hm=k_hm, kt_hm=kt_hm, v_hm=v_hm,
                          ot_lat=ot_lat, ot_ctx=ot_ctx, lse_lat=lse_lat, lse_ctx=lse_ctx,
                          y5_t=y5_t, y2=y2, yc=yc, ya=ya, out=out, z_t=z_t, og_t=og_t, gq=gq, gk=gk))
        xa = xa_new

    dxa, loss_blk = _loss_grad(xa, loss_target[0], n_lat, tm, "loss_grad")

    big_names = ["w_mod", "w_in", "w_conv_out", "w_attn_out", "w_out", "conv_w"]
    g_bmod, g_gpre, g_gpost, g_convb, g_lng, g_lnb, g_qg, g_kg = [], [], [], [], [], [], [], []
    g_cctx = jnp.zeros((d,), F32)
    pending_send, pending_layer, from_chips = None, None, {}

    def add_sibling_parts(send, swapped, layer):
        return [_add_sibling_part(sb, rc, BF16, f"reduce_sibling_add_{n}_l{layer}")
                for n, sb, rc in zip(big_names, send, swapped)]

    for l in reversed(range(depth)):
        tag = f"_l{l}"
        s = saved[l]
        wl = weights[l]
        p = s["p"]
        (dgb, dma, dmb, dgta, do_t, dout, dyc, dya, dy2, dl_t, acc_m, *swapped) = _merge_bwd(
            dxa, s["out"], s["yc"], s["ya"], s["ot_lat"], s["ot_ctx"], p, s["y2"], s["modv"], g_post[l][None, :],
            ln_g[l][None, :], ln_b[l][None, :], wl["wo"].T, wl["wc"].T, wl["wa"].T, esel, n_lat, tm_half,
            "merge_bwd" + tag, rider=_swap_rider(pending_send) if pending_send else None)
        exchange = _exchange_rider(add_sibling_parts(pending_send, swapped, pending_layer)) if swapped else None
        dl_r = dl_t[:heads].reshape(heads, 1, na)
        dot_hm = do_t.reshape(heads, HEAD_DIM, na)
        qt_hm, k_hm, kt_hm, v_hm = s["qt_hm"], s["k_hm"], s["kt_hm"], s["v_hm"]
        dqt_lat, dkt_hm, dvt_hm = _attn_bwd(qt_hm, k_hm, kt_hm, v_hm, dot_hm, s["lse_lat"], dl_r,
                                            n_lat, tq_bwd, tm, tm, "attn_bwd" + tag)
        dqt_ctx, dkt_ctx, dvt_ctx = _attn_bwd(
            qt_hm[:, :, n_lat:], k_hm[:, n_lat:], kt_hm[:, :, n_lat:], v_hm[:, n_lat:], dot_hm[:, :, n_lat:],
            s["lse_ctx"], dl_r[:, :, n_lat:], n_ctx, tm, tm, tm, "attn_ctx_bwd" + tag)
        dq, dkv, acc_q = _qknorm_bwd(
            dqt_lat.reshape(d, n_lat), dqt_ctx.reshape(d, n_ctx), dkt_hm.reshape(kw, na), dkt_ctx.reshape(kw, n_ctx),
            dvt_hm.reshape(kw, na), dvt_ctx.reshape(kw, n_ctx), p, cos_t, sin_t, s["gq"], s["gk"], esel, esel_t, n_lat, tm,
            "qknorm_bwd" + tag)
        da, dg, dconvw, *exchanged = _conv_bwd(dy2, p, wl["convw"], n_lat, tm, "conv_bwd" + tag, rider=exchange)
        if exchanged:
            from_chips[pending_layer] = exchanged
        segs = [da, dg, dgta, dq, dgb, dma, dmb]
        dxa, acc_h = _inproj_bwd(segs, dkv, s["xa"], dxa, s["modv"], g_pre[l][None, :], wl["wp"].T, n_lat,
                                 na if l else n_lat, tm, "inproj_bwd" + tag)

        dwp = [_grad_matmul(s["h_t"], sg, tbig, f"grad_w_in{k}" + tag) for k, sg in enumerate(segs + [dkv])]
        g_win = jnp.concatenate(dwp[:4] + [dwp[7]] + dwp[4:7], axis=1)
        g_wc = _grad_matmul(s["y5_t"], dyc, tbig, "grad_w_conv_out" + tag)
        g_wa = _grad_matmul(s["og_t"], dya, tbig, "grad_w_attn_out" + tag)
        g_wo = _grad_matmul(s["z_t"], dout, tbig, "grad_w_out" + tag)

        dmod = jnp.zeros((8, 3 * d), F32)
        dmod = dmod.at[0].set(jnp.concatenate([acc_h[0], acc_h[2], acc_m[0]]))
        dmod = dmod.at[1].set(jnp.concatenate([acc_h[1], acc_h[3], acc_m[1]]))
        dwm, dbm, dcv = _mod_bwd(dmod, cvec, cvec_t, wl["wmod"], "mod_bwd" + tag)
        by_dest = [_cols_by_dest(dwm[None]), _cols_by_dest(g_win[None]), _rows_by_dest(g_wc[None]),
                   _rows_by_dest(g_wa[None]), _rows_by_dest(g_wo[None]), _cols_by_dest(dconvw[None])]
        pending_send, pending_layer = [q.reshape((4, 2) + q.shape[1:]) for q in by_dest], l
        g_bmod.append(dbm[0])
        g_cctx = g_cctx + dcv[1]
        g_gpre.append(acc_h[4])
        g_gpost.append(acc_m[2])
        g_lng.append(acc_m[3])
        g_lnb.append(acc_m[4])
        g_convb.append(acc_m[5])
        g_qg.append(acc_q[0].reshape(heads, HEAD_DIM).sum(0))
        g_kg.append(acc_q[1, :kw].reshape(heads // GROUP, HEAD_DIM).sum(0))

    grad_x = dxa[None]

    def stack(lst):
        return jnp.stack(lst[::-1])

    big_w = dict(w_mod=w_mod, w_in=w_in, w_conv_out=w_conv_out, w_attn_out=w_attn_out, w_out=w_out, conv_w=conv_w)
    big_m = dict(w_mod=m_w_mod, w_in=m_w_in, w_conv_out=m_w_conv_out, w_attn_out=m_w_attn_out, w_out=m_w_out,
                 conv_w=m_conv_w)
    big_v = dict(w_mod=v_w_mod, w_in=v_w_in, w_conv_out=v_w_conv_out, w_attn_out=v_w_attn_out, w_out=v_w_out,
                 conv_w=v_conv_w)
    swapped = _run_rider(_swap_rider(pending_send), "reduce_sibling")
    from_chips[pending_layer] = _run_rider(
        _exchange_rider(add_sibling_parts(pending_send, swapped, pending_layer)), "reduce_chips")
    big_g, big_d, big_nm, big_nv = {}, {}, {}, {}
    for k, n in enumerate(big_names):
        st = jnp.concatenate([from_chips[l][k] for l in range(depth)], axis=1)
        big_g[n], big_d[n], big_nm[n], big_nv[n] = _sum_adamw(st, big_w[n], big_m[n], big_v[n], "adamw_" + n)

    small_names = ["c_ctx", "b_mod", "g_pre", "g_post", "conv_b", "ln_g", "ln_b", "q_norm_g", "k_norm_g", "loss"]
    zero1 = jnp.zeros((1,), F32)
    small_w = dict(c_ctx=c_ctx, b_mod=b_mod, g_pre=g_pre, g_post=g_post, conv_b=conv_b, ln_g=ln_g, ln_b=ln_b,
                   q_norm_g=q_norm_g, k_norm_g=k_norm_g, loss=zero1)
    small_m = dict(c_ctx=m_c_ctx, b_mod=m_b_mod, g_pre=m_g_pre, g_post=m_g_post, conv_b=m_conv_b, ln_g=m_ln_g,
                   ln_b=m_ln_b, q_norm_g=m_q_norm_g, k_norm_g=m_k_norm_g, loss=zero1)
    small_v = dict(c_ctx=v_c_ctx, b_mod=v_b_mod, g_pre=v_g_pre, g_post=v_g_post, conv_b=v_conv_b, ln_g=v_ln_g,
                   ln_b=v_ln_b, q_norm_g=v_q_norm_g, k_norm_g=v_k_norm_g, loss=zero1)
    small_g = dict(c_ctx=g_cctx, b_mod=stack(g_bmod), g_pre=stack(g_gpre), g_post=stack(g_gpost),
                   conv_b=stack(g_convb), ln_g=stack(g_lng), ln_b=stack(g_lnb), q_norm_g=stack(g_qg),
                   k_norm_g=stack(g_kg), loss=loss_blk[0, 0:1])
    small_shapes = [small_w[n].shape for n in small_names]

    def pack_small(tree):
        return _pack([tree[n] for n in small_names], d, 8)

    small_parts, = _all_gather([pack_small(small_g)], "gather_small_grads")
    small_out = _sum_adamw(small_parts, pack_small(small_w), pack_small(small_m), pack_small(small_v),
                           "adamw_replicated")
    sm_g, sm_d, sm_nm, sm_nv = [dict(zip(small_names, _unpack(o.reshape(-1), small_shapes))) for o in small_out]
    loss = sm_g["loss"].reshape(())

    order = ["c_ctx", "w_mod", "b_mod", "g_pre", "g_post", "w_in", "conv_w", "conv_b", "ln_g", "ln_b",
             "w_conv_out", "q_norm_g", "k_norm_g", "w_attn_out", "w_out"]

    def pick(big, small):
        return [big[n] if n in big else small[n] for n in order]

    return (loss, grad_x, *pick(big_g, sm_g), *pick(big_d, sm_d), *pick(big_nm, sm_nm), *pick(big_nv, sm_nv))
```

```python
import math

import jax
import jax.numpy as jnp
from jax import lax
from jax.experimental import pallas as pl
from jax.experimental.pallas import tpu as pltpu

F32 = jnp.float32
BF16 = jnp.bfloat16

HEAD_DIM = 64
GROUP = 4
GRID_W = 64
ROPE_THETA = 10000.0
EPS = 1e-6
ATTN_SCALE = HEAD_DIM ** -0.5
LOG2E = 1.4426950408889634
Q_PRESCALE = ATTN_SCALE * LOG2E
HALO = 16
CONV_ROWS = 64
ATTN_FWD_AHEAD = 5
ATTN_BWD_AHEAD = 3

ADAM_LR = 0.001
ADAM_B1 = 0.9
ADAM_B2 = 0.999
ADAM_EPS = 1e-08
ADAM_WD = 0.01
ADAM_STEP = 10

N_DEV = 8
MESH_AXES = ("x", "y", "c")
V7X_VMEM_LIMIT = 56 * 1024 * 1024
NEG_BIG = -1e30

MESH = pl.DeviceIdType.MESH
ANY = pl.BlockSpec(memory_space=pl.ANY)


def _pcall(body, **kw):
    return pl.pallas_call(body, **kw)


def _cp(*sem):
    return pltpu.CompilerParams(dimension_semantics=sem, vmem_limit_bytes=V7X_VMEM_LIMIT)


def _sig(x):
    return 0.5 * jnp.tanh(0.5 * x) + 0.5


def _mean(x):
    return jnp.mean(x, axis=-1, keepdims=True)


def _colsum(x):
    return jnp.sum(x, axis=0, keepdims=True)


def _bf_round(x):
    return x.astype(BF16).astype(F32)


def _dot(a, b):
    return jnp.dot(a, b, preferred_element_type=F32)


def _dot_nt(a, b):
    return lax.dot_general(a, b, (((1,), (1,)), ((), ())), preferred_element_type=F32)


def _split_dot(x, m):
    hi = x.astype(BF16)
    lo = (x - hi.astype(F32)).astype(BF16)
    return _dot(hi, m) + _dot(lo, m)


def _head_sum(x, es, es_t):
    return _split_dot(_split_dot(x, es), es_t)


def _full(shape):
    nd = len(shape)
    return pl.BlockSpec(shape, lambda *_: (0,) * nd)


def _rows(tm, width, colblk=0):
    return pl.BlockSpec((tm, width), lambda i: (i, colblk))


def _cols(height, tm):
    return pl.BlockSpec((height, tm), lambda i: (0, i))


def _split_cols(height, tm, n_lat):
    nl = n_lat // tm
    return (pl.BlockSpec((height, tm), lambda i: (0, jnp.minimum(i, nl - 1))),
            pl.BlockSpec((height, tm), lambda i: (0, jnp.maximum(i - nl, 0))))


def _my_place():
    return lax.axis_index("x"), lax.axis_index("y"), lax.axis_index("c")


def _sem_arrays(n):
    return [pltpu.SemaphoreType.DMA((n,)), pltpu.SemaphoreType.DMA((n,))]


def _gather_steps(x_refs, out_refs, send_sems, recv_sems, local_sems):
    n = len(x_refs)
    x, y, c = _my_place()
    me, sibling = (x, y, c), (x, y, 1 - c)
    chips = [(1 - x, y), (x, 1 - y), (1 - x, 1 - y)]

    def slab(a, px, py, pc):
        return out_refs[a].at[4 * px + 2 * py + pc]

    def copies(k, block, to, from_input=False):
        return [pltpu.make_async_remote_copy(
            src_ref=x_refs[a] if from_input else slab(a, *block), dst_ref=slab(a, *block),
            send_sem=send_sems.at[k * n + a], recv_sem=recv_sems.at[k * n + a],
            device_id=to, device_id_type=MESH) for a in range(n)]

    def mine():
        return [pltpu.make_async_copy(x_refs[a], slab(a, *me), local_sems.at[a]) for a in range(n)]

    def first():
        out = copies(0, me, sibling, True)
        for j, chip in enumerate(chips):
            out += copies(1 + j, me, (*chip, c), True)
        return out

    def onward(j):
        return copies(4 + j, (*chips[j], c), sibling)

    def start():
        for cp in mine() + first():
            cp.start()

    def relay():
        for j, chip in enumerate(chips):
            for cp in copies(1 + j, (*chip, c), me):
                cp.wait_recv()
            for cp in onward(j):
                cp.start()

    def finish():
        for cp in copies(0, sibling, me):
            cp.wait_recv()
        for j, chip in enumerate(chips):
            for cp in copies(4 + j, (*chip, 1 - c), me):
                cp.wait_recv()
        for cp in first() + onward(0) + onward(1) + onward(2):
            cp.wait_send()
        for cp in mine():
            cp.wait()

    return start, relay, finish


def _gather_scratch(n):
    return _sem_arrays(7 * n) + [pltpu.SemaphoreType.DMA((n,))]


def _all_gather(shards, name):
    n = len(shards)

    def body(*refs):
        start, relay, finish = _gather_steps(refs[:n], refs[n:2 * n], *refs[2 * n:])
        start()
        relay()
        finish()

    return _pcall(
        body, name=name,
        out_shape=[jax.ShapeDtypeStruct((N_DEV,) + q.shape, q.dtype) for q in shards],
        in_specs=[ANY] * n, out_specs=[ANY] * n,
        scratch_shapes=_gather_scratch(n),
    )(*shards)


def _swap_steps(buf_refs, recv_refs, send_sems, recv_sems):
    n = len(buf_refs)
    x, y, c = _my_place()

    def copies():
        return [pltpu.make_async_remote_copy(
            src_ref=buf_refs[a].at[k, 1 - c], dst_ref=recv_refs[a].at[k],
            send_sem=send_sems.at[k * n + a], recv_sem=recv_sems.at[k * n + a],
            device_id=(x, y, 1 - c), device_id_type=MESH) for k in range(4) for a in range(n)]

    def start():
        for cp in copies():
            cp.start()

    def finish():
        for cp in copies():
            cp.wait()

    return start, finish


def _exchange_steps(s_refs, recv_refs, send_sems, recv_sems, local_sems):
    n = len(s_refs)
    x, y, c = _my_place()
    mychip = 2 * x + y
    chips = [(1 - x, y), (x, 1 - y), (1 - x, 1 - y)]

    def copies():
        mine = [pltpu.make_async_copy(s_refs[a].at[mychip], recv_refs[a].at[mychip], local_sems.at[a])
                for a in range(n)]
        return mine + [pltpu.make_async_remote_copy(
            src_ref=s_refs[a].at[2 * px + py], dst_ref=recv_refs[a].at[mychip],
            send_sem=send_sems.at[j * n + a], recv_sem=recv_sems.at[j * n + a],
            device_id=(px, py, c), device_id_type=MESH) for j, (px, py) in enumerate(chips) for a in range(n)]

    def start():
        for cp in copies():
            cp.start()

    def finish():
        for cp in copies():
            cp.wait()

    return start, finish


def _swap_rider(bufs):
    n = len(bufs)
    return dict(steps=_swap_steps, ins=list(bufs), scratch=_sem_arrays(4 * n),
                outs=[jax.ShapeDtypeStruct((4,) + q.shape[2:], q.dtype) for q in bufs])


def _exchange_rider(parts):
    n = len(parts)
    return dict(steps=_exchange_steps, ins=list(parts), scratch=_sem_arrays(3 * n) + [pltpu.SemaphoreType.DMA((n,))],
                outs=[jax.ShapeDtypeStruct(q.shape, q.dtype) for q in parts])


def _run_rider(rider, name):
    n = len(rider["ins"])

    def body(*refs):
        start, finish = rider["steps"](refs[:n], refs[n:2 * n], *refs[2 * n:])
        start()
        finish()

    return _pcall(body, name=name, out_shape=rider["outs"], in_specs=[ANY] * n, out_specs=[ANY] * n,
                  scratch_shapes=rider["scratch"])(*rider["ins"])


def _ride(body, rider, n_in, n_out, nsteps):
    if rider is None:
        return body
    n = len(rider["ins"])
    ns = len(rider["scratch"])

    def wrapped(*refs):
        ins, r_in = refs[:n_in], refs[n_in:n_in + n]
        outs, r_out = refs[n_in + n:n_in + n + n_out], refs[n_in + n + n_out:n_in + 2 * n + n_out]
        rest = refs[n_in + 2 * n + n_out:]
        scratch, r_scratch = rest[:len(rest) - ns], rest[len(rest) - ns:]
        start, finish = rider["steps"](r_in, r_out, *r_scratch)
        pl.when(pl.program_id(0) == 0)(start)
        body(*ins, *outs, *scratch)
        pl.when(pl.program_id(0) == nsteps - 1)(finish)

    return wrapped


def _row_tile(a):
    for t in range(256, 7, -8):
        if a % t == 0:
            return t
    return a


def _add_sibling_part(buf, recv, out_dtype, name):
    _, _, nl, a, b = buf.shape
    ta = _row_tile(a)
    core = lax.axis_index("c").astype(jnp.int32).reshape(1)

    def body(core_ref, a_ref, b_ref, o_ref):
        o_ref[...] = (a_ref[...] + b_ref[...]).astype(o_ref.dtype)

    grid_spec = pltpu.PrefetchScalarGridSpec(
        num_scalar_prefetch=1, grid=(4, nl, a // ta),
        in_specs=[pl.BlockSpec((None, None, None, ta, b), lambda k, l, r, cr: (k, cr[0], l, r, 0)),
                  pl.BlockSpec((None, None, ta, b), lambda k, l, r, cr: (k, l, r, 0))],
        out_specs=pl.BlockSpec((None, None, ta, b), lambda k, l, r, cr: (k, l, r, 0)))
    return _pcall(body, name=name, grid_spec=grid_spec,
                  out_shape=jax.ShapeDtypeStruct((4, nl, a, b), out_dtype),
                  compiler_params=_cp("parallel", "parallel", "parallel"))(core, buf, recv)


def _sum_adamw(stack, w, m, v, name):
    ns, nl, a, b = stack.shape
    ta = _row_tile(a)
    c1 = 1.0 - ADAM_B1 ** ADAM_STEP
    c2 = 1.0 - ADAM_B2 ** ADAM_STEP

    def body(s_ref, w_ref, m_ref, v_ref, g_out, d_out, m_out, v_out):
        g = s_ref[0].astype(F32)
        for k in range(1, ns):
            g = g + s_ref[k].astype(F32)
        m_new = ADAM_B1 * m_ref[...] + (1.0 - ADAM_B1) * g
        v_new = ADAM_B2 * v_ref[...] + (1.0 - ADAM_B2) * (g * g)
        m_hat = m_new / c1
        v_hat = v_new / c2
        g_out[...] = g
        d_out[...] = -ADAM_LR * (m_hat / (jnp.sqrt(v_hat) + ADAM_EPS) + ADAM_WD * w_ref[...])
        m_out[...] = m_new
        v_out[...] = v_new

    blk = pl.BlockSpec((None, ta, b), lambda l, i: (l, i, 0))
    return _pcall(
        body, name=name, grid=(nl, a // ta),
        in_specs=[pl.BlockSpec((ns, None, ta, b), lambda l, i: (0, l, i, 0)), blk, blk, blk],
        out_specs=[blk] * 4,
        out_shape=[jax.ShapeDtypeStruct((nl, a, b), F32)] * 4,
        compiler_params=_cp("parallel", "parallel"))(stack, w, m, v)


def _mod_fwd(cvec, wmod, bmod, name):
    _, d = cvec.shape

    def body(c_ref, w_ref, b_ref, o_ref):
        cv = c_ref[...]
        cs = cv * _sig(cv)
        o_ref[...] = _dot(cs.astype(BF16), w_ref[...]) + b_ref[...]

    return _pcall(
        body, name=name, grid=(3,),
        in_specs=[_full((8, d)), pl.BlockSpec((d, d), lambda n: (0, n)), pl.BlockSpec((1, d), lambda n: (0, n))],
        out_specs=pl.BlockSpec((8, d), lambda n: (0, n)),
        out_shape=jax.ShapeDtypeStruct((8, 3 * d), F32),
        compiler_params=_cp("parallel"))(cvec, wmod, bmod)


def _mod_bwd(dmod, cvec, cvec_t, wmod, name):
    _, d = cvec.shape

    def body(dm_ref, c_ref, ct_ref, w_ref, dw_ref, db_ref, dc_ref):
        n = pl.program_id(0)
        dm = dm_ref[...]
        ct = ct_ref[...]
        cs_t = _bf_round(ct * _sig(ct))
        d0 = _bf_round(dm[0:1, :])
        d1 = _bf_round(dm[1:2, :])
        dw_ref[...] = cs_t[:, 0:1] * d0 + cs_t[:, 1:2] * d1
        db_ref[...] = dm[0:1, :] + dm[1:2, :]

        @pl.when(n == 0)
        def _():
            dc_ref[...] = jnp.zeros_like(dc_ref)

        dc_ref[...] += _dot_nt(dm.astype(BF16), w_ref[...])

        @pl.when(n == 2)
        def _():
            cv = c_ref[...]
            s = _sig(cv)
            dc_ref[...] = dc_ref[...] * (s * (1.0 + cv * (1.0 - s)))

    return _pcall(
        body, name=name, grid=(3,),
        in_specs=[pl.BlockSpec((8, d), lambda n: (0, n)), _full((8, d)), _full((d, 128)),
                  pl.BlockSpec((d, d), lambda n: (0, n))],
        out_specs=[pl.BlockSpec((d, d), lambda n: (0, n)), pl.BlockSpec((1, d), lambda n: (0, n)),
                   _full((8, d))],
        out_shape=[jax.ShapeDtypeStruct((d, 3 * d), F32), jax.ShapeDtypeStruct((1, 3 * d), F32),
                   jax.ShapeDtypeStruct((8, d), F32)],
        compiler_params=_cp("arbitrary"))(dmod, cvec, cvec_t, wmod)


def _seg_rows(mod_ref, lo, hi, is_ctx):
    return jnp.where(is_ctx, mod_ref[1:2, lo:hi], mod_ref[0:1, lo:hi])


def _inproj(xa, modv, gpre, wp, n_lat, tm, name, shards=()):
    na, d = xa.shape
    wcols = wp.shape[1]
    tn = 5 * d // 4 if (5 * d // 4) % 128 == 0 and wcols % (5 * d // 4) == 0 else d // 2
    ns = len(shards)
    ni, nj = na // tm, wcols // tn

    def body(x_ref, mod_ref, g_ref, w_ref, *rest):
        p_ref, ht_ref = rest[ns:ns + 2]
        h_s = rest[2 * ns + 2]
        i = pl.program_id(0)
        j = pl.program_id(1)

        if ns:
            start, relay, finish = _gather_steps(rest[:ns], rest[ns + 2:2 * ns + 2], *rest[2 * ns + 3:])
            pl.when(jnp.logical_and(i == 0, j == 0))(start)
            pl.when(jnp.logical_and(i == max(ni - 2, 0), j == 0))(relay)

        @pl.when(j == 0)
        def _():
            x = x_ref[...]
            r = lax.rsqrt(_mean(x * x) + EPS)
            row = i * tm + lax.broadcasted_iota(jnp.int32, (tm, 1), 0)
            is_ctx = row >= n_lat
            sh = _seg_rows(mod_ref, 0, d, is_ctx)
            sc = _seg_rows(mod_ref, d, 2 * d, is_ctx)
            h = (x * r * g_ref[...]) * (1.0 + sc) + sh
            h_s[...] = h.astype(BF16)
            ht_ref[...] = h.T.astype(BF16)

        p_ref[...] = _dot(h_s[...], w_ref[...])

        if ns:
            pl.when(jnp.logical_and(i == ni - 1, j == nj - 1))(finish)

    return _pcall(
        body, name=name, grid=(ni, nj),
        in_specs=[pl.BlockSpec((tm, d), lambda i, j: (i, 0)), _full((8, 3 * d)), _full((1, d)),
                  pl.BlockSpec((d, tn), lambda i, j: (0, j))] + [ANY] * ns,
        out_specs=[pl.BlockSpec((tm, tn), lambda i, j: (i, j)), pl.BlockSpec((d, tm), lambda i, j: (0, i))]
        + [ANY] * ns,
        out_shape=[jax.ShapeDtypeStruct((na, wcols), F32), jax.ShapeDtypeStruct((d, na), BF16)]
        + [jax.ShapeDtypeStruct((N_DEV,) + q.shape, q.dtype) for q in shards],
        scratch_shapes=[pltpu.VMEM((tm, d), BF16)] + (_gather_scratch(ns) if ns else []),
        compiler_params=_cp("arbitrary" if ns else "parallel", "arbitrary"))(xa, modv, gpre, wp, *shards)


def _lane_tile(t, width):
    if width >= 128:
        return jnp.tile(t, (1, width // 128))
    return t[:, :width]


def _partner(x):
    w = x.shape[-1]
    lane = lax.broadcasted_iota(jnp.int32, x.shape, 1)
    low = (lane % HEAD_DIM) < (HEAD_DIM // 2)
    return jnp.where(low, pltpu.roll(x, w - HEAD_DIM // 2, 1), pltpu.roll(x, HEAD_DIM // 2, 1))


def _qknorm_fwd(p, cos_t, sin_t, gq, gk, esel, esel_t, tm, name):
    na = p.shape[0]
    d = gq.shape[1]
    kw = d // GROUP

    def body(q_ref, kv_ref, cos_ref, sin_ref, gq_ref, gk_ref, es_ref, est_ref, qt_ref, ko_ref, vo_ref, kt_ref,
             vt_ref):
        cos = cos_ref[...]
        sin = sin_ref[...]

        def norm_rope(xh, g, w):
            ms = _head_sum(xh * xh, es_ref[0:w, :], est_ref[:, 0:w]) * (1.0 / HEAD_DIM)
            xn = xh * lax.rsqrt(ms + EPS) * g
            return xn * _lane_tile(cos, w) + _partner(xn) * _lane_tile(sin, w)

        qt_ref[...] = (norm_rope(q_ref[...], gq_ref[...], d) * Q_PRESCALE).T.astype(BF16)
        kv = kv_ref[...]
        k = norm_rope(kv[:, 0:kw], gk_ref[...], kw)
        v = kv[:, kw:2 * kw]
        ko_ref[...] = k.astype(BF16)
        vo_ref[...] = v.astype(BF16)
        kt_ref[...] = k.T.astype(BF16)
        vt_ref[...] = v.T.astype(BF16)

    cols = lambda w: pl.BlockSpec((w, tm), lambda i: (0, i))
    return _pcall(
        body, name=name, grid=(na // tm,),
        in_specs=[_rows(tm, d, 3), _rows(tm, d // 2, 14), _rows(tm, 128), _rows(tm, 128),
                  _full((1, d)), _full((1, kw)), _full((d, 128)), _full((128, d))],
        out_specs=[cols(d), _rows(tm, kw), _rows(tm, kw), cols(kw), cols(kw)],
        out_shape=[jax.ShapeDtypeStruct((d, na), BF16), jax.ShapeDtypeStruct((na, kw), BF16),
                   jax.ShapeDtypeStruct((na, kw), BF16), jax.ShapeDtypeStruct((kw, na), BF16),
                   jax.ShapeDtypeStruct((kw, na), BF16)],
        compiler_params=_cp("parallel"))(p, p, cos_t, sin_t, gq, gk, esel, esel_t)


def _attn_fwd(qt_hm, k_hm, vt_hm, na, tq, tk, cw, name):
    h, hd, _ = qt_hm.shape
    kv, nkeys, _ = k_hm.shape
    vrows = vt_hm.shape[1]
    nq, nk = na // tq, nkeys // tk
    nsub = tq // cw
    chains = [(a, u) for a in range(GROUP) for u in range(nsub)]

    def body(q_ref, k_ref, vt_ref, o_ref, lse_ref, m_s, acc_s):
        j = pl.program_id(2)

        @pl.when(j == 0)
        def _():
            m_s[...] = jnp.full_like(m_s, NEG_BIG)
            acc_s[...] = jnp.zeros_like(acc_s)

        k = k_ref[...]
        vt = vt_ref[...]

        def scores(n):
            a, u = chains[n]
            return _dot(k, q_ref[a, :, u * cw:(u + 1) * cw])

        s_tiles = {n: scores(n) for n in range(min(ATTN_FWD_AHEAD, len(chains)))}
        pending = None
        for n, (a, u) in enumerate(chains):
            cols = slice(u * cw, (u + 1) * cw)
            s_t = s_tiles.pop(n)
            m_prev = m_s[a, :, cols]
            m_new = jnp.maximum(m_prev, jnp.max(s_t, axis=0, keepdims=True))
            m_s[a, :, cols] = m_new
            pv = _dot(vt, jnp.exp2(s_t - m_new).astype(BF16))
            if pending is not None:
                pa, pcols, palpha, ppv = pending
                acc_s[pa, :, pcols] = palpha * acc_s[pa, :, pcols] + ppv
            pending = (a, cols, jnp.exp2(m_prev - m_new), pv)
            if n + ATTN_FWD_AHEAD < len(chains):
                s_tiles[n + ATTN_FWD_AHEAD] = scores(n + ATTN_FWD_AHEAD)
        pa, pcols, palpha, ppv = pending
        acc_s[pa, :, pcols] = palpha * acc_s[pa, :, pcols] + ppv

        @pl.when(j == nk - 1)
        def _():
            for a in range(GROUP):
                acc = acc_s[a]
                l = acc[hd:hd + 1, :]
                o_ref[a] = acc[0:hd, :] / l
                lse_ref[a] = m_s[a] + jnp.log2(l)

    return _pcall(
        body, name=name, grid=(kv, nq, nk),
        in_specs=[pl.BlockSpec((GROUP, hd, tq), lambda g, i, j: (g, 0, i)),
                  pl.BlockSpec((None, tk, hd), lambda g, i, j: (g, j, 0)),
                  pl.BlockSpec((None, vrows, tk), lambda g, i, j: (g, 0, j))],
        out_specs=[pl.BlockSpec((GROUP, hd, tq), lambda g, i, j: (g, 0, i)),
                   pl.BlockSpec((GROUP, 1, tq), lambda g, i, j: (g, 0, i))],
        out_shape=[jax.ShapeDtypeStruct((h, hd, na), F32), jax.ShapeDtypeStruct((h, 1, na), F32)],
        scratch_shapes=[pltpu.VMEM((GROUP, 1, tq), F32), pltpu.VMEM((GROUP, vrows, tq), F32)],
        compiler_params=_cp("parallel", "parallel", "arbitrary"))(qt_hm, k_hm, vt_hm)


def _window(win_ref, prev, cur, nxt, first, last, tm):
    win_ref[0:HALO, :] = jnp.where(first, 0.0, prev)
    win_ref[HALO:HALO + tm, :] = cur
    win_ref[HALO + tm:HALO + tm + HALO, :] = jnp.where(last, 0.0, nxt)


def _lane_blocks(d):
    return [slice(b, b + 128) for b in range(0, d, 128)]


def _sublane_shifts(slab):
    n = slab.shape[0]
    for b in range(8):
        sh = slab if b == 0 else pltpu.roll(slab, n - b, 0)
        for a8 in range(0, 2 * HALO, 8):
            yield a8 + b, sh[a8:a8 + CONV_ROWS, :]


def _halo_specs(tm, d, na, colblk):
    per = tm // HALO
    last_blk = na // HALO - 1
    prev = pl.BlockSpec((HALO, d), lambda i: (jnp.maximum(i * per - 1, 0), colblk))
    nxt = pl.BlockSpec((HALO, d), lambda i: (jnp.minimum((i + 1) * per, last_blk), colblk))
    return prev, nxt


def _seq_ends(i, n_lat, na, tm):
    first = jnp.logical_or(i == 0, i == n_lat // tm)
    last = jnp.logical_or(i == n_lat // tm - 1, i == na // tm - 1)
    return first, last


def _conv_fwd(p, conv_w, conv_b, ln_g, ln_b, n_lat, tm, name):
    na = p.shape[0]
    ktaps, d = conv_w.shape
    pad = ktaps // 2

    def body(a_ref, ap_ref, an_ref, g_ref, gp_ref, gn_ref, ga_ref, w_ref, cb_ref, lg_ref, lb_ref,
             y5_ref, y5t_ref, y2_ref, win):
        i = pl.program_id(0)
        first, last = _seq_ends(i, n_lat, na, tm)
        _window(win, ap_ref[...] * _sig(gp_ref[...]), a_ref[...] * _sig(g_ref[...]),
                an_ref[...] * _sig(gn_ref[...]), first, last, tm)

        def chunk(c, carry):
            r = pl.multiple_of(c * CONV_ROWS, CONV_ROWS)
            for lanes in _lane_blocks(d):
                acc = jnp.zeros((CONV_ROWS, 128), F32)
                for off, sh in _sublane_shifts(win[pl.ds(r, CONV_ROWS + 2 * HALO), lanes]):
                    k = off - (HALO - pad)
                    if 0 <= k < ktaps:
                        acc = acc + w_ref[k:k + 1, lanes] * sh
                y2_ref[pl.ds(r, CONV_ROWS), lanes] = acc + cb_ref[:, lanes]
            return carry

        lax.fori_loop(0, tm // CONV_ROWS, chunk, 0)
        y2 = y2_ref[...]
        xc = y2 - _mean(y2)
        y3 = xc * lax.rsqrt(_mean(xc * xc) + EPS) * lg_ref[...] + lb_ref[...]
        gate = ga_ref[...]
        y5 = (y3 * _sig(y3)) * (gate * _sig(gate))
        y5_ref[...] = y5.astype(BF16)
        y5t_ref[...] = y5.T.astype(BF16)

    ap, an = _halo_specs(tm, d, na, 0)
    gp, gn = _halo_specs(tm, d, na, 1)
    return _pcall(
        body, name=name, grid=(na // tm,),
        in_specs=[_rows(tm, d, 0), ap, an, _rows(tm, d, 1), gp, gn, _rows(tm, d, 2),
                  _full((ktaps, d)), _full((1, d)), _full((1, d)), _full((1, d))],
        out_specs=[_rows(tm, d), _cols(d, tm), _rows(tm, d)],
        out_shape=[jax.ShapeDtypeStruct((na, d), BF16), jax.ShapeDtypeStruct((d, na), BF16),
                   jax.ShapeDtypeStruct((na, d), F32)],
        scratch_shapes=[pltpu.VMEM((tm + 2 * HALO, d), F32)],
        compiler_params=_cp("parallel"))(p, p, p, p, p, p, p, conv_w, conv_b, ln_g, ln_b)


def _merge_fwd(y5, ot_lat, ot_ctx, p, xa, modv, gpost, wc, wa, wo, n_lat, tm, name, target=None):
    na, d = xa.shape

    def body(y5_ref, ol_ref, oc_ref, gb_ref, ma_ref, mb_ref, x_ref, mod_ref, gp_ref, wc_ref, wa_ref, wo_ref, *rest):
        if target is None:
            xn_ref, yc_ref, ya_ref, out_ref, zt_ref, ogt_ref = rest
        else:
            t_ref, xn_ref, yc_ref, ya_ref, out_ref, zt_ref, ogt_ref, loss_ref = rest
        is_ctx = pl.program_id(0) >= n_lat // tm
        gate_b = gb_ref[...]
        o = jnp.where(is_ctx, oc_ref[...], ol_ref[...]).T
        og = o * (gate_b * _sig(gate_b))
        ogt_ref[...] = og.T.astype(BF16)
        yc = _dot(y5_ref[...], wc_ref[...])
        ya = _dot(og.astype(BF16), wa_ref[...])
        yc_ref[...] = yc
        ya_ref[...] = ya
        z = _sig(ma_ref[...]) * yc + _sig(mb_ref[...]) * ya
        zt_ref[...] = z.T.astype(BF16)
        out = _dot(z.astype(BF16), wo_ref[...])
        out_ref[...] = out
        gt = _seg_rows(mod_ref, 2 * d, 3 * d, is_ctx)
        xn = x_ref[...] + gt * (out * lax.rsqrt(_mean(out * out) + EPS) * gp_ref[...])
        if target is None:
            xn_ref[...] = xn
        else:
            @pl.when(pl.program_id(0) == 0)
            def _():
                loss_ref[...] = jnp.zeros_like(loss_ref)

            err = jnp.where(is_ctx, 0.0, xn - t_ref[...])
            xn_ref[...] = err * (1.0 / d)
            loss_ref[...] += 0.5 * jnp.sum(_mean(err * err))

    f32o = jax.ShapeDtypeStruct((na, d), F32)
    bft = jax.ShapeDtypeStruct((d, na), BF16)
    ol, oc = _split_cols(d, tm, n_lat)
    nlt = n_lat // tm
    with_loss = target is not None
    return _pcall(
        body, name=name, grid=(na // tm,),
        in_specs=[_rows(tm, d), ol, oc, _rows(tm, d, 4), _rows(tm, d, 5), _rows(tm, d, 6), _rows(tm, d),
                  _full((8, 3 * d)), _full((1, d)), _full((d, d)), _full((d, d)), _full((d, d))]
        + ([pl.BlockSpec((tm, d), lambda i: (jnp.minimum(i, nlt - 1), 0))] if with_loss else []),
        out_specs=[_rows(tm, d)] * 4 + [_cols(d, tm)] * 2 + ([_full((8, 128))] if with_loss else []),
        out_shape=[f32o, f32o, f32o, f32o, bft, bft] + ([jax.ShapeDtypeStruct((8, 128), F32)] if with_loss else []),
        compiler_params=_cp("arbitrary" if with_loss else "parallel"),
    )(y5, ot_lat, ot_ctx, p, p, p, xa, modv, gpost, wc, wa, wo, *([target] if with_loss else []))


def _loss_grad(xa, target, n_lat, tm, name):
    na, d = xa.shape
    nlt = n_lat // tm

    def body(x_ref, t_ref, dx_ref, loss_ref):
        i = pl.program_id(0)

        @pl.when(i == 0)
        def _():
            loss_ref[...] = jnp.zeros_like(loss_ref)

        @pl.when(i < nlt)
        def _():
            err = x_ref[...] - t_ref[...]
            dx_ref[...] = err * (1.0 / d)
            loss_ref[...] += 0.5 * jnp.sum(_mean(err * err))

        @pl.when(i >= nlt)
        def _():
            dx_ref[...] = jnp.zeros_like(dx_ref)

    return _pcall(
        body, name=name, grid=(na // tm,),
        in_specs=[_rows(tm, d), pl.BlockSpec((tm, d), lambda i: (jnp.minimum(i, nlt - 1), 0))],
        out_specs=[_rows(tm, d), _full((8, 128))],
        out_shape=[jax.ShapeDtypeStruct((na, d), F32), jax.ShapeDtypeStruct((8, 128), F32)],
        compiler_params=_cp("arbitrary"))(xa, target)


def _merge_bwd(dxn, out, yc, ya, ot_lat, ot_ctx, p, y2, modv, gpost, ln_g, ln_b, wo_t, wc_t, wa_t, esel, n_lat, tm,
               name, rider=None):
    na, d = dxn.shape

    def body(dx_ref, out_ref, yc_ref, ya_ref, ol_ref, oc_ref, gb_ref, ma_ref, mb_ref, gta_ref, y2_ref,
             mod_ref, gp_ref, lg_ref, lb_ref, wot_ref, wct_ref, wat_ref, es_ref,
             dgb_ref, dma_ref, dmb_ref, dgta_ref, dot_ref, dout_ref, dyc_ref, dya_ref, dy2_ref, dlt_ref, acc_ref):
        i = pl.program_id(0)
        ctx_tile = i >= n_lat // tm

        @pl.when(i == 0)
        def _():
            acc_ref[...] = jnp.zeros_like(acc_ref)

        gt = _seg_rows(mod_ref, 2 * d, 3 * d, ctx_tile)
        gp = gp_ref[...]
        dx = dx_ref[...]
        out = out_ref[...]
        r2 = lax.rsqrt(_mean(out * out) + EPS)
        n2 = out * r2
        dgt = _colsum(dx * (n2 * gp))

        @pl.when(ctx_tile)
        def _():
            acc_ref[1:2, :] += dgt

        @pl.when(jnp.logical_not(ctx_tile))
        def _():
            acc_ref[0:1, :] += dgt

        acc_ref[2:3, :] += _colsum(dx * gt * n2)
        dn2 = dx * gt * gp
        dout = (r2 * (dn2 - n2 * _mean(dn2 * n2))).astype(BF16)
        dout_ref[...] = dout
        dz = _dot(dout, wot_ref[...])
        sa = _sig(ma_ref[...])
        sb = _sig(mb_ref[...])
        dyc = (dz * sa).astype(BF16)
        dya = (dz * sb).astype(BF16)
        dyc_ref[...] = dyc
        dya_ref[...] = dya
        dma_ref[...] = (dz * yc_ref[...] * sa * (1.0 - sa)).astype(BF16)
        dmb_ref[...] = (dz * ya_ref[...] * sb * (1.0 - sb)).astype(BF16)
        dy5 = _dot(dyc, wct_ref[...])
        dog = _dot(dya, wat_ref[...])

        gate_b = gb_ref[...]
        sgb = _sig(gate_b)
        o = jnp.where(ctx_tile, oc_ref[...], ol_ref[...]).T
        do = dog * (gate_b * sgb)
        dot_ref[...] = do.T.astype(BF16)
        dgb_ref[...] = (dog * o * (sgb * (1.0 + gate_b * (1.0 - sgb)))).astype(BF16)
        dlt_ref[...] = _split_dot(do * o, es_ref[...]).T

        y2 = y2_ref[...]
        xc = y2 - _mean(y2)
        rstd = lax.rsqrt(_mean(xc * xc) + EPS)
        xhat = xc * rstd
        lg = lg_ref[...]
        y3 = xhat * lg + lb_ref[...]
        s3 = _sig(y3)
        gate_a = gta_ref[...]
        sga = _sig(gate_a)
        dgta_ref[...] = (dy5 * (y3 * s3) * (sga * (1.0 + gate_a * (1.0 - sga)))).astype(BF16)
        dy3 = dy5 * (gate_a * sga) * (s3 * (1.0 + y3 * (1.0 - s3)))
        acc_ref[3:4, :] += _colsum(dy3 * xhat)
        acc_ref[4:5, :] += _colsum(dy3)
        dxh = dy3 * lg
        dy2 = rstd * (dxh - _mean(dxh) - xhat * _mean(dxh * xhat))
        dy2_ref[...] = dy2
        acc_ref[5:6, :] += _colsum(dy2)

    f32o = jax.ShapeDtypeStruct((na, d), F32)
    bfo = jax.ShapeDtypeStruct((na, d), BF16)
    r = _rows(tm, d)
    ol, oc = _split_cols(d, tm, n_lat)
    ins = (dxn, out, yc, ya, ot_lat, ot_ctx, p, p, p, p, y2, modv, gpost, ln_g, ln_b, wo_t, wc_t, wa_t, esel)
    extra = rider or dict(ins=[], outs=[], scratch=[])
    return _pcall(
        _ride(body, rider, len(ins), 11, na // tm), name=name, grid=(na // tm,),
        in_specs=[r, r, r, r, ol, oc, _rows(tm, d, 4), _rows(tm, d, 5), _rows(tm, d, 6), _rows(tm, d, 2), r,
                  _full((8, 3 * d)), _full((1, d)), _full((1, d)), _full((1, d)),
                  _full((d, d)), _full((d, d)), _full((d, d)), _full((d, 128))] + [ANY] * len(extra["ins"]),
        out_specs=[r] * 4 + [_cols(d, tm)] + [r] * 4 + [_cols(128, tm), _full((8, d))] + [ANY] * len(extra["outs"]),
        out_shape=[bfo] * 4 + [jax.ShapeDtypeStruct((d, na), BF16)] + [bfo] * 3
        + [f32o, jax.ShapeDtypeStruct((128, na), F32), jax.ShapeDtypeStruct((8, d), F32)] + extra["outs"],
        scratch_shapes=extra["scratch"],
        compiler_params=_cp("arbitrary"),
    )(*ins, *extra["ins"])


def _attn_bwd(qt_hm, k_hm, kt_hm, v_hm, dot_hm, lse_r, dl_r, na, tq, tk, cw, name):
    h, hd, _ = qt_hm.shape
    kv, nkeys, _ = k_hm.shape
    nq, nk = na // tq, nkeys // tk
    nsub = tq // cw
    chains = [(a, u) for a in range(GROUP) for u in range(nsub)]

    def body(qt_ref, k_ref, kt_ref, v_ref, dot_ref, lse_ref, dl_ref, dq_hbm, dk_ref, dv_ref,
             dq_acc, dk_acc, dv_acc, sem):
        g = pl.program_id(0)
        j = pl.program_id(1)
        i = pl.program_id(2)

        @pl.when(jnp.logical_and(j == 0, i == 0))
        def _():
            dq_acc[...] = jnp.zeros_like(dq_acc)

        @pl.when(i == 0)
        def _():
            dk_acc[...] = jnp.zeros_like(dk_acc)
            dv_acc[...] = jnp.zeros_like(dv_acc)

        k = k_ref[...]
        v = v_ref[...]
        kt = kt_ref[...]

        def products(n):
            a, u = chains[n]
            cols = slice(u * cw, (u + 1) * cw)
            return _dot(k, qt_ref[a, :, cols]), _dot(v, dot_ref[a, :, cols])

        def accumulate(done):
            a, u, dv_t, dk_t, dq_t = done
            dv_acc[...] += dv_t
            dk_acc[...] += dk_t
            at = pl.multiple_of(i * tq + u * cw, cw)
            dq_acc[a, :, pl.ds(at, cw)] += dq_t

        tiles = {n: products(n) for n in range(min(ATTN_BWD_AHEAD, len(chains)))}
        pending = None
        for n, (a, u) in enumerate(chains):
            cols = slice(u * cw, (u + 1) * cw)
            s_t, dp_t = tiles.pop(n)
            p_t = jnp.exp2(s_t - lse_ref[a, :, cols])
            ds_b = (p_t * (dp_t - dl_ref[a, :, cols])).astype(BF16)
            p_b = p_t.astype(BF16)
            dv_t = _dot_nt(dot_ref[a, :, cols], p_b)
            dk_t = _dot_nt(qt_ref[a, :, cols], ds_b)
            dq_t = _dot(kt, ds_b)
            if pending is not None:
                accumulate(pending)
            pending = (a, u, dv_t, dk_t, dq_t)
            if n + ATTN_BWD_AHEAD < len(chains):
                tiles[n + ATTN_BWD_AHEAD] = products(n + ATTN_BWD_AHEAD)
        accumulate(pending)

        @pl.when(i == nq - 1)
        def _():
            dk_ref[...] = dk_acc[...]
            dv_ref[...] = dv_acc[...]

        @pl.when(jnp.logical_and(j == nk - 1, i == nq - 1))
        def _():
            cp = pltpu.make_async_copy(dq_acc, dq_hbm.at[pl.ds(g * GROUP, GROUP)], sem)
            cp.start()
            cp.wait()

    qtspec = pl.BlockSpec((GROUP, hd, tq), lambda g, j, i: (g, 0, i))
    kspec = pl.BlockSpec((None, tk, hd), lambda g, j, i: (g, j, 0))
    ktspec = pl.BlockSpec((None, hd, tk), lambda g, j, i: (g, 0, j))
    rspec = pl.BlockSpec((GROUP, 1, tq), lambda g, j, i: (g, 0, i))
    return _pcall(
        body, name=name, grid=(kv, nk, nq),
        in_specs=[qtspec, kspec, ktspec, kspec, qtspec, rspec, rspec],
        out_specs=[ANY, ktspec, ktspec],
        out_shape=[jax.ShapeDtypeStruct((h, hd, na), F32), jax.ShapeDtypeStruct((kv, hd, nkeys), F32),
                   jax.ShapeDtypeStruct((kv, hd, nkeys), F32)],
        scratch_shapes=[pltpu.VMEM((GROUP, hd, na), F32), pltpu.VMEM((hd, tk), F32), pltpu.VMEM((hd, tk), F32),
                        pltpu.SemaphoreType.DMA],
        compiler_params=_cp("arbitrary", "arbitrary", "arbitrary"),
    )(qt_hm, k_hm, kt_hm, v_hm, dot_hm, lse_r, dl_r)


def _qknorm_bwd(dqt_lat, dqt_ctx, dkt, dkt_ctx, dvt, dvt_ctx, p, cos_t, sin_t, gq, gk, esel, esel_t, n_lat, tm, name):
    na = p.shape[0]
    d = gq.shape[1]
    kw = d // GROUP

    def body(dql_ref, dqc_ref, dkl_ref, dkc_ref, dvl_ref, dvc_ref, q_ref, kv_ref, cos_ref, sin_ref, gq_ref, gk_ref,
             es_ref, est_ref, dqo_ref, dkvo_ref, acc_ref):
        is_ctx = pl.program_id(0) >= n_lat // tm

        @pl.when(pl.program_id(0) == 0)
        def _():
            acc_ref[...] = jnp.zeros_like(acc_ref)

        dq_in = jnp.where(is_ctx, dqc_ref[...], dql_ref[...]).T
        dk_in = (dkl_ref[...] + jnp.where(is_ctx, dkc_ref[...], 0.0)).T
        dv_in = (dvl_ref[...] + jnp.where(is_ctx, dvc_ref[...], 0.0)).T

        cos = cos_ref[...]
        sin = sin_ref[...]

        def back(dy, xh, g, w):
            es, es_t = es_ref[0:w, :], est_ref[:, 0:w]
            dn = dy * _lane_tile(cos, w) - _partner(dy) * _lane_tile(sin, w)
            rs = lax.rsqrt(_head_sum(xh * xh, es, es_t) * (1.0 / HEAD_DIM) + EPS)
            y = xh * rs
            dg = _colsum(dn * y)
            dyn = dn * g
            dx = rs * (dyn - y * (_head_sum(dyn * y, es, es_t) * (1.0 / HEAD_DIM)))
            return dx, dg

        dq, dgq = back(dq_in * ATTN_SCALE, q_ref[...], gq_ref[...], d)
        dqo_ref[...] = dq.astype(BF16)
        acc_ref[0:1, :] += dgq
        kv = kv_ref[...]
        dk, dgk = back(dk_in * (1.0 / LOG2E), kv[:, 0:kw], gk_ref[...], kw)
        acc_ref[1:2, 0:kw] += dgk
        dkvo_ref[:, 0:kw] = dk.astype(BF16)
        dkvo_ref[:, kw:2 * kw] = dv_in.astype(BF16)

    dql, dqc = _split_cols(d, tm, n_lat)
    _, kvc = _split_cols(kw, tm, n_lat)
    return _pcall(
        body, name=name, grid=(na // tm,),
        in_specs=[dql, dqc, _cols(kw, tm), kvc, _cols(kw, tm), kvc, _rows(tm, d, 3), _rows(tm, d // 2, 14),
                  _rows(tm, 128), _rows(tm, 128), _full((1, d)), _full((1, kw)), _full((d, 128)), _full((128, d))],
        out_specs=[_rows(tm, d), _rows(tm, d // 2), _full((8, d))],
        out_shape=[jax.ShapeDtypeStruct((na, d), BF16), jax.ShapeDtypeStruct((na, d // 2), BF16),
                   jax.ShapeDtypeStruct((8, d), F32)],
        compiler_params=_cp("arbitrary"))(dqt_lat, dqt_ctx, dkt, dkt_ctx, dvt, dvt_ctx, p, p, cos_t, sin_t, gq, gk,
                                          esel, esel_t)


def _conv_bwd(dy2, p, conv_w, n_lat, tm, name, rider=None):
    na = p.shape[0]
    ktaps, d = conv_w.shape
    pad = ktaps // 2

    def body(dy_ref, dyp_ref, dyn_ref, a_ref, ap_ref, an_ref, g_ref, gp_ref, gn_ref, w_ref,
             da_ref, dg_ref, dw_ref, dwin, ywin, dy1_s, part):
        i = pl.program_id(0)

        @pl.when(i == 0)
        def _():
            part[...] = jnp.zeros_like(part)

        first, last = _seq_ends(i, n_lat, na, tm)
        a = a_ref[...]
        sg = _sig(g_ref[...])
        _window(dwin, dyp_ref[...], dy_ref[...], dyn_ref[...], first, last, tm)
        _window(ywin, ap_ref[...] * _sig(gp_ref[...]), a * sg, an_ref[...] * _sig(gn_ref[...]), first, last, tm)

        def chunk(c, carry):
            r = pl.multiple_of(c * CONV_ROWS, CONV_ROWS)
            for lanes in _lane_blocks(d):
                dy = dy_ref[pl.ds(r, CONV_ROWS), lanes]
                acc = jnp.zeros((CONV_ROWS, 128), F32)
                for off, sh in _sublane_shifts(dwin[pl.ds(r, CONV_ROWS + 2 * HALO), lanes]):
                    k = HALO + pad - off
                    if 0 <= k < ktaps:
                        acc = acc + w_ref[k:k + 1, lanes] * sh
                dy1_s[pl.ds(r, CONV_ROWS), lanes] = acc
                for off, sh in _sublane_shifts(ywin[pl.ds(r, CONV_ROWS + 2 * HALO), lanes]):
                    k = off - (HALO - pad)
                    if 0 <= k < ktaps:
                        part[k, :, lanes] += jnp.sum((dy * sh).reshape(CONV_ROWS // 8, 8, 128), axis=0)
            return carry

        lax.fori_loop(0, tm // CONV_ROWS, chunk, 0)
        dy1 = dy1_s[...]
        da_ref[...] = (dy1 * sg).astype(BF16)
        dg_ref[...] = (dy1 * a * sg * (1.0 - sg)).astype(BF16)

        @pl.when(i == na // tm - 1)
        def _():
            dw_ref[...] = jnp.sum(part[...], axis=1)

    dyp, dyn = _halo_specs(tm, d, na, 0)
    ap, an = _halo_specs(tm, d, na, 0)
    gp, gn = _halo_specs(tm, d, na, 1)
    bfo = jax.ShapeDtypeStruct((na, d), BF16)
    ins = (dy2, dy2, dy2, p, p, p, p, p, p, conv_w)
    extra = rider or dict(ins=[], outs=[], scratch=[])
    return _pcall(
        _ride(body, rider, len(ins), 3, na // tm), name=name, grid=(na // tm,),
        in_specs=[_rows(tm, d), dyp, dyn, _rows(tm, d, 0), ap, an, _rows(tm, d, 1), gp, gn, _full((ktaps, d))]
        + [ANY] * len(extra["ins"]),
        out_specs=[_rows(tm, d), _rows(tm, d), _full((ktaps, d))] + [ANY] * len(extra["outs"]),
        out_shape=[bfo, bfo, jax.ShapeDtypeStruct((ktaps, d), F32)] + extra["outs"],
        scratch_shapes=[pltpu.VMEM((tm + 2 * HALO, d), F32), pltpu.VMEM((tm + 2 * HALO, d), F32),
                        pltpu.VMEM((tm, d), F32), pltpu.VMEM((ktaps, 8, d), F32)] + extra["scratch"],
        compiler_params=_cp("arbitrary"))(*ins, *extra["ins"])


def _inproj_bwd(segs, dkv, xa, dxn, modv, gpre, wp_t, n_lat, out_rows, tm, name):
    na, d = xa.shape
    nseg = len(segs)
    wrows = wp_t.shape[0]

    def body(*refs):
        seg_refs = refs[:nseg]
        dkv_ref, x_ref, dxn_ref, mod_ref, g_ref, wt_hbm, dx_ref, acc_ref, wt, sem = refs[nseg:]
        i = pl.program_id(0)
        ctx_tile = i >= n_lat // tm

        @pl.when(i == 0)
        def _():
            cp = pltpu.make_async_copy(wt_hbm, wt, sem)
            cp.start()
            cp.wait()
            acc_ref[...] = jnp.zeros_like(acc_ref)

        dh = _dot(dkv_ref[...], wt[nseg * d:wrows, :])
        for s in range(nseg):
            dh = dh + _dot(seg_refs[s][...], wt[s * d:(s + 1) * d, :])
        x = x_ref[...]
        r = lax.rsqrt(_mean(x * x) + EPS)
        xn = x * r
        g = g_ref[...]
        sc1 = 1.0 + _seg_rows(mod_ref, d, 2 * d, ctx_tile)
        dsh = _colsum(dh)
        dsc = _colsum(dh * (xn * g))

        @pl.when(ctx_tile)
        def _():
            acc_ref[1:2, :] += dsh
            acc_ref[3:4, :] += dsc

        @pl.when(jnp.logical_not(ctx_tile))
        def _():
            acc_ref[0:1, :] += dsh
            acc_ref[2:3, :] += dsc

        acc_ref[4:5, :] += _colsum(dh * xn * sc1)
        dxh = dh * g * sc1

        @pl.when(i < out_tiles)
        def _():
            dx_ref[...] = dxn_ref[...] + r * (dxh - xn * _mean(dxh * xn))

    r_ = _rows(tm, d)
    out_tiles = out_rows // tm
    return _pcall(
        body, name=name, grid=(na // tm,),
        in_specs=[r_] * nseg + [_rows(tm, d // 2), r_, r_, _full((8, 3 * d)), _full((1, d)), ANY],
        out_specs=[pl.BlockSpec((tm, d), lambda i: (jnp.minimum(i, out_tiles - 1), 0)), _full((8, d))],
        out_shape=[jax.ShapeDtypeStruct((out_rows, d), F32), jax.ShapeDtypeStruct((8, d), F32)],
        scratch_shapes=[pltpu.VMEM(wp_t.shape, BF16), pltpu.SemaphoreType.DMA],
        compiler_params=_cp("arbitrary"))(*segs, dkv, xa, dxn, modv, gpre, wp_t)


def _grad_matmul(a_t, b, tk, name):
    ka, na = a_t.shape
    nb = b.shape[1]
    tn = min(nb, 1024)

    def body(a_ref, b_ref, o_ref):
        @pl.when(pl.program_id(1) == 0)
        def _():
            o_ref[...] = jnp.zeros_like(o_ref)

        o_ref[...] += _dot(a_ref[...], b_ref[...])

    return _pcall(
        body, name=name, grid=(nb // tn, na // tk),
        in_specs=[pl.BlockSpec((ka, tk), lambda n, k: (0, k)), pl.BlockSpec((tk, tn), lambda n, k: (k, n))],
        out_specs=pl.BlockSpec((ka, tn), lambda n, k: (0, n)),
        out_shape=jax.ShapeDtypeStruct((ka, nb), F32),
        compiler_params=_cp("parallel", "arbitrary"))(a_t, b)


def _pack(parts, cols, row_mult):
    flat = jnp.concatenate([q.astype(F32).reshape(-1) for q in parts])
    rows = -(-flat.shape[0] // (cols * row_mult)) * row_mult
    return jnp.pad(flat, (0, rows * cols - flat.shape[0])).reshape(1, rows, cols)


def _unpack(flat, shapes):
    out, off = [], 0
    for s in shapes:
        n = math.prod(s)
        out.append(flat[off:off + n].reshape(tuple(s)))
        off += n
    return out


def _cols_by_dest(g):
    l, a, w8 = g.shape
    return g.reshape(l, a, N_DEV, w8 // N_DEV).transpose(2, 0, 1, 3)


def _rows_by_dest(g):
    l, r8, b = g.shape
    return g.reshape(l, N_DEV, r8 // N_DEV, b).transpose(1, 0, 2, 3)


def _cols_from_src(s):
    n, l, a, w = s.shape
    return s.transpose(1, 2, 0, 3).reshape(l, a, n * w)


def _rows_from_src(s):
    n, l, r, b = s.shape
    return s.transpose(1, 0, 2, 3).reshape(l, n * r, b)


def _rope_tables(n_lat, n_ctx):
    half = HEAD_DIM // 2
    rows = n_lat // GRID_W
    row = jnp.repeat(jnp.arange(rows, dtype=F32), GRID_W)
    col = jnp.tile(jnp.arange(GRID_W, dtype=F32), rows)
    inv_freq = ROPE_THETA ** (-jnp.arange(0, half, 2, dtype=F32) / half)
    ang = jnp.concatenate([row[:, None] * inv_freq, col[:, None] * inv_freq], axis=-1)
    cos, sin = jnp.cos(ang), jnp.sin(ang)
    cos_t = jnp.concatenate([jnp.tile(cos, (1, 4)), jnp.ones((n_ctx, 128), F32)], axis=0)
    sin_t = jnp.concatenate([jnp.tile(jnp.concatenate([-sin, sin], axis=-1), (1, 2)),
                             jnp.zeros((n_ctx, 128), F32)], axis=0)
    return cos_t, sin_t


def _to_heads(t):
    na, w = t.shape
    return t.reshape(na, w // HEAD_DIM, HEAD_DIM).transpose(1, 0, 2)


def kernel(x, c, ctx, c_ctx, w_mod, b_mod, g_pre, g_post, w_in, conv_w, conv_b, ln_g, ln_b, w_conv_out, q_norm_g, k_norm_g, w_attn_out, w_out, loss_target, m_c_ctx, m_w_mod, m_b_mod, m_g_pre, m_g_post, m_w_in, m_conv_w, m_conv_b, m_ln_g, m_ln_b, m_w_conv_out, m_q_norm_g, m_k_norm_g, m_w_attn_out, m_w_out, v_c_ctx, v_w_mod, v_b_mod, v_g_pre, v_g_post, v_w_in, v_conv_w, v_conv_b, v_ln_g, v_ln_b, v_w_conv_out, v_q_norm_g, v_k_norm_g, v_w_attn_out, v_w_out):
    depth, d, _ = w_mod.shape
    n_lat, n_ctx = x.shape[1], ctx.shape[1]
    na = n_lat + n_ctx
    heads = d // HEAD_DIM
    kvh = heads // GROUP
    kw = d // GROUP
    ktaps = conv_w.shape[1]
    tm = n_ctx
    tm_half = tm // 2
    tbig = 3 * tm if na % (3 * tm) == 0 else tm
    tq_lat = 16 * tm if n_lat % (16 * tm) == 0 else tm
    tq_bwd = 16 * tm if n_lat % (16 * tm) == 0 else tm

    def layer_shards(l):
        return [w_mod[l].astype(BF16), w_in[l].astype(BF16), w_conv_out[l].astype(BF16), w_attn_out[l].astype(BF16),
                w_out[l].astype(BF16), conv_w[l]]

    def full_weights(gathered):
        s_mod, s_in, s_co, s_ao, s_oo, s_cw = [q[:, None] for q in gathered]
        win = _cols_from_src(s_in)[0]
        wp = jnp.concatenate([win[:, :4 * d], win[:, 4 * d + 2 * kw:], win[:, 4 * d:4 * d + 2 * kw]], axis=1)
        return dict(wmod=_cols_from_src(s_mod)[0], wp=wp, wc=_rows_from_src(s_co)[0], wa=_rows_from_src(s_ao)[0],
                    wo=_rows_from_src(s_oo)[0], convw=_cols_from_src(s_cw)[0])

    weights = [full_weights(_all_gather(layer_shards(0), "gather_weights_l0"))]

    cos_t, sin_t = _rope_tables(n_lat, n_ctx)
    lane = jnp.arange(d)
    esel = (lane[:, None] // HEAD_DIM == jnp.arange(128)[None, :]).astype(BF16)
    esel_t = esel.T
    cvec = jnp.zeros((8, d), F32).at[0].set(c[0]).at[1].set(c_ctx)
    cvec_t = jnp.zeros((d, 128), F32).at[:, 0].set(c[0]).at[:, 1].set(c_ctx)

    xa = jnp.concatenate([x[0], ctx[0]], axis=0)

    saved = []
    for l in range(depth):
        tag = f"_l{l}"
        gq = jnp.tile(q_norm_g[l], heads)[None, :]
        gk = jnp.tile(k_norm_g[l], heads // GROUP)[None, :]
        wl = weights[l]
        modv = _mod_fwd(cvec, wl["wmod"], b_mod[l][None, :], "mod_fwd" + tag)
        p, h_t, *gathered = _inproj(xa, modv, g_pre[l][None, :], wl["wp"], n_lat, tbig, "inproj" + tag,
                                    shards=layer_shards(l + 1) if l + 1 < depth else ())
        if gathered:
            weights.append(full_weights(gathered))
        q_t, kr, vb, k_t, v_t = _qknorm_fwd(p, cos_t, sin_t, gq, gk, esel, esel_t, tm, "qknorm_fwd" + tag)
        qt_hm = q_t.reshape(heads, HEAD_DIM, na)
        kt_hm = k_t.reshape(kvh, HEAD_DIM, na)
        k_hm, v_hm = _to_heads(kr), _to_heads(vb)
        vt_ones = jnp.concatenate([v_t.reshape(kvh, HEAD_DIM, na), jnp.ones((kvh, 16, na), BF16)], axis=1)
        ot_lat, lse_lat = _attn_fwd(qt_hm, k_hm, vt_ones, n_lat, tq_lat, tbig, tm, "attn_fwd" + tag)
        ot_ctx, lse_ctx = _attn_fwd(qt_hm[:, :, n_lat:], k_hm[:, n_lat:], vt_ones[:, :, n_lat:], n_ctx, tm, tm, tm,
                                    "attn_ctx_fwd" + tag)
        ot_lat, ot_ctx = ot_lat.reshape(d, n_lat), ot_ctx.reshape(d, n_ctx)
        y5, y5_t, y2 = _conv_fwd(p, wl["convw"], conv_b[l][None, :], ln_g[l][None, :], ln_b[l][None, :], n_lat, tm,
                                 "conv_fwd" + tag)
        xa_new, yc, ya, out, z_t, og_t, *loss_out = _merge_fwd(
            y5, ot_lat, ot_ctx, p, xa, modv, g_post[l][None, :], wl["wc"], wl["wa"], wl["wo"], n_lat, tm,
            "merge_fwd" + tag, target=loss_target[0] if l == depth - 1 else None)
        saved.append(dict(xa=xa, modv=modv, p=p, h_t=h_t, qt_hm=qt_hm, k_hm=k_hm, kt_hm=kt_hm, v_hm=v_hm,
                          ot_lat=ot_lat, ot_ctx=ot_ctx, lse_lat=lse_lat, lse_ctx=lse_ctx,
                          y5_t=y5_t, y2=y2, yc=yc, ya=ya, out=out, z_t=z_t, og_t=og_t, gq=gq, gk=gk))
        xa = xa_new

    dxa, loss_blk = xa, loss_out[0]

    big_names = ["w_mod", "w_in", "w_conv_out", "w_attn_out", "w_out", "conv_w"]
    g_bmod, g_gpre, g_gpost, g_convb, g_lng, g_lnb, g_qg, g_kg = [], [], [], [], [], [], [], []
    g_cctx = jnp.zeros((d,), F32)
    pending_send, pending_layer, from_chips = None, None, {}

    def add_sibling_parts(send, swapped, layer):
        return [_add_sibling_part(sb, rc, BF16, f"reduce_sibling_add_{n}_l{layer}")
                for n, sb, rc in zip(big_names, send, swapped)]

    for l in reversed(range(depth)):
        tag = f"_l{l}"
        s = saved[l]
        wl = weights[l]
        p = s["p"]
        (dgb, dma, dmb, dgta, do_t, dout, dyc, dya, dy2, dl_t, acc_m, *swapped) = _merge_bwd(
            dxa, s["out"], s["yc"], s["ya"], s["ot_lat"], s["ot_ctx"], p, s["y2"], s["modv"], g_post[l][None, :],
            ln_g[l][None, :], ln_b[l][None, :], wl["wo"].T, wl["wc"].T, wl["wa"].T, esel, n_lat, tm_half,
            "merge_bwd" + tag, rider=_swap_rider(pending_send) if pending_send else None)
        exchange = _exchange_rider(add_sibling_parts(pending_send, swapped, pending_layer)) if swapped else None
        dl_r = dl_t[:heads].reshape(heads, 1, na)
        dot_hm = do_t.reshape(heads, HEAD_DIM, na)
        qt_hm, k_hm, kt_hm, v_hm = s["qt_hm"], s["k_hm"], s["kt_hm"], s["v_hm"]
        dqt_lat, dkt_hm, dvt_hm = _attn_bwd(qt_hm, k_hm, kt_hm, v_hm, dot_hm, s["lse_lat"], dl_r,
                                            n_lat, tq_bwd, tm, tm, "attn_bwd" + tag)
        dqt_ctx, dkt_ctx, dvt_ctx = _attn_bwd(
            qt_hm[:, :, n_lat:], k_hm[:, n_lat:], kt_hm[:, :, n_lat:], v_hm[:, n_lat:], dot_hm[:, :, n_lat:],
            s["lse_ctx"], dl_r[:, :, n_lat:], n_ctx, tm, tm, tm, "attn_ctx_bwd" + tag)
        dq, dkv, acc_q = _qknorm_bwd(
            dqt_lat.reshape(d, n_lat), dqt_ctx.reshape(d, n_ctx), dkt_hm.reshape(kw, na), dkt_ctx.reshape(kw, n_ctx),
            dvt_hm.reshape(kw, na), dvt_ctx.reshape(kw, n_ctx), p, cos_t, sin_t, s["gq"], s["gk"], esel, esel_t, n_lat, tm,
            "qknorm_bwd" + tag)
        da, dg, dconvw, *exchanged = _conv_bwd(dy2, p, wl["convw"], n_lat, tm, "conv_bwd" + tag, rider=exchange)
        if exchanged:
            from_chips[pending_layer] = exchanged
        segs = [da, dg, dgta, dq, dgb, dma, dmb]
        dxa, acc_h = _inproj_bwd(segs, dkv, s["xa"], dxa, s["modv"], g_pre[l][None, :], wl["wp"].T, n_lat,
                                 na if l else n_lat, tm, "inproj_bwd" + tag)

        dwp = [_grad_matmul(s["h_t"], sg, tbig, f"grad_w_in{k}" + tag) for k, sg in enumerate(segs + [dkv])]
        g_win = jnp.concatenate(dwp[:4] + [dwp[7]] + dwp[4:7], axis=1)
        g_wc = _grad_matmul(s["y5_t"], dyc, tbig, "grad_w_conv_out" + tag)
        g_wa = _grad_matmul(s["og_t"], dya, tbig, "grad_w_attn_out" + tag)
        g_wo = _grad_matmul(s["z_t"], dout, tbig, "grad_w_out" + tag)

        dmod = jnp.zeros((8, 3 * d), F32)
        dmod = dmod.at[0].set(jnp.concatenate([acc_h[0], acc_h[2], acc_m[0]]))
        dmod = dmod.at[1].set(jnp.concatenate([acc_h[1], acc_h[3], acc_m[1]]))
        dwm, dbm, dcv = _mod_bwd(dmod, cvec, cvec_t, wl["wmod"], "mod_bwd" + tag)
        by_dest = [_cols_by_dest(dwm[None]), _cols_by_dest(g_win[None]), _rows_by_dest(g_wc[None]),
                   _rows_by_dest(g_wa[None]), _rows_by_dest(g_wo[None]), _cols_by_dest(dconvw[None])]
        pending_send, pending_layer = [q.reshape((4, 2) + q.shape[1:]) for q in by_dest], l
        g_bmod.append(dbm[0])
        g_cctx = g_cctx + dcv[1]
        g_gpre.append(acc_h[4])
        g_gpost.append(acc_m[2])
        g_lng.append(acc_m[3])
        g_lnb.append(acc_m[4])
        g_convb.append(acc_m[5])
        g_qg.append(acc_q[0].reshape(heads, HEAD_DIM).sum(0))
        g_kg.append(acc_q[1, :kw].reshape(heads // GROUP, HEAD_DIM).sum(0))

    grad_x = dxa[None]

    def stack(lst):
        return jnp.stack(lst[::-1])

    big_w = dict(w_mod=w_mod, w_in=w_in, w_conv_out=w_conv_out, w_attn_out=w_attn_out, w_out=w_out, conv_w=conv_w)
    big_m = dict(w_mod=m_w_mod, w_in=m_w_in, w_conv_out=m_w_conv_out, w_attn_out=m_w_attn_out, w_out=m_w_out,
                 conv_w=m_conv_w)
    big_v = dict(w_mod=v_w_mod, w_in=v_w_in, w_conv_out=v_w_conv_out, w_attn_out=v_w_attn_out, w_out=v_w_out,
                 conv_w=v_conv_w)
    swapped = _run_rider(_swap_rider(pending_send), "reduce_sibling")
    from_chips[pending_layer] = _run_rider(
        _exchange_rider(add_sibling_parts(pending_send, swapped, pending_layer)), "reduce_chips")
    big_g, big_d, big_nm, big_nv = {}, {}, {}, {}
    for k, n in enumerate(big_names):
        st = jnp.concatenate([from_chips[l][k] for l in range(depth)], axis=1)
        big_g[n], big_d[n], big_nm[n], big_nv[n] = _sum_adamw(st, big_w[n], big_m[n], big_v[n], "adamw_" + n)

    small_names = ["c_ctx", "b_mod", "g_pre", "g_post", "conv_b", "ln_g", "ln_b", "q_norm_g", "k_norm_g", "loss"]
    zero1 = jnp.zeros((1,), F32)
    small_w = dict(c_ctx=c_ctx, b_mod=b_mod, g_pre=g_pre, g_post=g_post, conv_b=conv_b, ln_g=ln_g, ln_b=ln_b,
                   q_norm_g=q_norm_g, k_norm_g=k_norm_g, loss=zero1)
    small_m = dict(c_ctx=m_c_ctx, b_mod=m_b_mod, g_pre=m_g_pre, g_post=m_g_post, conv_b=m_conv_b, ln_g=m_ln_g,
                   ln_b=m_ln_b, q_norm_g=m_q_norm_g, k_norm_g=m_k_norm_g, loss=zero1)
    small_v = dict(c_ctx=v_c_ctx, b_mod=v_b_mod, g_pre=v_g_pre, g_post=v_g_post, conv_b=v_conv_b, ln_g=v_ln_g,
                   ln_b=v_ln_b, q_norm_g=v_q_norm_g, k_norm_g=v_k_norm_g, loss=zero1)
    small_g = dict(c_ctx=g_cctx, b_mod=stack(g_bmod), g_pre=stack(g_gpre), g_post=stack(g_gpost),
                   conv_b=stack(g_convb), ln_g=stack(g_lng), ln_b=stack(g_lnb), q_norm_g=stack(g_qg),
                   k_norm_g=stack(g_kg), loss=loss_blk[0, 0:1])
    small_shapes = [small_w[n].shape for n in small_names]

    def pack_small(tree):
        return _pack([tree[n] for n in small_names], d, 8)

    small_parts, = _all_gather([pack_small(small_g)], "gather_small_grads")
    small_out = _sum_adamw(small_parts, pack_small(small_w), pack_small(small_m), pack_small(small_v),
                           "adamw_replicated")
    sm_g, sm_d, sm_nm, sm_nv = [dict(zip(small_names, _unpack(o.reshape(-1), small_shapes))) for o in small_out]
    loss = sm_g["loss"].reshape(())

    order = ["c_ctx", "w_mod", "b_mod", "g_pre", "g_post", "w_in", "conv_w", "conv_b", "ln_g", "ln_b",
             "w_conv_out", "q_norm_g", "k_norm_g", "w_attn_out", "w_out"]

    def pick(big, small):
        return [big[n] if n in big else small[n] for n in order]

    return (loss, grad_x, *pick(big_g, sm_g), *pick(big_d, sm_d), *pick(big_nm, sm_nm), *pick(big_nv, sm_nv))
```
